```python
import jax, jax.numpy as jnp
from jax import lax
import numpy as np

D_MODEL = 2048
BATCH = 4
SEQ = 2048
DEPTH = 2

HEAD_DIM = 64
N_MIXERS = 4
GROUP_WIDTH = D_MODEL // N_MIXERS
N_GROUP_HEADS = GROUP_WIDTH // HEAD_DIM
D_MIX = N_MIXERS * GROUP_WIDTH
FNET_W = GROUP_WIDTH
GMLP_W = GROUP_WIDTH
CHUNK = 128
SWA_HEADS = N_GROUP_HEADS
SWA_KV_HEADS = 2
SWA_HALF_WINDOW = 128
DIL_HEADS = N_GROUP_HEADS
DIL_CONFIGS = ((128, 1), (512, 4), (2048, 16))
ROPE_THETA = 500000.0
ROPE_DIM = HEAD_DIM // 4
N_EXPERTS = 16
EXPERT_FF = D_MODEL // 2
EC_FACTOR = 2
D_IN = FNET_W + 2 * GMLP_W + (SWA_HEADS + 2 * SWA_KV_HEADS) * HEAD_DIM + 3 * DIL_HEADS * HEAD_DIM
EPS = 1e-6
NEG_INF = -1e30

kernel_name = 'hybrid_parallel_mixers_ec_moe_encoder'


def rmsnorm(x, g):
    xf = x.astype(jnp.float32)
    y = xf * lax.rsqrt(jnp.mean(xf * xf, axis=-1, keepdims=True) + EPS)
    return (y * g.astype(jnp.float32)).astype(x.dtype)


def modulate(h, shift, scale):
    return h * (1 + scale[:, None, :]) + shift[:, None, :]


def split_points():
    sizes = (FNET_W, 2 * GMLP_W, SWA_HEADS * HEAD_DIM, SWA_KV_HEADS * HEAD_DIM,
             SWA_KV_HEADS * HEAD_DIM, DIL_HEADS * HEAD_DIM, DIL_HEADS * HEAD_DIM,
             DIL_HEADS * HEAD_DIM)
    return [int(v) for v in np.cumsum(sizes)[:-1]]


def rope(x, positions):
    half = ROPE_DIM // 2
    inv = jnp.power(jnp.float32(ROPE_THETA), -jnp.arange(0, ROPE_DIM, 2, dtype=jnp.float32) / ROPE_DIM)
    ang = positions.astype(jnp.float32)[..., None] * inv
    cos = jnp.cos(ang)[:, :, None, :]
    sin = jnp.sin(ang)[:, :, None, :]
    xr = x[..., :ROPE_DIM].astype(jnp.float32)
    x1, x2 = xr[..., :half], xr[..., half:]
    rot = jnp.concatenate([x1 * cos - x2 * sin, x2 * cos + x1 * sin], axis=-1)
    return jnp.concatenate([rot.astype(x.dtype), x[..., ROPE_DIM:]], axis=-1)


def banded_attention(q, k, v, half_window, sink=None):
    n, L, h, dh = q.shape
    hk = k.shape[2]
    g = h // hk
    w = half_window
    nb = -(-L // w)
    lp = nb * w
    pad = lp - L
    qp = jnp.pad(q, ((0, 0), (0, pad), (0, 0), (0, 0)))
    kp = jnp.pad(k, ((0, 0), (w, pad + w), (0, 0), (0, 0)))
    vp = jnp.pad(v, ((0, 0), (w, pad + w), (0, 0), (0, 0)))

    def neighbour_blocks(t):
        tb = t.reshape(n, nb + 2, w, hk, dh)
        return jnp.concatenate([tb[:, :-2], tb[:, 1:-1], tb[:, 2:]], axis=2)

    kb = neighbour_blocks(kp)
    vb = neighbour_blocks(vp)
    qb = qp.reshape(n, nb, w, hk, g, dh)
    s = jnp.einsum('nbqhgd,nbkhd->nbhgqk', qb, kb,
                   preferred_element_type=jnp.float32) * (dh ** -0.5)
    qpos = jnp.arange(nb)[:, None] * w + jnp.arange(w)[None, :]
    kpos = (jnp.arange(nb)[:, None] - 1) * w + jnp.arange(3 * w)[None, :]
    kp3 = kpos[:, None, :]
    valid = (jnp.abs(kp3 - qpos[:, :, None]) <= w) & (kp3 >= 0) & (kp3 < L)
    s = jnp.where(valid[None, :, None, None, :, :], s, NEG_INF)
    if sink is None:
        lse = jax.nn.logsumexp(s, axis=-1)
    else:
        sk = jnp.broadcast_to(sink.astype(jnp.float32).reshape(hk, g)[None, None, :, :, None, None],
                              s.shape[:-1] + (1,))
        lse = jax.nn.logsumexp(jnp.concatenate([s, sk], axis=-1), axis=-1)
    p = jnp.exp(s - lse[..., None])
    out = jnp.einsum('nbhgqk,nbkhd->nbqhgd', p.astype(v.dtype), vb)
    out = out.reshape(n, lp, h, dh)[:, :L]
    lse = lse.transpose(0, 1, 4, 2, 3).reshape(n, lp, h)[:, :L]
    return out, lse


def dilated_attention(q, k, v):
    b, s, h, dh = q.shape
    outs, lses = [], []
    for window, dilation in DIL_CONFIGS:
        steps = window // (2 * dilation)
        ls = s // dilation

        def gather(t):
            return t.reshape(b, ls, dilation, h, dh).transpose(0, 2, 1, 3, 4).reshape(b * dilation, ls, h, dh)

        o, lse = banded_attention(gather(q), gather(k), gather(v), steps)
        outs.append(o.reshape(b, dilation, ls, h, dh).transpose(0, 2, 1, 3, 4).reshape(b, s, h, dh))
        lses.append(lse.reshape(b, dilation, ls, h).transpose(0, 2, 1, 3).reshape(b, s, h))
    wts = jax.nn.softmax(jnp.stack(lses, axis=0), axis=0)
    return jnp.einsum('cbsh,cbshd->bshd', wts.astype(q.dtype), jnp.stack(outs, axis=0))


def fourier_mix(xa, fnet_w):
    b, s, _ = xa.shape
    xg = xa.reshape(b, s, N_GROUP_HEADS, HEAD_DIM).astype(jnp.float32)
    f = jnp.fft.fft2(xg, axes=(1, 3), norm='ortho').real.astype(xa.dtype)
    return jnp.einsum('bshc,hcd->bshd', f, fnet_w).reshape(b, s, FNET_W)


def spatial_gating(uv, sgu_norm, sgu_w, sgu_b):
    uv = jax.nn.gelu(uv)
    u, v = jnp.split(uv, 2, axis=-1)
    v = rmsnorm(v, sgu_norm)
    b, s, _ = v.shape
    vc = v.reshape(b, s // CHUNK, CHUNK, N_GROUP_HEADS, HEAD_DIM)
    z = jnp.einsum('hpq,bnqhc->bnphc', sgu_w, vc) + sgu_b.T[None, None, :, :, None]
    return u * z.reshape(b, s, GMLP_W)


def expert_choice_ffn(h, router_w, w_gate, w_up, w_down):
    b, s, d = h.shape
    cap = max(1, EC_FACTOR * s // N_EXPERTS)
    aff = jax.nn.softmax(jnp.einsum('bsd,de->bse', h, router_w,
                                    preferred_element_type=jnp.float32), axis=-1)
    gate, idx = lax.top_k(aff.transpose(0, 2, 1), cap)
    bidx = jnp.arange(b)[:, None, None]
    xe = h[bidx, idx]
    a = jnp.einsum('becd,edf->becf', xe, w_gate)
    u = jnp.einsum('becd,edf->becf', xe, w_up)
    o = jnp.einsum('becf,efd->becd', jax.nn.silu(a) * u, w_down)
    o = o * gate[..., None].astype(o.dtype)
    return jnp.zeros_like(h).at[bidx, idx].add(o)


def hybrid_layer(x, c, positions, ada_w, ada_b, norm_mix, w_in, fnet_w, sgu_norm, sgu_w,
                 sgu_b, swa_sink, group_norm, w_out, norm_ffn, router_w, exp_w_gate,
                 exp_w_up, exp_w_down):
    b, s, _ = x.shape
    mod = jnp.dot(jax.nn.silu(c), ada_w) + ada_b
    shift_m, scale_m, gate_m, shift_f, scale_f, gate_f = jnp.split(mod, 6, axis=-1)

    h = modulate(rmsnorm(x, norm_mix), shift_m, scale_m)
    proj = jnp.einsum('bsd,de->bse', h, w_in)
    xa, uv, qc, kc, vc, qd, kd, vd = jnp.split(proj, split_points(), axis=-1)

    def heads(t, n):
        return t.reshape(b, s, n, HEAD_DIM)

    y_a = fourier_mix(xa, fnet_w)
    y_b = spatial_gating(uv, sgu_norm, sgu_w, sgu_b)
    y_c, _ = banded_attention(rope(heads(qc, SWA_HEADS), positions),
                              rope(heads(kc, SWA_KV_HEADS), positions),
                              heads(vc, SWA_KV_HEADS), SWA_HALF_WINDOW, swa_sink)
    y_d = dilated_attention(rope(heads(qd, DIL_HEADS), positions),
                            rope(heads(kd, DIL_HEADS), positions),
                            heads(vd, DIL_HEADS))
    y = jnp.concatenate([y_a, y_b, y_c.reshape(b, s, -1), y_d.reshape(b, s, -1)], axis=-1)
    y = rmsnorm(y.reshape(b, s, N_MIXERS, GROUP_WIDTH),
                group_norm.reshape(N_MIXERS, GROUP_WIDTH)).reshape(b, s, D_MIX)
    x = x + gate_m[:, None, :] * jnp.einsum('bse,ed->bsd', y, w_out)

    h = modulate(rmsnorm(x, norm_ffn), shift_f, scale_f)
    x = x + gate_f[:, None, :] * expert_choice_ffn(h, router_w, exp_w_gate, exp_w_up, exp_w_down)
    return x


def setup_inputs(seed: int = 0) -> dict:
    key = jax.random.key(seed)
    ks = jax.random.split(key, 24)
    f32 = jnp.float32

    def nrm(k, shape, scale):
        return jax.random.normal(k, shape, f32) * scale

    def gain(k, shape):
        return 1.0 + 0.02 * jax.random.normal(k, shape, f32)

    L = DEPTH
    return {
        'x': nrm(ks[0], (BATCH, SEQ, D_MODEL), 1.0),
        'c': nrm(ks[1], (BATCH, D_MODEL), 1.0),
        'positions': jnp.arange(SEQ, dtype=jnp.int32)[None, :]
                     + jax.random.randint(ks[2], (BATCH, 1), 0, 1024, dtype=jnp.int32),
        'ada_w': nrm(ks[3], (L, D_MODEL, 6 * D_MODEL), 0.5 * D_MODEL ** -0.5),
        'ada_b': nrm(ks[4], (L, 6 * D_MODEL), 0.02),
        'norm_mix': gain(ks[5], (L, D_MODEL)),
        'w_in': nrm(ks[6], (L, D_MODEL, D_IN), D_MODEL ** -0.5),
        'fnet_w': nrm(ks[7], (L, N_GROUP_HEADS, HEAD_DIM, HEAD_DIM), HEAD_DIM ** -0.5),
        'sgu_norm': gain(ks[8], (L, GMLP_W)),
        'sgu_w': nrm(ks[9], (L, N_GROUP_HEADS, CHUNK, CHUNK), 0.5 * CHUNK ** -0.5),
        'sgu_b': 1.0 + nrm(ks[10], (L, N_GROUP_HEADS, CHUNK), 0.1),
        'swa_sink': nrm(ks[11], (L, SWA_HEADS), 0.5),
        'group_norm': gain(ks[12], (L, D_MIX)),
        'w_out': nrm(ks[13], (L, D_MIX, D_MODEL), D_MIX ** -0.5),
        'norm_ffn': gain(ks[14], (L, D_MODEL)),
        'router_w': nrm(ks[15], (L, D_MODEL, N_EXPERTS), D_MODEL ** -0.5),
        'exp_w_gate': nrm(ks[16], (L, N_EXPERTS, D_MODEL, EXPERT_FF), D_MODEL ** -0.5),
        'exp_w_up': nrm(ks[17], (L, N_EXPERTS, D_MODEL, EXPERT_FF), D_MODEL ** -0.5),
        'exp_w_down': nrm(ks[18], (L, N_EXPERTS, EXPERT_FF, D_MODEL), EXPERT_FF ** -0.5),
        'final_norm': gain(ks[19], (D_MODEL,)),
    }


def reference(x, c, positions, ada_w, ada_b, norm_mix, w_in, fnet_w, sgu_norm, sgu_w, sgu_b,
              swa_sink, group_norm, w_out, norm_ffn, router_w, exp_w_gate, exp_w_up,
              exp_w_down, final_norm):
    for l in range(DEPTH):
        x = hybrid_layer(x, c, positions, ada_w[l], ada_b[l], norm_mix[l], w_in[l], fnet_w[l],
                         sgu_norm[l], sgu_w[l], sgu_b[l], swa_sink[l], group_norm[l], w_out[l],
                         norm_ffn[l], router_w[l], exp_w_gate[l], exp_w_up[l], exp_w_down[l])
    return rmsnorm(x, final_norm)
```

```python
import functools
import math

import jax
import jax.numpy as jnp
from jax import lax
from jax.experimental import pallas as pl
from jax.experimental.pallas import tpu as pltpu

F32 = jnp.float32
BF16 = jnp.bfloat16

HEAD_DIM = 64
N_GROUP_HEADS = 8
GROUP_WIDTH = 512
CHUNK = 128
SWA_HALF_WINDOW = 128
DIL_CONFIGS = ((128, 1), (512, 4), (2048, 16))
ROPE_THETA = 500000.0
ROPE_DIM = 16
N_EXPERTS = 16
EC_FACTOR = 2
EPS = 1e-6
NEG_INF = -1e30

SEC = 256
SEC_XA, SEC_U, SEC_V, SEC_QC, SEC_KVC, SEC_QD, SEC_KD, SEC_VD = 0, 2, 4, 6, 8, 9, 11, 13
N_SEC = 15

VMEM_LIMIT = 56 * 1024 * 1024


def _cp(sem, vmem=None):
    return pltpu.CompilerParams(dimension_semantics=sem, vmem_limit_bytes=vmem or VMEM_LIMIT)


def _nt(a, b):
    return lax.dot_general(a, b, (((1,), (1,)), ((), ())), preferred_element_type=F32)


def _ada_kernel(c_ref, w_ref, b_ref, o_ref):
    c = c_ref[...]
    s = (c * jax.nn.sigmoid(c)).astype(BF16)
    o_ref[0] = jnp.dot(s, w_ref[0].astype(BF16), preferred_element_type=F32) + b_ref[0]


def _ada(c_pad, ada_w, ada_b):
    n_layers, d, n6 = ada_w.shape
    rows = c_pad.shape[0]
    tn = 1024
    return pl.pallas_call(
        _ada_kernel,
        grid=(n_layers, n6 // tn),
        in_specs=[pl.BlockSpec((rows, d), lambda l, j: (0, 0)),
                  pl.BlockSpec((1, d, tn), lambda l, j: (l, 0, j)),
                  pl.BlockSpec((1, 1, tn), lambda l, j: (l, 0, j))],
        out_specs=pl.BlockSpec((1, rows, tn), lambda l, j: (l, 0, j)),
        out_shape=jax.ShapeDtypeStruct((n_layers, rows, n6), F32),
        compiler_params=_cp(("arbitrary", "arbitrary")),
        name="ada",
    )(c_pad, ada_w, ada_b.reshape(n_layers, 1, n6))


def _in_kernel(x_ref, mod_ref, g_ref, w_ref, o_ref, h_ref):
    @pl.when(pl.program_id(1) == 0)
    def _():
        x = x_ref[...]
        ms = jnp.mean(x * x, axis=-1, keepdims=True)
        y = x * lax.rsqrt(ms + EPS) * g_ref[...]
        h_ref[...] = (y * (1.0 + mod_ref[1:2, :]) + mod_ref[0:1, :]).astype(BF16)

    o_ref[...] = jnp.dot(h_ref[...], w_ref[...], preferred_element_type=F32)


def _in_proj(x2, mod_l, norm_g, w_bf, seq):
    t, d = x2.shape
    tm = 1024
    tpb = seq // tm
    return pl.pallas_call(
        _in_kernel,
        grid=(t // tm, N_SEC),
        in_specs=[pl.BlockSpec((tm, d), lambda i, j: (i, 0)),
                  pl.BlockSpec((None, 6, d), lambda i, j: (i // tpb, 0, 0)),
                  pl.BlockSpec((1, d), lambda i, j: (0, 0)),
                  pl.BlockSpec((d, SEC), lambda i, j: (0, j))],
        out_specs=pl.BlockSpec((None, tm, SEC), lambda i, j: (j, i, 0)),
        out_shape=jax.ShapeDtypeStruct((N_SEC, t, SEC), F32),
        scratch_shapes=[pltpu.VMEM((tm, d), BF16)],
        compiler_params=_cp(("arbitrary", "arbitrary")),
        name="in_proj",
    )(x2, mod_l, norm_g, w_bf)


def _f1_kernel(x_ref, bdc_ref, bds_ref, bdw_ref, z_ref, m_ref, *, scale):
    @pl.when(pl.program_id(1) == 0)
    def _():
        w = bdw_ref[...]
        mc = jnp.dot(bdc_ref[...], w, preferred_element_type=F32, precision=lax.Precision.HIGHEST)
        msn = jnp.dot(bds_ref[...], w, preferred_element_type=F32, precision=lax.Precision.HIGHEST)
        m_ref[:, 0:SEC] = (mc * scale).astype(BF16)
        m_ref[:, SEC:2 * SEC] = (msn * scale).astype(BF16)

    z_ref[...] = jnp.dot(x_ref[...].astype(BF16), m_ref[...], preferred_element_type=F32).astype(BF16)


def _f2_kernel(c_ref, s_ref, z_ref, o_ref):
    z = z_ref[...]
    o_ref[...] = (jnp.dot(c_ref[...], z[:, 0:SEC], preferred_element_type=F32)
                  - jnp.dot(s_ref[...], z[:, SEC:2 * SEC], preferred_element_type=F32))


def _fourier(proj, bdc, bds, bdw, dft_c, dft_s, batch, seq):
    t = batch * seq
    tm = 1024
    scale = 1.0 / math.sqrt(seq * HEAD_DIM)
    z = pl.pallas_call(
        functools.partial(_f1_kernel, scale=scale),
        grid=(2, t // tm),
        in_specs=[pl.BlockSpec((None, tm, SEC), lambda s, i: (SEC_XA + s, i, 0)),
                  pl.BlockSpec((SEC, SEC), lambda s, i: (0, 0)),
                  pl.BlockSpec((SEC, SEC), lambda s, i: (0, 0)),
                  pl.BlockSpec((None, SEC, SEC), lambda s, i: (s, 0, 0))],
        out_specs=pl.BlockSpec((None, tm, 2 * SEC), lambda s, i: (s, i, 0)),
        out_shape=jax.ShapeDtypeStruct((2, t, 2 * SEC), BF16),
        scratch_shapes=[pltpu.VMEM((SEC, 2 * SEC), BF16)],
        compiler_params=_cp(("arbitrary", "arbitrary")),
        name="fnet_chan",
    )(proj, bdc, bds, bdw)
    rt = 512
    return pl.pallas_call(
        _f2_kernel,
        grid=(seq // rt, batch, 2),
        in_specs=[pl.BlockSpec((rt, seq), lambda r, b, s: (r, 0)),
                  pl.BlockSpec((rt, seq), lambda r, b, s: (r, 0)),
                  pl.BlockSpec((None, seq, 2 * SEC), lambda r, b, s: (s, b, 0))],
        out_specs=pl.BlockSpec((None, rt, SEC), lambda r, b, s: (b, r, s)),
        out_shape=jax.ShapeDtypeStruct((batch, seq, GROUP_WIDTH), F32),
        compiler_params=_cp(("arbitrary", "arbitrary", "arbitrary")),
        name="fnet_seq",
    )(dft_c, dft_s, z)


def _sgu_kernel(u_ref, v_ref, g_ref, w_ref, b_ref, o_ref, *, tm):
    gv = [jax.nn.gelu(v_ref[s]) for s in range(2)]
    ms = (jnp.sum(gv[0] * gv[0], axis=-1, keepdims=True)
          + jnp.sum(gv[1] * gv[1], axis=-1, keepdims=True)) * (1.0 / GROUP_WIDTH)
    r = lax.rsqrt(ms + EPS)
    vn = [(gv[s] * r * g_ref[:, s * SEC:(s + 1) * SEC]).astype(BF16) for s in range(2)]
    for s in range(2):
        gu = jax.nn.gelu(u_ref[s])
        for c in range(tm // CHUNK):
            rows = slice(c * CHUNK, (c + 1) * CHUNK)
            for hh in range(4):
                h = s * 4 + hh
                cols = slice(hh * HEAD_DIM, (hh + 1) * HEAD_DIM)
                z = jnp.dot(w_ref[h], vn[s][rows, cols], preferred_element_type=F32) + b_ref[:, h:h + 1]
                o_ref[rows, s * SEC + hh * HEAD_DIM:s * SEC + (hh + 1) * HEAD_DIM] = gu[rows, cols] * z


def _sgu(proj, sgu_norm, sgu_w_bf, sgu_bt):
    t = proj.shape[1]
    tm = 512
    return pl.pallas_call(
        functools.partial(_sgu_kernel, tm=tm),
        grid=(t // tm,),
        in_specs=[pl.BlockSpec((2, tm, SEC), lambda i: (SEC_U // 2, i, 0)),
                  pl.BlockSpec((2, tm, SEC), lambda i: (SEC_V // 2, i, 0)),
                  pl.BlockSpec((1, GROUP_WIDTH), lambda i: (0, 0)),
                  pl.BlockSpec((N_GROUP_HEADS, CHUNK, CHUNK), lambda i: (0, 0, 0)),
                  pl.BlockSpec((CHUNK, N_GROUP_HEADS), lambda i: (0, 0))],
        out_specs=pl.BlockSpec((tm, GROUP_WIDTH), lambda i: (i, 0)),
        out_shape=jax.ShapeDtypeStruct((t, GROUP_WIDTH), F32),
        compiler_params=_cp(("arbitrary",)),
        name="sgu",
    )(proj, proj, sgu_norm, sgu_w_bf, sgu_bt)


ATT_TQ = 128
PAIR = 2 * HEAD_DIM


def _rope(x, cos_t, sin_t):
    n = x.shape[-1]
    half = ROPE_DIM // 2
    lane = lax.broadcasted_iota(jnp.int32, x.shape, 1) % HEAD_DIM
    swapped = jnp.where(lane < half, pltpu.roll(x, n - half, 1), pltpu.roll(x, half, 1))
    return x * cos_t + swapped * sin_t


def _pair_tile(q, k, v, valid):
    lo = lax.broadcasted_iota(jnp.int32, q.shape, 1) < HEAD_DIM
    zero = jnp.zeros_like(q)
    parts = []
    for qh in (jnp.where(lo, q, zero), jnp.where(lo, zero, q)):
        s = jnp.where(valid, _nt(qh, k), NEG_INF)
        m = jnp.max(s, axis=-1, keepdims=True)
        p = jnp.exp(s - m)
        l = jnp.sum(p, axis=-1, keepdims=True)
        parts.append((jnp.dot(p.astype(BF16), v, preferred_element_type=F32), m, l))
    (oa, ma, la), (ob, mb, lb) = parts
    shp = oa.shape
    return (jnp.where(lo, oa, ob),
            jnp.where(lo, jnp.broadcast_to(ma, shp), jnp.broadcast_to(mb, shp)),
            jnp.where(lo, jnp.broadcast_to(la, shp), jnp.broadcast_to(lb, shp)))


def _band_valid(tq, tk, w, delta):
    d = (lax.broadcasted_iota(jnp.int32, (tq, tk), 1) - lax.broadcasted_iota(jnp.int32, (tq, tk), 0)) + delta
    return (d <= w) & (d >= -w)


def _swa_kernel(q_ref, kv_ref, cos_ref, sin_ref, sink_ref, o_ref, qs_ref, ks_ref, vs_ref, *, seq):
    w, tq = SWA_HALF_WINDOW, ATT_TQ
    tk = tq + 2 * w
    cos, sin = cos_ref[...], sin_ref[...]
    for s in range(2):
        qs_ref[s] = (_rope(q_ref[s], cos, sin) * (HEAD_DIM ** -0.5)).astype(BF16)
    kk = _rope(kv_ref[:, 0:PAIR], cos[:, 0:PAIR], sin[:, 0:PAIR])
    vv = kv_ref[:, PAIR:2 * PAIR]
    lo = lax.broadcasted_iota(jnp.int32, kk.shape, 1) < HEAD_DIM
    kk_sw, vv_sw = pltpu.roll(kk, HEAD_DIM, 1), pltpu.roll(vv, HEAD_DIM, 1)
    ks_ref[0] = jnp.where(lo, kk, kk_sw).astype(BF16)
    ks_ref[1] = jnp.where(lo, kk_sw, kk).astype(BF16)
    vs_ref[0] = jnp.where(lo, vv, vv_sw).astype(BF16)
    vs_ref[1] = jnp.where(lo, vv_sw, vv).astype(BF16)

    def tile(i, carry):
        q0 = pl.multiple_of(i * tq, tq)
        k0 = pl.multiple_of(jnp.clip(q0 - w, 0, seq - tk), tq)
        valid = _band_valid(tq, tk, w, k0 - q0)
        for pair in range(4):
            g = pair // 2
            lanes = slice((pair % 2) * PAIR, (pair % 2 + 1) * PAIR)
            o, m, l = _pair_tile(qs_ref[g, pl.ds(q0, tq), lanes], ks_ref[g, pl.ds(k0, tk), :],
                                 vs_ref[g, pl.ds(k0, tk), :], valid)
            sink = sink_ref[:, pair * PAIR:(pair + 1) * PAIR]
            m2 = jnp.maximum(m, sink)
            a = jnp.exp(m - m2)
            o_ref[pl.ds(q0, tq), pair * PAIR:(pair + 1) * PAIR] = o * a / (l * a + jnp.exp(sink - m2))
        return carry

    lax.fori_loop(0, seq // tq, tile, 0)


def _swa(proj, cos_q, sin_q, sink_row, batch, seq):
    t = batch * seq
    tab = pl.BlockSpec((None, seq, SEC), lambda b: (b, 0, 0))
    return pl.pallas_call(
        functools.partial(_swa_kernel, seq=seq),
        grid=(batch,),
        in_specs=[pl.BlockSpec((2, seq, SEC), lambda b: (SEC_QC // 2, b, 0)),
                  pl.BlockSpec((None, seq, SEC), lambda b: (SEC_KVC, b, 0)),
                  tab, tab,
                  pl.BlockSpec((1, GROUP_WIDTH), lambda b: (0, 0))],
        out_specs=pl.BlockSpec((seq, GROUP_WIDTH), lambda b: (b, 0)),
        out_shape=jax.ShapeDtypeStruct((t, GROUP_WIDTH), F32),
        scratch_shapes=[pltpu.VMEM((2, seq, SEC), BF16),
                        pltpu.VMEM((2, seq, PAIR), BF16),
                        pltpu.VMEM((2, seq, PAIR), BF16)],
        compiler_params=_cp(("arbitrary",)),
        name="swa",
    )(proj, proj, cos_q, sin_q, sink_row)


def _dil_kernel(q_ref, k_ref, v_ref, cos_ref, sin_ref, o_ref,
                qf_ref, kf_ref, vf_ref, m_ref, l_ref, acc_ref, qd_ref, kd_ref, vd_ref, *, seq):
    cos, sin = cos_ref[...], sin_ref[...]
    q = _rope(q_ref[...], cos, sin) * (HEAD_DIM ** -0.5)
    k = _rope(k_ref[...], cos, sin)
    for pair in range(2):
        lanes = slice(pair * PAIR, (pair + 1) * PAIR)
        qf_ref[pair] = q[:, lanes]
        kf_ref[pair] = k[:, lanes]
        vf_ref[pair] = v_ref[:, lanes]
    m_ref[...] = jnp.full(m_ref.shape, NEG_INF, F32)
    l_ref[...] = jnp.zeros(l_ref.shape, F32)
    acc_ref[...] = jnp.zeros(acc_ref.shape, F32)

    for window, dil in DIL_CONFIGS:
        w = window // (2 * dil)
        ls = seq // dil
        tq = min(ATT_TQ, ls)
        tk = min(tq + 2 * w, ls)

        def rows_of(start, size, dil=dil):
            return pl.ds(start, size) if dil == 1 else pl.ds(start, size, stride=dil)

        def residue(r, carry, w=w, ls=ls, tq=tq, tk=tk, dil=dil, rows_of=rows_of):
            for pair in range(2):
                qd_ref[pair, 0:ls, :] = qf_ref[pair, rows_of(r, ls), :].astype(BF16)
                kd_ref[pair, 0:ls, :] = kf_ref[pair, rows_of(r, ls), :].astype(BF16)
                vd_ref[pair, 0:ls, :] = vf_ref[pair, rows_of(r, ls), :].astype(BF16)

            def tile(i, c2):
                q0 = pl.multiple_of(i * tq, tq)
                k0 = pl.multiple_of(jnp.clip(q0 - w, 0, ls - tk), w)
                valid = _band_valid(tq, tk, w, k0 - q0)
                rows = rows_of(r + dil * q0, tq)
                for pair in range(2):
                    o, m, l = _pair_tile(qd_ref[pair, pl.ds(q0, tq), :], kd_ref[pair, pl.ds(k0, tk), :],
                                         vd_ref[pair, pl.ds(k0, tk), :], valid)
                    m_old = m_ref[pair, rows, :]
                    m_new = jnp.maximum(m_old, m)
                    a, b = jnp.exp(m_old - m_new), jnp.exp(m - m_new)
                    acc_ref[pair, rows, :] = acc_ref[pair, rows, :] * a + o * b
                    l_ref[pair, rows, :] = l_ref[pair, rows, :] * a + l * b
                    m_ref[pair, rows, :] = m_new
                return c2

            lax.fori_loop(0, ls // tq, tile, 0)
            return carry

        lax.fori_loop(0, dil, residue, 0)

    for pair in range(2):
        o_ref[:, pair * PAIR:(pair + 1) * PAIR] = acc_ref[pair] / l_ref[pair]


def _dilated(proj, cos_q, sin_q, batch, seq):
    t = batch * seq

    def sec(base):
        return pl.BlockSpec((None, seq, SEC), lambda b, g: (base + g, b, 0))

    tab = pl.BlockSpec((None, seq, SEC), lambda b, g: (b, 0, 0))
    return pl.pallas_call(
        functools.partial(_dil_kernel, seq=seq),
        grid=(batch, 2),
        in_specs=[sec(SEC_QD), sec(SEC_KD), sec(SEC_VD), tab, tab],
        out_specs=pl.BlockSpec((seq, SEC), lambda b, g: (b, g)),
        out_shape=jax.ShapeDtypeStruct((t, GROUP_WIDTH), F32),
        scratch_shapes=[pltpu.VMEM((2, seq, PAIR), F32)] * 6 + [pltpu.VMEM((2, seq, PAIR), BF16)] * 3,
        compiler_params=_cp(("arbitrary", "arbitrary")),
        name="dilated",
    )(proj, proj, proj, cos_q, sin_q)


def _gnorm(y, g):
    return y * lax.rsqrt(jnp.mean(y * y, axis=-1, keepdims=True) + EPS) * g


def _out_kernel(ya_ref, yb_ref, yc_ref, yd_ref, gn_ref, w_ref, x_ref, mod_ref, nf_ref, rcat_ref,
                xn_ref, h_ref, aff_ref):
    ys = (ya_ref, yb_ref, yc_ref, yd_ref)
    gw = GROUP_WIDTH
    acc = None
    for g in range(4):
        yn = _gnorm(ys[g][...], gn_ref[:, g * gw:(g + 1) * gw]).astype(BF16)
        part = jnp.dot(yn, w_ref[g * gw:(g + 1) * gw, :], preferred_element_type=F32)
        acc = part if acc is None else acc + part
    xn = x_ref[...] + mod_ref[2:3, :] * acc
    xn_ref[...] = xn
    h = _gnorm(xn, nf_ref[...]) * (1.0 + mod_ref[4:5, :]) + mod_ref[3:4, :]
    h_hi = h.astype(BF16)
    h_lo = (h - h_hi.astype(F32)).astype(BF16)
    h_ref[...] = h_hi
    a = _nt(rcat_ref[...], h_hi)
    b = _nt(rcat_ref[0:N_EXPERTS, :], h_lo)
    logits = a[0:N_EXPERTS] + a[N_EXPERTS:2 * N_EXPERTS] + b
    mxl = jnp.max(logits, axis=0, keepdims=True)
    ex = jnp.exp(logits - mxl)
    aff_ref[...] = ex / jnp.sum(ex, axis=0, keepdims=True)


def _out_proj(ya, yb, yc, yd, gn, w_bf, x2, mod_l, nf, rcat, seq):
    t, d = x2.shape
    tm = 256
    tpb = seq // tm
    ysp = pl.BlockSpec((tm, GROUP_WIDTH), lambda i: (i, 0))
    return pl.pallas_call(
        _out_kernel,
        grid=(t // tm,),
        in_specs=[ysp] * 4 + [
            pl.BlockSpec((1, d), lambda i: (0, 0)),
            pl.BlockSpec((d, d), lambda i: (0, 0)),
            pl.BlockSpec((tm, d), lambda i: (i, 0)),
            pl.BlockSpec((None, 6, d), lambda i: (i // tpb, 0, 0)),
            pl.BlockSpec((1, d), lambda i: (0, 0)),
            pl.BlockSpec((2 * N_EXPERTS, d), lambda i: (0, 0))],
        out_specs=[pl.BlockSpec((tm, d), lambda i: (i, 0)),
                   pl.BlockSpec((tm, d), lambda i: (i, 0)),
                   pl.BlockSpec((N_EXPERTS, tm), lambda i: (0, i))],
        out_shape=[jax.ShapeDtypeStruct((t, d), F32),
                   jax.ShapeDtypeStruct((t, d), BF16),
                   jax.ShapeDtypeStruct((N_EXPERTS, t), F32)],
        compiler_params=_cp(("arbitrary",)),
        name="out_proj",
    )(ya, yb, yc, yd, gn, w_bf, x2, mod_l, nf, rcat)


def _route_kernel(aff_ref, tri_ref, sel_ref, pos_ref, *, cap):
    aff = aff_ref[...]

    def as_f32(bits):
        return lax.bitcast_convert_type(bits, F32)

    def step(k, thr):
        cand = thr | lax.shift_left(jnp.int32(1), 30 - k)
        cnt = jnp.sum((aff >= as_f32(cand)).astype(jnp.int32), axis=1, keepdims=True)
        return jnp.where(cnt >= cap, cand, thr)

    thr = lax.fori_loop(0, 31, step, jnp.zeros((aff.shape[0], 1), jnp.int32))
    gt = aff >= as_f32(thr + 1)
    eq = (aff >= as_f32(thr)) & jnp.logical_not(gt)
    need = (cap - jnp.sum(gt.astype(jnp.int32), axis=1, keepdims=True)).astype(F32)
    eq_rank = jnp.dot(eq.astype(BF16), tri_ref[...], preferred_element_type=F32)
    sel = gt | (eq & (eq_rank < need))
    sel_ref[...] = sel.astype(F32)
    pos_ref[...] = jnp.dot(sel.astype(BF16), tri_ref[...], preferred_element_type=F32)


def _route(aff_t, tri, batch, seq, cap):
    e = aff_t.shape[0]
    blk = pl.BlockSpec((e, seq), lambda b: (0, b))
    shp = jax.ShapeDtypeStruct(aff_t.shape, F32)
    return pl.pallas_call(
        functools.partial(_route_kernel, cap=cap),
        grid=(batch,),
        in_specs=[blk, pl.BlockSpec((seq, seq), lambda b: (0, 0))],
        out_specs=[blk, blk],
        out_shape=[shp, shp],
        compiler_params=_cp(("arbitrary",)),
        name="route",
    )(aff_t, tri)


def _onehot(sel_row, pos_row, cap):
    r = lax.broadcasted_iota(jnp.int32, (cap, sel_row.shape[1]), 0).astype(F32)
    return (pos_row == r) & (sel_row > 0.5)


def _moe_kernel(sel_ref, pos_ref, aff_ref, h_ref, wg_ref, wu_ref, wd_ref, o_ref,
                xe_ref, gate_ref, acc_ref, *, cap):
    e = pl.program_id(1)
    f = pl.program_id(2)

    @pl.when(f == 0)
    def _():
        oh = _onehot(sel_ref[pl.ds(e, 1), :], pos_ref[pl.ds(e, 1), :], cap)
        xe_ref[...] = jnp.dot(oh.astype(BF16), h_ref[...], preferred_element_type=F32).astype(BF16)
        gate_ref[...] = jnp.sum(jnp.where(oh, aff_ref[pl.ds(e, 1), :], 0.0), axis=1, keepdims=True)
        acc_ref[...] = jnp.zeros_like(acc_ref)

    xe = xe_ref[...]
    a = jnp.dot(xe, wg_ref[...], preferred_element_type=F32)
    u = jnp.dot(xe, wu_ref[...], preferred_element_type=F32)
    act = (a * jax.nn.sigmoid(a) * u).astype(BF16)
    acc_ref[...] += jnp.dot(act, wd_ref[...], preferred_element_type=F32)

    @pl.when(f == pl.num_programs(2) - 1)
    def _():
        o_ref[...] = (acc_ref[...] * gate_ref[...]).astype(BF16)


def _moe(sel, pos, aff_t, h2, wg, wu, wd, batch, seq, cap):
    t, d = h2.shape
    ne, _, ff = wg.shape
    tf = 256
    rowblk = pl.BlockSpec((ne, seq), lambda b, e, f: (0, b))
    return pl.pallas_call(
        functools.partial(_moe_kernel, cap=cap),
        grid=(batch, ne, ff // tf),
        in_specs=[rowblk, rowblk, rowblk,
                  pl.BlockSpec((seq, d), lambda b, e, f: (b, 0)),
                  pl.BlockSpec((None, d, tf), lambda b, e, f: (e, 0, f)),
                  pl.BlockSpec((None, d, tf), lambda b, e, f: (e, 0, f)),
                  pl.BlockSpec((None, tf, d), lambda b, e, f: (e, f, 0))],
        out_specs=pl.BlockSpec((None, None, cap, d), lambda b, e, f: (b, e, 0, 0)),
        out_shape=jax.ShapeDtypeStruct((batch, ne, cap, d), BF16),
        scratch_shapes=[pltpu.VMEM((cap, d), BF16), pltpu.VMEM((cap, 1), F32), pltpu.VMEM((cap, d), F32)],
        compiler_params=_cp(("arbitrary", "arbitrary", "arbitrary")),
        name="moe",
    )(sel, pos, aff_t, h2, wg, wu, wd)


def _scatter_kernel(sel_ref, pos_ref, o_ref, xn_ref, mod_ref, fn_ref, out_ref, acc_ref, *, cap, final_norm):
    e = pl.program_id(1)

    @pl.when(e == 0)
    def _():
        acc_ref[...] = jnp.zeros_like(acc_ref)

    oh = _onehot(sel_ref[pl.ds(e, 1), :], pos_ref[pl.ds(e, 1), :], cap).astype(BF16)
    acc_ref[...] += lax.dot_general(oh, o_ref[...], (((0,), (0,)), ((), ())), preferred_element_type=F32)

    @pl.when(e == pl.num_programs(1) - 1)
    def _():
        x = xn_ref[...] + mod_ref[5:6, :] * acc_ref[...]
        if final_norm:
            x = _gnorm(x, fn_ref[...])
        out_ref[...] = x


def _scatter(sel, pos, o, xn, mod_l, fn, seq, cap, final_norm):
    t, d = xn.shape
    ne = sel.shape[0]
    tm = 512
    tpb = seq // tm
    rowblk = pl.BlockSpec((ne, tm), lambda i, e: (0, i))
    return pl.pallas_call(
        functools.partial(_scatter_kernel, cap=cap, final_norm=final_norm),
        grid=(t // tm, ne),
        in_specs=[rowblk, rowblk,
                  pl.BlockSpec((None, None, cap, d), lambda i, e: (i // tpb, e, 0, 0)),
                  pl.BlockSpec((tm, d), lambda i, e: (i, 0)),
                  pl.BlockSpec((None, 6, d), lambda i, e: (i // tpb, 0, 0)),
                  pl.BlockSpec((1, d), lambda i, e: (0, 0))],
        out_specs=pl.BlockSpec((tm, d), lambda i, e: (i, 0)),
        out_shape=jax.ShapeDtypeStruct((t, d), F32),
        scratch_shapes=[pltpu.VMEM((tm, d), F32)],
        compiler_params=_cp(("arbitrary", "arbitrary")),
        name="scatter",
    )(sel, pos, o, xn, mod_l, fn)


def _rope_tables(positions):
    b, s = positions.shape
    inv = jnp.power(jnp.float32(ROPE_THETA), -jnp.arange(0, ROPE_DIM, 2, dtype=F32) / ROPE_DIM)
    ang = positions.astype(F32)[..., None] * inv
    cos, sin = jnp.cos(ang), jnp.sin(ang)
    rest = HEAD_DIM - ROPE_DIM
    c64 = jnp.concatenate([cos, cos, jnp.ones((b, s, rest), F32)], axis=-1)
    s64 = jnp.concatenate([-sin, sin, jnp.zeros((b, s, rest), F32)], axis=-1)
    return jnp.tile(c64, (1, 1, 4)), jnp.tile(s64, (1, 1, 4))


def _dft_tables(n):
    k = jnp.arange(n, dtype=jnp.int32)
    ang = ((k[:, None] * k[None, :]) % n).astype(F32) * (2.0 * math.pi / n)
    return jnp.cos(ang), jnp.sin(ang)


def _block_diag(blocks):
    n, a, _ = blocks.shape
    eye = jnp.eye(n, dtype=blocks.dtype)
    return (eye[:, None, :, None] * blocks[:, :, None, :]).reshape(n * a, n * a)


def kernel(x, c, positions, ada_w, ada_b, norm_mix, w_in, fnet_w, sgu_norm, sgu_w, sgu_b, swa_sink,
           group_norm, w_out, norm_ffn, router_w, exp_w_gate, exp_w_up, exp_w_down, final_norm):
    batch, seq, d = x.shape
    depth = ada_w.shape[0]
    t = batch * seq
    cap = max(1, EC_FACTOR * seq // N_EXPERTS)

    rows = 16
    c_pad = jnp.zeros((rows, d), F32).at[:batch].set(c)
    mod = _ada(c_pad, ada_w, ada_b)[:, :batch].reshape(depth, batch, 6, d)

    cos_q, sin_q = _rope_tables(positions)
    dft_c, dft_s = _dft_tables(seq)
    dft_c, dft_s = dft_c.astype(BF16), dft_s.astype(BF16)
    cc, sc = _dft_tables(HEAD_DIM)
    bdc = _block_diag(jnp.broadcast_to(cc, (4, HEAD_DIM, HEAD_DIM)))
    bds = _block_diag(jnp.broadcast_to(sc, (4, HEAD_DIM, HEAD_DIM)))
    tri = (jnp.arange(seq)[:, None] < jnp.arange(seq)[None, :]).astype(BF16)

    x2 = x.reshape(t, d)
    for l in range(depth):
        mod_l = mod[l]
        proj = _in_proj(x2, mod_l, norm_mix[l].reshape(1, d), w_in[l].astype(BF16), seq)

        bdw = jnp.stack([_block_diag(fnet_w[l, 0:4]), _block_diag(fnet_w[l, 4:8])])
        ya = _fourier(proj, bdc, bds, bdw, dft_c, dft_s, batch, seq).reshape(t, GROUP_WIDTH)
        yb = _sgu(proj, sgu_norm[l].reshape(1, GROUP_WIDTH), sgu_w[l].astype(BF16), sgu_b[l].T)
        sink_row = jnp.repeat(swa_sink[l], HEAD_DIM).reshape(1, GROUP_WIDTH)
        yc = _swa(proj, cos_q, sin_q, sink_row, batch, seq)
        yd = _dilated(proj, cos_q, sin_q, batch, seq)

        r_t = router_w[l].T
        r_hi = r_t.astype(BF16)
        r_lo = (r_t - r_hi.astype(F32)).astype(BF16)
        rcat = jnp.concatenate([r_hi, r_lo], axis=0)
        xn, h2, aff_t = _out_proj(ya, yb, yc, yd, group_norm[l].reshape(1, d), w_out[l].astype(BF16),
                                  x2, mod_l, norm_ffn[l].reshape(1, d), rcat, seq)
        sel, pos = _route(aff_t, tri, batch, seq, cap)
        o = _moe(sel, pos, aff_t, h2, exp_w_gate[l].astype(BF16), exp_w_up[l].astype(BF16),
                 exp_w_down[l].astype(BF16), batch, seq, cap)
        x2 = _scatter(sel, pos, o, xn, mod_l, final_norm.reshape(1, d), seq, cap, l == depth - 1)
    return x2.reshape(batch, seq, d)
```

```python
import functools
import math

import jax
import jax.numpy as jnp
from jax import lax
from jax.experimental import pallas as pl
from jax.experimental.pallas import tpu as pltpu

F32 = jnp.float32
BF16 = jnp.bfloat16

HEAD_DIM = 64
N_GROUP_HEADS = 8
GROUP_WIDTH = 512
CHUNK = 128
SWA_HALF_WINDOW = 128
DIL_CONFIGS = ((128, 1), (512, 4), (2048, 16))
ROPE_THETA = 500000.0
ROPE_DIM = 16
N_EXPERTS = 16
EC_FACTOR = 2
EPS = 1e-6
NEG_INF = -1e30

SEC = 256
SEC_XA, SEC_U, SEC_V, SEC_QC, SEC_KVC, SEC_QD, SEC_KD, SEC_VD = 0, 2, 4, 6, 8, 9, 11, 13
N_SEC = 15

VMEM_LIMIT = 56 * 1024 * 1024


def _cp(sem, vmem=None):
    return pltpu.CompilerParams(dimension_semantics=sem, vmem_limit_bytes=vmem or VMEM_LIMIT)


def _nt(a, b):
    return lax.dot_general(a, b, (((1,), (1,)), ((), ())), preferred_element_type=F32)


def _ada_kernel(c_ref, w_ref, b_ref, o_ref):
    c = c_ref[...]
    s = (c * jax.nn.sigmoid(c)).astype(BF16)
    o_ref[0] = jnp.dot(s, w_ref[0].astype(BF16), preferred_element_type=F32) + b_ref[0]


def _ada(c_pad, ada_w, ada_b):
    n_layers, d, n6 = ada_w.shape
    rows = c_pad.shape[0]
    tn = 1024
    return pl.pallas_call(
        _ada_kernel,
        grid=(n_layers, n6 // tn),
        in_specs=[pl.BlockSpec((rows, d), lambda l, j: (0, 0)),
                  pl.BlockSpec((1, d, tn), lambda l, j: (l, 0, j)),
                  pl.BlockSpec((1, 1, tn), lambda l, j: (l, 0, j))],
        out_specs=pl.BlockSpec((1, rows, tn), lambda l, j: (l, 0, j)),
        out_shape=jax.ShapeDtypeStruct((n_layers, rows, n6), F32),
        compiler_params=_cp(("arbitrary", "arbitrary")),
        name="ada",
    )(c_pad, ada_w, ada_b.reshape(n_layers, 1, n6))


def _in_kernel(x_ref, mod_ref, g_ref, w_ref, o_ref, h_ref):
    @pl.when(pl.program_id(1) == 0)
    def _():
        x = x_ref[...]
        ms = jnp.mean(x * x, axis=-1, keepdims=True)
        y = x * lax.rsqrt(ms + EPS) * g_ref[...]
        h_ref[...] = (y * (1.0 + mod_ref[1:2, :]) + mod_ref[0:1, :]).astype(BF16)

    o_ref[...] = jnp.dot(h_ref[...], w_ref[...], preferred_element_type=F32)


def _in_proj(x2, mod_l, norm_g, w_bf, seq):
    t, d = x2.shape
    tm = 1024
    tpb = seq // tm
    return pl.pallas_call(
        _in_kernel,
        grid=(t // tm, N_SEC),
        in_specs=[pl.BlockSpec((tm, d), lambda i, j: (i, 0)),
                  pl.BlockSpec((None, 6, d), lambda i, j: (i // tpb, 0, 0)),
                  pl.BlockSpec((1, d), lambda i, j: (0, 0)),
                  pl.BlockSpec((d, SEC), lambda i, j: (0, j))],
        out_specs=pl.BlockSpec((None, tm, SEC), lambda i, j: (j, i, 0)),
        out_shape=jax.ShapeDtypeStruct((N_SEC, t, SEC), F32),
        scratch_shapes=[pltpu.VMEM((tm, d), BF16)],
        compiler_params=_cp(("arbitrary", "arbitrary")),
        name="in_proj",
    )(x2, mod_l, norm_g, w_bf)


def _f1_kernel(x_ref, bdc_ref, bds_ref, bdw_ref, z_ref, m_ref, *, scale):
    @pl.when(pl.program_id(1) == 0)
    def _():
        w = bdw_ref[...]
        mc = jnp.dot(bdc_ref[...], w, preferred_element_type=F32, precision=lax.Precision.HIGHEST)
        msn = jnp.dot(bds_ref[...], w, preferred_element_type=F32, precision=lax.Precision.HIGHEST)
        m_ref[:, 0:SEC] = (mc * scale).astype(BF16)
        m_ref[:, SEC:2 * SEC] = (msn * scale).astype(BF16)

    z_ref[...] = jnp.dot(x_ref[...].astype(BF16), m_ref[...], preferred_element_type=F32).astype(BF16)


def _f2_kernel(c_ref, s_ref, z_ref, o_ref):
    z = z_ref[...]
    o_ref[...] = (jnp.dot(c_ref[...], z[:, 0:SEC], preferred_element_type=F32)
                  - jnp.dot(s_ref[...], z[:, SEC:2 * SEC], preferred_element_type=F32))


def _fourier(proj, bdc, bds, bdw, dft_c, dft_s, batch, seq):
    t = batch * seq
    tm = 1024
    scale = 1.0 / math.sqrt(seq * HEAD_DIM)
    z = pl.pallas_call(
        functools.partial(_f1_kernel, scale=scale),
        grid=(2, t // tm),
        in_specs=[pl.BlockSpec((None, tm, SEC), lambda s, i: (SEC_XA + s, i, 0)),
                  pl.BlockSpec((SEC, SEC), lambda s, i: (0, 0)),
                  pl.BlockSpec((SEC, SEC), lambda s, i: (0, 0)),
                  pl.BlockSpec((None, SEC, SEC), lambda s, i: (s, 0, 0))],
        out_specs=pl.BlockSpec((None, tm, 2 * SEC), lambda s, i: (s, i, 0)),
        out_shape=jax.ShapeDtypeStruct((2, t, 2 * SEC), BF16),
        scratch_shapes=[pltpu.VMEM((SEC, 2 * SEC), BF16)],
        compiler_params=_cp(("arbitrary", "arbitrary")),
        name="fnet_chan",
    )(proj, bdc, bds, bdw)
    rt = 512
    return pl.pallas_call(
        _f2_kernel,
        grid=(seq // rt, batch, 2),
        in_specs=[pl.BlockSpec((rt, seq), lambda r, b, s: (r, 0)),
                  pl.BlockSpec((rt, seq), lambda r, b, s: (r, 0)),
                  pl.BlockSpec((None, seq, 2 * SEC), lambda r, b, s: (s, b, 0))],
        out_specs=pl.BlockSpec((None, rt, SEC), lambda r, b, s: (b, r, s)),
        out_shape=jax.ShapeDtypeStruct((batch, seq, GROUP_WIDTH), F32),
        compiler_params=_cp(("arbitrary", "arbitrary", "arbitrary")),
        name="fnet_seq",
    )(dft_c, dft_s, z)


def _sgu_kernel(u_ref, v_ref, g_ref, w_ref, b_ref, o_ref, *, tm):
    gv = [jax.nn.gelu(v_ref[s]) for s in range(2)]
    ms = (jnp.sum(gv[0] * gv[0], axis=-1, keepdims=True)
          + jnp.sum(gv[1] * gv[1], axis=-1, keepdims=True)) * (1.0 / GROUP_WIDTH)
    r = lax.rsqrt(ms + EPS)
    vn = [(gv[s] * r * g_ref[:, s * SEC:(s + 1) * SEC]).astype(BF16) for s in range(2)]
    for s in range(2):
        gu = jax.nn.gelu(u_ref[s])
        for c in range(tm // CHUNK):
            rows = slice(c * CHUNK, (c + 1) * CHUNK)
            for hh in range(4):
                h = s * 4 + hh
                cols = slice(hh * HEAD_DIM, (hh + 1) * HEAD_DIM)
                z = jnp.dot(w_ref[h], vn[s][rows, cols], preferred_element_type=F32) + b_ref[:, h:h + 1]
                o_ref[rows, s * SEC + hh * HEAD_DIM:s * SEC + (hh + 1) * HEAD_DIM] = gu[rows, cols] * z


def _sgu(proj, sgu_norm, sgu_w_bf, sgu_bt):
    t = proj.shape[1]
    tm = 512
    return pl.pallas_call(
        functools.partial(_sgu_kernel, tm=tm),
        grid=(t // tm,),
        in_specs=[pl.BlockSpec((2, tm, SEC), lambda i: (SEC_U // 2, i, 0)),
                  pl.BlockSpec((2, tm, SEC), lambda i: (SEC_V // 2, i, 0)),
                  pl.BlockSpec((1, GROUP_WIDTH), lambda i: (0, 0)),
                  pl.BlockSpec((N_GROUP_HEADS, CHUNK, CHUNK), lambda i: (0, 0, 0)),
                  pl.BlockSpec((CHUNK, N_GROUP_HEADS), lambda i: (0, 0))],
        out_specs=pl.BlockSpec((tm, GROUP_WIDTH), lambda i: (i, 0)),
        out_shape=jax.ShapeDtypeStruct((t, GROUP_WIDTH), F32),
        compiler_params=_cp(("arbitrary",)),
        name="sgu",
    )(proj, proj, sgu_norm, sgu_w_bf, sgu_bt)


ATT_TQ = 128
PAIR = 2 * HEAD_DIM


def _rope(x, cos_t, sin_t):
    n = x.shape[-1]
    half = ROPE_DIM // 2
    lane = lax.broadcasted_iota(jnp.int32, x.shape, 1) % HEAD_DIM
    swapped = jnp.where(lane < half, pltpu.roll(x, n - half, 1), pltpu.roll(x, half, 1))
    return x * cos_t + swapped * sin_t


def _pair_tile(q, k, v, valid):
    lo = lax.broadcasted_iota(jnp.int32, q.shape, 1) < HEAD_DIM
    zero = jnp.zeros_like(q)
    parts = []
    for qh in (jnp.where(lo, q, zero), jnp.where(lo, zero, q)):
        s = jnp.where(valid, _nt(qh, k), NEG_INF)
        m = jnp.max(s, axis=-1, keepdims=True)
        p = jnp.exp(s - m)
        l = jnp.sum(p, axis=-1, keepdims=True)
        parts.append((jnp.dot(p.astype(BF16), v, preferred_element_type=F32), m, l))
    (oa, ma, la), (ob, mb, lb) = parts
    shp = oa.shape
    return (jnp.where(lo, oa, ob),
            jnp.where(lo, jnp.broadcast_to(ma, shp), jnp.broadcast_to(mb, shp)),
            jnp.where(lo, jnp.broadcast_to(la, shp), jnp.broadcast_to(lb, shp)))


def _band_valid(tq, tk, w, delta):
    d = (lax.broadcasted_iota(jnp.int32, (tq, tk), 1) - lax.broadcasted_iota(jnp.int32, (tq, tk), 0)) + delta
    return (d <= w) & (d >= -w)


def _swa_kernel(q_ref, kv_ref, cos_ref, sin_ref, sink_ref, o_ref, qs_ref, ks_ref, vs_ref, *, seq):
    w, tq = SWA_HALF_WINDOW, ATT_TQ
    tk = tq + 2 * w
    cos, sin = cos_ref[...], sin_ref[...]
    for s in range(2):
        qs_ref[s] = (_rope(q_ref[s], cos, sin) * (HEAD_DIM ** -0.5)).astype(BF16)
    kk = _rope(kv_ref[:, 0:PAIR], cos[:, 0:PAIR], sin[:, 0:PAIR])
    vv = kv_ref[:, PAIR:2 * PAIR]
    lo = lax.broadcasted_iota(jnp.int32, kk.shape, 1) < HEAD_DIM
    kk_sw, vv_sw = pltpu.roll(kk, HEAD_DIM, 1), pltpu.roll(vv, HEAD_DIM, 1)
    ks_ref[0] = jnp.where(lo, kk, kk_sw).astype(BF16)
    ks_ref[1] = jnp.where(lo, kk_sw, kk).astype(BF16)
    vs_ref[0] = jnp.where(lo, vv, vv_sw).astype(BF16)
    vs_ref[1] = jnp.where(lo, vv_sw, vv).astype(BF16)

    def tile(i, carry):
        q0 = pl.multiple_of(i * tq, tq)
        k0 = pl.multiple_of(jnp.clip(q0 - w, 0, seq - tk), tq)
        valid = _band_valid(tq, tk, w, k0 - q0)
        for pair in range(4):
            g = pair // 2
            lanes = slice((pair % 2) * PAIR, (pair % 2 + 1) * PAIR)
            o, m, l = _pair_tile(qs_ref[g, pl.ds(q0, tq), lanes], ks_ref[g, pl.ds(k0, tk), :],
                                 vs_ref[g, pl.ds(k0, tk), :], valid)
            sink = sink_ref[:, pair * PAIR:(pair + 1) * PAIR]
            m2 = jnp.maximum(m, sink)
            a = jnp.exp(m - m2)
            o_ref[pl.ds(q0, tq), pair * PAIR:(pair + 1) * PAIR] = o * a / (l * a + jnp.exp(sink - m2))
        return carry

    lax.fori_loop(0, seq // tq, tile, 0)


def _swa(proj, cos_q, sin_q, sink_row, batch, seq):
    t = batch * seq
    tab = pl.BlockSpec((None, seq, SEC), lambda b: (b, 0, 0))
    return pl.pallas_call(
        functools.partial(_swa_kernel, seq=seq),
        grid=(batch,),
        in_specs=[pl.BlockSpec((2, seq, SEC), lambda b: (SEC_QC // 2, b, 0)),
                  pl.BlockSpec((None, seq, SEC), lambda b: (SEC_KVC, b, 0)),
                  tab, tab,
                  pl.BlockSpec((1, GROUP_WIDTH), lambda b: (0, 0))],
        out_specs=pl.BlockSpec((seq, GROUP_WIDTH), lambda b: (b, 0)),
        out_shape=jax.ShapeDtypeStruct((t, GROUP_WIDTH), F32),
        scratch_shapes=[pltpu.VMEM((2, seq, SEC), BF16),
                        pltpu.VMEM((2, seq, PAIR), BF16),
                        pltpu.VMEM((2, seq, PAIR), BF16)],
        compiler_params=_cp(("arbitrary",)),
        name="swa",
    )(proj, proj, cos_q, sin_q, sink_row)


def _dil_kernel(q_ref, k_ref, v_ref, cos_ref, sin_ref, o_ref,
                qf_ref, kf_ref, vf_ref, m_ref, l_ref, acc_ref, qd_ref, kd_ref, vd_ref, *, seq):
    cos, sin = cos_ref[...], sin_ref[...]
    q = _rope(q_ref[...], cos, sin) * (HEAD_DIM ** -0.5)
    k = _rope(k_ref[...], cos, sin)
    for pair in range(2):
        lanes = slice(pair * PAIR, (pair + 1) * PAIR)
        qf_ref[pair] = q[:, lanes]
        kf_ref[pair] = k[:, lanes]
        vf_ref[pair] = v_ref[:, lanes]
    m_ref[...] = jnp.full(m_ref.shape, NEG_INF, F32)
    l_ref[...] = jnp.zeros(l_ref.shape, F32)
    acc_ref[...] = jnp.zeros(acc_ref.shape, F32)

    for window, dil in DIL_CONFIGS:
        w = window // (2 * dil)
        ls = seq // dil
        tq = min(ATT_TQ, ls)
        tk = min(tq + 2 * w, ls)

        def rows_of(start, size, dil=dil):
            return pl.ds(start, size) if dil == 1 else pl.ds(start, size, stride=dil)

        def residue(r, carry, w=w, ls=ls, tq=tq, tk=tk, dil=dil, rows_of=rows_of):
            for pair in range(2):
                qd_ref[pair, 0:ls, :] = qf_ref[pair, rows_of(r, ls), :].astype(BF16)
                kd_ref[pair, 0:ls, :] = kf_ref[pair, rows_of(r, ls), :].astype(BF16)
                vd_ref[pair, 0:ls, :] = vf_ref[pair, rows_of(r, ls), :].astype(BF16)

            def tile(i, c2):
                q0 = pl.multiple_of(i * tq, tq)
                k0 = pl.multiple_of(jnp.clip(q0 - w, 0, ls - tk), w)
                valid = _band_valid(tq, tk, w, k0 - q0)
                rows = rows_of(r + dil * q0, tq)
                for pair in range(2):
                    o, m, l = _pair_tile(qd_ref[pair, pl.ds(q0, tq), :], kd_ref[pair, pl.ds(k0, tk), :],
                                         vd_ref[pair, pl.ds(k0, tk), :], valid)
                    m_old = m_ref[pair, rows, :]
                    m_new = jnp.maximum(m_old, m)
                    a, b = jnp.exp(m_old - m_new), jnp.exp(m - m_new)
                    acc_ref[pair, rows, :] = acc_ref[pair, rows, :] * a + o * b
                    l_ref[pair, rows, :] = l_ref[pair, rows, :] * a + l * b
                    m_ref[pair, rows, :] = m_new
                return c2

            lax.fori_loop(0, ls // tq, tile, 0)
            return carry

        lax.fori_loop(0, dil, residue, 0)

    for pair in range(2):
        o_ref[:, pair * PAIR:(pair + 1) * PAIR] = acc_ref[pair] / l_ref[pair]


def _dilated(proj, cos_q, sin_q, batch, seq):
    t = batch * seq

    def sec(base):
        return pl.BlockSpec((None, seq, SEC), lambda b, g: (base + g, b, 0))

    tab = pl.BlockSpec((None, seq, SEC), lambda b, g: (b, 0, 0))
    return pl.pallas_call(
        functools.partial(_dil_kernel, seq=seq),
        grid=(batch, 2),
        in_specs=[sec(SEC_QD), sec(SEC_KD), sec(SEC_VD), tab, tab],
        out_specs=pl.BlockSpec((seq, SEC), lambda b, g: (b, g)),
        out_shape=jax.ShapeDtypeStruct((t, GROUP_WIDTH), F32),
        scratch_shapes=[pltpu.VMEM((2, seq, PAIR), F32)] * 6 + [pltpu.VMEM((2, seq, PAIR), BF16)] * 3,
        compiler_params=_cp(("arbitrary", "arbitrary")),
        name="dilated",
    )(proj, proj, proj, cos_q, sin_q)


def _gnorm(y, g):
    return y * lax.rsqrt(jnp.mean(y * y, axis=-1, keepdims=True) + EPS) * g


def _out_kernel(ya_ref, yb_ref, yc_ref, yd_ref, gn_ref, w_ref, x_ref, mod_ref, nf_ref, rcat_ref,
                xn_ref, h_ref, aff_ref):
    ys = (ya_ref, yb_ref, yc_ref, yd_ref)
    gw = GROUP_WIDTH
    acc = None
    for g in range(4):
        yn = _gnorm(ys[g][...], gn_ref[:, g * gw:(g + 1) * gw]).astype(BF16)
        part = jnp.dot(yn, w_ref[g * gw:(g + 1) * gw, :], preferred_element_type=F32)
        acc = part if acc is None else acc + part
    xn = x_ref[...] + mod_ref[2:3, :] * acc
    xn_ref[...] = xn
    h = _gnorm(xn, nf_ref[...]) * (1.0 + mod_ref[4:5, :]) + mod_ref[3:4, :]
    h_ref[...] = h
    h_hi = h.astype(BF16)
    h_lo = (h - h_hi.astype(F32)).astype(BF16)
    a = _nt(rcat_ref[...], h_hi)
    b = _nt(rcat_ref[0:N_EXPERTS, :], h_lo)
    logits = a[0:N_EXPERTS] + a[N_EXPERTS:2 * N_EXPERTS] + b
    mxl = jnp.max(logits, axis=0, keepdims=True)
    ex = jnp.exp(logits - mxl)
    aff_ref[...] = ex / jnp.sum(ex, axis=0, keepdims=True)


def _out_proj(ya, yb, yc, yd, gn, w_bf, x2, mod_l, nf, rcat, seq):
    t, d = x2.shape
    tm = 256
    tpb = seq // tm
    ysp = pl.BlockSpec((tm, GROUP_WIDTH), lambda i: (i, 0))
    return pl.pallas_call(
        _out_kernel,
        grid=(t // tm,),
        in_specs=[ysp] * 4 + [
            pl.BlockSpec((1, d), lambda i: (0, 0)),
            pl.BlockSpec((d, d), lambda i: (0, 0)),
            pl.BlockSpec((tm, d), lambda i: (i, 0)),
            pl.BlockSpec((None, 6, d), lambda i: (i // tpb, 0, 0)),
            pl.BlockSpec((1, d), lambda i: (0, 0)),
            pl.BlockSpec((2 * N_EXPERTS, d), lambda i: (0, 0))],
        out_specs=[pl.BlockSpec((tm, d), lambda i: (i, 0)),
                   pl.BlockSpec((tm, d), lambda i: (i, 0)),
                   pl.BlockSpec((N_EXPERTS, tm), lambda i: (0, i))],
        out_shape=[jax.ShapeDtypeStruct((t, d), F32),
                   jax.ShapeDtypeStruct((t, d), F32),
                   jax.ShapeDtypeStruct((N_EXPERTS, t), F32)],
        compiler_params=_cp(("arbitrary",)),
        name="out_proj",
    )(ya, yb, yc, yd, gn, w_bf, x2, mod_l, nf, rcat)


LANES = 128


def _route_kernel(aff_ref, tri_ref, sel_ref, pos_ref, idx_ref, gate_ref, *, cap, seq):
    aff = aff_ref[...]

    def as_f32(bits):
        return lax.bitcast_convert_type(bits, F32)

    def step(k, thr):
        cand = thr | lax.shift_left(jnp.int32(1), 30 - k)
        cnt = jnp.sum((aff >= as_f32(cand)).astype(jnp.int32), axis=1, keepdims=True)
        return jnp.where(cnt >= cap, cand, thr)

    thr = lax.fori_loop(0, 31, step, jnp.zeros((aff.shape[0], 1), jnp.int32))
    gt = aff >= as_f32(thr + 1)
    eq = (aff >= as_f32(thr)) & jnp.logical_not(gt)
    need = (cap - jnp.sum(gt.astype(jnp.int32), axis=1, keepdims=True)).astype(F32)
    eq_rank = jnp.dot(eq.astype(BF16), tri_ref[...], preferred_element_type=F32)
    sel = gt | (eq & (eq_rank < need))
    sel_ref[...] = sel.astype(F32)
    pos_ref[...] = jnp.dot(sel.astype(BF16), tri_ref[...], preferred_element_type=F32)

    tok = (lax.broadcasted_iota(jnp.int32, (1, seq), 1) + pl.program_id(0) * seq).astype(F32)
    slot = lax.broadcasted_iota(jnp.int32, (cap, seq), 0).astype(F32)

    def per_expert(e, carry):
        oh = (pos_ref[pl.ds(e, 1), :] == slot) & (sel_ref[pl.ds(e, 1), :] > 0.5)
        idx_col = jnp.sum(jnp.where(oh, tok, 0.0), axis=1, keepdims=True)
        gate_col = jnp.sum(jnp.where(oh, aff_ref[pl.ds(e, 1), :], 0.0), axis=1, keepdims=True)
        idx_ref[e] = jnp.broadcast_to(idx_col, (cap, LANES)).astype(jnp.int32)
        gate_ref[e] = jnp.broadcast_to(gate_col, (cap, LANES))
        return carry

    lax.fori_loop(0, aff.shape[0], per_expert, 0)


def _route(aff_t, tri, batch, seq, cap):
    e = aff_t.shape[0]
    blk = pl.BlockSpec((e, seq), lambda b: (0, b))
    colblk = pl.BlockSpec((e, cap, LANES), lambda b: (0, b, 0))
    shp = jax.ShapeDtypeStruct(aff_t.shape, F32)
    return pl.pallas_call(
        functools.partial(_route_kernel, cap=cap, seq=seq),
        grid=(batch,),
        in_specs=[blk, pl.BlockSpec((seq, seq), lambda b: (0, 0))],
        out_specs=[blk, blk, colblk, colblk],
        out_shape=[shp, shp,
                   jax.ShapeDtypeStruct((e, batch * cap, LANES), jnp.int32),
                   jax.ShapeDtypeStruct((e, batch * cap, LANES), F32)],
        compiler_params=_cp(("arbitrary",)),
        name="route",
    )(aff_t, tri)


def _onehot(sel_row, pos_row, cap):
    r = lax.broadcasted_iota(jnp.int32, (cap, sel_row.shape[1]), 0).astype(F32)
    return (pos_row == r) & (sel_row > 0.5)


def _moe_kernel(idx_ref, h_hbm, gate_ref, wg_ref, wu_ref, wd_ref, o_ref,
                xg_ref, xe_ref, acc_ref, sem, *, rows):
    e = pl.program_id(0)
    f = pl.program_id(1)

    def gather(expert):
        def body(r, carry):
            tok = idx_ref[expert * rows + r]
            pltpu.make_async_copy(h_hbm.at[pl.ds(tok, 1), :], xg_ref.at[pl.ds(r, 1), :], sem.at[0]).start()
            return carry
        lax.fori_loop(0, rows, body, 0, unroll=8)

    @pl.when((e == 0) & (f == 0))
    def _():
        gather(0)

    @pl.when(f == 0)
    def _():
        pltpu.make_async_copy(h_hbm.at[pl.ds(0, rows), :], xg_ref, sem.at[0]).wait()
        xe_ref[...] = xg_ref[...].astype(BF16)
        acc_ref[...] = jnp.zeros_like(acc_ref)

        @pl.when(e + 1 < pl.num_programs(0))
        def _():
            gather(e + 1)

    xe = xe_ref[...]
    a = jnp.dot(xe, wg_ref[...].astype(BF16), preferred_element_type=F32)
    u = jnp.dot(xe, wu_ref[...].astype(BF16), preferred_element_type=F32)
    act = (a * jax.nn.sigmoid(a) * u).astype(BF16)
    acc_ref[...] += jnp.dot(act, wd_ref[...].astype(BF16), preferred_element_type=F32)

    @pl.when(f == pl.num_programs(1) - 1)
    def _():
        o_ref[...] = (acc_ref[...] * gate_ref[:, 0:1]).astype(BF16)


def _moe(idx_flat, gate_col, h2, wg, wu, wd, layer):
    t, d = h2.shape
    _, ne, _, ff = wg.shape
    rows = gate_col.shape[1]
    tf = 256
    grid_spec = pltpu.PrefetchScalarGridSpec(
        num_scalar_prefetch=1,
        grid=(ne, ff // tf),
        in_specs=[pl.BlockSpec(memory_space=pl.ANY),
                  pl.BlockSpec((None, rows, LANES), lambda e, f, idx: (e, 0, 0)),
                  pl.BlockSpec((None, None, d, tf), lambda e, f, idx: (layer, e, 0, f)),
                  pl.BlockSpec((None, None, d, tf), lambda e, f, idx: (layer, e, 0, f)),
                  pl.BlockSpec((None, None, tf, d), lambda e, f, idx: (layer, e, f, 0))],
        out_specs=pl.BlockSpec((None, rows, d), lambda e, f, idx: (e, 0, 0)),
        scratch_shapes=[pltpu.VMEM((rows, d), F32), pltpu.VMEM((rows, d), BF16), pltpu.VMEM((rows, d), F32),
                        pltpu.SemaphoreType.DMA((1,))],
    )
    return pl.pallas_call(
        functools.partial(_moe_kernel, rows=rows),
        grid_spec=grid_spec,
        out_shape=jax.ShapeDtypeStruct((ne, rows, d), BF16),
        compiler_params=_cp(("arbitrary", "arbitrary")),
        name="moe",
    )(idx_flat, h2, gate_col, wg, wu, wd)


def _scatter_kernel(sel_ref, pos_ref, o_ref, xn_ref, mod_ref, fn_ref, out_ref, acc_ref, *, cap, final_norm):
    e = pl.program_id(1)

    @pl.when(e == 0)
    def _():
        acc_ref[...] = jnp.zeros_like(acc_ref)

    oh = _onehot(sel_ref[pl.ds(e, 1), :], pos_ref[pl.ds(e, 1), :], cap).astype(BF16)
    acc_ref[...] += lax.dot_general(oh, o_ref[...], (((0,), (0,)), ((), ())), preferred_element_type=F32)

    @pl.when(e == pl.num_programs(1) - 1)
    def _():
        x = xn_ref[...] + mod_ref[5:6, :] * acc_ref[...]
        if final_norm:
            x = _gnorm(x, fn_ref[...])
        out_ref[...] = x


def _scatter(sel, pos, o, xn, mod_l, fn, seq, cap, final_norm):
    t, d = xn.shape
    ne = sel.shape[0]
    tm = 512
    tpb = seq // tm
    rowblk = pl.BlockSpec((ne, tm), lambda i, e: (0, i))
    return pl.pallas_call(
        functools.partial(_scatter_kernel, cap=cap, final_norm=final_norm),
        grid=(t // tm, ne),
        in_specs=[rowblk, rowblk,
                  pl.BlockSpec((None, cap, d), lambda i, e: (e, i // tpb, 0)),
                  pl.BlockSpec((tm, d), lambda i, e: (i, 0)),
                  pl.BlockSpec((None, 6, d), lambda i, e: (i // tpb, 0, 0)),
                  pl.BlockSpec((1, d), lambda i, e: (0, 0))],
        out_specs=pl.BlockSpec((tm, d), lambda i, e: (i, 0)),
        out_shape=jax.ShapeDtypeStruct((t, d), F32),
        scratch_shapes=[pltpu.VMEM((tm, d), F32)],
        compiler_params=_cp(("arbitrary", "arbitrary")),
        name="scatter",
    )(sel, pos, o, xn, mod_l, fn)


def _rope_tables(positions):
    b, s = positions.shape
    inv = jnp.power(jnp.float32(ROPE_THETA), -jnp.arange(0, ROPE_DIM, 2, dtype=F32) / ROPE_DIM)
    ang = positions.astype(F32)[..., None] * inv
    cos, sin = jnp.cos(ang), jnp.sin(ang)
    rest = HEAD_DIM - ROPE_DIM
    c64 = jnp.concatenate([cos, cos, jnp.ones((b, s, rest), F32)], axis=-1)
    s64 = jnp.concatenate([-sin, sin, jnp.zeros((b, s, rest), F32)], axis=-1)
    return jnp.tile(c64, (1, 1, 4)), jnp.tile(s64, (1, 1, 4))


def _dft_tables(n):
    k = jnp.arange(n, dtype=jnp.int32)
    ang = ((k[:, None] * k[None, :]) % n).astype(F32) * (2.0 * math.pi / n)
    return jnp.cos(ang), jnp.sin(ang)


def _block_diag(blocks):
    n, a, _ = blocks.shape
    eye = jnp.eye(n, dtype=blocks.dtype)
    return (eye[:, None, :, None] * blocks[:, :, None, :]).reshape(n * a, n * a)


def kernel(x, c, positions, ada_w, ada_b, norm_mix, w_in, fnet_w, sgu_norm, sgu_w, sgu_b, swa_sink,
           group_norm, w_out, norm_ffn, router_w, exp_w_gate, exp_w_up, exp_w_down, final_norm):
    batch, seq, d = x.shape
    depth = ada_w.shape[0]
    t = batch * seq
    cap = max(1, EC_FACTOR * seq // N_EXPERTS)

    rows = 16
    c_pad = jnp.zeros((rows, d), F32).at[:batch].set(c)
    mod = _ada(c_pad, ada_w, ada_b)[:, :batch].reshape(depth, batch, 6, d)

    cos_q, sin_q = _rope_tables(positions)
    dft_c, dft_s = _dft_tables(seq)
    dft_c, dft_s = dft_c.astype(BF16), dft_s.astype(BF16)
    cc, sc = _dft_tables(HEAD_DIM)
    bdc = _block_diag(jnp.broadcast_to(cc, (4, HEAD_DIM, HEAD_DIM)))
    bds = _block_diag(jnp.broadcast_to(sc, (4, HEAD_DIM, HEAD_DIM)))
    tri = (jnp.arange(seq)[:, None] < jnp.arange(seq)[None, :]).astype(BF16)

    x2 = x.reshape(t, d)
    for l in range(depth):
        mod_l = mod[l]
        proj = _in_proj(x2, mod_l, norm_mix[l].reshape(1, d), w_in[l].astype(BF16), seq)

        bdw = jnp.stack([_block_diag(fnet_w[l, 0:4]), _block_diag(fnet_w[l, 4:8])])
        ya = _fourier(proj, bdc, bds, bdw, dft_c, dft_s, batch, seq).reshape(t, GROUP_WIDTH)
        yb = _sgu(proj, sgu_norm[l].reshape(1, GROUP_WIDTH), sgu_w[l].astype(BF16), sgu_b[l].T)
        sink_row = jnp.repeat(swa_sink[l], HEAD_DIM).reshape(1, GROUP_WIDTH)
        yc = _swa(proj, cos_q, sin_q, sink_row, batch, seq)
        yd = _dilated(proj, cos_q, sin_q, batch, seq)

        r_t = router_w[l].T
        r_hi = r_t.astype(BF16)
        r_lo = (r_t - r_hi.astype(F32)).astype(BF16)
        rcat = jnp.concatenate([r_hi, r_lo], axis=0)
        xn, h2, aff_t = _out_proj(ya, yb, yc, yd, group_norm[l].reshape(1, d), w_out[l].astype(BF16),
                                  x2, mod_l, norm_ffn[l].reshape(1, d), rcat, seq)
        sel, pos, idx, gate_col = _route(aff_t, tri, batch, seq, cap)
        o = _moe(idx[:, :, 0].reshape(-1), gate_col, h2, exp_w_gate, exp_w_up, exp_w_down, l)
        x2 = _scatter(sel, pos, o, xn, mod_l, final_norm.reshape(1, d), seq, cap, l == depth - 1)
    return x2.reshape(batch, seq, d)
```

```python
import functools
import math

import jax
import jax.numpy as jnp
from jax import lax
from jax.experimental import pallas as pl
from jax.experimental.pallas import tpu as pltpu

F32 = jnp.float32
BF16 = jnp.bfloat16

HEAD_DIM = 64
N_GROUP_HEADS = 8
GROUP_WIDTH = 512
CHUNK = 128
SWA_HALF_WINDOW = 128
DIL_CONFIGS = ((128, 1), (512, 4), (2048, 16))
ROPE_THETA = 500000.0
ROPE_DIM = 16
N_EXPERTS = 16
EC_FACTOR = 2
EPS = 1e-6
NEG_INF = -1e30

SEC = 256
SEC_XA, SEC_U, SEC_V, SEC_QC, SEC_KVC, SEC_QD, SEC_KD, SEC_VD = 0, 2, 4, 6, 8, 9, 11, 13
N_SEC = 15

VMEM_LIMIT = 56 * 1024 * 1024


def _cp(sem, vmem=None):
    return pltpu.CompilerParams(dimension_semantics=sem, vmem_limit_bytes=vmem or VMEM_LIMIT)


def _nt(a, b):
    return lax.dot_general(a, b, (((1,), (1,)), ((), ())), preferred_element_type=F32)


def _ada_kernel(c_ref, w_ref, b_ref, o_ref):
    c = c_ref[...]
    s = (c * jax.nn.sigmoid(c)).astype(BF16)
    o_ref[0] = jnp.dot(s, w_ref[0].astype(BF16), preferred_element_type=F32) + b_ref[0]


def _ada(c_pad, ada_w, ada_b):
    n_layers, d, n6 = ada_w.shape
    rows = c_pad.shape[0]
    tn = 1024
    return pl.pallas_call(
        _ada_kernel,
        grid=(n_layers, n6 // tn),
        in_specs=[pl.BlockSpec((rows, d), lambda l, j: (0, 0)),
                  pl.BlockSpec((1, d, tn), lambda l, j: (l, 0, j)),
                  pl.BlockSpec((1, 1, tn), lambda l, j: (l, 0, j))],
        out_specs=pl.BlockSpec((1, rows, tn), lambda l, j: (l, 0, j)),
        out_shape=jax.ShapeDtypeStruct((n_layers, rows, n6), F32),
        compiler_params=_cp(("arbitrary", "arbitrary")),
        name="ada",
    )(c_pad, ada_w, ada_b.reshape(n_layers, 1, n6))


NORM_ROWS = 16
IN_SECS = 3


def _in_kernel(x_ref, mod_ref, g_ref, w_ref, o_ref, h_ref):
    @pl.when(pl.program_id(1) == 0)
    def _():
        gain = g_ref[...] * (1.0 + mod_ref[1:2, :])
        shift = mod_ref[0:1, :]

        def chunk(c, carry):
            rows = pl.ds(pl.multiple_of(c * NORM_ROWS, NORM_ROWS), NORM_ROWS)
            x = x_ref[rows, :]
            r = lax.rsqrt(jnp.mean(x * x, axis=-1, keepdims=True) + EPS)
            h_ref[rows, :] = (x * r * gain + shift).astype(BF16)
            return carry

        lax.fori_loop(0, x_ref.shape[0] // NORM_ROWS, chunk, 0, unroll=8)

    res = jnp.dot(h_ref[...], w_ref[...], preferred_element_type=F32)
    for s in range(IN_SECS):
        o_ref[s] = res[:, s * SEC:(s + 1) * SEC]


def _in_proj(x2, mod_l, norm_g, w_bf, seq):
    t, d = x2.shape
    tm = 1024
    tpb = seq // tm
    return pl.pallas_call(
        _in_kernel,
        grid=(t // tm, N_SEC // IN_SECS),
        in_specs=[pl.BlockSpec((tm, d), lambda i, j: (i, 0)),
                  pl.BlockSpec((None, 6, d), lambda i, j: (i // tpb, 0, 0)),
                  pl.BlockSpec((1, d), lambda i, j: (0, 0)),
                  pl.BlockSpec((d, IN_SECS * SEC), lambda i, j: (0, j))],
        out_specs=pl.BlockSpec((IN_SECS, tm, SEC), lambda i, j: (j, i, 0)),
        out_shape=jax.ShapeDtypeStruct((N_SEC, t, SEC), F32),
        scratch_shapes=[pltpu.VMEM((tm, d), BF16)],
        compiler_params=_cp(("arbitrary", "arbitrary")),
        name="in_proj",
    )(x2, mod_l, norm_g, w_bf)


def _f1_kernel(x_ref, bdc_ref, bds_ref, bdw_ref, z_ref, m_ref, *, scale):
    @pl.when(pl.program_id(1) == 0)
    def _():
        w = bdw_ref[...]
        mc = jnp.dot(bdc_ref[...], w, preferred_element_type=F32, precision=lax.Precision.HIGHEST)
        msn = jnp.dot(bds_ref[...], w, preferred_element_type=F32, precision=lax.Precision.HIGHEST)
        m_ref[:, 0:SEC] = (mc * scale).astype(BF16)
        m_ref[:, SEC:2 * SEC] = (msn * scale).astype(BF16)

    z_ref[...] = jnp.dot(x_ref[...].astype(BF16), m_ref[...], preferred_element_type=F32).astype(BF16)


def _f2_kernel(c_ref, s_ref, z_ref, o_ref):
    z = z_ref[...]
    o_ref[...] = (jnp.dot(c_ref[...], z[:, 0:SEC], preferred_element_type=F32)
                  - jnp.dot(s_ref[...], z[:, SEC:2 * SEC], preferred_element_type=F32))


def _fourier(proj, bdc, bds, bdw, dft_c, dft_s, batch, seq):
    t = batch * seq
    tm = 1024
    scale = 1.0 / math.sqrt(seq * HEAD_DIM)
    z = pl.pallas_call(
        functools.partial(_f1_kernel, scale=scale),
        grid=(2, t // tm),
        in_specs=[pl.BlockSpec((None, tm, SEC), lambda s, i: (SEC_XA + s, i, 0)),
                  pl.BlockSpec((SEC, SEC), lambda s, i: (0, 0)),
                  pl.BlockSpec((SEC, SEC), lambda s, i: (0, 0)),
                  pl.BlockSpec((None, SEC, SEC), lambda s, i: (s, 0, 0))],
        out_specs=pl.BlockSpec((None, tm, 2 * SEC), lambda s, i: (s, i, 0)),
        out_shape=jax.ShapeDtypeStruct((2, t, 2 * SEC), BF16),
        scratch_shapes=[pltpu.VMEM((SEC, 2 * SEC), BF16)],
        compiler_params=_cp(("arbitrary", "arbitrary")),
        name="fnet_chan",
    )(proj, bdc, bds, bdw)
    rt = 512
    return pl.pallas_call(
        _f2_kernel,
        grid=(seq // rt, batch, 2),
        in_specs=[pl.BlockSpec((rt, seq), lambda r, b, s: (r, 0)),
                  pl.BlockSpec((rt, seq), lambda r, b, s: (r, 0)),
                  pl.BlockSpec((None, seq, 2 * SEC), lambda r, b, s: (s, b, 0))],
        out_specs=pl.BlockSpec((None, rt, SEC), lambda r, b, s: (b, r, s)),
        out_shape=jax.ShapeDtypeStruct((batch, seq, GROUP_WIDTH), F32),
        compiler_params=_cp(("arbitrary", "arbitrary", "arbitrary")),
        name="fnet_seq",
    )(dft_c, dft_s, z)


def _sgu_kernel(u_ref, v_ref, g_ref, w_ref, b_ref, o_ref, *, tm):
    gv = [jax.nn.gelu(v_ref[s]) for s in range(2)]
    ms = (jnp.sum(gv[0] * gv[0], axis=-1, keepdims=True)
          + jnp.sum(gv[1] * gv[1], axis=-1, keepdims=True)) * (1.0 / GROUP_WIDTH)
    r = lax.rsqrt(ms + EPS)
    vn = [(gv[s] * r * g_ref[:, s * SEC:(s + 1) * SEC]).astype(BF16) for s in range(2)]
    for s in range(2):
        gu = jax.nn.gelu(u_ref[s])
        for c in range(tm // CHUNK):
            rows = slice(c * CHUNK, (c + 1) * CHUNK)
            for hh in range(4):
                h = s * 4 + hh
                cols = slice(hh * HEAD_DIM, (hh + 1) * HEAD_DIM)
                z = jnp.dot(w_ref[h], vn[s][rows, cols], preferred_element_type=F32) + b_ref[:, h:h + 1]
                o_ref[rows, s * SEC + hh * HEAD_DIM:s * SEC + (hh + 1) * HEAD_DIM] = gu[rows, cols] * z


def _sgu(proj, sgu_norm, sgu_w_bf, sgu_bt):
    t = proj.shape[1]
    tm = 512
    return pl.pallas_call(
        functools.partial(_sgu_kernel, tm=tm),
        grid=(t // tm,),
        in_specs=[pl.BlockSpec((2, tm, SEC), lambda i: (SEC_U // 2, i, 0)),
                  pl.BlockSpec((2, tm, SEC), lambda i: (SEC_V // 2, i, 0)),
                  pl.BlockSpec((1, GROUP_WIDTH), lambda i: (0, 0)),
                  pl.BlockSpec((N_GROUP_HEADS, CHUNK, CHUNK), lambda i: (0, 0, 0)),
                  pl.BlockSpec((CHUNK, N_GROUP_HEADS), lambda i: (0, 0))],
        out_specs=pl.BlockSpec((tm, GROUP_WIDTH), lambda i: (i, 0)),
        out_shape=jax.ShapeDtypeStruct((t, GROUP_WIDTH), F32),
        compiler_params=_cp(("arbitrary",)),
        name="sgu",
    )(proj, proj, sgu_norm, sgu_w_bf, sgu_bt)


ATT_TQ = 128
PAIR = 2 * HEAD_DIM


def _rope(x, cos_t, sin_t):
    n = x.shape[-1]
    half = ROPE_DIM // 2
    lane = lax.broadcasted_iota(jnp.int32, x.shape, 1) % HEAD_DIM
    swapped = jnp.where(lane < half, pltpu.roll(x, n - half, 1), pltpu.roll(x, half, 1))
    return x * cos_t + swapped * sin_t


def _lo_half(shape):
    return lax.broadcasted_iota(jnp.int32, shape, 1) < HEAD_DIM


def _stage_scores(q_pairs, k, valid, s_ref, slot, grp):
    tq = q_pairs[0].shape[0]
    lo = _lo_half(q_pairs[0].shape)
    zero = jnp.zeros_like(q_pairs[0])
    blocks = []
    for q in q_pairs:
        blocks += [jnp.where(lo, q, zero), jnp.where(lo, zero, q)]
    s = _nt(jnp.concatenate(blocks, axis=0), k)
    for i in range(len(blocks)):
        s_ref[slot, grp, i * tq:(i + 1) * tq, :] = jnp.where(valid, s[i * tq:(i + 1) * tq], NEG_INF)


def _stage_softmax(s_ref, p_ref, m_ref, l_ref, slot, grp, tq):
    s = s_ref[slot, grp]
    m = jnp.max(s, axis=-1, keepdims=True)
    p = jnp.exp(s - m)
    p_ref[slot, grp] = p.astype(BF16)
    l = jnp.sum(p, axis=-1, keepdims=True)
    n_pairs = s.shape[0] // (2 * tq)
    shp = (tq, PAIR)
    lo = _lo_half(shp)
    for i in range(n_pairs):
        a, b = slice(2 * i * tq, (2 * i + 1) * tq), slice((2 * i + 1) * tq, (2 * i + 2) * tq)
        m_ref[slot, grp * n_pairs + i] = jnp.where(lo, jnp.broadcast_to(m[a], shp), jnp.broadcast_to(m[b], shp))
        l_ref[slot, grp * n_pairs + i] = jnp.where(lo, jnp.broadcast_to(l[a], shp), jnp.broadcast_to(l[b], shp))


def _stage_pv(p_ref, v, slot, grp, tq):
    o = jnp.dot(p_ref[slot, grp], v, preferred_element_type=F32)
    lo = _lo_half((tq, PAIR))
    return [jnp.where(lo, o[2 * i * tq:(2 * i + 1) * tq], o[(2 * i + 1) * tq:(2 * i + 2) * tq])
            for i in range(o.shape[0] // (2 * tq))]


ATT_ITEMS = 4


def _pipeline(n_items, stage_a, stage_b, stage_c):
    def body(jj, carry):
        for stage in (stage_a, stage_b, stage_c):
            for t in range(ATT_ITEMS):
                stage(jj * ATT_ITEMS + t, t)
        return carry

    lax.fori_loop(0, n_items // ATT_ITEMS, body, 0)


def _band_valid(tq, tk, w, delta):
    d = (lax.broadcasted_iota(jnp.int32, (tq, tk), 1) - lax.broadcasted_iota(jnp.int32, (tq, tk), 0)) + delta
    return (d <= w) & (d >= -w)


def _swa_kernel(q_ref, kv_ref, cos_ref, sin_ref, sink_ref, o_ref, qs_ref, ks_ref, vs_ref,
                s_ref, p_ref, m_ref, l_ref, *, seq):
    w, tq = SWA_HALF_WINDOW, ATT_TQ
    tk = tq + 2 * w
    cos, sin = cos_ref[...], sin_ref[...]
    for s in range(2):
        qs_ref[s] = (_rope(q_ref[s], cos, sin) * (HEAD_DIM ** -0.5)).astype(BF16)
    kk = _rope(kv_ref[:, 0:PAIR], cos[:, 0:PAIR], sin[:, 0:PAIR])
    vv = kv_ref[:, PAIR:2 * PAIR]
    lo = lax.broadcasted_iota(jnp.int32, kk.shape, 1) < HEAD_DIM
    kk_sw, vv_sw = pltpu.roll(kk, HEAD_DIM, 1), pltpu.roll(vv, HEAD_DIM, 1)
    ks_ref[0] = jnp.where(lo, kk, kk_sw).astype(BF16)
    ks_ref[1] = jnp.where(lo, kk_sw, kk).astype(BF16)
    vs_ref[0] = jnp.where(lo, vv, vv_sw).astype(BF16)
    vs_ref[1] = jnp.where(lo, vv_sw, vv).astype(BF16)

    def coords(j):
        q0 = pl.multiple_of(j * tq, tq)
        return q0, pl.multiple_of(jnp.clip(q0 - w, 0, seq - tk), tq)

    def scores(j, slot):
        q0, k0 = coords(j)
        valid = _band_valid(tq, tk, w, k0 - q0)
        for g in range(2):
            q_pairs = [qs_ref[g, pl.ds(q0, tq), 0:PAIR], qs_ref[g, pl.ds(q0, tq), PAIR:2 * PAIR]]
            _stage_scores(q_pairs, ks_ref[g, pl.ds(k0, tk), :], valid, s_ref, slot, g)

    def softmax(j, slot):
        for g in range(2):
            _stage_softmax(s_ref, p_ref, m_ref, l_ref, slot, g, tq)

    def output(j, slot):
        q0, k0 = coords(j)
        for g in range(2):
            outs = _stage_pv(p_ref, vs_ref[g, pl.ds(k0, tk), :], slot, g, tq)
            for i, o in enumerate(outs):
                pair = 2 * g + i
                m, l = m_ref[slot, pair], l_ref[slot, pair]
                sink = sink_ref[:, pair * PAIR:(pair + 1) * PAIR]
                m2 = jnp.maximum(m, sink)
                a = jnp.exp(m - m2)
                o_ref[pl.ds(q0, tq), pair * PAIR:(pair + 1) * PAIR] = o * a / (l * a + jnp.exp(sink - m2))

    _pipeline(seq // tq, scores, softmax, output)


def _swa(proj, cos_q, sin_q, sink_row, batch, seq):
    t = batch * seq
    tab = pl.BlockSpec((None, seq, SEC), lambda b: (b, 0, 0))
    return pl.pallas_call(
        functools.partial(_swa_kernel, seq=seq),
        grid=(batch,),
        in_specs=[pl.BlockSpec((2, seq, SEC), lambda b: (SEC_QC // 2, b, 0)),
                  pl.BlockSpec((None, seq, SEC), lambda b: (SEC_KVC, b, 0)),
                  tab, tab,
                  pl.BlockSpec((1, GROUP_WIDTH), lambda b: (0, 0))],
        out_specs=pl.BlockSpec((seq, GROUP_WIDTH), lambda b: (b, 0)),
        out_shape=jax.ShapeDtypeStruct((t, GROUP_WIDTH), F32),
        scratch_shapes=[pltpu.VMEM((2, seq, SEC), BF16),
                        pltpu.VMEM((2, seq, PAIR), BF16),
                        pltpu.VMEM((2, seq, PAIR), BF16),
                        pltpu.VMEM((ATT_ITEMS, 2, 4 * ATT_TQ, ATT_TQ + 2 * SWA_HALF_WINDOW), F32),
                        pltpu.VMEM((ATT_ITEMS, 2, 4 * ATT_TQ, ATT_TQ + 2 * SWA_HALF_WINDOW), BF16),
                        pltpu.VMEM((ATT_ITEMS, 4, ATT_TQ, PAIR), F32),
                        pltpu.VMEM((ATT_ITEMS, 4, ATT_TQ, PAIR), F32)],
        compiler_params=_cp(("arbitrary",)),
        name="swa",
    )(proj, proj, cos_q, sin_q, sink_row)


def _dil_kernel(q_ref, k_ref, v_ref, cos_ref, sin_ref, o_ref,
                qf_ref, kf_ref, vf_ref, m_ref, l_ref, acc_ref, qd_ref, kd_ref, vd_ref,
                s_ref, p_ref, mt_ref, lt_ref, *, seq):
    cos, sin = cos_ref[...], sin_ref[...]
    q = _rope(q_ref[...], cos, sin) * (HEAD_DIM ** -0.5)
    k = _rope(k_ref[...], cos, sin)
    for pair in range(2):
        lanes = slice(pair * PAIR, (pair + 1) * PAIR)
        qf_ref[pair] = q[:, lanes]
        kf_ref[pair] = k[:, lanes]
        vf_ref[pair] = v_ref[:, lanes]
    m_ref[...] = jnp.full(m_ref.shape, NEG_INF, F32)
    l_ref[...] = jnp.zeros(l_ref.shape, F32)
    acc_ref[...] = jnp.zeros(acc_ref.shape, F32)

    for window, dil in DIL_CONFIGS:
        w = window // (2 * dil)
        ls = seq // dil
        tq = min(ATT_TQ, ls)
        tk = min(tq + 2 * w, ls)

        def rows_of(start, size, dil=dil):
            return pl.ds(start, size) if dil == 1 else pl.ds(start, size, stride=dil)

        def permute(r, carry, ls=ls, rows_of=rows_of):
            dst = pl.ds(pl.multiple_of(r * ls, ls), ls)
            for pair in range(2):
                qd_ref[pair, dst, :] = qf_ref[pair, rows_of(r, ls), :].astype(BF16)
                kd_ref[pair, dst, :] = kf_ref[pair, rows_of(r, ls), :].astype(BF16)
                vd_ref[pair, dst, :] = vf_ref[pair, rows_of(r, ls), :].astype(BF16)
            return carry

        lax.fori_loop(0, dil, permute, 0)
        tiles_per_res = ls // tq

        def coords(j, w=w, ls=ls, tq=tq, tk=tk, tpr=tiles_per_res):
            r = j // tpr
            q0 = (j % tpr) * tq
            k0 = jnp.clip(q0 - w, 0, ls - tk)
            return r, q0, pl.multiple_of(r * ls + k0, w), k0 - q0

        def scores(j, slot, w=w, tq=tq, tk=tk, coords=coords):
            _, _, krow, delta = coords(j)
            valid = _band_valid(tq, tk, w, delta)
            qrow = pl.multiple_of(j * tq, tq)
            for pair in range(2):
                _stage_scores([qd_ref[pair, pl.ds(qrow, tq), :]], kd_ref[pair, pl.ds(krow, tk), :], valid,
                              s_ref.at[:, :, :, 0:tk], slot, pair)

        def softmax(j, slot, tq=tq, tk=tk):
            for pair in range(2):
                _stage_softmax(s_ref.at[:, :, :, 0:tk], p_ref.at[:, :, :, 0:tk], mt_ref, lt_ref, slot, pair, tq)

        def merge(j, slot, tq=tq, tk=tk, dil=dil, coords=coords, rows_of=rows_of):
            r, q0, krow, _ = coords(j)
            rows = rows_of(r + dil * q0, tq)
            for pair in range(2):
                (o,) = _stage_pv(p_ref.at[:, :, :, 0:tk], vd_ref[pair, pl.ds(krow, tk), :], slot, pair, tq)
                m, l = mt_ref[slot, pair], lt_ref[slot, pair]
                m_old = m_ref[pair, rows, :]
                m_new = jnp.maximum(m_old, m)
                a, b = jnp.exp(m_old - m_new), jnp.exp(m - m_new)
                acc_ref[pair, rows, :] = acc_ref[pair, rows, :] * a + o * b
                l_ref[pair, rows, :] = l_ref[pair, rows, :] * a + l * b
                m_ref[pair, rows, :] = m_new

        _pipeline(seq // tq, scores, softmax, merge)

    for pair in range(2):
        o_ref[:, pair * PAIR:(pair + 1) * PAIR] = acc_ref[pair] / l_ref[pair]


def _dilated(proj, cos_q, sin_q, batch, seq):
    t = batch * seq

    def sec(base):
        return pl.BlockSpec((None, seq, SEC), lambda b, g: (base + g, b, 0))

    tab = pl.BlockSpec((None, seq, SEC), lambda b, g: (b, 0, 0))
    return pl.pallas_call(
        functools.partial(_dil_kernel, seq=seq),
        grid=(batch, 2),
        in_specs=[sec(SEC_QD), sec(SEC_KD), sec(SEC_VD), tab, tab],
        out_specs=pl.BlockSpec((seq, SEC), lambda b, g: (b, g)),
        out_shape=jax.ShapeDtypeStruct((t, GROUP_WIDTH), F32),
        scratch_shapes=[pltpu.VMEM((2, seq, PAIR), F32)] * 6 + [pltpu.VMEM((2, seq, PAIR), BF16)] * 3 + [
            pltpu.VMEM((ATT_ITEMS, 2, 2 * ATT_TQ, 2 * ATT_TQ), F32),
            pltpu.VMEM((ATT_ITEMS, 2, 2 * ATT_TQ, 2 * ATT_TQ), BF16),
            pltpu.VMEM((ATT_ITEMS, 2, ATT_TQ, PAIR), F32),
            pltpu.VMEM((ATT_ITEMS, 2, ATT_TQ, PAIR), F32)],
        compiler_params=_cp(("arbitrary", "arbitrary")),
        name="dilated",
    )(proj, proj, proj, cos_q, sin_q)


def _gnorm(y, g):
    return y * lax.rsqrt(jnp.mean(y * y, axis=-1, keepdims=True) + EPS) * g


def _out_kernel(ya_ref, yb_ref, yc_ref, yd_ref, gn_ref, w_ref, x_ref, mod_ref, nf_ref, rcat_ref,
                xn_ref, h_ref, aff_ref):
    ys = (ya_ref, yb_ref, yc_ref, yd_ref)
    gw = GROUP_WIDTH
    acc = None
    for g in range(4):
        yn = _gnorm(ys[g][...], gn_ref[:, g * gw:(g + 1) * gw]).astype(BF16)
        part = jnp.dot(yn, w_ref[g * gw:(g + 1) * gw, :], preferred_element_type=F32)
        acc = part if acc is None else acc + part
    xn = x_ref[...] + mod_ref[2:3, :] * acc
    xn_ref[...] = xn
    h = _gnorm(xn, nf_ref[...]) * (1.0 + mod_ref[4:5, :]) + mod_ref[3:4, :]
    h_ref[...] = h
    h_hi = h.astype(BF16)
    h_lo = (h - h_hi.astype(F32)).astype(BF16)
    a = _nt(rcat_ref[...], h_hi)
    b = _nt(rcat_ref[0:N_EXPERTS, :], h_lo)
    logits = a[0:N_EXPERTS] + a[N_EXPERTS:2 * N_EXPERTS] + b
    mxl = jnp.max(logits, axis=0, keepdims=True)
    ex = jnp.exp(logits - mxl)
    aff_ref[...] = ex / jnp.sum(ex, axis=0, keepdims=True)


def _out_proj(ya, yb, yc, yd, gn, w_bf, x2, mod_l, nf, rcat, seq):
    t, d = x2.shape
    tm = 256
    tpb = seq // tm
    ysp = pl.BlockSpec((tm, GROUP_WIDTH), lambda i: (i, 0))
    return pl.pallas_call(
        _out_kernel,
        grid=(t // tm,),
        in_specs=[ysp] * 4 + [
            pl.BlockSpec((1, d), lambda i: (0, 0)),
            pl.BlockSpec((d, d), lambda i: (0, 0)),
            pl.BlockSpec((tm, d), lambda i: (i, 0)),
            pl.BlockSpec((None, 6, d), lambda i: (i // tpb, 0, 0)),
            pl.BlockSpec((1, d), lambda i: (0, 0)),
            pl.BlockSpec((2 * N_EXPERTS, d), lambda i: (0, 0))],
        out_specs=[pl.BlockSpec((tm, d), lambda i: (i, 0)),
                   pl.BlockSpec((tm, d), lambda i: (i, 0)),
                   pl.BlockSpec((N_EXPERTS, tm), lambda i: (0, i))],
        out_shape=[jax.ShapeDtypeStruct((t, d), F32),
                   jax.ShapeDtypeStruct((t, d), F32),
                   jax.ShapeDtypeStruct((N_EXPERTS, t), F32)],
        compiler_params=_cp(("arbitrary",)),
        name="out_proj",
    )(ya, yb, yc, yd, gn, w_bf, x2, mod_l, nf, rcat)


LANES = 128


def _route_kernel(aff_ref, tri_ref, sel_ref, pos_ref, idx_ref, gate_ref, *, cap, seq):
    aff = aff_ref[...]

    def as_f32(bits):
        return lax.bitcast_convert_type(bits, F32)

    def step(k, thr):
        cand = thr | lax.shift_left(jnp.int32(1), 30 - k)
        cnt = jnp.sum((aff >= as_f32(cand)).astype(jnp.int32), axis=1, keepdims=True)
        return jnp.where(cnt >= cap, cand, thr)

    thr = lax.fori_loop(0, 31, step, jnp.zeros((aff.shape[0], 1), jnp.int32))
    gt = aff >= as_f32(thr + 1)
    eq = (aff >= as_f32(thr)) & jnp.logical_not(gt)
    need = (cap - jnp.sum(gt.astype(jnp.int32), axis=1, keepdims=True)).astype(F32)
    eq_rank = jnp.dot(eq.astype(BF16), tri_ref[...], preferred_element_type=F32)
    sel = gt | (eq & (eq_rank < need))
    sel_ref[...] = sel.astype(F32)
    pos_ref[...] = jnp.dot(sel.astype(BF16), tri_ref[...], preferred_element_type=F32)

    tok = (lax.broadcasted_iota(jnp.int32, (1, seq), 1) + pl.program_id(0) * seq).astype(F32)
    slot = lax.broadcasted_iota(jnp.int32, (cap, seq), 0).astype(F32)

    def per_expert(e, carry):
        oh = (pos_ref[pl.ds(e, 1), :] == slot) & (sel_ref[pl.ds(e, 1), :] > 0.5)
        idx_col = jnp.sum(jnp.where(oh, tok, 0.0), axis=1, keepdims=True)
        gate_col = jnp.sum(jnp.where(oh, aff_ref[pl.ds(e, 1), :], 0.0), axis=1, keepdims=True)
        idx_ref[e] = jnp.broadcast_to(idx_col, (cap, LANES)).astype(jnp.int32)
        gate_ref[e] = jnp.broadcast_to(gate_col, (cap, LANES))
        return carry

    lax.fori_loop(0, aff.shape[0], per_expert, 0)


def _route(aff_t, tri, batch, seq, cap):
    e = aff_t.shape[0]
    blk = pl.BlockSpec((e, seq), lambda b: (0, b))
    colblk = pl.BlockSpec((e, cap, LANES), lambda b: (0, b, 0))
    shp = jax.ShapeDtypeStruct(aff_t.shape, F32)
    return pl.pallas_call(
        functools.partial(_route_kernel, cap=cap, seq=seq),
        grid=(batch,),
        in_specs=[blk, pl.BlockSpec((seq, seq), lambda b: (0, 0))],
        out_specs=[blk, blk, colblk, colblk],
        out_shape=[shp, shp,
                   jax.ShapeDtypeStruct((e, batch * cap, LANES), jnp.int32),
                   jax.ShapeDtypeStruct((e, batch * cap, LANES), F32)],
        compiler_params=_cp(("arbitrary",)),
        name="route",
    )(aff_t, tri)


def _onehot(sel_row, pos_row, cap):
    r = lax.broadcasted_iota(jnp.int32, (cap, sel_row.shape[1]), 0).astype(F32)
    return (pos_row == r) & (sel_row > 0.5)


def _moe_kernel(idx_ref, h_hbm, gate_ref, wg_ref, wu_ref, wd_ref, o_ref,
                xg_ref, xe_ref, acc_ref, sem, *, rows):
    e = pl.program_id(0)
    f = pl.program_id(1)

    def gather(expert):
        def body(r, carry):
            tok = idx_ref[expert * rows + r]
            pltpu.make_async_copy(h_hbm.at[pl.ds(tok, 1), :], xg_ref.at[pl.ds(r, 1), :], sem.at[0]).start()
            return carry
        lax.fori_loop(0, rows, body, 0, unroll=8)

    @pl.when((e == 0) & (f == 0))
    def _():
        gather(0)

    @pl.when(f == 0)
    def _():
        pltpu.make_async_copy(h_hbm.at[pl.ds(0, rows), :], xg_ref, sem.at[0]).wait()
        xe_ref[...] = xg_ref[...].astype(BF16)
        acc_ref[...] = jnp.zeros_like(acc_ref)

        @pl.when(e + 1 < pl.num_programs(0))
        def _():
            gather(e + 1)

    xe = xe_ref[...]
    a = jnp.dot(xe, wg_ref[...].astype(BF16), preferred_element_type=F32)
    u = jnp.dot(xe, wu_ref[...].astype(BF16), preferred_element_type=F32)
    act = (a * jax.nn.sigmoid(a) * u).astype(BF16)
    acc_ref[...] += jnp.dot(act, wd_ref[...].astype(BF16), preferred_element_type=F32)

    @pl.when(f == pl.num_programs(1) - 1)
    def _():
        o_ref[...] = (acc_ref[...] * gate_ref[:, 0:1]).astype(BF16)


def _moe(idx_flat, gate_col, h2, wg, wu, wd, layer):
    t, d = h2.shape
    _, ne, _, ff = wg.shape
    rows = gate_col.shape[1]
    tf = 256
    grid_spec = pltpu.PrefetchScalarGridSpec(
        num_scalar_prefetch=1,
        grid=(ne, ff // tf),
        in_specs=[pl.BlockSpec(memory_space=pl.ANY),
                  pl.BlockSpec((None, rows, LANES), lambda e, f, idx: (e, 0, 0)),
                  pl.BlockSpec((None, None, d, tf), lambda e, f, idx: (layer, e, 0, f)),
                  pl.BlockSpec((None, None, d, tf), lambda e, f, idx: (layer, e, 0, f)),
                  pl.BlockSpec((None, None, tf, d), lambda e, f, idx: (layer, e, f, 0))],
        out_specs=pl.BlockSpec((None, rows, d), lambda e, f, idx: (e, 0, 0)),
        scratch_shapes=[pltpu.VMEM((rows, d), F32), pltpu.VMEM((rows, d), BF16), pltpu.VMEM((rows, d), F32),
                        pltpu.SemaphoreType.DMA((1,))],
    )
    return pl.pallas_call(
        functools.partial(_moe_kernel, rows=rows),
        grid_spec=grid_spec,
        out_shape=jax.ShapeDtypeStruct((ne, rows, d), BF16),
        compiler_params=_cp(("arbitrary", "arbitrary")),
        name="moe",
    )(idx_flat, h2, gate_col, wg, wu, wd)


def _scatter_kernel(sel_ref, pos_ref, o_ref, xn_ref, mod_ref, fn_ref, out_ref, acc_ref, *, cap, final_norm):
    e = pl.program_id(1)

    @pl.when(e == 0)
    def _():
        acc_ref[...] = jnp.zeros_like(acc_ref)

    oh = _onehot(sel_ref[pl.ds(e, 1), :], pos_ref[pl.ds(e, 1), :], cap).astype(BF16)
    acc_ref[...] += lax.dot_general(oh, o_ref[...], (((0,), (0,)), ((), ())), preferred_element_type=F32)

    @pl.when(e == pl.num_programs(1) - 1)
    def _():
        x = xn_ref[...] + mod_ref[5:6, :] * acc_ref[...]
        if final_norm:
            x = _gnorm(x, fn_ref[...])
        out_ref[...] = x


def _scatter(sel, pos, o, xn, mod_l, fn, seq, cap, final_norm):
    t, d = xn.shape
    ne = sel.shape[0]
    tm = 512
    tpb = seq // tm
    rowblk = pl.BlockSpec((ne, tm), lambda i, e: (0, i))
    return pl.pallas_call(
        functools.partial(_scatter_kernel, cap=cap, final_norm=final_norm),
        grid=(t // tm, ne),
        in_specs=[rowblk, rowblk,
                  pl.BlockSpec((None, cap, d), lambda i, e: (e, i // tpb, 0)),
                  pl.BlockSpec((tm, d), lambda i, e: (i, 0)),
                  pl.BlockSpec((None, 6, d), lambda i, e: (i // tpb, 0, 0)),
                  pl.BlockSpec((1, d), lambda i, e: (0, 0))],
        out_specs=pl.BlockSpec((tm, d), lambda i, e: (i, 0)),
        out_shape=jax.ShapeDtypeStruct((t, d), F32),
        scratch_shapes=[pltpu.VMEM((tm, d), F32)],
        compiler_params=_cp(("arbitrary", "arbitrary")),
        name="scatter",
    )(sel, pos, o, xn, mod_l, fn)


def _rope_tables(positions):
    b, s = positions.shape
    inv = jnp.power(jnp.float32(ROPE_THETA), -jnp.arange(0, ROPE_DIM, 2, dtype=F32) / ROPE_DIM)
    ang = positions.astype(F32)[..., None] * inv
    cos, sin = jnp.cos(ang), jnp.sin(ang)
    rest = HEAD_DIM - ROPE_DIM
    c64 = jnp.concatenate([cos, cos, jnp.ones((b, s, rest), F32)], axis=-1)
    s64 = jnp.concatenate([-sin, sin, jnp.zeros((b, s, rest), F32)], axis=-1)
    return jnp.tile(c64, (1, 1, 4)), jnp.tile(s64, (1, 1, 4))


def _dft_tables(n):
    k = jnp.arange(n, dtype=jnp.int32)
    ang = ((k[:, None] * k[None, :]) % n).astype(F32) * (2.0 * math.pi / n)
    return jnp.cos(ang), jnp.sin(ang)


def _block_diag(blocks):
    n, a, _ = blocks.shape
    eye = jnp.eye(n, dtype=blocks.dtype)
    return (eye[:, None, :, None] * blocks[:, :, None, :]).reshape(n * a, n * a)


def kernel(x, c, positions, ada_w, ada_b, norm_mix, w_in, fnet_w, sgu_norm, sgu_w, sgu_b, swa_sink,
           group_norm, w_out, norm_ffn, router_w, exp_w_gate, exp_w_up, exp_w_down, final_norm):
    batch, seq, d = x.shape
    depth = ada_w.shape[0]
    t = batch * seq
    cap = max(1, EC_FACTOR * seq // N_EXPERTS)

    rows = 16
    c_pad = jnp.zeros((rows, d), F32).at[:batch].set(c)
    mod = _ada(c_pad, ada_w, ada_b)[:, :batch].reshape(depth, batch, 6, d)

    cos_q, sin_q = _rope_tables(positions)
    dft_c, dft_s = _dft_tables(seq)
    dft_c, dft_s = dft_c.astype(BF16), dft_s.astype(BF16)
    cc, sc = _dft_tables(HEAD_DIM)
    bdc = _block_diag(jnp.broadcast_to(cc, (4, HEAD_DIM, HEAD_DIM)))
    bds = _block_diag(jnp.broadcast_to(sc, (4, HEAD_DIM, HEAD_DIM)))
    tri = (jnp.arange(seq)[:, None] < jnp.arange(seq)[None, :]).astype(BF16)

    x2 = x.reshape(t, d)
    for l in range(depth):
        mod_l = mod[l]
        proj = _in_proj(x2, mod_l, norm_mix[l].reshape(1, d), w_in[l].astype(BF16), seq)

        bdw = jnp.stack([_block_diag(fnet_w[l, 0:4]), _block_diag(fnet_w[l, 4:8])])
        ya = _fourier(proj, bdc, bds, bdw, dft_c, dft_s, batch, seq).reshape(t, GROUP_WIDTH)
        yb = _sgu(proj, sgu_norm[l].reshape(1, GROUP_WIDTH), sgu_w[l].astype(BF16), sgu_b[l].T)
        sink_row = jnp.repeat(swa_sink[l], HEAD_DIM).reshape(1, GROUP_WIDTH)
        yc = _swa(proj, cos_q, sin_q, sink_row, batch, seq)
        yd = _dilated(proj, cos_q, sin_q, batch, seq)

        r_t = router_w[l].T
        r_hi = r_t.astype(BF16)
        r_lo = (r_t - r_hi.astype(F32)).astype(BF16)
        rcat = jnp.concatenate([r_hi, r_lo], axis=0)
        xn, h2, aff_t = _out_proj(ya, yb, yc, yd, group_norm[l].reshape(1, d), w_out[l].astype(BF16),
                                  x2, mod_l, norm_ffn[l].reshape(1, d), rcat, seq)
        sel, pos, idx, gate_col = _route(aff_t, tri, batch, seq, cap)
        o = _moe(idx[:, :, 0].reshape(-1), gate_col, h2, exp_w_gate, exp_w_up, exp_w_down, l)
        x2 = _scatter(sel, pos, o, xn, mod_l, final_norm.reshape(1, d), seq, cap, l == depth - 1)
    return x2.reshape(batch, seq, d)
```

```python
import functools
import math

import jax
import jax.numpy as jnp
from jax import lax
from jax.experimental import pallas as pl
from jax.experimental.pallas import tpu as pltpu

F32 = jnp.float32
BF16 = jnp.bfloat16

HEAD_DIM = 64
N_GROUP_HEADS = 8
GROUP_WIDTH = 512
CHUNK = 128
SWA_HALF_WINDOW = 128
DIL_CONFIGS = ((128, 1), (512, 4), (2048, 16))
ROPE_THETA = 500000.0
ROPE_DIM = 16
N_EXPERTS = 16
EC_FACTOR = 2
EPS = 1e-6
NEG_INF = -1e30

LANES = 128
SEC = 256
SEC_XA, SEC_U, SEC_V, SEC_QC, SEC_KVC, SEC_QD, SEC_KD, SEC_VD = 0, 2, 4, 6, 8, 9, 11, 13
N_SEC = 15

VMEM_LIMIT = 56 * 1024 * 1024


def _cp(sem, vmem=None):
    return pltpu.CompilerParams(dimension_semantics=sem, vmem_limit_bytes=vmem or VMEM_LIMIT)


def _nt(a, b):
    return lax.dot_general(a, b, (((1,), (1,)), ((), ())), preferred_element_type=F32)


def _ada_kernel(c_ref, w_ref, b_ref, o_ref):
    c = c_ref[...]
    s = (c * jax.nn.sigmoid(c)).astype(BF16)
    o_ref[0] = jnp.dot(s, w_ref[0].astype(BF16), preferred_element_type=F32) + b_ref[0]


def _ada(c_pad, ada_w, ada_b):
    n_layers, d, n6 = ada_w.shape
    rows = c_pad.shape[0]
    tn = 1024
    return pl.pallas_call(
        _ada_kernel,
        grid=(n_layers, n6 // tn),
        in_specs=[pl.BlockSpec((rows, d), lambda l, j: (0, 0)),
                  pl.BlockSpec((1, d, tn), lambda l, j: (l, 0, j)),
                  pl.BlockSpec((1, 1, tn), lambda l, j: (l, 0, j))],
        out_specs=pl.BlockSpec((1, rows, tn), lambda l, j: (l, 0, j)),
        out_shape=jax.ShapeDtypeStruct((n_layers, rows, n6), F32),
        compiler_params=_cp(("arbitrary", "arbitrary")),
        name="ada",
    )(c_pad, ada_w, ada_b.reshape(n_layers, 1, n6))


NORM_ROWS = 16
IN_SECS = 3


def _in_kernel(x_ref, mod_ref, g_ref, w_ref, o_ref, h_ref):
    @pl.when(pl.program_id(1) == 0)
    def _():
        gain = g_ref[...] * (1.0 + mod_ref[1:2, :])
        shift = mod_ref[0:1, :]

        def chunk(c, carry):
            rows = pl.ds(pl.multiple_of(c * NORM_ROWS, NORM_ROWS), NORM_ROWS)
            x = x_ref[rows, :]
            r = lax.rsqrt(jnp.mean(x * x, axis=-1, keepdims=True) + EPS)
            h_ref[rows, :] = (x * r * gain + shift).astype(BF16)
            return carry

        lax.fori_loop(0, x_ref.shape[0] // NORM_ROWS, chunk, 0, unroll=8)

    res = jnp.dot(h_ref[...], w_ref[...], preferred_element_type=F32)
    for s in range(IN_SECS):
        o_ref[s] = res[:, s * SEC:(s + 1) * SEC]


def _in_proj(x2, mod_l, norm_g, w_bf, seq):
    t, d = x2.shape
    tm = 1024
    tpb = seq // tm
    return pl.pallas_call(
        _in_kernel,
        grid=(t // tm, N_SEC // IN_SECS),
        in_specs=[pl.BlockSpec((tm, d), lambda i, j: (i, 0)),
                  pl.BlockSpec((None, 6, d), lambda i, j: (i // tpb, 0, 0)),
                  pl.BlockSpec((1, d), lambda i, j: (0, 0)),
                  pl.BlockSpec((d, IN_SECS * SEC), lambda i, j: (0, j))],
        out_specs=pl.BlockSpec((IN_SECS, tm, SEC), lambda i, j: (j, i, 0)),
        out_shape=jax.ShapeDtypeStruct((N_SEC, t, SEC), F32),
        scratch_shapes=[pltpu.VMEM((tm, d), BF16)],
        compiler_params=_cp(("arbitrary", "arbitrary")),
        name="in_proj",
    )(x2, mod_l, norm_g, w_bf)


def _f1_kernel(x_ref, bdc_ref, bds_ref, bdw_ref, z_ref, m_ref, *, scale):
    @pl.when(pl.program_id(1) == 0)
    def _():
        w = bdw_ref[...]
        mc = jnp.dot(bdc_ref[...], w, preferred_element_type=F32, precision=lax.Precision.HIGHEST)
        msn = jnp.dot(bds_ref[...], w, preferred_element_type=F32, precision=lax.Precision.HIGHEST)
        m_ref[:, 0:SEC] = (mc * scale).astype(BF16)
        m_ref[:, SEC:2 * SEC] = (msn * scale).astype(BF16)

    z_ref[...] = jnp.dot(x_ref[...].astype(BF16), m_ref[...], preferred_element_type=F32).astype(BF16)


def _f2_kernel(c_ref, s_ref, z_ref, o_ref):
    z = z_ref[...]
    o_ref[...] = (jnp.dot(c_ref[...], z[:, 0:SEC], preferred_element_type=F32)
                  - jnp.dot(s_ref[...], z[:, SEC:2 * SEC], preferred_element_type=F32))


def _fourier(proj, bdc, bds, bdw, dft_c, dft_s, batch, seq):
    t = batch * seq
    tm = 1024
    scale = 1.0 / math.sqrt(seq * HEAD_DIM)
    z = pl.pallas_call(
        functools.partial(_f1_kernel, scale=scale),
        grid=(2, t // tm),
        in_specs=[pl.BlockSpec((None, tm, SEC), lambda s, i: (SEC_XA + s, i, 0)),
                  pl.BlockSpec((SEC, SEC), lambda s, i: (0, 0)),
                  pl.BlockSpec((SEC, SEC), lambda s, i: (0, 0)),
                  pl.BlockSpec((None, SEC, SEC), lambda s, i: (s, 0, 0))],
        out_specs=pl.BlockSpec((None, tm, 2 * SEC), lambda s, i: (s, i, 0)),
        out_shape=jax.ShapeDtypeStruct((2, t, 2 * SEC), BF16),
        scratch_shapes=[pltpu.VMEM((SEC, 2 * SEC), BF16)],
        compiler_params=_cp(("arbitrary", "arbitrary")),
        name="fnet_chan",
    )(proj, bdc, bds, bdw)
    rt = 512
    return pl.pallas_call(
        _f2_kernel,
        grid=(seq // rt, batch, 2),
        in_specs=[pl.BlockSpec((rt, seq), lambda r, b, s: (r, 0)),
                  pl.BlockSpec((rt, seq), lambda r, b, s: (r, 0)),
                  pl.BlockSpec((None, seq, 2 * SEC), lambda r, b, s: (s, b, 0))],
        out_specs=pl.BlockSpec((None, rt, SEC), lambda r, b, s: (b, r, s)),
        out_shape=jax.ShapeDtypeStruct((batch, seq, GROUP_WIDTH), F32),
        compiler_params=_cp(("arbitrary", "arbitrary", "arbitrary")),
        name="fnet_seq",
    )(dft_c, dft_s, z)


def _sgu_kernel(u_ref, v_ref, g_ref, w_ref, b_ref, o_ref, *, tm):
    gv = [jax.nn.gelu(v_ref[s]) for s in range(2)]
    ms = (jnp.sum(gv[0] * gv[0], axis=-1, keepdims=True)
          + jnp.sum(gv[1] * gv[1], axis=-1, keepdims=True)) * (1.0 / GROUP_WIDTH)
    r = lax.rsqrt(ms + EPS)
    vn = [(gv[s] * r * g_ref[:, s * SEC:(s + 1) * SEC]).astype(BF16) for s in range(2)]
    for s in range(2):
        gu = jax.nn.gelu(u_ref[s])
        for c in range(tm // CHUNK):
            rows = slice(c * CHUNK, (c + 1) * CHUNK)
            for hh in range(4):
                h = s * 4 + hh
                cols = slice(hh * HEAD_DIM, (hh + 1) * HEAD_DIM)
                z = jnp.dot(w_ref[h], vn[s][rows, cols], preferred_element_type=F32) + b_ref[:, h:h + 1]
                o_ref[rows, s * SEC + hh * HEAD_DIM:s * SEC + (hh + 1) * HEAD_DIM] = gu[rows, cols] * z


def _sgu(proj, sgu_norm, sgu_w_bf, sgu_bt):
    t = proj.shape[1]
    tm = 512
    return pl.pallas_call(
        functools.partial(_sgu_kernel, tm=tm),
        grid=(t // tm,),
        in_specs=[pl.BlockSpec((2, tm, SEC), lambda i: (SEC_U // 2, i, 0)),
                  pl.BlockSpec((2, tm, SEC), lambda i: (SEC_V // 2, i, 0)),
                  pl.BlockSpec((1, GROUP_WIDTH), lambda i: (0, 0)),
                  pl.BlockSpec((N_GROUP_HEADS, CHUNK, CHUNK), lambda i: (0, 0, 0)),
                  pl.BlockSpec((CHUNK, N_GROUP_HEADS), lambda i: (0, 0))],
        out_specs=pl.BlockSpec((tm, GROUP_WIDTH), lambda i: (i, 0)),
        out_shape=jax.ShapeDtypeStruct((t, GROUP_WIDTH), F32),
        compiler_params=_cp(("arbitrary",)),
        name="sgu",
    )(proj, proj, sgu_norm, sgu_w_bf, sgu_bt)


ATT_TQ = 128
PAIR = 2 * HEAD_DIM


def _rope(x, cos_t, sin_t):
    n = x.shape[-1]
    half = ROPE_DIM // 2
    lane = lax.broadcasted_iota(jnp.int32, x.shape, 1) % HEAD_DIM
    swapped = jnp.where(lane < half, pltpu.roll(x, n - half, 1), pltpu.roll(x, half, 1))
    return x * cos_t + swapped * sin_t


def _lo_half(shape):
    return lax.broadcasted_iota(jnp.int32, shape, 1) < HEAD_DIM


def _stage_scores(q_pairs, k, valid, s_ref, slot, grp):
    tq = q_pairs[0].shape[0]
    lo = _lo_half(q_pairs[0].shape)
    zero = jnp.zeros_like(q_pairs[0])
    blocks = []
    for q in q_pairs:
        blocks += [jnp.where(lo, q, zero), jnp.where(lo, zero, q)]
    s = _nt(jnp.concatenate(blocks, axis=0), k)
    for i in range(len(blocks)):
        s_ref[slot, grp, i * tq:(i + 1) * tq, :] = jnp.where(valid, s[i * tq:(i + 1) * tq], NEG_INF)


def _stage_softmax(s_ref, p_ref, m_ref, l_ref, slot, grp, tq):
    s = s_ref[slot, grp]
    m = jnp.max(s, axis=-1, keepdims=True)
    p = jnp.exp(s - m)
    p_ref[slot, grp] = p.astype(BF16)
    l = jnp.sum(p, axis=-1, keepdims=True)
    n_pairs = s.shape[0] // (2 * tq)
    shp = (tq, PAIR)
    lo = _lo_half(shp)
    for i in range(n_pairs):
        a, b = slice(2 * i * tq, (2 * i + 1) * tq), slice((2 * i + 1) * tq, (2 * i + 2) * tq)
        m_ref[slot, grp * n_pairs + i] = jnp.where(lo, jnp.broadcast_to(m[a], shp), jnp.broadcast_to(m[b], shp))
        l_ref[slot, grp * n_pairs + i] = jnp.where(lo, jnp.broadcast_to(l[a], shp), jnp.broadcast_to(l[b], shp))


def _stage_pv(p_ref, v, slot, grp, tq):
    o = jnp.dot(p_ref[slot, grp], v, preferred_element_type=F32)
    lo = _lo_half((tq, PAIR))
    return [jnp.where(lo, o[2 * i * tq:(2 * i + 1) * tq], o[(2 * i + 1) * tq:(2 * i + 2) * tq])
            for i in range(o.shape[0] // (2 * tq))]


ATT_ITEMS = 4


def _pipeline(n_items, stage_a, stage_b, stage_c):
    def body(jj, carry):
        for stage in (stage_a, stage_b, stage_c):
            for t in range(ATT_ITEMS):
                stage(jj * ATT_ITEMS + t, t)
        return carry

    lax.fori_loop(0, n_items // ATT_ITEMS, body, 0)


def _band_valid(tq, tk, w, delta):
    d = (lax.broadcasted_iota(jnp.int32, (tq, tk), 1) - lax.broadcasted_iota(jnp.int32, (tq, tk), 0)) + delta
    return (d <= w) & (d >= -w)


def _swa_kernel(q_ref, kv_ref, cos_ref, sin_ref, sink_ref, o_ref, qs_ref, ks_ref, vs_ref,
                s_ref, p_ref, m_ref, l_ref, *, seq):
    w, tq = SWA_HALF_WINDOW, ATT_TQ
    tk = tq + 2 * w
    cos, sin = cos_ref[...], sin_ref[...]
    for s in range(2):
        qs_ref[s] = (_rope(q_ref[s], cos, sin) * (HEAD_DIM ** -0.5)).astype(BF16)
    kk = _rope(kv_ref[:, 0:PAIR], cos[:, 0:PAIR], sin[:, 0:PAIR])
    vv = kv_ref[:, PAIR:2 * PAIR]
    lo = lax.broadcasted_iota(jnp.int32, kk.shape, 1) < HEAD_DIM
    kk_sw, vv_sw = pltpu.roll(kk, HEAD_DIM, 1), pltpu.roll(vv, HEAD_DIM, 1)
    ks_ref[0] = jnp.where(lo, kk, kk_sw).astype(BF16)
    ks_ref[1] = jnp.where(lo, kk_sw, kk).astype(BF16)
    vs_ref[0] = jnp.where(lo, vv, vv_sw).astype(BF16)
    vs_ref[1] = jnp.where(lo, vv_sw, vv).astype(BF16)

    def coords(j):
        q0 = pl.multiple_of(j * tq, tq)
        return q0, pl.multiple_of(jnp.clip(q0 - w, 0, seq - tk), tq)

    def scores(j, slot):
        q0, k0 = coords(j)
        valid = _band_valid(tq, tk, w, k0 - q0)
        for g in range(2):
            q_pairs = [qs_ref[g, pl.ds(q0, tq), 0:PAIR], qs_ref[g, pl.ds(q0, tq), PAIR:2 * PAIR]]
            _stage_scores(q_pairs, ks_ref[g, pl.ds(k0, tk), :], valid, s_ref, slot, g)

    def softmax(j, slot):
        for g in range(2):
            _stage_softmax(s_ref, p_ref, m_ref, l_ref, slot, g, tq)

    def output(j, slot):
        q0, k0 = coords(j)
        for g in range(2):
            outs = _stage_pv(p_ref, vs_ref[g, pl.ds(k0, tk), :], slot, g, tq)
            for i, o in enumerate(outs):
                pair = 2 * g + i
                m, l = m_ref[slot, pair], l_ref[slot, pair]
                sink = sink_ref[:, pair * PAIR:(pair + 1) * PAIR]
                m2 = jnp.maximum(m, sink)
                a = jnp.exp(m - m2)
                o_ref[pl.ds(q0, tq), pair * PAIR:(pair + 1) * PAIR] = o * a / (l * a + jnp.exp(sink - m2))

    _pipeline(seq // tq, scores, softmax, output)


def _swa(proj, cos_q, sin_q, sink_row, batch, seq):
    t = batch * seq
    tab = pl.BlockSpec((None, seq, SEC), lambda b: (b, 0, 0))
    return pl.pallas_call(
        functools.partial(_swa_kernel, seq=seq),
        grid=(batch,),
        in_specs=[pl.BlockSpec((2, seq, SEC), lambda b: (SEC_QC // 2, b, 0)),
                  pl.BlockSpec((None, seq, SEC), lambda b: (SEC_KVC, b, 0)),
                  tab, tab,
                  pl.BlockSpec((1, GROUP_WIDTH), lambda b: (0, 0))],
        out_specs=pl.BlockSpec((seq, GROUP_WIDTH), lambda b: (b, 0)),
        out_shape=jax.ShapeDtypeStruct((t, GROUP_WIDTH), F32),
        scratch_shapes=[pltpu.VMEM((2, seq, SEC), BF16),
                        pltpu.VMEM((2, seq, PAIR), BF16),
                        pltpu.VMEM((2, seq, PAIR), BF16),
                        pltpu.VMEM((ATT_ITEMS, 2, 4 * ATT_TQ, ATT_TQ + 2 * SWA_HALF_WINDOW), F32),
                        pltpu.VMEM((ATT_ITEMS, 2, 4 * ATT_TQ, ATT_TQ + 2 * SWA_HALF_WINDOW), BF16),
                        pltpu.VMEM((ATT_ITEMS, 4, ATT_TQ, PAIR), F32),
                        pltpu.VMEM((ATT_ITEMS, 4, ATT_TQ, PAIR), F32)],
        compiler_params=_cp(("arbitrary",)),
        name="swa",
    )(proj, proj, cos_q, sin_q, sink_row)


def _dil_kernel(q_ref, k_ref, v_ref, cos_ref, sin_ref, o_ref,
                qf_ref, kf_ref, vf_ref, m_ref, l_ref, acc_ref, qd_ref, kd_ref, vd_ref,
                s_ref, p_ref, mt_ref, lt_ref, *, seq):
    cos, sin = cos_ref[...], sin_ref[...]
    q = _rope(q_ref[...], cos, sin) * (HEAD_DIM ** -0.5)
    k = _rope(k_ref[...], cos, sin)
    for pair in range(2):
        lanes = slice(pair * PAIR, (pair + 1) * PAIR)
        qf_ref[pair] = q[:, lanes]
        kf_ref[pair] = k[:, lanes]
        vf_ref[pair] = v_ref[:, lanes]
    m_ref[...] = jnp.full(m_ref.shape, NEG_INF, F32)
    l_ref[...] = jnp.zeros(l_ref.shape, F32)
    acc_ref[...] = jnp.zeros(acc_ref.shape, F32)

    for window, dil in DIL_CONFIGS:
        w = window // (2 * dil)
        ls = seq // dil
        tq = min(ATT_TQ, ls)
        tk = min(tq + 2 * w, ls)

        def rows_of(start, size, dil=dil):
            return pl.ds(start, size) if dil == 1 else pl.ds(start, size, stride=dil)

        def permute(r, carry, ls=ls, rows_of=rows_of):
            dst = pl.ds(pl.multiple_of(r * ls, ls), ls)
            for pair in range(2):
                qd_ref[pair, dst, :] = qf_ref[pair, rows_of(r, ls), :].astype(BF16)
                kd_ref[pair, dst, :] = kf_ref[pair, rows_of(r, ls), :].astype(BF16)
                vd_ref[pair, dst, :] = vf_ref[pair, rows_of(r, ls), :].astype(BF16)
            return carry

        lax.fori_loop(0, dil, permute, 0)
        tiles_per_res = ls // tq

        def coords(j, w=w, ls=ls, tq=tq, tk=tk, tpr=tiles_per_res):
            r = j // tpr
            q0 = (j % tpr) * tq
            k0 = jnp.clip(q0 - w, 0, ls - tk)
            return r, q0, pl.multiple_of(r * ls + k0, w), k0 - q0

        def scores(j, slot, w=w, tq=tq, tk=tk, coords=coords):
            _, _, krow, delta = coords(j)
            valid = _band_valid(tq, tk, w, delta)
            qrow = pl.multiple_of(j * tq, tq)
            for pair in range(2):
                _stage_scores([qd_ref[pair, pl.ds(qrow, tq), :]], kd_ref[pair, pl.ds(krow, tk), :], valid,
                              s_ref.at[:, :, :, 0:tk], slot, pair)

        def softmax(j, slot, tq=tq, tk=tk):
            for pair in range(2):
                _stage_softmax(s_ref.at[:, :, :, 0:tk], p_ref.at[:, :, :, 0:tk], mt_ref, lt_ref, slot, pair, tq)

        def merge(j, slot, tq=tq, tk=tk, dil=dil, coords=coords, rows_of=rows_of):
            r, q0, krow, _ = coords(j)
            rows = rows_of(r + dil * q0, tq)
            for pair in range(2):
                (o,) = _stage_pv(p_ref.at[:, :, :, 0:tk], vd_ref[pair, pl.ds(krow, tk), :], slot, pair, tq)
                m, l = mt_ref[slot, pair], lt_ref[slot, pair]
                m_old = m_ref[pair, rows, :]
                m_new = jnp.maximum(m_old, m)
                a, b = jnp.exp(m_old - m_new), jnp.exp(m - m_new)
                acc_ref[pair, rows, :] = acc_ref[pair, rows, :] * a + o * b
                l_ref[pair, rows, :] = l_ref[pair, rows, :] * a + l * b
                m_ref[pair, rows, :] = m_new

        _pipeline(seq // tq, scores, softmax, merge)

    for pair in range(2):
        o_ref[:, pair * PAIR:(pair + 1) * PAIR] = acc_ref[pair] / l_ref[pair]


def _dilated(proj, cos_q, sin_q, batch, seq):
    t = batch * seq

    def sec(base):
        return pl.BlockSpec((None, seq, SEC), lambda b, g: (base + g, b, 0))

    tab = pl.BlockSpec((None, seq, SEC), lambda b, g: (b, 0, 0))
    return pl.pallas_call(
        functools.partial(_dil_kernel, seq=seq),
        grid=(batch, 2),
        in_specs=[sec(SEC_QD), sec(SEC_KD), sec(SEC_VD), tab, tab],
        out_specs=pl.BlockSpec((seq, SEC), lambda b, g: (b, g)),
        out_shape=jax.ShapeDtypeStruct((t, GROUP_WIDTH), F32),
        scratch_shapes=[pltpu.VMEM((2, seq, PAIR), F32)] * 6 + [pltpu.VMEM((2, seq, PAIR), BF16)] * 3 + [
            pltpu.VMEM((ATT_ITEMS, 2, 2 * ATT_TQ, 2 * ATT_TQ), F32),
            pltpu.VMEM((ATT_ITEMS, 2, 2 * ATT_TQ, 2 * ATT_TQ), BF16),
            pltpu.VMEM((ATT_ITEMS, 2, ATT_TQ, PAIR), F32),
            pltpu.VMEM((ATT_ITEMS, 2, ATT_TQ, PAIR), F32)],
        compiler_params=_cp(("arbitrary", "arbitrary")),
        name="dilated",
    )(proj, proj, proj, cos_q, sin_q)


def _gnorm(y, g):
    return y * lax.rsqrt(jnp.mean(y * y, axis=-1, keepdims=True) + EPS) * g


def _out_kernel(ya_ref, yb_ref, yc_ref, yd_ref, gn_ref, w_ref, x_ref, mod_ref, nf_ref, rcat_ref,
                xn_ref, h_ref, aff_ref):
    ys = (ya_ref, yb_ref, yc_ref, yd_ref)
    gw = GROUP_WIDTH
    acc = None
    for g in range(4):
        yn = _gnorm(ys[g][...], gn_ref[:, g * gw:(g + 1) * gw]).astype(BF16)
        part = jnp.dot(yn, w_ref[g * gw:(g + 1) * gw, :], preferred_element_type=F32)
        acc = part if acc is None else acc + part
    xn = x_ref[...] + mod_ref[2:3, :] * acc
    xn_ref[...] = xn
    h = _gnorm(xn, nf_ref[...]) * (1.0 + mod_ref[4:5, :]) + mod_ref[3:4, :]
    row_tiles = h.shape[1] // LANES
    for c in range(row_tiles):
        h_ref[pl.ds(c, h.shape[0], stride=row_tiles), :] = h[:, c * LANES:(c + 1) * LANES]
    h_hi = h.astype(BF16)
    h_lo = (h - h_hi.astype(F32)).astype(BF16)
    a = _nt(rcat_ref[...], h_hi)
    b = _nt(rcat_ref[0:N_EXPERTS, :], h_lo)
    logits = a[0:N_EXPERTS] + a[N_EXPERTS:2 * N_EXPERTS] + b
    mxl = jnp.max(logits, axis=0, keepdims=True)
    ex = jnp.exp(logits - mxl)
    aff_ref[...] = ex / jnp.sum(ex, axis=0, keepdims=True)


def _out_proj(ya, yb, yc, yd, gn, w_bf, x2, mod_l, nf, rcat, seq):
    t, d = x2.shape
    tm = 256
    tpb = seq // tm
    ROW_TILES = d // LANES
    ysp = pl.BlockSpec((tm, GROUP_WIDTH), lambda i: (i, 0))
    return pl.pallas_call(
        _out_kernel,
        grid=(t // tm,),
        in_specs=[ysp] * 4 + [
            pl.BlockSpec((1, d), lambda i: (0, 0)),
            pl.BlockSpec((d, d), lambda i: (0, 0)),
            pl.BlockSpec((tm, d), lambda i: (i, 0)),
            pl.BlockSpec((None, 6, d), lambda i: (i // tpb, 0, 0)),
            pl.BlockSpec((1, d), lambda i: (0, 0)),
            pl.BlockSpec((2 * N_EXPERTS, d), lambda i: (0, 0))],
        out_specs=[pl.BlockSpec((tm, d), lambda i: (i, 0)),
                   pl.BlockSpec((tm * ROW_TILES, LANES), lambda i: (i, 0)),
                   pl.BlockSpec((N_EXPERTS, tm), lambda i: (0, i))],
        out_shape=[jax.ShapeDtypeStruct((t, d), F32),
                   jax.ShapeDtypeStruct((t * ROW_TILES, LANES), F32),
                   jax.ShapeDtypeStruct((N_EXPERTS, t), F32)],
        compiler_params=_cp(("arbitrary",)),
        name="out_proj",
    )(ya, yb, yc, yd, gn, w_bf, x2, mod_l, nf, rcat)


def _route_kernel(aff_ref, tri_ref, sel_ref, pos_ref, idx_ref, gate_ref, *, cap, seq):
    aff = aff_ref[...]

    def as_f32(bits):
        return lax.bitcast_convert_type(bits, F32)

    def step(k, thr):
        cand = thr | lax.shift_left(jnp.int32(1), 30 - k)
        cnt = jnp.sum((aff >= as_f32(cand)).astype(jnp.int32), axis=1, keepdims=True)
        return jnp.where(cnt >= cap, cand, thr)

    thr = lax.fori_loop(0, 31, step, jnp.zeros((aff.shape[0], 1), jnp.int32))
    gt = aff >= as_f32(thr + 1)
    eq = (aff >= as_f32(thr)) & jnp.logical_not(gt)
    need = (cap - jnp.sum(gt.astype(jnp.int32), axis=1, keepdims=True)).astype(F32)
    eq_rank = jnp.dot(eq.astype(BF16), tri_ref[...], preferred_element_type=F32)
    sel = gt | (eq & (eq_rank < need))
    sel_ref[...] = sel.astype(F32)
    pos_ref[...] = jnp.dot(sel.astype(BF16), tri_ref[...], preferred_element_type=F32)

    tok = (lax.broadcasted_iota(jnp.int32, (1, seq), 1) + pl.program_id(0) * seq).astype(F32)
    slot = lax.broadcasted_iota(jnp.int32, (cap, seq), 0).astype(F32)

    def per_expert(e, carry):
        oh = (pos_ref[pl.ds(e, 1), :] == slot) & (sel_ref[pl.ds(e, 1), :] > 0.5)
        idx_col = jnp.sum(jnp.where(oh, tok, 0.0), axis=1, keepdims=True)
        gate_col = jnp.sum(jnp.where(oh, aff_ref[pl.ds(e, 1), :], 0.0), axis=1, keepdims=True)
        idx_ref[e] = jnp.broadcast_to(idx_col, (cap, LANES)).astype(jnp.int32)
        gate_ref[e] = jnp.broadcast_to(gate_col, (cap, LANES))
        return carry

    lax.fori_loop(0, aff.shape[0], per_expert, 0)


def _route(aff_t, tri, batch, seq, cap):
    e = aff_t.shape[0]
    blk = pl.BlockSpec((e, seq), lambda b: (0, b))
    colblk = pl.BlockSpec((e, cap, LANES), lambda b: (0, b, 0))
    shp = jax.ShapeDtypeStruct(aff_t.shape, F32)
    return pl.pallas_call(
        functools.partial(_route_kernel, cap=cap, seq=seq),
        grid=(batch,),
        in_specs=[blk, pl.BlockSpec((seq, seq), lambda b: (0, 0))],
        out_specs=[blk, blk, colblk, colblk],
        out_shape=[shp, shp,
                   jax.ShapeDtypeStruct((e, batch * cap, LANES), jnp.int32),
                   jax.ShapeDtypeStruct((e, batch * cap, LANES), F32)],
        compiler_params=_cp(("arbitrary",)),
        name="route",
    )(aff_t, tri)


def _onehot(sel_row, pos_row, cap):
    r = lax.broadcasted_iota(jnp.int32, (cap, sel_row.shape[1]), 0).astype(F32)
    return (pos_row == r) & (sel_row > 0.5)


def _moe_kernel(idx_ref, h_hbm, gate_ref, wg_ref, wu_ref, wd_ref, o_ref,
                xg_ref, xe_ref, acc_ref, sem, *, rows):
    e = pl.program_id(0)
    f = pl.program_id(1)

    row_tiles = xe_ref.shape[1] // LANES

    def gather(expert):
        def body(i, carry):
            for u in range(8):
                r = i * 8 + u
                tok = idx_ref[expert * rows + r]
                pltpu.make_async_copy(h_hbm.at[pl.ds(pl.multiple_of(tok * row_tiles, row_tiles), row_tiles), :],
                                      xg_ref.at[:, r, :], sem.at[0]).start()
            return carry
        lax.fori_loop(0, rows // 8, body, 0)

    @pl.when((e == 0) & (f == 0))
    def _():
        gather(0)

    @pl.when(f == 0)
    def _():
        for c in range(row_tiles):
            pltpu.make_async_copy(h_hbm.at[pl.ds(0, rows), :], xg_ref.at[c], sem.at[0]).wait()
        for c in range(row_tiles):
            xe_ref[:, c * LANES:(c + 1) * LANES] = xg_ref[c].astype(BF16)
        acc_ref[...] = jnp.zeros_like(acc_ref)

        @pl.when(e + 1 < pl.num_programs(0))
        def _():
            gather(e + 1)

    xe = xe_ref[...]
    a = jnp.dot(xe, wg_ref[...].astype(BF16), preferred_element_type=F32)
    u = jnp.dot(xe, wu_ref[...].astype(BF16), preferred_element_type=F32)
    act = (a * jax.nn.sigmoid(a) * u).astype(BF16)
    acc_ref[...] += jnp.dot(act, wd_ref[...].astype(BF16), preferred_element_type=F32)

    @pl.when(f == pl.num_programs(1) - 1)
    def _():
        o_ref[...] = (acc_ref[...] * gate_ref[:, 0:1]).astype(BF16)


def _moe(idx_flat, gate_col, h2, wg, wu, wd, layer):
    _, ne, d, ff = wg.shape
    rows = gate_col.shape[1]
    tf = 256
    grid_spec = pltpu.PrefetchScalarGridSpec(
        num_scalar_prefetch=1,
        grid=(ne, ff // tf),
        in_specs=[pl.BlockSpec(memory_space=pl.ANY),
                  pl.BlockSpec((None, rows, LANES), lambda e, f, idx: (e, 0, 0)),
                  pl.BlockSpec((None, None, d, tf), lambda e, f, idx: (layer, e, 0, f)),
                  pl.BlockSpec((None, None, d, tf), lambda e, f, idx: (layer, e, 0, f)),
                  pl.BlockSpec((None, None, tf, d), lambda e, f, idx: (layer, e, f, 0))],
        out_specs=pl.BlockSpec((None, rows, d), lambda e, f, idx: (e, 0, 0)),
        scratch_shapes=[pltpu.VMEM((d // LANES, rows, LANES), F32), pltpu.VMEM((rows, d), BF16),
                        pltpu.VMEM((rows, d), F32), pltpu.SemaphoreType.DMA((1,))],
    )
    return pl.pallas_call(
        functools.partial(_moe_kernel, rows=rows),
        grid_spec=grid_spec,
        out_shape=jax.ShapeDtypeStruct((ne, rows, d), BF16),
        compiler_params=_cp(("arbitrary", "arbitrary")),
        name="moe",
    )(idx_flat, h2, gate_col, wg, wu, wd)


def _scatter_kernel(sel_ref, pos_ref, o_ref, xn_ref, mod_ref, fn_ref, out_ref, acc_ref, *, cap, final_norm):
    e = pl.program_id(1)

    @pl.when(e == 0)
    def _():
        acc_ref[...] = jnp.zeros_like(acc_ref)

    oh = _onehot(sel_ref[pl.ds(e, 1), :], pos_ref[pl.ds(e, 1), :], cap).astype(BF16)
    acc_ref[...] += lax.dot_general(oh, o_ref[...], (((0,), (0,)), ((), ())), preferred_element_type=F32)

    @pl.when(e == pl.num_programs(1) - 1)
    def _():
        x = xn_ref[...] + mod_ref[5:6, :] * acc_ref[...]
        if final_norm:
            x = _gnorm(x, fn_ref[...])
        out_ref[...] = x


def _scatter(sel, pos, o, xn, mod_l, fn, seq, cap, final_norm):
    t, d = xn.shape
    ne = sel.shape[0]
    tm = 512
    tpb = seq // tm
    rowblk = pl.BlockSpec((ne, tm), lambda i, e: (0, i))
    return pl.pallas_call(
        functools.partial(_scatter_kernel, cap=cap, final_norm=final_norm),
        grid=(t // tm, ne),
        in_specs=[rowblk, rowblk,
                  pl.BlockSpec((None, cap, d), lambda i, e: (e, i // tpb, 0)),
                  pl.BlockSpec((tm, d), lambda i, e: (i, 0)),
                  pl.BlockSpec((None, 6, d), lambda i, e: (i // tpb, 0, 0)),
                  pl.BlockSpec((1, d), lambda i, e: (0, 0))],
        out_specs=pl.BlockSpec((tm, d), lambda i, e: (i, 0)),
        out_shape=jax.ShapeDtypeStruct((t, d), F32),
        scratch_shapes=[pltpu.VMEM((tm, d), F32)],
        compiler_params=_cp(("arbitrary", "arbitrary")),
        name="scatter",
    )(sel, pos, o, xn, mod_l, fn)


def _rope_tables(positions):
    b, s = positions.shape
    inv = jnp.power(jnp.float32(ROPE_THETA), -jnp.arange(0, ROPE_DIM, 2, dtype=F32) / ROPE_DIM)
    ang = positions.astype(F32)[..., None] * inv
    cos, sin = jnp.cos(ang), jnp.sin(ang)
    rest = HEAD_DIM - ROPE_DIM
    c64 = jnp.concatenate([cos, cos, jnp.ones((b, s, rest), F32)], axis=-1)
    s64 = jnp.concatenate([-sin, sin, jnp.zeros((b, s, rest), F32)], axis=-1)
    return jnp.tile(c64, (1, 1, 4)), jnp.tile(s64, (1, 1, 4))


def _dft_tables(n):
    k = jnp.arange(n, dtype=jnp.int32)
    ang = ((k[:, None] * k[None, :]) % n).astype(F32) * (2.0 * math.pi / n)
    return jnp.cos(ang), jnp.sin(ang)


def _dft_tables_split(n):
    k = jnp.arange(n, dtype=jnp.int32)[:, None]
    m0 = jnp.arange(LANES, dtype=jnp.int32)[None, :]
    m1 = jnp.arange(n // LANES, dtype=jnp.int32)[None, :]
    a = ((k * m0) % n).astype(F32) * (2.0 * math.pi / n)
    b = ((k * m1 * LANES) % n).astype(F32) * (2.0 * math.pi / n)
    ca, sa, cb, sb = jnp.cos(a)[:, None, :], jnp.sin(a)[:, None, :], jnp.cos(b)[:, :, None], jnp.sin(b)[:, :, None]
    return (ca * cb - sa * sb).reshape(n, n), (sa * cb + ca * sb).reshape(n, n)


def _block_diag(blocks):
    n, a, _ = blocks.shape
    eye = jnp.eye(n, dtype=blocks.dtype)
    return (eye[:, None, :, None] * blocks[:, :, None, :]).reshape(n * a, n * a)


def kernel(x, c, positions, ada_w, ada_b, norm_mix, w_in, fnet_w, sgu_norm, sgu_w, sgu_b, swa_sink,
           group_norm, w_out, norm_ffn, router_w, exp_w_gate, exp_w_up, exp_w_down, final_norm):
    batch, seq, d = x.shape
    depth = ada_w.shape[0]
    t = batch * seq
    cap = max(1, EC_FACTOR * seq // N_EXPERTS)

    rows = 16
    c_pad = jnp.zeros((rows, d), F32).at[:batch].set(c)
    mod = _ada(c_pad, ada_w, ada_b)[:, :batch].reshape(depth, batch, 6, d)

    cos_q, sin_q = _rope_tables(positions)
    dft_c, dft_s = _dft_tables_split(seq)
    dft_c, dft_s = dft_c.astype(BF16), dft_s.astype(BF16)
    cc, sc = _dft_tables(HEAD_DIM)
    bdc = _block_diag(jnp.broadcast_to(cc, (4, HEAD_DIM, HEAD_DIM)))
    bds = _block_diag(jnp.broadcast_to(sc, (4, HEAD_DIM, HEAD_DIM)))
    tri = (jnp.arange(seq)[:, None] < jnp.arange(seq)[None, :]).astype(BF16)

    x2 = x.reshape(t, d)
    for l in range(depth):
        mod_l = mod[l]
        proj = _in_proj(x2, mod_l, norm_mix[l].reshape(1, d), w_in[l].astype(BF16), seq)

        bdw = jnp.stack([_block_diag(fnet_w[l, 0:4]), _block_diag(fnet_w[l, 4:8])])
        ya = _fourier(proj, bdc, bds, bdw, dft_c, dft_s, batch, seq).reshape(t, GROUP_WIDTH)
        yb = _sgu(proj, sgu_norm[l].reshape(1, GROUP_WIDTH), sgu_w[l].astype(BF16), sgu_b[l].T)
        sink_row = jnp.repeat(swa_sink[l], HEAD_DIM).reshape(1, GROUP_WIDTH)
        yc = _swa(proj, cos_q, sin_q, sink_row, batch, seq)
        yd = _dilated(proj, cos_q, sin_q, batch, seq)

        r_t = router_w[l].T
        r_hi = r_t.astype(BF16)
        r_lo = (r_t - r_hi.astype(F32)).astype(BF16)
        rcat = jnp.concatenate([r_hi, r_lo], axis=0)
        xn, h2, aff_t = _out_proj(ya, yb, yc, yd, group_norm[l].reshape(1, d), w_out[l].astype(BF16),
                                  x2, mod_l, norm_ffn[l].reshape(1, d), rcat, seq)
        sel, pos, idx, gate_col = _route(aff_t, tri, batch, seq, cap)
        o = _moe(idx[:, :, 0].reshape(-1), gate_col, h2, exp_w_gate, exp_w_up, exp_w_down, l)
        x2 = _scatter(sel, pos, o, xn, mod_l, final_norm.reshape(1, d), seq, cap, l == depth - 1)
    return x2.reshape(batch, seq, d)
```

```python
import functools
import math

import jax
import jax.numpy as jnp
from jax import lax
from jax.experimental import pallas as pl
from jax.experimental.pallas import tpu as pltpu

F32 = jnp.float32
BF16 = jnp.bfloat16

HEAD_DIM = 64
N_GROUP_HEADS = 8
GROUP_WIDTH = 512
CHUNK = 128
SWA_HALF_WINDOW = 128
DIL_CONFIGS = ((128, 1), (512, 4), (2048, 16))
ROPE_THETA = 500000.0
ROPE_DIM = 16
N_EXPERTS = 16
EC_FACTOR = 2
EPS = 1e-6
NEG_INF = -1e30

LANES = 128
SEC = 256
SEC_XA, SEC_U, SEC_V, SEC_QC, SEC_KVC, SEC_QD, SEC_KD, SEC_VD = 0, 2, 4, 6, 8, 9, 11, 13
N_SEC = 15

VMEM_LIMIT = 56 * 1024 * 1024


def _cp(sem, vmem=None):
    return pltpu.CompilerParams(dimension_semantics=sem, vmem_limit_bytes=vmem or VMEM_LIMIT)


def _nt(a, b):
    return lax.dot_general(a, b, (((1,), (1,)), ((), ())), preferred_element_type=F32)


def _ada_kernel(c_ref, w_ref, b_ref, o_ref):
    c = c_ref[...]
    s = (c * jax.nn.sigmoid(c)).astype(BF16)
    o_ref[0] = jnp.dot(s, w_ref[0].astype(BF16), preferred_element_type=F32) + b_ref[0]


def _ada(c_pad, ada_w, ada_b):
    n_layers, d, n6 = ada_w.shape
    rows = c_pad.shape[0]
    tn = 1024
    return pl.pallas_call(
        _ada_kernel,
        grid=(n_layers, n6 // tn),
        in_specs=[pl.BlockSpec((rows, d), lambda l, j: (0, 0)),
                  pl.BlockSpec((1, d, tn), lambda l, j: (l, 0, j)),
                  pl.BlockSpec((1, 1, tn), lambda l, j: (l, 0, j))],
        out_specs=pl.BlockSpec((1, rows, tn), lambda l, j: (l, 0, j)),
        out_shape=jax.ShapeDtypeStruct((n_layers, rows, n6), F32),
        compiler_params=_cp(("arbitrary", "arbitrary")),
        name="ada",
    )(c_pad, ada_w, ada_b.reshape(n_layers, 1, n6))


NORM_ROWS = 16
IN_SECS = 3


def _in_kernel(x_ref, mod_ref, g_ref, w_ref, o_ref, h_ref):
    @pl.when(pl.program_id(1) == 0)
    def _():
        gain = g_ref[...] * (1.0 + mod_ref[1:2, :])
        shift = mod_ref[0:1, :]

        def chunk(c, carry):
            rows = pl.ds(pl.multiple_of(c * NORM_ROWS, NORM_ROWS), NORM_ROWS)
            x = x_ref[rows, :]
            r = lax.rsqrt(jnp.mean(x * x, axis=-1, keepdims=True) + EPS)
            h_ref[rows, :] = (x * r * gain + shift).astype(BF16)
            return carry

        lax.fori_loop(0, x_ref.shape[0] // NORM_ROWS, chunk, 0, unroll=8)

    res = jnp.dot(h_ref[...], w_ref[...], preferred_element_type=F32)
    for s in range(IN_SECS):
        o_ref[s] = res[:, s * SEC:(s + 1) * SEC]


def _in_proj(x2, mod_l, norm_g, w_bf, seq):
    t, d = x2.shape
    tm = 1024
    tpb = seq // tm
    return pl.pallas_call(
        _in_kernel,
        grid=(t // tm, N_SEC // IN_SECS),
        in_specs=[pl.BlockSpec((tm, d), lambda i, j: (i, 0)),
                  pl.BlockSpec((None, 6, d), lambda i, j: (i // tpb, 0, 0)),
                  pl.BlockSpec((1, d), lambda i, j: (0, 0)),
                  pl.BlockSpec((d, IN_SECS * SEC), lambda i, j: (0, j))],
        out_specs=pl.BlockSpec((IN_SECS, tm, SEC), lambda i, j: (j, i, 0)),
        out_shape=jax.ShapeDtypeStruct((N_SEC, t, SEC), F32),
        scratch_shapes=[pltpu.VMEM((tm, d), BF16)],
        compiler_params=_cp(("arbitrary", "arbitrary")),
        name="in_proj",
    )(x2, mod_l, norm_g, w_bf)


def _f1_kernel(x_ref, bdc_ref, bds_ref, bdw_ref, z_ref, m_ref, *, scale):
    @pl.when(pl.program_id(1) == 0)
    def _():
        w = bdw_ref[...]
        mc = jnp.dot(bdc_ref[...], w, preferred_element_type=F32, precision=lax.Precision.HIGHEST)
        msn = jnp.dot(bds_ref[...], w, preferred_element_type=F32, precision=lax.Precision.HIGHEST)
        m_ref[:, 0:SEC] = (mc * scale).astype(BF16)
        m_ref[:, SEC:2 * SEC] = (msn * scale).astype(BF16)

    z_ref[...] = jnp.dot(x_ref[...].astype(BF16), m_ref[...], preferred_element_type=F32).astype(BF16)


def _f2_kernel(c_ref, s_ref, z_ref, o_ref):
    z = z_ref[...]
    o_ref[...] = (jnp.dot(c_ref[...], z[:, 0:SEC], preferred_element_type=F32)
                  - jnp.dot(s_ref[...], z[:, SEC:2 * SEC], preferred_element_type=F32))


def _fourier(proj, bdc, bds, bdw, dft_c, dft_s, batch, seq):
    t = batch * seq
    tm = 1024
    scale = 1.0 / math.sqrt(seq * HEAD_DIM)
    z = pl.pallas_call(
        functools.partial(_f1_kernel, scale=scale),
        grid=(2, t // tm),
        in_specs=[pl.BlockSpec((None, tm, SEC), lambda s, i: (SEC_XA + s, i, 0)),
                  pl.BlockSpec((SEC, SEC), lambda s, i: (0, 0)),
                  pl.BlockSpec((SEC, SEC), lambda s, i: (0, 0)),
                  pl.BlockSpec((None, SEC, SEC), lambda s, i: (s, 0, 0))],
        out_specs=pl.BlockSpec((None, tm, 2 * SEC), lambda s, i: (s, i, 0)),
        out_shape=jax.ShapeDtypeStruct((2, t, 2 * SEC), BF16),
        scratch_shapes=[pltpu.VMEM((SEC, 2 * SEC), BF16)],
        compiler_params=_cp(("arbitrary", "arbitrary")),
        name="fnet_chan",
    )(proj, bdc, bds, bdw)
    rt = 512
    return pl.pallas_call(
        _f2_kernel,
        grid=(seq // rt, batch, 2),
        in_specs=[pl.BlockSpec((rt, seq), lambda r, b, s: (r, 0)),
                  pl.BlockSpec((rt, seq), lambda r, b, s: (r, 0)),
                  pl.BlockSpec((None, seq, 2 * SEC), lambda r, b, s: (s, b, 0))],
        out_specs=pl.BlockSpec((None, rt, SEC), lambda r, b, s: (b, r, s)),
        out_shape=jax.ShapeDtypeStruct((batch, seq, GROUP_WIDTH), F32),
        compiler_params=_cp(("arbitrary", "arbitrary", "arbitrary")),
        name="fnet_seq",
    )(dft_c, dft_s, z)


def _sgu_kernel(u_ref, v_ref, g_ref, w_ref, b_ref, o_ref, *, tm):
    gv = [jax.nn.gelu(v_ref[s]) for s in range(2)]
    ms = (jnp.sum(gv[0] * gv[0], axis=-1, keepdims=True)
          + jnp.sum(gv[1] * gv[1], axis=-1, keepdims=True)) * (1.0 / GROUP_WIDTH)
    r = lax.rsqrt(ms + EPS)
    vn = [(gv[s] * r * g_ref[:, s * SEC:(s + 1) * SEC]).astype(BF16) for s in range(2)]
    for s in range(2):
        gu = jax.nn.gelu(u_ref[s])
        for c in range(tm // CHUNK):
            rows = slice(c * CHUNK, (c + 1) * CHUNK)
            for hh in range(4):
                h = s * 4 + hh
                cols = slice(hh * HEAD_DIM, (hh + 1) * HEAD_DIM)
                z = jnp.dot(w_ref[h], vn[s][rows, cols], preferred_element_type=F32) + b_ref[:, h:h + 1]
                o_ref[rows, s * SEC + hh * HEAD_DIM:s * SEC + (hh + 1) * HEAD_DIM] = gu[rows, cols] * z


def _sgu(proj, sgu_norm, sgu_w_bf, sgu_bt):
    t = proj.shape[1]
    tm = 512
    return pl.pallas_call(
        functools.partial(_sgu_kernel, tm=tm),
        grid=(t // tm,),
        in_specs=[pl.BlockSpec((2, tm, SEC), lambda i: (SEC_U // 2, i, 0)),
                  pl.BlockSpec((2, tm, SEC), lambda i: (SEC_V // 2, i, 0)),
                  pl.BlockSpec((1, GROUP_WIDTH), lambda i: (0, 0)),
                  pl.BlockSpec((N_GROUP_HEADS, CHUNK, CHUNK), lambda i: (0, 0, 0)),
                  pl.BlockSpec((CHUNK, N_GROUP_HEADS), lambda i: (0, 0))],
        out_specs=pl.BlockSpec((tm, GROUP_WIDTH), lambda i: (i, 0)),
        out_shape=jax.ShapeDtypeStruct((t, GROUP_WIDTH), F32),
        compiler_params=_cp(("arbitrary",)),
        name="sgu",
    )(proj, proj, sgu_norm, sgu_w_bf, sgu_bt)


ATT_TQ = 128
PAIR = 2 * HEAD_DIM


def _rope(x, cos_t, sin_t):
    n = x.shape[-1]
    half = ROPE_DIM // 2
    lane = lax.broadcasted_iota(jnp.int32, x.shape, 1) % HEAD_DIM
    swapped = jnp.where(lane < half, pltpu.roll(x, n - half, 1), pltpu.roll(x, half, 1))
    return x * cos_t + swapped * sin_t


def _lo_half(shape):
    return lax.broadcasted_iota(jnp.int32, shape, 1) < HEAD_DIM


def _stage_scores(q_pairs, k, valid, s_ref, slot, grp):
    tq = q_pairs[0].shape[0]
    lo = _lo_half(q_pairs[0].shape)
    zero = jnp.zeros_like(q_pairs[0])
    blocks = []
    for q in q_pairs:
        blocks += [jnp.where(lo, q, zero), jnp.where(lo, zero, q)]
    s = _nt(jnp.concatenate(blocks, axis=0), k)
    for i in range(len(blocks)):
        s_ref[slot, grp, i * tq:(i + 1) * tq, :] = jnp.where(valid, s[i * tq:(i + 1) * tq], NEG_INF)


def _stage_softmax(s_ref, p_ref, m_ref, l_ref, slot, grp, tq):
    s = s_ref[slot, grp]
    m = jnp.max(s, axis=-1, keepdims=True)
    p = jnp.exp(s - m)
    p_ref[slot, grp] = p.astype(BF16)
    l = jnp.sum(p, axis=-1, keepdims=True)
    n_pairs = s.shape[0] // (2 * tq)
    shp = (tq, PAIR)
    lo = _lo_half(shp)
    for i in range(n_pairs):
        a, b = slice(2 * i * tq, (2 * i + 1) * tq), slice((2 * i + 1) * tq, (2 * i + 2) * tq)
        m_ref[slot, grp * n_pairs + i] = jnp.where(lo, jnp.broadcast_to(m[a], shp), jnp.broadcast_to(m[b], shp))
        l_ref[slot, grp * n_pairs + i] = jnp.where(lo, jnp.broadcast_to(l[a], shp), jnp.broadcast_to(l[b], shp))


def _stage_pv(p_ref, v, slot, grp, tq):
    o = jnp.dot(p_ref[slot, grp], v, preferred_element_type=F32)
    lo = _lo_half((tq, PAIR))
    return [jnp.where(lo, o[2 * i * tq:(2 * i + 1) * tq], o[(2 * i + 1) * tq:(2 * i + 2) * tq])
            for i in range(o.shape[0] // (2 * tq))]


ATT_ITEMS = 4


def _pipeline(n_items, stage_a, stage_b, stage_c):
    def body(jj, carry):
        for stage in (stage_a, stage_b, stage_c):
            for t in range(ATT_ITEMS):
                stage(jj * ATT_ITEMS + t, t)
        return carry

    lax.fori_loop(0, n_items // ATT_ITEMS, body, 0)


def _band_valid(tq, tk, w, delta):
    d = (lax.broadcasted_iota(jnp.int32, (tq, tk), 1) - lax.broadcasted_iota(jnp.int32, (tq, tk), 0)) + delta
    return (d <= w) & (d >= -w)


def _swa_kernel(q_ref, kv_ref, cos_ref, sin_ref, sink_ref, o_ref, qs_ref, ks_ref, vs_ref,
                s_ref, p_ref, m_ref, l_ref, *, seq):
    w, tq = SWA_HALF_WINDOW, ATT_TQ
    tk = tq + 2 * w
    cos, sin = cos_ref[...], sin_ref[...]
    for s in range(2):
        qs_ref[s] = (_rope(q_ref[s], cos, sin) * (HEAD_DIM ** -0.5)).astype(BF16)
    kk = _rope(kv_ref[:, 0:PAIR], cos[:, 0:PAIR], sin[:, 0:PAIR])
    vv = kv_ref[:, PAIR:2 * PAIR]
    lo = lax.broadcasted_iota(jnp.int32, kk.shape, 1) < HEAD_DIM
    kk_sw, vv_sw = pltpu.roll(kk, HEAD_DIM, 1), pltpu.roll(vv, HEAD_DIM, 1)
    ks_ref[0] = jnp.where(lo, kk, kk_sw).astype(BF16)
    ks_ref[1] = jnp.where(lo, kk_sw, kk).astype(BF16)
    vs_ref[0] = jnp.where(lo, vv, vv_sw).astype(BF16)
    vs_ref[1] = jnp.where(lo, vv_sw, vv).astype(BF16)

    def coords(j):
        q0 = pl.multiple_of(j * tq, tq)
        return q0, pl.multiple_of(jnp.clip(q0 - w, 0, seq - tk), tq)

    def scores(j, slot):
        q0, k0 = coords(j)
        valid = _band_valid(tq, tk, w, k0 - q0)
        for g in range(2):
            q_pairs = [qs_ref[g, pl.ds(q0, tq), 0:PAIR], qs_ref[g, pl.ds(q0, tq), PAIR:2 * PAIR]]
            _stage_scores(q_pairs, ks_ref[g, pl.ds(k0, tk), :], valid, s_ref, slot, g)

    def softmax(j, slot):
        for g in range(2):
            _stage_softmax(s_ref, p_ref, m_ref, l_ref, slot, g, tq)

    def output(j, slot):
        q0, k0 = coords(j)
        for g in range(2):
            outs = _stage_pv(p_ref, vs_ref[g, pl.ds(k0, tk), :], slot, g, tq)
            for i, o in enumerate(outs):
                pair = 2 * g + i
                m, l = m_ref[slot, pair], l_ref[slot, pair]
                sink = sink_ref[:, pair * PAIR:(pair + 1) * PAIR]
                m2 = jnp.maximum(m, sink)
                a = jnp.exp(m - m2)
                o_ref[pl.ds(q0, tq), pair * PAIR:(pair + 1) * PAIR] = o * a / (l * a + jnp.exp(sink - m2))

    _pipeline(seq // tq, scores, softmax, output)


def _swa(proj, cos_q, sin_q, sink_row, batch, seq):
    t = batch * seq
    tab = pl.BlockSpec((None, seq, SEC), lambda b: (b, 0, 0))
    return pl.pallas_call(
        functools.partial(_swa_kernel, seq=seq),
        grid=(batch,),
        in_specs=[pl.BlockSpec((2, seq, SEC), lambda b: (SEC_QC // 2, b, 0)),
                  pl.BlockSpec((None, seq, SEC), lambda b: (SEC_KVC, b, 0)),
                  tab, tab,
                  pl.BlockSpec((1, GROUP_WIDTH), lambda b: (0, 0))],
        out_specs=pl.BlockSpec((seq, GROUP_WIDTH), lambda b: (b, 0)),
        out_shape=jax.ShapeDtypeStruct((t, GROUP_WIDTH), F32),
        scratch_shapes=[pltpu.VMEM((2, seq, SEC), BF16),
                        pltpu.VMEM((2, seq, PAIR), BF16),
                        pltpu.VMEM((2, seq, PAIR), BF16),
                        pltpu.VMEM((ATT_ITEMS, 2, 4 * ATT_TQ, ATT_TQ + 2 * SWA_HALF_WINDOW), F32),
                        pltpu.VMEM((ATT_ITEMS, 2, 4 * ATT_TQ, ATT_TQ + 2 * SWA_HALF_WINDOW), BF16),
                        pltpu.VMEM((ATT_ITEMS, 4, ATT_TQ, PAIR), F32),
                        pltpu.VMEM((ATT_ITEMS, 4, ATT_TQ, PAIR), F32)],
        compiler_params=_cp(("arbitrary",)),
        name="swa",
    )(proj, proj, cos_q, sin_q, sink_row)


def _dil_kernel(q_ref, k_ref, v_ref, cos_ref, sin_ref, o_ref,
                qf_ref, kf_ref, vf_ref, m_ref, l_ref, acc_ref, qd_ref, kd_ref, vd_ref,
                s_ref, p_ref, mt_ref, lt_ref, *, seq):
    cos, sin = cos_ref[...], sin_ref[...]
    q = _rope(q_ref[...], cos, sin) * (HEAD_DIM ** -0.5)
    k = _rope(k_ref[...], cos, sin)
    for pair in range(2):
        lanes = slice(pair * PAIR, (pair + 1) * PAIR)
        qf_ref[pair] = q[:, lanes]
        kf_ref[pair] = k[:, lanes]
        vf_ref[pair] = v_ref[:, lanes]
    m_ref[...] = jnp.full(m_ref.shape, NEG_INF, F32)
    l_ref[...] = jnp.zeros(l_ref.shape, F32)
    acc_ref[...] = jnp.zeros(acc_ref.shape, F32)

    for window, dil in DIL_CONFIGS:
        w = window // (2 * dil)
        ls = seq // dil
        tq = min(ATT_TQ, ls)
        tk = min(tq + 2 * w, ls)

        def rows_of(start, size, dil=dil):
            return pl.ds(start, size) if dil == 1 else pl.ds(start, size, stride=dil)

        def permute(r, carry, ls=ls, rows_of=rows_of):
            dst = pl.ds(pl.multiple_of(r * ls, ls), ls)
            for pair in range(2):
                qd_ref[pair, dst, :] = qf_ref[pair, rows_of(r, ls), :].astype(BF16)
                kd_ref[pair, dst, :] = kf_ref[pair, rows_of(r, ls), :].astype(BF16)
                vd_ref[pair, dst, :] = vf_ref[pair, rows_of(r, ls), :].astype(BF16)
            return carry

        lax.fori_loop(0, dil, permute, 0)
        tiles_per_res = ls // tq

        def coords(j, w=w, ls=ls, tq=tq, tk=tk, tpr=tiles_per_res):
            r = j // tpr
            q0 = (j % tpr) * tq
            k0 = jnp.clip(q0 - w, 0, ls - tk)
            return r, q0, pl.multiple_of(r * ls + k0, w), k0 - q0

        def scores(j, slot, w=w, tq=tq, tk=tk, coords=coords):
            _, _, krow, delta = coords(j)
            valid = _band_valid(tq, tk, w, delta)
            qrow = pl.multiple_of(j * tq, tq)
            for pair in range(2):
                _stage_scores([qd_ref[pair, pl.ds(qrow, tq), :]], kd_ref[pair, pl.ds(krow, tk), :], valid,
                              s_ref.at[:, :, :, 0:tk], slot, pair)

        def softmax(j, slot, tq=tq, tk=tk):
            for pair in range(2):
                _stage_softmax(s_ref.at[:, :, :, 0:tk], p_ref.at[:, :, :, 0:tk], mt_ref, lt_ref, slot, pair, tq)

        def merge(j, slot, tq=tq, tk=tk, dil=dil, coords=coords, rows_of=rows_of):
            r, q0, krow, _ = coords(j)
            rows = rows_of(r + dil * q0, tq)
            for pair in range(2):
                (o,) = _stage_pv(p_ref.at[:, :, :, 0:tk], vd_ref[pair, pl.ds(krow, tk), :], slot, pair, tq)
                m, l = mt_ref[slot, pair], lt_ref[slot, pair]
                m_old = m_ref[pair, rows, :]
                m_new = jnp.maximum(m_old, m)
                a, b = jnp.exp(m_old - m_new), jnp.exp(m - m_new)
                acc_ref[pair, rows, :] = acc_ref[pair, rows, :] * a + o * b
                l_ref[pair, rows, :] = l_ref[pair, rows, :] * a + l * b
                m_ref[pair, rows, :] = m_new

        _pipeline(seq // tq, scores, softmax, merge)

    for pair in range(2):
        o_ref[:, pair * PAIR:(pair + 1) * PAIR] = acc_ref[pair] / l_ref[pair]


def _dilated(proj, cos_q, sin_q, batch, seq):
    t = batch * seq

    def sec(base):
        return pl.BlockSpec((None, seq, SEC), lambda b, g: (base + g, b, 0))

    tab = pl.BlockSpec((None, seq, SEC), lambda b, g: (b, 0, 0))
    return pl.pallas_call(
        functools.partial(_dil_kernel, seq=seq),
        grid=(batch, 2),
        in_specs=[sec(SEC_QD), sec(SEC_KD), sec(SEC_VD), tab, tab],
        out_specs=pl.BlockSpec((seq, SEC), lambda b, g: (b, g)),
        out_shape=jax.ShapeDtypeStruct((t, GROUP_WIDTH), F32),
        scratch_shapes=[pltpu.VMEM((2, seq, PAIR), F32)] * 6 + [pltpu.VMEM((2, seq, PAIR), BF16)] * 3 + [
            pltpu.VMEM((ATT_ITEMS, 2, 2 * ATT_TQ, 2 * ATT_TQ), F32),
            pltpu.VMEM((ATT_ITEMS, 2, 2 * ATT_TQ, 2 * ATT_TQ), BF16),
            pltpu.VMEM((ATT_ITEMS, 2, ATT_TQ, PAIR), F32),
            pltpu.VMEM((ATT_ITEMS, 2, ATT_TQ, PAIR), F32)],
        compiler_params=_cp(("arbitrary", "arbitrary")),
        name="dilated",
    )(proj, proj, proj, cos_q, sin_q)


def _gnorm(y, g):
    return y * lax.rsqrt(jnp.mean(y * y, axis=-1, keepdims=True) + EPS) * g


def _out_kernel(ya_ref, yb_ref, yc_ref, yd_ref, gn_ref, w_ref, x_ref, mod_ref, nf_ref, rcat_ref,
                xn_ref, h_ref, aff_ref):
    ys = (ya_ref, yb_ref, yc_ref, yd_ref)
    gw = GROUP_WIDTH
    acc = None
    for g in range(4):
        yn = _gnorm(ys[g][...], gn_ref[:, g * gw:(g + 1) * gw]).astype(BF16)
        part = jnp.dot(yn, w_ref[g * gw:(g + 1) * gw, :], preferred_element_type=F32)
        acc = part if acc is None else acc + part
    xn = x_ref[...] + mod_ref[2:3, :] * acc
    xn_ref[...] = xn
    h = _gnorm(xn, nf_ref[...]) * (1.0 + mod_ref[4:5, :]) + mod_ref[3:4, :]
    row_tiles = h.shape[1] // LANES
    for c in range(row_tiles):
        h_ref[pl.ds(c, h.shape[0], stride=row_tiles), :] = h[:, c * LANES:(c + 1) * LANES]
    h_hi = h.astype(BF16)
    h_lo = (h - h_hi.astype(F32)).astype(BF16)
    a = _nt(rcat_ref[...], h_hi)
    b = _nt(rcat_ref[0:N_EXPERTS, :], h_lo)
    logits = a[0:N_EXPERTS] + a[N_EXPERTS:2 * N_EXPERTS] + b
    mxl = jnp.max(logits, axis=0, keepdims=True)
    ex = jnp.exp(logits - mxl)
    aff_ref[...] = ex / jnp.sum(ex, axis=0, keepdims=True)


def _out_proj(ya, yb, yc, yd, gn, w_bf, x2, mod_l, nf, rcat, seq):
    t, d = x2.shape
    tm = 256
    tpb = seq // tm
    ROW_TILES = d // LANES
    ysp = pl.BlockSpec((tm, GROUP_WIDTH), lambda i: (i, 0))
    return pl.pallas_call(
        _out_kernel,
        grid=(t // tm,),
        in_specs=[ysp] * 4 + [
            pl.BlockSpec((1, d), lambda i: (0, 0)),
            pl.BlockSpec((d, d), lambda i: (0, 0)),
            pl.BlockSpec((tm, d), lambda i: (i, 0)),
            pl.BlockSpec((None, 6, d), lambda i: (i // tpb, 0, 0)),
            pl.BlockSpec((1, d), lambda i: (0, 0)),
            pl.BlockSpec((2 * N_EXPERTS, d), lambda i: (0, 0))],
        out_specs=[pl.BlockSpec((tm, d), lambda i: (i, 0)),
                   pl.BlockSpec((tm * ROW_TILES, LANES), lambda i: (i, 0)),
                   pl.BlockSpec((N_EXPERTS, tm), lambda i: (0, i))],
        out_shape=[jax.ShapeDtypeStruct((t, d), F32),
                   jax.ShapeDtypeStruct((t * ROW_TILES, LANES), F32),
                   jax.ShapeDtypeStruct((N_EXPERTS, t), F32)],
        compiler_params=_cp(("arbitrary",)),
        name="out_proj",
    )(ya, yb, yc, yd, gn, w_bf, x2, mod_l, nf, rcat)


def _route_kernel(aff_ref, tri_ref, idx_ref, gate_ref, dst_ref, cnt_ref, start_ref,
                  sel_ref, pos_ref, q_ref, *, cap, seq):
    aff = aff_ref[...]

    def as_f32(bits):
        return lax.bitcast_convert_type(bits, F32)

    def step(k, thr):
        cand = thr | lax.shift_left(jnp.int32(1), 30 - k)
        cnt = jnp.sum((aff >= as_f32(cand)).astype(jnp.int32), axis=1, keepdims=True)
        return jnp.where(cnt >= cap, cand, thr)

    thr = lax.fori_loop(0, 31, step, jnp.zeros((aff.shape[0], 1), jnp.int32))
    gt = aff >= as_f32(thr + 1)
    eq = (aff >= as_f32(thr)) & jnp.logical_not(gt)
    need = (cap - jnp.sum(gt.astype(jnp.int32), axis=1, keepdims=True)).astype(F32)
    eq_rank = jnp.dot(eq.astype(BF16), tri_ref[...], preferred_element_type=F32)
    sel = gt | (eq & (eq_rank < need))
    n_e = aff.shape[0]
    sel_f = sel.astype(F32)
    sel_bf = sel.astype(BF16)
    sel_ref[...] = sel_f
    pos_ref[...] = jnp.dot(sel_bf, tri_ref[...], preferred_element_type=F32)

    cnt = jnp.sum(sel_f, axis=0, keepdims=True)
    cnt8 = jnp.broadcast_to(cnt, (8, seq)).astype(BF16)
    start = (jnp.dot(cnt8, tri_ref[...], preferred_element_type=F32)[0:1]
             + (pl.program_id(0) * (n_e * cap)).astype(F32))
    lower = (lax.broadcasted_iota(jnp.int32, (n_e, n_e), 1)
             < lax.broadcasted_iota(jnp.int32, (n_e, n_e), 0)).astype(BF16)
    q_ref[...] = start + jnp.dot(lower, sel_bf, preferred_element_type=F32)
    cnt_ref[...] = cnt
    start_ref[...] = start

    tok = (lax.broadcasted_iota(jnp.int32, (1, seq), 1) + pl.program_id(0) * seq).astype(F32)
    slot = lax.broadcasted_iota(jnp.int32, (cap, seq), 0).astype(F32)

    def per_expert(e, carry):
        oh = (pos_ref[pl.ds(e, 1), :] == slot) & (sel_ref[pl.ds(e, 1), :] > 0.5)

        def pick(row):
            return jnp.broadcast_to(jnp.sum(jnp.where(oh, row, 0.0), axis=1, keepdims=True), (cap, LANES))

        idx_ref[e] = pick(tok).astype(jnp.int32)
        gate_ref[e] = pick(aff_ref[pl.ds(e, 1), :])
        dst_ref[e] = pick(q_ref[pl.ds(e, 1), :]).astype(jnp.int32)
        return carry

    lax.fori_loop(0, n_e, per_expert, 0)


def _route(aff_t, tri, batch, seq, cap):
    e, t = aff_t.shape
    blk = pl.BlockSpec((e, seq), lambda b: (0, b))
    colblk = pl.BlockSpec((e, cap, LANES), lambda b: (0, b, 0))
    rowblk = pl.BlockSpec((1, seq), lambda b: (0, b))
    col_i32 = jax.ShapeDtypeStruct((e, batch * cap, LANES), jnp.int32)
    return pl.pallas_call(
        functools.partial(_route_kernel, cap=cap, seq=seq),
        grid=(batch,),
        in_specs=[blk, pl.BlockSpec((seq, seq), lambda b: (0, 0))],
        out_specs=[colblk, colblk, colblk, rowblk, rowblk],
        out_shape=[col_i32, jax.ShapeDtypeStruct((e, batch * cap, LANES), F32), col_i32,
                   jax.ShapeDtypeStruct((1, t), F32), jax.ShapeDtypeStruct((1, t), F32)],
        scratch_shapes=[pltpu.VMEM((e, seq), F32)] * 3,
        compiler_params=_cp(("arbitrary",)),
        name="route",
    )(aff_t, tri)


def _moe_kernel(idx_ref, dst_ref, h_hbm, gate_ref, wg_ref, wu_ref, wd_ref, sorted_hbm,
                xg_ref, xe_ref, acc_ref, ob_ref, sem, *, rows):
    e = pl.program_id(0)
    f = pl.program_id(1)
    n_e = pl.num_programs(0)

    row_tiles = xe_ref.shape[1] // LANES

    def scatter():
        def body(i, carry):
            for u in range(8):
                r = i * 8 + u
                pltpu.make_async_copy(ob_ref.at[:, r, :], sorted_hbm.at[:, dst_ref[e * rows + r], :],
                                      sem.at[1]).start()
            return carry
        lax.fori_loop(0, rows // 8, body, 0)

    def wait_scatter():
        for c in range(row_tiles):
            pltpu.make_async_copy(ob_ref.at[c], sorted_hbm.at[c, pl.ds(0, rows), :], sem.at[1]).wait()

    def gather(expert):
        def body(i, carry):
            for u in range(8):
                r = i * 8 + u
                tok = idx_ref[expert * rows + r]
                pltpu.make_async_copy(h_hbm.at[pl.ds(pl.multiple_of(tok * row_tiles, row_tiles), row_tiles), :],
                                      xg_ref.at[:, r, :], sem.at[0]).start()
            return carry
        lax.fori_loop(0, rows // 8, body, 0)

    @pl.when((e == 0) & (f == 0))
    def _():
        gather(0)

    @pl.when(f == 0)
    def _():
        for c in range(row_tiles):
            pltpu.make_async_copy(h_hbm.at[pl.ds(0, rows), :], xg_ref.at[c], sem.at[0]).wait()
        for c in range(row_tiles):
            xe_ref[:, c * LANES:(c + 1) * LANES] = xg_ref[c].astype(BF16)
        acc_ref[...] = jnp.zeros_like(acc_ref)

        @pl.when(e + 1 < n_e)
        def _():
            gather(e + 1)

    xe = xe_ref[...]
    a = jnp.dot(xe, wg_ref[...].astype(BF16), preferred_element_type=F32)
    u = jnp.dot(xe, wu_ref[...].astype(BF16), preferred_element_type=F32)
    act = (a * jax.nn.sigmoid(a) * u).astype(BF16)
    acc_ref[...] += jnp.dot(act, wd_ref[...].astype(BF16), preferred_element_type=F32)

    @pl.when(f == pl.num_programs(1) - 1)
    def _():
        @pl.when(e > 0)
        def _():
            wait_scatter()

        gate = gate_ref[:, 0:1]
        for c in range(row_tiles):
            ob_ref[c] = acc_ref[:, c * LANES:(c + 1) * LANES] * gate
        scatter()

        @pl.when(e == n_e - 1)
        def _():
            wait_scatter()


def _moe(idx_flat, dst_flat, gate_col, h2, wg, wu, wd, layer):
    _, ne, d, ff = wg.shape
    rows = gate_col.shape[1]
    row_tiles = d // LANES
    tf = 256
    grid_spec = pltpu.PrefetchScalarGridSpec(
        num_scalar_prefetch=2,
        grid=(ne, ff // tf),
        in_specs=[pl.BlockSpec(memory_space=pl.ANY),
                  pl.BlockSpec((None, rows, LANES), lambda e, f, idx, dst: (e, 0, 0)),
                  pl.BlockSpec((None, None, d, tf), lambda e, f, idx, dst: (layer, e, 0, f)),
                  pl.BlockSpec((None, None, d, tf), lambda e, f, idx, dst: (layer, e, 0, f)),
                  pl.BlockSpec((None, None, tf, d), lambda e, f, idx, dst: (layer, e, f, 0))],
        out_specs=pl.BlockSpec(memory_space=pl.ANY),
        scratch_shapes=[pltpu.VMEM((row_tiles, rows, LANES), F32), pltpu.VMEM((rows, d), BF16),
                        pltpu.VMEM((rows, d), F32), pltpu.VMEM((row_tiles, rows, LANES), F32),
                        pltpu.SemaphoreType.DMA((2,))],
    )
    return pl.pallas_call(
        functools.partial(_moe_kernel, rows=rows),
        grid_spec=grid_spec,
        out_shape=jax.ShapeDtypeStruct((row_tiles, ne * rows, LANES), F32),
        compiler_params=_cp(("arbitrary", "arbitrary")),
        name="moe",
    )(idx_flat, dst_flat, h2, gate_col, wg, wu, wd)


COMBINE_CHUNK = 256
COMBINE_TM = 256


def _combine_kernel(tile_ref, start_ref, cnt_ref, sorted_hbm, xn_ref, mod_ref, fn_ref, out_ref,
                    buf_ref, acc_ref, done_ref, sem, *, tm, total, final_norm):
    i = pl.program_id(0)
    n_tiles = pl.num_programs(0)
    row_tiles = buf_ref.shape[1]
    kc = COMBINE_CHUNK

    def span(tile):
        lo = tile_ref[tile]
        base = lax.shift_left(lax.shift_right_logical(lo, 3), 3)
        n_chunks = jnp.maximum(lax.shift_right_logical(tile_ref[tile + 1] - base + (kc - 1), 8), 1)
        return base, n_chunks

    def chunk_rows(base, j):
        return pl.multiple_of(jnp.minimum(base + j * kc, total - kc), 8)

    def chunk_copy(first_row, slot):
        return pltpu.make_async_copy(sorted_hbm.at[:, pl.ds(first_row, kc), :], buf_ref.at[slot], sem.at[slot])

    base, n_chunks = span(i)

    @pl.when(i == 0)
    def _():
        done_ref[0] = 0
        chunk_copy(chunk_rows(base, 0), 0).start()

    eye = (lax.broadcasted_iota(jnp.int32, (tm, tm), 0) == lax.broadcasted_iota(jnp.int32, (tm, tm), 1))
    lo_col = jnp.sum(jnp.where(eye, start_ref[...], 0.0), axis=1, keepdims=True)
    hi_col = lo_col + jnp.sum(jnp.where(eye, cnt_ref[...], 0.0), axis=1, keepdims=True)
    acc_ref[...] = jnp.zeros_like(acc_ref)

    def chunk(j, carry):
        done = done_ref[0]
        slot = done % 2
        first_row = chunk_rows(base, j)
        chunk_copy(first_row, slot).wait()

        @pl.when(j + 1 < n_chunks)
        def _():
            chunk_copy(chunk_rows(base, j + 1), 1 - slot).start()

        @pl.when((j + 1 == n_chunks) & (i + 1 < n_tiles))
        def _():
            chunk_copy(chunk_rows(span(i + 1)[0], 0), 1 - slot).start()

        row = (first_row + lax.broadcasted_iota(jnp.int32, (1, kc), 1)).astype(F32)
        nominal = (base + j * kc).astype(F32)
        pick = ((row >= lo_col) & (row < hi_col) & (row >= nominal)).astype(BF16)
        for c in range(row_tiles):
            acc_ref[:, c * LANES:(c + 1) * LANES] += jnp.dot(pick, buf_ref[slot, c].astype(BF16),
                                                             preferred_element_type=F32)
        done_ref[0] = done + 1
        return carry

    lax.fori_loop(0, n_chunks, chunk, 0)

    x = xn_ref[...] + mod_ref[5:6, :] * acc_ref[...]
    if final_norm:
        x = _gnorm(x, fn_ref[...])
    out_ref[...] = x


def _combine(tile_start, start_row, cnt_row, sorted_rows, xn, mod_l, fn, seq, final_norm):
    t, d = xn.shape
    row_tiles, total, _ = sorted_rows.shape
    tm = COMBINE_TM
    tpb = seq // tm
    grid_spec = pltpu.PrefetchScalarGridSpec(
        num_scalar_prefetch=1,
        grid=(t // tm,),
        in_specs=[pl.BlockSpec((1, tm), lambda i, ts: (0, i)),
                  pl.BlockSpec((1, tm), lambda i, ts: (0, i)),
                  pl.BlockSpec(memory_space=pl.ANY),
                  pl.BlockSpec((tm, d), lambda i, ts: (i, 0)),
                  pl.BlockSpec((None, 6, d), lambda i, ts: (i // tpb, 0, 0)),
                  pl.BlockSpec((1, d), lambda i, ts: (0, 0))],
        out_specs=pl.BlockSpec((tm, d), lambda i, ts: (i, 0)),
        scratch_shapes=[pltpu.VMEM((2, row_tiles, COMBINE_CHUNK, LANES), F32), pltpu.VMEM((tm, d), F32),
                        pltpu.SMEM((1,), jnp.int32), pltpu.SemaphoreType.DMA((2,))],
    )
    return pl.pallas_call(
        functools.partial(_combine_kernel, tm=tm, total=total, final_norm=final_norm),
        grid_spec=grid_spec,
        out_shape=jax.ShapeDtypeStruct((t, d), F32),
        compiler_params=_cp(("arbitrary",)),
        name="combine",
    )(tile_start, start_row, cnt_row, sorted_rows, xn, mod_l, fn)


def _rope_tables(positions):
    b, s = positions.shape
    inv = jnp.power(jnp.float32(ROPE_THETA), -jnp.arange(0, ROPE_DIM, 2, dtype=F32) / ROPE_DIM)
    ang = positions.astype(F32)[..., None] * inv
    cos, sin = jnp.cos(ang), jnp.sin(ang)
    rest = HEAD_DIM - ROPE_DIM
    c64 = jnp.concatenate([cos, cos, jnp.ones((b, s, rest), F32)], axis=-1)
    s64 = jnp.concatenate([-sin, sin, jnp.zeros((b, s, rest), F32)], axis=-1)
    return jnp.tile(c64, (1, 1, 4)), jnp.tile(s64, (1, 1, 4))


def _dft_tables(n):
    k = jnp.arange(n, dtype=jnp.int32)
    ang = ((k[:, None] * k[None, :]) % n).astype(F32) * (2.0 * math.pi / n)
    return jnp.cos(ang), jnp.sin(ang)


def _dft_tables_split(n):
    k = jnp.arange(n, dtype=jnp.int32)[:, None]
    m0 = jnp.arange(LANES, dtype=jnp.int32)[None, :]
    m1 = jnp.arange(n // LANES, dtype=jnp.int32)[None, :]
    a = ((k * m0) % n).astype(F32) * (2.0 * math.pi / n)
    b = ((k * m1 * LANES) % n).astype(F32) * (2.0 * math.pi / n)
    ca, sa, cb, sb = jnp.cos(a)[:, None, :], jnp.sin(a)[:, None, :], jnp.cos(b)[:, :, None], jnp.sin(b)[:, :, None]
    return (ca * cb - sa * sb).reshape(n, n), (sa * cb + ca * sb).reshape(n, n)


def _block_diag(blocks):
    n, a, _ = blocks.shape
    eye = jnp.eye(n, dtype=blocks.dtype)
    return (eye[:, None, :, None] * blocks[:, :, None, :]).reshape(n * a, n * a)


def kernel(x, c, positions, ada_w, ada_b, norm_mix, w_in, fnet_w, sgu_norm, sgu_w, sgu_b, swa_sink,
           group_norm, w_out, norm_ffn, router_w, exp_w_gate, exp_w_up, exp_w_down, final_norm):
    batch, seq, d = x.shape
    depth = ada_w.shape[0]
    t = batch * seq
    cap = max(1, EC_FACTOR * seq // N_EXPERTS)

    rows = 16
    c_pad = jnp.zeros((rows, d), F32).at[:batch].set(c)
    mod = _ada(c_pad, ada_w, ada_b)[:, :batch].reshape(depth, batch, 6, d)

    cos_q, sin_q = _rope_tables(positions)
    dft_c, dft_s = _dft_tables_split(seq)
    dft_c, dft_s = dft_c.astype(BF16), dft_s.astype(BF16)
    cc, sc = _dft_tables(HEAD_DIM)
    bdc = _block_diag(jnp.broadcast_to(cc, (4, HEAD_DIM, HEAD_DIM)))
    bds = _block_diag(jnp.broadcast_to(sc, (4, HEAD_DIM, HEAD_DIM)))
    tri = (jnp.arange(seq)[:, None] < jnp.arange(seq)[None, :]).astype(BF16)

    x2 = x.reshape(t, d)
    for l in range(depth):
        mod_l = mod[l]
        proj = _in_proj(x2, mod_l, norm_mix[l].reshape(1, d), w_in[l].astype(BF16), seq)

        bdw = jnp.stack([_block_diag(fnet_w[l, 0:4]), _block_diag(fnet_w[l, 4:8])])
        ya = _fourier(proj, bdc, bds, bdw, dft_c, dft_s, batch, seq).reshape(t, GROUP_WIDTH)
        yb = _sgu(proj, sgu_norm[l].reshape(1, GROUP_WIDTH), sgu_w[l].astype(BF16), sgu_b[l].T)
        sink_row = jnp.repeat(swa_sink[l], HEAD_DIM).reshape(1, GROUP_WIDTH)
        yc = _swa(proj, cos_q, sin_q, sink_row, batch, seq)
        yd = _dilated(proj, cos_q, sin_q, batch, seq)

        r_t = router_w[l].T
        r_hi = r_t.astype(BF16)
        r_lo = (r_t - r_hi.astype(F32)).astype(BF16)
        rcat = jnp.concatenate([r_hi, r_lo], axis=0)
        xn, h2, aff_t = _out_proj(ya, yb, yc, yd, group_norm[l].reshape(1, d), w_out[l].astype(BF16),
                                  x2, mod_l, norm_ffn[l].reshape(1, d), rcat, seq)
        idx, gate_col, dst, cnt_row, start_row = _route(aff_t, tri, batch, seq, cap)
        sorted_rows = _moe(idx[:, :, 0].reshape(-1), dst[:, :, 0].reshape(-1), gate_col, h2,
                           exp_w_gate, exp_w_up, exp_w_down, l)
        n_pairs = batch * N_EXPERTS * cap
        tile_start = jnp.concatenate([start_row[0, ::COMBINE_TM].astype(jnp.int32), jnp.full((1,), n_pairs, jnp.int32)])
        x2 = _combine(tile_start, start_row, cnt_row, sorted_rows, xn, mod_l, final_norm.reshape(1, d), seq,
                      l == depth - 1)
    return x2.reshape(batch, seq, d)
```

```python
import functools
import math

import jax
import jax.numpy as jnp
from jax import lax
from jax.experimental import pallas as pl
from jax.experimental.pallas import tpu as pltpu

F32 = jnp.float32
BF16 = jnp.bfloat16

HEAD_DIM = 64
N_GROUP_HEADS = 8
GROUP_WIDTH = 512
CHUNK = 128
SWA_HALF_WINDOW = 128
DIL_CONFIGS = ((128, 1), (512, 4), (2048, 16))
ROPE_THETA = 500000.0
ROPE_DIM = 16
N_EXPERTS = 16
EC_FACTOR = 2
EPS = 1e-6
NEG_INF = -1e30

LANES = 128
SEC = 256
SEC_XA, SEC_U, SEC_V, SEC_QC, SEC_KVC, SEC_QD, SEC_KD, SEC_VD = 0, 2, 4, 6, 8, 9, 11, 13
N_SEC = 15

VMEM_LIMIT = 56 * 1024 * 1024


def _cp(sem, vmem=None):
    return pltpu.CompilerParams(dimension_semantics=sem, vmem_limit_bytes=vmem or VMEM_LIMIT)


def _nt(a, b):
    return lax.dot_general(a, b, (((1,), (1,)), ((), ())), preferred_element_type=F32)


def _ada_kernel(c_ref, w_ref, b_ref, o_ref):
    c = c_ref[...]
    s = (c * jax.nn.sigmoid(c)).astype(BF16)
    o_ref[0] = jnp.dot(s, w_ref[0].astype(BF16), preferred_element_type=F32) + b_ref[0]


def _ada(c_pad, ada_w, ada_b):
    n_layers, d, n6 = ada_w.shape
    rows = c_pad.shape[0]
    tn = 1024
    return pl.pallas_call(
        _ada_kernel,
        grid=(n_layers, n6 // tn),
        in_specs=[pl.BlockSpec((rows, d), lambda l, j: (0, 0)),
                  pl.BlockSpec((1, d, tn), lambda l, j: (l, 0, j)),
                  pl.BlockSpec((1, 1, tn), lambda l, j: (l, 0, j))],
        out_specs=pl.BlockSpec((1, rows, tn), lambda l, j: (l, 0, j)),
        out_shape=jax.ShapeDtypeStruct((n_layers, rows, n6), F32),
        compiler_params=_cp(("arbitrary", "arbitrary")),
        name="ada",
    )(c_pad, ada_w, ada_b.reshape(n_layers, 1, n6))


NORM_ROWS = 16
IN_SECS = 3


def _in_kernel(x_ref, mod_ref, g_ref, w_ref, o_ref, h_ref):
    @pl.when(pl.program_id(1) == 0)
    def _():
        gain = g_ref[...] * (1.0 + mod_ref[1:2, :])
        shift = mod_ref[0:1, :]

        def chunk(c, carry):
            rows = pl.ds(pl.multiple_of(c * NORM_ROWS, NORM_ROWS), NORM_ROWS)
            x = x_ref[rows, :]
            r = lax.rsqrt(jnp.mean(x * x, axis=-1, keepdims=True) + EPS)
            h_ref[rows, :] = (x * r * gain + shift).astype(BF16)
            return carry

        lax.fori_loop(0, x_ref.shape[0] // NORM_ROWS, chunk, 0, unroll=8)

    res = jnp.dot(h_ref[...], w_ref[...], preferred_element_type=F32)
    for s in range(IN_SECS):
        o_ref[s] = res[:, s * SEC:(s + 1) * SEC]


def _in_proj(x2, mod_l, norm_g, w_bf, seq):
    t, d = x2.shape
    tm = 1024
    tpb = seq // tm
    return pl.pallas_call(
        _in_kernel,
        grid=(t // tm, N_SEC // IN_SECS),
        in_specs=[pl.BlockSpec((tm, d), lambda i, j: (i, 0)),
                  pl.BlockSpec((None, 6, d), lambda i, j: (i // tpb, 0, 0)),
                  pl.BlockSpec((1, d), lambda i, j: (0, 0)),
                  pl.BlockSpec((d, IN_SECS * SEC), lambda i, j: (0, j))],
        out_specs=pl.BlockSpec((IN_SECS, tm, SEC), lambda i, j: (j, i, 0)),
        out_shape=jax.ShapeDtypeStruct((N_SEC, t, SEC), F32),
        scratch_shapes=[pltpu.VMEM((tm, d), BF16)],
        compiler_params=_cp(("arbitrary", "arbitrary")),
        name="in_proj",
    )(x2, mod_l, norm_g, w_bf)


def _f1_kernel(x_ref, bdc_ref, bds_ref, bdw_ref, z_ref, m_ref, *, scale):
    @pl.when(pl.program_id(1) == 0)
    def _():
        w = bdw_ref[...]
        mc = jnp.dot(bdc_ref[...], w, preferred_element_type=F32, precision=lax.Precision.HIGHEST)
        msn = jnp.dot(bds_ref[...], w, preferred_element_type=F32, precision=lax.Precision.HIGHEST)
        m_ref[:, 0:SEC] = (mc * scale).astype(BF16)
        m_ref[:, SEC:2 * SEC] = (msn * scale).astype(BF16)

    z_ref[...] = jnp.dot(x_ref[...].astype(BF16), m_ref[...], preferred_element_type=F32).astype(BF16)


def _f2_kernel(c_ref, s_ref, z_ref, o_ref):
    z = z_ref[...]
    o_ref[...] = (jnp.dot(c_ref[...], z[:, 0:SEC], preferred_element_type=F32)
                  - jnp.dot(s_ref[...], z[:, SEC:2 * SEC], preferred_element_type=F32))


def _fourier(proj, bdc, bds, bdw, dft_c, dft_s, batch, seq):
    t = batch * seq
    tm = 1024
    scale = 1.0 / math.sqrt(seq * HEAD_DIM)
    z = pl.pallas_call(
        functools.partial(_f1_kernel, scale=scale),
        grid=(2, t // tm),
        in_specs=[pl.BlockSpec((None, tm, SEC), lambda s, i: (SEC_XA + s, i, 0)),
                  pl.BlockSpec((SEC, SEC), lambda s, i: (0, 0)),
                  pl.BlockSpec((SEC, SEC), lambda s, i: (0, 0)),
                  pl.BlockSpec((None, SEC, SEC), lambda s, i: (s, 0, 0))],
        out_specs=pl.BlockSpec((None, tm, 2 * SEC), lambda s, i: (s, i, 0)),
        out_shape=jax.ShapeDtypeStruct((2, t, 2 * SEC), BF16),
        scratch_shapes=[pltpu.VMEM((SEC, 2 * SEC), BF16)],
        compiler_params=_cp(("arbitrary", "arbitrary")),
        name="fnet_chan",
    )(proj, bdc, bds, bdw)
    rt = 512
    return pl.pallas_call(
        _f2_kernel,
        grid=(seq // rt, batch, 2),
        in_specs=[pl.BlockSpec((rt, seq), lambda r, b, s: (r, 0)),
                  pl.BlockSpec((rt, seq), lambda r, b, s: (r, 0)),
                  pl.BlockSpec((None, seq, 2 * SEC), lambda r, b, s: (s, b, 0))],
        out_specs=pl.BlockSpec((None, rt, SEC), lambda r, b, s: (b, r, s)),
        out_shape=jax.ShapeDtypeStruct((batch, seq, GROUP_WIDTH), F32),
        compiler_params=_cp(("arbitrary", "arbitrary", "arbitrary")),
        name="fnet_seq",
    )(dft_c, dft_s, z)


def _sgu_kernel(u_ref, v_ref, g_ref, w_ref, b_ref, o_ref, *, tm):
    gv = [jax.nn.gelu(v_ref[s]) for s in range(2)]
    ms = (jnp.sum(gv[0] * gv[0], axis=-1, keepdims=True)
          + jnp.sum(gv[1] * gv[1], axis=-1, keepdims=True)) * (1.0 / GROUP_WIDTH)
    r = lax.rsqrt(ms + EPS)
    vn = [(gv[s] * r * g_ref[:, s * SEC:(s + 1) * SEC]).astype(BF16) for s in range(2)]
    lo = lax.broadcasted_iota(jnp.int32, (CHUNK, 2 * HEAD_DIM), 1) < HEAD_DIM
    zero = jnp.zeros((CHUNK, 2 * HEAD_DIM), BF16)
    for s in range(2):
        gu = jax.nn.gelu(u_ref[s])
        for c in range(tm // CHUNK):
            rows = slice(c * CHUNK, (c + 1) * CHUNK)
            for pp in range(2):
                pair = 2 * s + pp
                lanes = slice(pp * 2 * HEAD_DIM, (pp + 1) * 2 * HEAD_DIM)
                vp = vn[s][rows, lanes]
                rhs = jnp.concatenate([jnp.where(lo, vp, zero), jnp.where(lo, zero, vp)], axis=0)
                z = jnp.dot(w_ref[pair], rhs, preferred_element_type=F32)
                bias = jnp.where(lo, b_ref[:, 2 * pair:2 * pair + 1], b_ref[:, 2 * pair + 1:2 * pair + 2])
                o_ref[rows, s * SEC + pp * 2 * HEAD_DIM:s * SEC + (pp + 1) * 2 * HEAD_DIM] = gu[rows, lanes] * (z + bias)


def _sgu(proj, sgu_norm, sgu_w_bf, sgu_bt):
    t = proj.shape[1]
    tm = 512
    return pl.pallas_call(
        functools.partial(_sgu_kernel, tm=tm),
        grid=(t // tm,),
        in_specs=[pl.BlockSpec((2, tm, SEC), lambda i: (SEC_U // 2, i, 0)),
                  pl.BlockSpec((2, tm, SEC), lambda i: (SEC_V // 2, i, 0)),
                  pl.BlockSpec((1, GROUP_WIDTH), lambda i: (0, 0)),
                  pl.BlockSpec((N_GROUP_HEADS // 2, CHUNK, 2 * CHUNK), lambda i: (0, 0, 0)),
                  pl.BlockSpec((CHUNK, N_GROUP_HEADS), lambda i: (0, 0))],
        out_specs=pl.BlockSpec((tm, GROUP_WIDTH), lambda i: (i, 0)),
        out_shape=jax.ShapeDtypeStruct((t, GROUP_WIDTH), F32),
        compiler_params=_cp(("arbitrary",)),
        name="sgu",
    )(proj, proj, sgu_norm, sgu_w_bf, sgu_bt)


ATT_TQ = 128
PAIR = 2 * HEAD_DIM


def _rope(x, cos_t, sin_t):
    n = x.shape[-1]
    half = ROPE_DIM // 2
    lane = lax.broadcasted_iota(jnp.int32, x.shape, 1) % HEAD_DIM
    swapped = jnp.where(lane < half, pltpu.roll(x, n - half, 1), pltpu.roll(x, half, 1))
    return x * cos_t + swapped * sin_t


def _lo_half(shape):
    return lax.broadcasted_iota(jnp.int32, shape, 1) < HEAD_DIM


def _stage_scores(q_pairs, k, valid, s_ref, slot, grp):
    tq = q_pairs[0].shape[0]
    lo = _lo_half(q_pairs[0].shape)
    zero = jnp.zeros_like(q_pairs[0])
    blocks = []
    for q in q_pairs:
        blocks += [jnp.where(lo, q, zero), jnp.where(lo, zero, q)]
    s = _nt(jnp.concatenate(blocks, axis=0), k)
    for i in range(len(blocks)):
        s_ref[slot, grp, i * tq:(i + 1) * tq, :] = jnp.where(valid, s[i * tq:(i + 1) * tq], NEG_INF)


def _stage_softmax(s_ref, p_ref, m_ref, l_ref, slot, grp, tq):
    s = s_ref[slot, grp]
    m = jnp.max(s, axis=-1, keepdims=True)
    p = jnp.exp(s - m)
    p_ref[slot, grp] = p.astype(BF16)
    l = jnp.sum(p, axis=-1, keepdims=True)
    n_pairs = s.shape[0] // (2 * tq)
    shp = (tq, PAIR)
    lo = _lo_half(shp)
    for i in range(n_pairs):
        a, b = slice(2 * i * tq, (2 * i + 1) * tq), slice((2 * i + 1) * tq, (2 * i + 2) * tq)
        m_ref[slot, grp * n_pairs + i] = jnp.where(lo, jnp.broadcast_to(m[a], shp), jnp.broadcast_to(m[b], shp))
        l_ref[slot, grp * n_pairs + i] = jnp.where(lo, jnp.broadcast_to(l[a], shp), jnp.broadcast_to(l[b], shp))


def _stage_pv(p_ref, v, slot, grp, tq):
    o = jnp.dot(p_ref[slot, grp], v, preferred_element_type=F32)
    lo = _lo_half((tq, PAIR))
    return [jnp.where(lo, o[2 * i * tq:(2 * i + 1) * tq], o[(2 * i + 1) * tq:(2 * i + 2) * tq])
            for i in range(o.shape[0] // (2 * tq))]


ATT_ITEMS = 4


def _pipeline(n_items, stage_a, stage_b, stage_c):
    def body(jj, carry):
        for stage in (stage_a, stage_b, stage_c):
            for t in range(ATT_ITEMS):
                stage(jj * ATT_ITEMS + t, t)
        return carry

    lax.fori_loop(0, n_items // ATT_ITEMS, body, 0)


def _band_valid(tq, tk, w, delta):
    d = (lax.broadcasted_iota(jnp.int32, (tq, tk), 1) - lax.broadcasted_iota(jnp.int32, (tq, tk), 0)) + delta
    return (d <= w) & (d >= -w)


def _swa_kernel(q_ref, kv_ref, cos_ref, sin_ref, sink_ref, o_ref, qs_ref, ks_ref, vs_ref,
                s_ref, p_ref, m_ref, l_ref, *, seq):
    w, tq = SWA_HALF_WINDOW, ATT_TQ
    tk = tq + 2 * w
    cos, sin = cos_ref[...], sin_ref[...]
    for s in range(2):
        qs_ref[s] = (_rope(q_ref[s], cos, sin) * (HEAD_DIM ** -0.5)).astype(BF16)
    kk = _rope(kv_ref[:, 0:PAIR], cos[:, 0:PAIR], sin[:, 0:PAIR])
    vv = kv_ref[:, PAIR:2 * PAIR]
    lo = lax.broadcasted_iota(jnp.int32, kk.shape, 1) < HEAD_DIM
    kk_sw, vv_sw = pltpu.roll(kk, HEAD_DIM, 1), pltpu.roll(vv, HEAD_DIM, 1)
    ks_ref[0] = jnp.where(lo, kk, kk_sw).astype(BF16)
    ks_ref[1] = jnp.where(lo, kk_sw, kk).astype(BF16)
    vs_ref[0] = jnp.where(lo, vv, vv_sw).astype(BF16)
    vs_ref[1] = jnp.where(lo, vv_sw, vv).astype(BF16)

    def coords(j):
        q0 = pl.multiple_of(j * tq, tq)
        return q0, pl.multiple_of(jnp.clip(q0 - w, 0, seq - tk), tq)

    def scores(j, slot):
        q0, k0 = coords(j)
        valid = _band_valid(tq, tk, w, k0 - q0)
        for g in range(2):
            q_pairs = [qs_ref[g, pl.ds(q0, tq), 0:PAIR], qs_ref[g, pl.ds(q0, tq), PAIR:2 * PAIR]]
            _stage_scores(q_pairs, ks_ref[g, pl.ds(k0, tk), :], valid, s_ref, slot, g)

    def softmax(j, slot):
        for g in range(2):
            _stage_softmax(s_ref, p_ref, m_ref, l_ref, slot, g, tq)

    def output(j, slot):
        q0, k0 = coords(j)
        for g in range(2):
            outs = _stage_pv(p_ref, vs_ref[g, pl.ds(k0, tk), :], slot, g, tq)
            for i, o in enumerate(outs):
                pair = 2 * g + i
                m, l = m_ref[slot, pair], l_ref[slot, pair]
                sink = sink_ref[:, pair * PAIR:(pair + 1) * PAIR]
                m2 = jnp.maximum(m, sink)
                a = jnp.exp(m - m2)
                o_ref[pl.ds(q0, tq), pair * PAIR:(pair + 1) * PAIR] = o * a / (l * a + jnp.exp(sink - m2))

    _pipeline(seq // tq, scores, softmax, output)


def _swa(proj, cos_q, sin_q, sink_row, batch, seq):
    t = batch * seq
    tab = pl.BlockSpec((None, seq, SEC), lambda b: (b, 0, 0))
    return pl.pallas_call(
        functools.partial(_swa_kernel, seq=seq),
        grid=(batch,),
        in_specs=[pl.BlockSpec((2, seq, SEC), lambda b: (SEC_QC // 2, b, 0)),
                  pl.BlockSpec((None, seq, SEC), lambda b: (SEC_KVC, b, 0)),
                  tab, tab,
                  pl.BlockSpec((1, GROUP_WIDTH), lambda b: (0, 0))],
        out_specs=pl.BlockSpec((seq, GROUP_WIDTH), lambda b: (b, 0)),
        out_shape=jax.ShapeDtypeStruct((t, GROUP_WIDTH), F32),
        scratch_shapes=[pltpu.VMEM((2, seq, SEC), BF16),
                        pltpu.VMEM((2, seq, PAIR), BF16),
                        pltpu.VMEM((2, seq, PAIR), BF16),
                        pltpu.VMEM((ATT_ITEMS, 2, 4 * ATT_TQ, ATT_TQ + 2 * SWA_HALF_WINDOW), F32),
                        pltpu.VMEM((ATT_ITEMS, 2, 4 * ATT_TQ, ATT_TQ + 2 * SWA_HALF_WINDOW), BF16),
                        pltpu.VMEM((ATT_ITEMS, 4, ATT_TQ, PAIR), F32),
                        pltpu.VMEM((ATT_ITEMS, 4, ATT_TQ, PAIR), F32)],
        compiler_params=_cp(("arbitrary",)),
        name="swa",
    )(proj, proj, cos_q, sin_q, sink_row)


def _dil_kernel(q_ref, k_ref, v_ref, cos_ref, sin_ref, o_ref,
                qf_ref, kf_ref, vf_ref, m_ref, l_ref, acc_ref, qd_ref, kd_ref, vd_ref,
                s_ref, p_ref, mt_ref, lt_ref, *, seq):
    cos, sin = cos_ref[...], sin_ref[...]
    q = _rope(q_ref[...], cos, sin) * (HEAD_DIM ** -0.5)
    k = _rope(k_ref[...], cos, sin)
    for pair in range(2):
        lanes = slice(pair * PAIR, (pair + 1) * PAIR)
        qf_ref[pair] = q[:, lanes]
        kf_ref[pair] = k[:, lanes]
        vf_ref[pair] = v_ref[:, lanes]
    m_ref[...] = jnp.full(m_ref.shape, NEG_INF, F32)
    l_ref[...] = jnp.zeros(l_ref.shape, F32)
    acc_ref[...] = jnp.zeros(acc_ref.shape, F32)

    for window, dil in DIL_CONFIGS:
        w = window // (2 * dil)
        ls = seq // dil
        tq = min(ATT_TQ, ls)
        tk = min(tq + 2 * w, ls)

        def rows_of(start, size, dil=dil):
            return pl.ds(start, size) if dil == 1 else pl.ds(start, size, stride=dil)

        def permute(r, carry, ls=ls, rows_of=rows_of):
            dst = pl.ds(pl.multiple_of(r * ls, ls), ls)
            for pair in range(2):
                qd_ref[pair, dst, :] = qf_ref[pair, rows_of(r, ls), :].astype(BF16)
                kd_ref[pair, dst, :] = kf_ref[pair, rows_of(r, ls), :].astype(BF16)
                vd_ref[pair, dst, :] = vf_ref[pair, rows_of(r, ls), :].astype(BF16)
            return carry

        lax.fori_loop(0, dil, permute, 0)
        tiles_per_res = ls // tq

        def coords(j, w=w, ls=ls, tq=tq, tk=tk, tpr=tiles_per_res):
            r = j // tpr
            q0 = (j % tpr) * tq
            k0 = jnp.clip(q0 - w, 0, ls - tk)
            return r, q0, pl.multiple_of(r * ls + k0, w), k0 - q0

        def scores(j, slot, w=w, tq=tq, tk=tk, coords=coords):
            _, _, krow, delta = coords(j)
            valid = _band_valid(tq, tk, w, delta)
            qrow = pl.multiple_of(j * tq, tq)
            for pair in range(2):
                _stage_scores([qd_ref[pair, pl.ds(qrow, tq), :]], kd_ref[pair, pl.ds(krow, tk), :], valid,
                              s_ref.at[:, :, :, 0:tk], slot, pair)

        def softmax(j, slot, tq=tq, tk=tk):
            for pair in range(2):
                _stage_softmax(s_ref.at[:, :, :, 0:tk], p_ref.at[:, :, :, 0:tk], mt_ref, lt_ref, slot, pair, tq)

        def merge(j, slot, tq=tq, tk=tk, dil=dil, coords=coords, rows_of=rows_of):
            r, q0, krow, _ = coords(j)
            rows = rows_of(r + dil * q0, tq)
            for pair in range(2):
                (o,) = _stage_pv(p_ref.at[:, :, :, 0:tk], vd_ref[pair, pl.ds(krow, tk), :], slot, pair, tq)
                m, l = mt_ref[slot, pair], lt_ref[slot, pair]
                m_old = m_ref[pair, rows, :]
                m_new = jnp.maximum(m_old, m)
                a, b = jnp.exp(m_old - m_new), jnp.exp(m - m_new)
                acc_ref[pair, rows, :] = acc_ref[pair, rows, :] * a + o * b
                l_ref[pair, rows, :] = l_ref[pair, rows, :] * a + l * b
                m_ref[pair, rows, :] = m_new

        _pipeline(seq // tq, scores, softmax, merge)

    for pair in range(2):
        o_ref[:, pair * PAIR:(pair + 1) * PAIR] = acc_ref[pair] / l_ref[pair]


def _dilated(proj, cos_q, sin_q, batch, seq):
    t = batch * seq

    def sec(base):
        return pl.BlockSpec((None, seq, SEC), lambda b, g: (base + g, b, 0))

    tab = pl.BlockSpec((None, seq, SEC), lambda b, g: (b, 0, 0))
    return pl.pallas_call(
        functools.partial(_dil_kernel, seq=seq),
        grid=(batch, 2),
        in_specs=[sec(SEC_QD), sec(SEC_KD), sec(SEC_VD), tab, tab],
        out_specs=pl.BlockSpec((seq, SEC), lambda b, g: (b, g)),
        out_shape=jax.ShapeDtypeStruct((t, GROUP_WIDTH), F32),
        scratch_shapes=[pltpu.VMEM((2, seq, PAIR), F32)] * 6 + [pltpu.VMEM((2, seq, PAIR), BF16)] * 3 + [
            pltpu.VMEM((ATT_ITEMS, 2, 2 * ATT_TQ, 2 * ATT_TQ), F32),
            pltpu.VMEM((ATT_ITEMS, 2, 2 * ATT_TQ, 2 * ATT_TQ), BF16),
            pltpu.VMEM((ATT_ITEMS, 2, ATT_TQ, PAIR), F32),
            pltpu.VMEM((ATT_ITEMS, 2, ATT_TQ, PAIR), F32)],
        compiler_params=_cp(("arbitrary", "arbitrary")),
        name="dilated",
    )(proj, proj, proj, cos_q, sin_q)


def _gnorm(y, g):
    return y * lax.rsqrt(jnp.mean(y * y, axis=-1, keepdims=True) + EPS) * g


OUT_SUB = 256


def _out_kernel(ya_ref, yb_ref, yc_ref, yd_ref, gn_ref, w_ref, x_ref, mod_ref, nf_ref, rcat_ref,
                xn_ref, h_ref, aff_ref):
    ys = (ya_ref, yb_ref, yc_ref, yd_ref)
    gw = GROUP_WIDTH
    row_tiles = x_ref.shape[1] // LANES
    for sub in range(x_ref.shape[0] // OUT_SUB):
        rows = slice(sub * OUT_SUB, (sub + 1) * OUT_SUB)
        acc = None
        for g in range(4):
            yn = _gnorm(ys[g][rows, :], gn_ref[:, g * gw:(g + 1) * gw]).astype(BF16)
            part = jnp.dot(yn, w_ref[g * gw:(g + 1) * gw, :], preferred_element_type=F32)
            acc = part if acc is None else acc + part
        xn = x_ref[rows, :] + mod_ref[2:3, :] * acc
        xn_ref[rows, :] = xn
        h = _gnorm(xn, nf_ref[...]) * (1.0 + mod_ref[4:5, :]) + mod_ref[3:4, :]
        for c in range(row_tiles):
            h_ref[pl.ds(sub * OUT_SUB * row_tiles + c, OUT_SUB, stride=row_tiles), :] = h[:, c * LANES:(c + 1) * LANES]
        h_hi = h.astype(BF16)
        h_lo = (h - h_hi.astype(F32)).astype(BF16)
        a = _nt(rcat_ref[...], h_hi)
        b = _nt(rcat_ref[0:N_EXPERTS, :], h_lo)
        logits = a[0:N_EXPERTS] + a[N_EXPERTS:2 * N_EXPERTS] + b
        mxl = jnp.max(logits, axis=0, keepdims=True)
        ex = jnp.exp(logits - mxl)
        aff_ref[:, rows] = ex / jnp.sum(ex, axis=0, keepdims=True)


def _out_proj(ya, yb, yc, yd, gn, w_bf, x2, mod_l, nf, rcat, seq):
    t, d = x2.shape
    tm = 2 * OUT_SUB
    tpb = seq // tm
    ROW_TILES = d // LANES
    ysp = pl.BlockSpec((tm, GROUP_WIDTH), lambda i: (i, 0))
    return pl.pallas_call(
        _out_kernel,
        grid=(t // tm,),
        in_specs=[ysp] * 4 + [
            pl.BlockSpec((1, d), lambda i: (0, 0)),
            pl.BlockSpec((d, d), lambda i: (0, 0)),
            pl.BlockSpec((tm, d), lambda i: (i, 0)),
            pl.BlockSpec((None, 6, d), lambda i: (i // tpb, 0, 0)),
            pl.BlockSpec((1, d), lambda i: (0, 0)),
            pl.BlockSpec((2 * N_EXPERTS, d), lambda i: (0, 0))],
        out_specs=[pl.BlockSpec((tm, d), lambda i: (i, 0)),
                   pl.BlockSpec((tm * ROW_TILES, LANES), lambda i: (i, 0)),
                   pl.BlockSpec((N_EXPERTS, tm), lambda i: (0, i))],
        out_shape=[jax.ShapeDtypeStruct((t, d), F32),
                   jax.ShapeDtypeStruct((t * ROW_TILES, LANES), F32),
                   jax.ShapeDtypeStruct((N_EXPERTS, t), F32)],
        compiler_params=_cp(("arbitrary",)),
        name="out_proj",
    )(ya, yb, yc, yd, gn, w_bf, x2, mod_l, nf, rcat)


def _route_kernel(aff_ref, tri_ref, idx_ref, gate_ref, dst_ref, cnt_ref, start_ref,
                  sel_ref, pos_ref, q_ref, *, cap, seq):
    aff = aff_ref[...]

    def as_f32(bits):
        return lax.bitcast_convert_type(bits, F32)

    def step(k, thr):
        cand = thr | lax.shift_left(jnp.int32(1), 30 - k)
        cnt = jnp.sum((aff >= as_f32(cand)).astype(jnp.int32), axis=1, keepdims=True)
        return jnp.where(cnt >= cap, cand, thr)

    thr = lax.fori_loop(0, 31, step, jnp.zeros((aff.shape[0], 1), jnp.int32))
    gt = aff >= as_f32(thr + 1)
    eq = (aff >= as_f32(thr)) & jnp.logical_not(gt)
    need = (cap - jnp.sum(gt.astype(jnp.int32), axis=1, keepdims=True)).astype(F32)
    eq_rank = jnp.dot(eq.astype(BF16), tri_ref[...], preferred_element_type=F32)
    sel = gt | (eq & (eq_rank < need))
    n_e = aff.shape[0]
    sel_f = sel.astype(F32)
    sel_bf = sel.astype(BF16)
    sel_ref[...] = sel_f
    pos_ref[...] = jnp.dot(sel_bf, tri_ref[...], preferred_element_type=F32)

    cnt = jnp.sum(sel_f, axis=0, keepdims=True)
    cnt8 = jnp.broadcast_to(cnt, (8, seq)).astype(BF16)
    start = (jnp.dot(cnt8, tri_ref[...], preferred_element_type=F32)[0:1]
             + (pl.program_id(0) * (n_e * cap)).astype(F32))
    lower = (lax.broadcasted_iota(jnp.int32, (n_e, n_e), 1)
             < lax.broadcasted_iota(jnp.int32, (n_e, n_e), 0)).astype(BF16)
    q_ref[...] = start + jnp.dot(lower, sel_bf, preferred_element_type=F32)
    cnt_ref[...] = cnt
    start_ref[...] = start

    tok = (lax.broadcasted_iota(jnp.int32, (1, seq), 1) + pl.program_id(0) * seq).astype(F32)
    slot = lax.broadcasted_iota(jnp.int32, (cap, seq), 0).astype(F32)

    def per_expert(e, carry):
        oh = (pos_ref[pl.ds(e, 1), :] == slot) & (sel_ref[pl.ds(e, 1), :] > 0.5)

        def pick(row):
            return jnp.broadcast_to(jnp.sum(jnp.where(oh, row, 0.0), axis=1, keepdims=True), (cap, LANES))

        idx_ref[e] = pick(tok).astype(jnp.int32)
        gate_ref[e] = pick(aff_ref[pl.ds(e, 1), :])
        dst_ref[e] = pick(q_ref[pl.ds(e, 1), :]).astype(jnp.int32)
        return carry

    lax.fori_loop(0, n_e, per_expert, 0)


def _route(aff_t, tri, batch, seq, cap):
    e, t = aff_t.shape
    blk = pl.BlockSpec((e, seq), lambda b: (0, b))
    colblk = pl.BlockSpec((e, cap, LANES), lambda b: (0, b, 0))
    rowblk = pl.BlockSpec((1, seq), lambda b: (0, b))
    col_i32 = jax.ShapeDtypeStruct((e, batch * cap, LANES), jnp.int32)
    return pl.pallas_call(
        functools.partial(_route_kernel, cap=cap, seq=seq),
        grid=(batch,),
        in_specs=[blk, pl.BlockSpec((seq, seq), lambda b: (0, 0))],
        out_specs=[colblk, colblk, colblk, rowblk, rowblk],
        out_shape=[col_i32, jax.ShapeDtypeStruct((e, batch * cap, LANES), F32), col_i32,
                   jax.ShapeDtypeStruct((1, t), F32), jax.ShapeDtypeStruct((1, t), F32)],
        scratch_shapes=[pltpu.VMEM((e, seq), F32)] * 3,
        compiler_params=_cp(("arbitrary",)),
        name="route",
    )(aff_t, tri)


DMA_UNROLL = 32


def _moe_kernel(idx_ref, dst_ref, h_hbm, gate_ref, wg_ref, wu_ref, wd_ref, sorted_hbm,
                xg_ref, xe_ref, acc_ref, ob_ref, sem, *, rows):
    e = pl.program_id(0)
    f = pl.program_id(1)
    n_e = pl.num_programs(0)

    row_tiles = xe_ref.shape[1] // LANES

    def scatter():
        def body(i, carry):
            for u in range(DMA_UNROLL):
                r = i * DMA_UNROLL + u
                pltpu.make_async_copy(ob_ref.at[:, r, :], sorted_hbm.at[:, dst_ref[e * rows + r], :],
                                      sem.at[1]).start()
            return carry
        lax.fori_loop(0, rows // DMA_UNROLL, body, 0)

    def wait_scatter():
        for c in range(row_tiles):
            pltpu.make_async_copy(ob_ref.at[c], sorted_hbm.at[c, pl.ds(0, rows), :], sem.at[1]).wait()

    def gather(expert):
        def body(i, carry):
            for u in range(DMA_UNROLL):
                r = i * DMA_UNROLL + u
                tok = idx_ref[expert * rows + r]
                pltpu.make_async_copy(h_hbm.at[pl.ds(pl.multiple_of(tok * row_tiles, row_tiles), row_tiles), :],
                                      xg_ref.at[:, r, :], sem.at[0]).start()
            return carry
        lax.fori_loop(0, rows // DMA_UNROLL, body, 0)

    @pl.when((e == 0) & (f == 0))
    def _():
        gather(0)

    @pl.when(f == 0)
    def _():
        for c in range(row_tiles):
            pltpu.make_async_copy(h_hbm.at[pl.ds(0, rows), :], xg_ref.at[c], sem.at[0]).wait()
        for c in range(row_tiles):
            xe_ref[:, c * LANES:(c + 1) * LANES] = xg_ref[c].astype(BF16)
        acc_ref[...] = jnp.zeros_like(acc_ref)

        @pl.when(e + 1 < n_e)
        def _():
            gather(e + 1)

    xe = xe_ref[...]
    a = jnp.dot(xe, wg_ref[...].astype(BF16), preferred_element_type=F32)
    u = jnp.dot(xe, wu_ref[...].astype(BF16), preferred_element_type=F32)
    act = (a * jax.nn.sigmoid(a) * u).astype(BF16)
    acc_ref[...] += jnp.dot(act, wd_ref[...].astype(BF16), preferred_element_type=F32)

    @pl.when(f == pl.num_programs(1) - 1)
    def _():
        @pl.when(e > 0)
        def _():
            wait_scatter()

        gate = gate_ref[:, 0:1]
        for c in range(row_tiles):
            ob_ref[c] = acc_ref[:, c * LANES:(c + 1) * LANES] * gate
        scatter()

        @pl.when(e == n_e - 1)
        def _():
            wait_scatter()


def _moe(idx_flat, dst_flat, gate_col, h2, wg, wu, wd, layer):
    _, ne, d, ff = wg.shape
    rows = gate_col.shape[1]
    row_tiles = d // LANES
    tf = 256
    grid_spec = pltpu.PrefetchScalarGridSpec(
        num_scalar_prefetch=2,
        grid=(ne, ff // tf),
        in_specs=[pl.BlockSpec(memory_space=pl.ANY),
                  pl.BlockSpec((None, rows, LANES), lambda e, f, idx, dst: (e, 0, 0)),
                  pl.BlockSpec((None, None, d, tf), lambda e, f, idx, dst: (layer, e, 0, f)),
                  pl.BlockSpec((None, None, d, tf), lambda e, f, idx, dst: (layer, e, 0, f)),
                  pl.BlockSpec((None, None, tf, d), lambda e, f, idx, dst: (layer, e, f, 0))],
        out_specs=pl.BlockSpec(memory_space=pl.ANY),
        scratch_shapes=[pltpu.VMEM((row_tiles, rows, LANES), F32), pltpu.VMEM((rows, d), BF16),
                        pltpu.VMEM((rows, d), F32), pltpu.VMEM((row_tiles, rows, LANES), F32),
                        pltpu.SemaphoreType.DMA((2,))],
    )
    return pl.pallas_call(
        functools.partial(_moe_kernel, rows=rows),
        grid_spec=grid_spec,
        out_shape=jax.ShapeDtypeStruct((row_tiles, ne * rows, LANES), F32),
        compiler_params=_cp(("arbitrary", "arbitrary")),
        name="moe",
    )(idx_flat, dst_flat, h2, gate_col, wg, wu, wd)


COMBINE_CHUNK = 256
COMBINE_TM = 256


def _combine_kernel(tile_ref, start_ref, cnt_ref, sorted_hbm, xn_ref, mod_ref, fn_ref, out_ref,
                    buf_ref, acc_ref, done_ref, sem, *, tm, total, final_norm):
    i = pl.program_id(0)
    n_tiles = pl.num_programs(0)
    row_tiles = buf_ref.shape[1]
    kc = COMBINE_CHUNK

    def span(tile):
        lo = tile_ref[tile]
        base = lax.shift_left(lax.shift_right_logical(lo, 3), 3)
        n_chunks = jnp.maximum(lax.shift_right_logical(tile_ref[tile + 1] - base + (kc - 1), 8), 1)
        return base, n_chunks

    def chunk_rows(base, j):
        return pl.multiple_of(jnp.minimum(base + j * kc, total - kc), 8)

    def chunk_copy(first_row, slot):
        return pltpu.make_async_copy(sorted_hbm.at[:, pl.ds(first_row, kc), :], buf_ref.at[slot], sem.at[slot])

    base, n_chunks = span(i)

    @pl.when(i == 0)
    def _():
        done_ref[0] = 0
        chunk_copy(chunk_rows(base, 0), 0).start()

    eye = (lax.broadcasted_iota(jnp.int32, (tm, tm), 0) == lax.broadcasted_iota(jnp.int32, (tm, tm), 1))
    lo_col = jnp.sum(jnp.where(eye, start_ref[...], 0.0), axis=1, keepdims=True)
    hi_col = lo_col + jnp.sum(jnp.where(eye, cnt_ref[...], 0.0), axis=1, keepdims=True)
    acc_ref[...] = jnp.zeros_like(acc_ref)

    def chunk(j, carry):
        done = done_ref[0]
        slot = done % 2
        first_row = chunk_rows(base, j)
        chunk_copy(first_row, slot).wait()

        @pl.when(j + 1 < n_chunks)
        def _():
            chunk_copy(chunk_rows(base, j + 1), 1 - slot).start()

        @pl.when((j + 1 == n_chunks) & (i + 1 < n_tiles))
        def _():
            chunk_copy(chunk_rows(span(i + 1)[0], 0), 1 - slot).start()

        row = (first_row + lax.broadcasted_iota(jnp.int32, (1, kc), 1)).astype(F32)
        nominal = (base + j * kc).astype(F32)
        pick = ((row >= lo_col) & (row < hi_col) & (row >= nominal)).astype(BF16)
        for c in range(row_tiles):
            acc_ref[:, c * LANES:(c + 1) * LANES] += jnp.dot(pick, buf_ref[slot, c].astype(BF16),
                                                             preferred_element_type=F32)
        done_ref[0] = done + 1
        return carry

    lax.fori_loop(0, n_chunks, chunk, 0)

    x = xn_ref[...] + mod_ref[5:6, :] * acc_ref[...]
    if final_norm:
        x = _gnorm(x, fn_ref[...])
    out_ref[...] = x


def _combine(tile_start, start_row, cnt_row, sorted_rows, xn, mod_l, fn, seq, final_norm):
    t, d = xn.shape
    row_tiles, total, _ = sorted_rows.shape
    tm = COMBINE_TM
    tpb = seq // tm
    grid_spec = pltpu.PrefetchScalarGridSpec(
        num_scalar_prefetch=1,
        grid=(t // tm,),
        in_specs=[pl.BlockSpec((1, tm), lambda i, ts: (0, i)),
                  pl.BlockSpec((1, tm), lambda i, ts: (0, i)),
                  pl.BlockSpec(memory_space=pl.ANY),
                  pl.BlockSpec((tm, d), lambda i, ts: (i, 0)),
                  pl.BlockSpec((None, 6, d), lambda i, ts: (i // tpb, 0, 0)),
                  pl.BlockSpec((1, d), lambda i, ts: (0, 0))],
        out_specs=pl.BlockSpec((tm, d), lambda i, ts: (i, 0)),
        scratch_shapes=[pltpu.VMEM((2, row_tiles, COMBINE_CHUNK, LANES), F32), pltpu.VMEM((tm, d), F32),
                        pltpu.SMEM((1,), jnp.int32), pltpu.SemaphoreType.DMA((2,))],
    )
    return pl.pallas_call(
        functools.partial(_combine_kernel, tm=tm, total=total, final_norm=final_norm),
        grid_spec=grid_spec,
        out_shape=jax.ShapeDtypeStruct((t, d), F32),
        compiler_params=_cp(("arbitrary",)),
        name="combine",
    )(tile_start, start_row, cnt_row, sorted_rows, xn, mod_l, fn)


def _rope_tables(positions):
    b, s = positions.shape
    inv = jnp.power(jnp.float32(ROPE_THETA), -jnp.arange(0, ROPE_DIM, 2, dtype=F32) / ROPE_DIM)
    ang = positions.astype(F32)[..., None] * inv
    cos, sin = jnp.cos(ang), jnp.sin(ang)
    rest = HEAD_DIM - ROPE_DIM
    c64 = jnp.concatenate([cos, cos, jnp.ones((b, s, rest), F32)], axis=-1)
    s64 = jnp.concatenate([-sin, sin, jnp.zeros((b, s, rest), F32)], axis=-1)
    return jnp.tile(c64, (1, 1, 4)), jnp.tile(s64, (1, 1, 4))


def _dft_tables(n):
    k = jnp.arange(n, dtype=jnp.int32)
    ang = ((k[:, None] * k[None, :]) % n).astype(F32) * (2.0 * math.pi / n)
    return jnp.cos(ang), jnp.sin(ang)


def _dft_tables_split(n):
    k = jnp.arange(n, dtype=jnp.int32)[:, None]
    m0 = jnp.arange(LANES, dtype=jnp.int32)[None, :]
    m1 = jnp.arange(n // LANES, dtype=jnp.int32)[None, :]
    a = ((k * m0) % n).astype(F32) * (2.0 * math.pi / n)
    b = ((k * m1 * LANES) % n).astype(F32) * (2.0 * math.pi / n)
    ca, sa, cb, sb = jnp.cos(a)[:, None, :], jnp.sin(a)[:, None, :], jnp.cos(b)[:, :, None], jnp.sin(b)[:, :, None]
    return (ca * cb - sa * sb).reshape(n, n), (sa * cb + ca * sb).reshape(n, n)


def _block_diag(blocks):
    n, a, _ = blocks.shape
    eye = jnp.eye(n, dtype=blocks.dtype)
    return (eye[:, None, :, None] * blocks[:, :, None, :]).reshape(n * a, n * a)


def kernel(x, c, positions, ada_w, ada_b, norm_mix, w_in, fnet_w, sgu_norm, sgu_w, sgu_b, swa_sink,
           group_norm, w_out, norm_ffn, router_w, exp_w_gate, exp_w_up, exp_w_down, final_norm):
    batch, seq, d = x.shape
    depth = ada_w.shape[0]
    t = batch * seq
    cap = max(1, EC_FACTOR * seq // N_EXPERTS)

    rows = 16
    c_pad = jnp.zeros((rows, d), F32).at[:batch].set(c)
    mod = _ada(c_pad, ada_w, ada_b)[:, :batch].reshape(depth, batch, 6, d)

    cos_q, sin_q = _rope_tables(positions)
    dft_c, dft_s = _dft_tables_split(seq)
    dft_c, dft_s = dft_c.astype(BF16), dft_s.astype(BF16)
    cc, sc = _dft_tables(HEAD_DIM)
    bdc = _block_diag(jnp.broadcast_to(cc, (4, HEAD_DIM, HEAD_DIM)))
    bds = _block_diag(jnp.broadcast_to(sc, (4, HEAD_DIM, HEAD_DIM)))
    tri = (jnp.arange(seq)[:, None] < jnp.arange(seq)[None, :]).astype(BF16)

    x2 = x.reshape(t, d)
    for l in range(depth):
        mod_l = mod[l]
        proj = _in_proj(x2, mod_l, norm_mix[l].reshape(1, d), w_in[l].astype(BF16), seq)

        bdw = jnp.stack([_block_diag(fnet_w[l, 0:4]), _block_diag(fnet_w[l, 4:8])])
        ya = _fourier(proj, bdc, bds, bdw, dft_c, dft_s, batch, seq).reshape(t, GROUP_WIDTH)
        sgu_w_pairs = jnp.concatenate([sgu_w[l, 0::2], sgu_w[l, 1::2]], axis=-1).astype(BF16)
        yb = _sgu(proj, sgu_norm[l].reshape(1, GROUP_WIDTH), sgu_w_pairs, sgu_b[l].T)
        sink_row = jnp.repeat(swa_sink[l], HEAD_DIM).reshape(1, GROUP_WIDTH)
        yc = _swa(proj, cos_q, sin_q, sink_row, batch, seq)
        yd = _dilated(proj, cos_q, sin_q, batch, seq)

        r_t = router_w[l].T
        r_hi = r_t.astype(BF16)
        r_lo = (r_t - r_hi.astype(F32)).astype(BF16)
        rcat = jnp.concatenate([r_hi, r_lo], axis=0)
        xn, h2, aff_t = _out_proj(ya, yb, yc, yd, group_norm[l].reshape(1, d), w_out[l].astype(BF16),
                                  x2, mod_l, norm_ffn[l].reshape(1, d), rcat, seq)
        idx, gate_col, dst, cnt_row, start_row = _route(aff_t, tri, batch, seq, cap)
        sorted_rows = _moe(idx[:, :, 0].reshape(-1), dst[:, :, 0].reshape(-1), gate_col, h2,
                           exp_w_gate, exp_w_up, exp_w_down, l)
        n_pairs = batch * N_EXPERTS * cap
        tile_start = jnp.concatenate([start_row[0, ::COMBINE_TM].astype(jnp.int32), jnp.full((1,), n_pairs, jnp.int32)])
        x2 = _combine(tile_start, start_row, cnt_row, sorted_rows, xn, mod_l, final_norm.reshape(1, d), seq,
                      l == depth - 1)
    return x2.reshape(batch, seq, d)
```

```python
import functools
import math

import jax
import jax.numpy as jnp
from jax import lax
from jax.experimental import pallas as pl
from jax.experimental.pallas import tpu as pltpu

F32 = jnp.float32
BF16 = jnp.bfloat16

HEAD_DIM = 64
N_GROUP_HEADS = 8
GROUP_WIDTH = 512
CHUNK = 128
SWA_HALF_WINDOW = 128
DIL_CONFIGS = ((128, 1), (512, 4), (2048, 16))
ROPE_THETA = 500000.0
ROPE_DIM = 16
N_EXPERTS = 16
EC_FACTOR = 2
EPS = 1e-6
NEG_INF = -1e30

LANES = 128
SEC = 256
SEC_XA, SEC_U, SEC_V, SEC_QC, SEC_KVC, SEC_QD, SEC_KD, SEC_VD = 0, 2, 4, 6, 8, 9, 11, 13
N_SEC = 15

VMEM_LIMIT = 56 * 1024 * 1024


def _cp(sem, vmem=None):
    return pltpu.CompilerParams(dimension_semantics=sem, vmem_limit_bytes=vmem or VMEM_LIMIT)


def _nt(a, b):
    return lax.dot_general(a, b, (((1,), (1,)), ((), ())), preferred_element_type=F32)


def _ada_kernel(c_ref, w_ref, b_ref, o_ref):
    @pl.when(pl.program_id(1) == 0)
    def _():
        o_ref[0] = jnp.broadcast_to(b_ref[0], o_ref.shape[1:])

    c = c_ref[...]
    s = (c * jax.nn.sigmoid(c)).astype(BF16)
    o_ref[0] += jnp.dot(s, w_ref[0].astype(BF16), preferred_element_type=F32)


def _ada(c_pad, ada_w, ada_b):
    n_layers, d, n6 = ada_w.shape
    rows = c_pad.shape[0]
    tk = 256
    return pl.pallas_call(
        _ada_kernel,
        grid=(n_layers, d // tk),
        in_specs=[pl.BlockSpec((rows, tk), lambda l, k: (0, k)),
                  pl.BlockSpec((1, tk, n6), lambda l, k: (l, k, 0)),
                  pl.BlockSpec((1, 1, n6), lambda l, k: (l, 0, 0))],
        out_specs=pl.BlockSpec((1, rows, n6), lambda l, k: (l, 0, 0)),
        out_shape=jax.ShapeDtypeStruct((n_layers, rows, n6), F32),
        compiler_params=_cp(("arbitrary", "arbitrary")),
        name="ada",
    )(c_pad, ada_w, ada_b.reshape(n_layers, 1, n6))


NORM_ROWS = 16
IN_SECS = 5


def _in_kernel(x_ref, mod_ref, g_ref, w_ref, o_ref, h_ref):
    @pl.when(pl.program_id(1) == 0)
    def _():
        gain = g_ref[...] * (1.0 + mod_ref[1:2, :])
        shift = mod_ref[0:1, :]

        def chunk(c, carry):
            rows = pl.ds(pl.multiple_of(c * NORM_ROWS, NORM_ROWS), NORM_ROWS)
            x = x_ref[rows, :]
            r = lax.rsqrt(jnp.mean(x * x, axis=-1, keepdims=True) + EPS)
            h_ref[rows, :] = (x * r * gain + shift).astype(BF16)
            return carry

        lax.fori_loop(0, x_ref.shape[0] // NORM_ROWS, chunk, 0, unroll=8)

    res = jnp.dot(h_ref[...], w_ref[...], preferred_element_type=F32)
    for s in range(IN_SECS):
        o_ref[s] = res[:, s * SEC:(s + 1) * SEC]


def _in_proj(x2, mod_l, norm_g, w_bf, seq):
    t, d = x2.shape
    tm = 1024
    tpb = seq // tm
    return pl.pallas_call(
        _in_kernel,
        grid=(t // tm, N_SEC // IN_SECS),
        in_specs=[pl.BlockSpec((tm, d), lambda i, j: (i, 0)),
                  pl.BlockSpec((None, 6, d), lambda i, j: (i // tpb, 0, 0)),
                  pl.BlockSpec((1, d), lambda i, j: (0, 0)),
                  pl.BlockSpec((d, IN_SECS * SEC), lambda i, j: (0, j))],
        out_specs=pl.BlockSpec((IN_SECS, tm, SEC), lambda i, j: (j, i, 0)),
        out_shape=jax.ShapeDtypeStruct((N_SEC, t, SEC), F32),
        scratch_shapes=[pltpu.VMEM((tm, d), BF16)],
        compiler_params=_cp(("arbitrary", "arbitrary")),
        name="in_proj",
    )(x2, mod_l, norm_g, w_bf)


def _f1_kernel(x_ref, bdc_ref, bds_ref, bdw_ref, z_ref, m_ref, *, scale):
    @pl.when(pl.program_id(1) == 0)
    def _():
        w = bdw_ref[...]
        mc = jnp.dot(bdc_ref[...], w, preferred_element_type=F32, precision=lax.Precision.HIGHEST)
        msn = jnp.dot(bds_ref[...], w, preferred_element_type=F32, precision=lax.Precision.HIGHEST)
        m_ref[:, 0:SEC] = (mc * scale).astype(BF16)
        m_ref[:, SEC:2 * SEC] = (msn * scale).astype(BF16)

    z_ref[...] = jnp.dot(x_ref[...].astype(BF16), m_ref[...], preferred_element_type=F32).astype(BF16)


def _f2_kernel(c_ref, s_ref, z_ref, o_ref):
    z = z_ref[...]
    o_ref[...] = (jnp.dot(c_ref[...], z[:, 0:SEC], preferred_element_type=F32)
                  - jnp.dot(s_ref[...], z[:, SEC:2 * SEC], preferred_element_type=F32))


def _fourier(proj, bdc, bds, bdw, dft_c, dft_s, batch, seq):
    t = batch * seq
    tm = 1024
    scale = 1.0 / math.sqrt(seq * HEAD_DIM)
    z = pl.pallas_call(
        functools.partial(_f1_kernel, scale=scale),
        grid=(2, t // tm),
        in_specs=[pl.BlockSpec((None, tm, SEC), lambda s, i: (SEC_XA + s, i, 0)),
                  pl.BlockSpec((SEC, SEC), lambda s, i: (0, 0)),
                  pl.BlockSpec((SEC, SEC), lambda s, i: (0, 0)),
                  pl.BlockSpec((None, SEC, SEC), lambda s, i: (s, 0, 0))],
        out_specs=pl.BlockSpec((None, tm, 2 * SEC), lambda s, i: (s, i, 0)),
        out_shape=jax.ShapeDtypeStruct((2, t, 2 * SEC), BF16),
        scratch_shapes=[pltpu.VMEM((SEC, 2 * SEC), BF16)],
        compiler_params=_cp(("arbitrary", "arbitrary")),
        name="fnet_chan",
    )(proj, bdc, bds, bdw)
    rt = 512
    return pl.pallas_call(
        _f2_kernel,
        grid=(seq // rt, batch, 2),
        in_specs=[pl.BlockSpec((rt, seq), lambda r, b, s: (r, 0)),
                  pl.BlockSpec((rt, seq), lambda r, b, s: (r, 0)),
                  pl.BlockSpec((None, seq, 2 * SEC), lambda r, b, s: (s, b, 0))],
        out_specs=pl.BlockSpec((None, rt, SEC), lambda r, b, s: (b, r, s)),
        out_shape=jax.ShapeDtypeStruct((batch, seq, GROUP_WIDTH), F32),
        compiler_params=_cp(("arbitrary", "arbitrary", "arbitrary")),
        name="fnet_seq",
    )(dft_c, dft_s, z)


def _sgu_kernel(u_ref, v_ref, g_ref, w_ref, b_ref, o_ref, *, tm):
    gv = [jax.nn.gelu(v_ref[s]) for s in range(2)]
    ms = (jnp.sum(gv[0] * gv[0], axis=-1, keepdims=True)
          + jnp.sum(gv[1] * gv[1], axis=-1, keepdims=True)) * (1.0 / GROUP_WIDTH)
    r = lax.rsqrt(ms + EPS)
    vn = [(gv[s] * r * g_ref[:, s * SEC:(s + 1) * SEC]).astype(BF16) for s in range(2)]
    lo = lax.broadcasted_iota(jnp.int32, (CHUNK, 2 * HEAD_DIM), 1) < HEAD_DIM
    zero = jnp.zeros((CHUNK, 2 * HEAD_DIM), BF16)
    for s in range(2):
        gu = jax.nn.gelu(u_ref[s])
        for c in range(tm // CHUNK):
            rows = slice(c * CHUNK, (c + 1) * CHUNK)
            for pp in range(2):
                pair = 2 * s + pp
                lanes = slice(pp * 2 * HEAD_DIM, (pp + 1) * 2 * HEAD_DIM)
                vp = vn[s][rows, lanes]
                rhs = jnp.concatenate([jnp.where(lo, vp, zero), jnp.where(lo, zero, vp)], axis=0)
                z = jnp.dot(w_ref[pair], rhs, preferred_element_type=F32)
                bias = jnp.where(lo, b_ref[:, 2 * pair:2 * pair + 1], b_ref[:, 2 * pair + 1:2 * pair + 2])
                o_ref[rows, s * SEC + pp * 2 * HEAD_DIM:s * SEC + (pp + 1) * 2 * HEAD_DIM] = gu[rows, lanes] * (z + bias)


def _sgu(proj, sgu_norm, sgu_w_bf, sgu_bt):
    t = proj.shape[1]
    tm = 512
    return pl.pallas_call(
        functools.partial(_sgu_kernel, tm=tm),
        grid=(t // tm,),
        in_specs=[pl.BlockSpec((2, tm, SEC), lambda i: (SEC_U // 2, i, 0)),
                  pl.BlockSpec((2, tm, SEC), lambda i: (SEC_V // 2, i, 0)),
                  pl.BlockSpec((1, GROUP_WIDTH), lambda i: (0, 0)),
                  pl.BlockSpec((N_GROUP_HEADS // 2, CHUNK, 2 * CHUNK), lambda i: (0, 0, 0)),
                  pl.BlockSpec((CHUNK, N_GROUP_HEADS), lambda i: (0, 0))],
        out_specs=pl.BlockSpec((tm, GROUP_WIDTH), lambda i: (i, 0)),
        out_shape=jax.ShapeDtypeStruct((t, GROUP_WIDTH), F32),
        compiler_params=_cp(("arbitrary",)),
        name="sgu",
    )(proj, proj, sgu_norm, sgu_w_bf, sgu_bt)


ATT_TQ = 128
PAIR = 2 * HEAD_DIM


def _rope(x, cos_t, sin_t):
    n = x.shape[-1]
    half = ROPE_DIM // 2
    lane = lax.broadcasted_iota(jnp.int32, x.shape, 1) % HEAD_DIM
    swapped = jnp.where(lane < half, pltpu.roll(x, n - half, 1), pltpu.roll(x, half, 1))
    return x * cos_t + swapped * sin_t


def _lo_half(shape):
    return lax.broadcasted_iota(jnp.int32, shape, 1) < HEAD_DIM


def _stage_scores(q_pairs, k, valid, s_ref, slot, grp):
    tq = q_pairs[0].shape[0]
    lo = _lo_half(q_pairs[0].shape)
    zero = jnp.zeros_like(q_pairs[0])
    blocks = []
    for q in q_pairs:
        blocks += [jnp.where(lo, q, zero), jnp.where(lo, zero, q)]
    s = _nt(jnp.concatenate(blocks, axis=0), k)
    for i in range(len(blocks)):
        s_ref[slot, grp, i * tq:(i + 1) * tq, :] = jnp.where(valid, s[i * tq:(i + 1) * tq], NEG_INF)


def _stage_softmax(s_ref, p_ref, m_ref, l_ref, slot, grp, tq):
    s = s_ref[slot, grp]
    m = jnp.max(s, axis=-1, keepdims=True)
    p = jnp.exp(s - m)
    p_ref[slot, grp] = p.astype(BF16)
    l = jnp.sum(p, axis=-1, keepdims=True)
    n_pairs = s.shape[0] // (2 * tq)
    shp = (tq, PAIR)
    lo = _lo_half(shp)
    for i in range(n_pairs):
        a, b = slice(2 * i * tq, (2 * i + 1) * tq), slice((2 * i + 1) * tq, (2 * i + 2) * tq)
        m_ref[slot, grp * n_pairs + i] = jnp.where(lo, jnp.broadcast_to(m[a], shp), jnp.broadcast_to(m[b], shp))
        l_ref[slot, grp * n_pairs + i] = jnp.where(lo, jnp.broadcast_to(l[a], shp), jnp.broadcast_to(l[b], shp))


def _stage_pv(p_ref, v, slot, grp, tq):
    o = jnp.dot(p_ref[slot, grp], v, preferred_element_type=F32)
    lo = _lo_half((tq, PAIR))
    return [jnp.where(lo, o[2 * i * tq:(2 * i + 1) * tq], o[(2 * i + 1) * tq:(2 * i + 2) * tq])
            for i in range(o.shape[0] // (2 * tq))]


ATT_ITEMS = 4


def _pipeline(n_items, stage_a, stage_b, stage_c):
    def body(jj, carry):
        for stage in (stage_a, stage_b, stage_c):
            for t in range(ATT_ITEMS):
                stage(jj * ATT_ITEMS + t, t)
        return carry

    lax.fori_loop(0, n_items // ATT_ITEMS, body, 0)


def _band_valid(tq, tk, w, delta):
    d = (lax.broadcasted_iota(jnp.int32, (tq, tk), 1) - lax.broadcasted_iota(jnp.int32, (tq, tk), 0)) + delta
    return (d <= w) & (d >= -w)


def _swa_kernel(q_ref, kv_ref, cos_ref, sin_ref, sink_ref, o_ref, qs_ref, ks_ref, vs_ref,
                s_ref, p_ref, m_ref, l_ref, *, seq):
    w, tq = SWA_HALF_WINDOW, ATT_TQ
    tk = tq + 2 * w
    cos, sin = cos_ref[...], sin_ref[...]
    for s in range(2):
        qs_ref[s] = (_rope(q_ref[s], cos, sin) * (HEAD_DIM ** -0.5)).astype(BF16)
    kk = _rope(kv_ref[:, 0:PAIR], cos[:, 0:PAIR], sin[:, 0:PAIR])
    vv = kv_ref[:, PAIR:2 * PAIR]
    lo = lax.broadcasted_iota(jnp.int32, kk.shape, 1) < HEAD_DIM
    kk_sw, vv_sw = pltpu.roll(kk, HEAD_DIM, 1), pltpu.roll(vv, HEAD_DIM, 1)
    ks_ref[0] = jnp.where(lo, kk, kk_sw).astype(BF16)
    ks_ref[1] = jnp.where(lo, kk_sw, kk).astype(BF16)
    vs_ref[0] = jnp.where(lo, vv, vv_sw).astype(BF16)
    vs_ref[1] = jnp.where(lo, vv_sw, vv).astype(BF16)

    def coords(j):
        q0 = pl.multiple_of(j * tq, tq)
        return q0, pl.multiple_of(jnp.clip(q0 - w, 0, seq - tk), tq)

    def scores(j, slot):
        q0, k0 = coords(j)
        valid = _band_valid(tq, tk, w, k0 - q0)
        for g in range(2):
            q_pairs = [qs_ref[g, pl.ds(q0, tq), 0:PAIR], qs_ref[g, pl.ds(q0, tq), PAIR:2 * PAIR]]
            _stage_scores(q_pairs, ks_ref[g, pl.ds(k0, tk), :], valid, s_ref, slot, g)

    def softmax(j, slot):
        for g in range(2):
            _stage_softmax(s_ref, p_ref, m_ref, l_ref, slot, g, tq)

    def output(j, slot):
        q0, k0 = coords(j)
        for g in range(2):
            outs = _stage_pv(p_ref, vs_ref[g, pl.ds(k0, tk), :], slot, g, tq)
            for i, o in enumerate(outs):
                pair = 2 * g + i
                m, l = m_ref[slot, pair], l_ref[slot, pair]
                sink = sink_ref[:, pair * PAIR:(pair + 1) * PAIR]
                m2 = jnp.maximum(m, sink)
                a = jnp.exp(m - m2)
                o_ref[pl.ds(q0, tq), pair * PAIR:(pair + 1) * PAIR] = o * a / (l * a + jnp.exp(sink - m2))

    _pipeline(seq // tq, scores, softmax, output)


def _swa(proj, cos_q, sin_q, sink_row, batch, seq):
    t = batch * seq
    tab = pl.BlockSpec((None, seq, SEC), lambda b: (b, 0, 0))
    return pl.pallas_call(
        functools.partial(_swa_kernel, seq=seq),
        grid=(batch,),
        in_specs=[pl.BlockSpec((2, seq, SEC), lambda b: (SEC_QC // 2, b, 0)),
                  pl.BlockSpec((None, seq, SEC), lambda b: (SEC_KVC, b, 0)),
                  tab, tab,
                  pl.BlockSpec((1, GROUP_WIDTH), lambda b: (0, 0))],
        out_specs=pl.BlockSpec((seq, GROUP_WIDTH), lambda b: (b, 0)),
        out_shape=jax.ShapeDtypeStruct((t, GROUP_WIDTH), F32),
        scratch_shapes=[pltpu.VMEM((2, seq, SEC), BF16),
                        pltpu.VMEM((2, seq, PAIR), BF16),
                        pltpu.VMEM((2, seq, PAIR), BF16),
                        pltpu.VMEM((ATT_ITEMS, 2, 4 * ATT_TQ, ATT_TQ + 2 * SWA_HALF_WINDOW), F32),
                        pltpu.VMEM((ATT_ITEMS, 2, 4 * ATT_TQ, ATT_TQ + 2 * SWA_HALF_WINDOW), BF16),
                        pltpu.VMEM((ATT_ITEMS, 4, ATT_TQ, PAIR), F32),
                        pltpu.VMEM((ATT_ITEMS, 4, ATT_TQ, PAIR), F32)],
        compiler_params=_cp(("arbitrary",)),
        name="swa",
    )(proj, proj, cos_q, sin_q, sink_row)


def _dil_kernel(q_ref, k_ref, v_ref, cos_ref, sin_ref, o_ref,
                qf_ref, kf_ref, vf_ref, m_ref, l_ref, acc_ref, qd_ref, kd_ref, vd_ref,
                s_ref, p_ref, mt_ref, lt_ref, *, seq):
    cos, sin = cos_ref[...], sin_ref[...]
    q = _rope(q_ref[...], cos, sin) * (HEAD_DIM ** -0.5)
    k = _rope(k_ref[...], cos, sin)
    for pair in range(2):
        lanes = slice(pair * PAIR, (pair + 1) * PAIR)
        qf_ref[pair] = q[:, lanes]
        kf_ref[pair] = k[:, lanes]
        vf_ref[pair] = v_ref[:, lanes]
    m_ref[...] = jnp.full(m_ref.shape, NEG_INF, F32)
    l_ref[...] = jnp.zeros(l_ref.shape, F32)
    acc_ref[...] = jnp.zeros(acc_ref.shape, F32)

    for window, dil in DIL_CONFIGS:
        w = window // (2 * dil)
        ls = seq // dil
        tq = min(ATT_TQ, ls)
        tk = min(tq + 2 * w, ls)

        def rows_of(start, size, dil=dil):
            return pl.ds(start, size) if dil == 1 else pl.ds(start, size, stride=dil)

        def permute(r, carry, ls=ls, rows_of=rows_of):
            dst = pl.ds(pl.multiple_of(r * ls, ls), ls)
            for pair in range(2):
                qd_ref[pair, dst, :] = qf_ref[pair, rows_of(r, ls), :].astype(BF16)
                kd_ref[pair, dst, :] = kf_ref[pair, rows_of(r, ls), :].astype(BF16)
                vd_ref[pair, dst, :] = vf_ref[pair, rows_of(r, ls), :].astype(BF16)
            return carry

        lax.fori_loop(0, dil, permute, 0)
        tiles_per_res = ls // tq

        def coords(j, w=w, ls=ls, tq=tq, tk=tk, tpr=tiles_per_res):
            r = j // tpr
            q0 = (j % tpr) * tq
            k0 = jnp.clip(q0 - w, 0, ls - tk)
            return r, q0, pl.multiple_of(r * ls + k0, w), k0 - q0

        def scores(j, slot, w=w, tq=tq, tk=tk, coords=coords):
            _, _, krow, delta = coords(j)
            valid = _band_valid(tq, tk, w, delta)
            qrow = pl.multiple_of(j * tq, tq)
            for pair in range(2):
                _stage_scores([qd_ref[pair, pl.ds(qrow, tq), :]], kd_ref[pair, pl.ds(krow, tk), :], valid,
                              s_ref.at[:, :, :, 0:tk], slot, pair)

        def softmax(j, slot, tq=tq, tk=tk):
            for pair in range(2):
                _stage_softmax(s_ref.at[:, :, :, 0:tk], p_ref.at[:, :, :, 0:tk], mt_ref, lt_ref, slot, pair, tq)

        def merge(j, slot, tq=tq, tk=tk, dil=dil, coords=coords, rows_of=rows_of):
            r, q0, krow, _ = coords(j)
            rows = rows_of(r + dil * q0, tq)
            for pair in range(2):
                (o,) = _stage_pv(p_ref.at[:, :, :, 0:tk], vd_ref[pair, pl.ds(krow, tk), :], slot, pair, tq)
                m, l = mt_ref[slot, pair], lt_ref[slot, pair]
                m_old = m_ref[pair, rows, :]
                m_new = jnp.maximum(m_old, m)
                a, b = jnp.exp(m_old - m_new), jnp.exp(m - m_new)
                acc_ref[pair, rows, :] = acc_ref[pair, rows, :] * a + o * b
                l_ref[pair, rows, :] = l_ref[pair, rows, :] * a + l * b
                m_ref[pair, rows, :] = m_new

        _pipeline(seq // tq, scores, softmax, merge)

    for pair in range(2):
        o_ref[:, pair * PAIR:(pair + 1) * PAIR] = acc_ref[pair] / l_ref[pair]


def _dilated(proj, cos_q, sin_q, batch, seq):
    t = batch * seq

    def sec(base):
        return pl.BlockSpec((None, seq, SEC), lambda b, g: (base + g, b, 0))

    tab = pl.BlockSpec((None, seq, SEC), lambda b, g: (b, 0, 0))
    return pl.pallas_call(
        functools.partial(_dil_kernel, seq=seq),
        grid=(batch, 2),
        in_specs=[sec(SEC_QD), sec(SEC_KD), sec(SEC_VD), tab, tab],
        out_specs=pl.BlockSpec((seq, SEC), lambda b, g: (b, g)),
        out_shape=jax.ShapeDtypeStruct((t, GROUP_WIDTH), F32),
        scratch_shapes=[pltpu.VMEM((2, seq, PAIR), F32)] * 6 + [pltpu.VMEM((2, seq, PAIR), BF16)] * 3 + [
            pltpu.VMEM((ATT_ITEMS, 2, 2 * ATT_TQ, 2 * ATT_TQ), F32),
            pltpu.VMEM((ATT_ITEMS, 2, 2 * ATT_TQ, 2 * ATT_TQ), BF16),
            pltpu.VMEM((ATT_ITEMS, 2, ATT_TQ, PAIR), F32),
            pltpu.VMEM((ATT_ITEMS, 2, ATT_TQ, PAIR), F32)],
        compiler_params=_cp(("arbitrary", "arbitrary")),
        name="dilated",
    )(proj, proj, proj, cos_q, sin_q)


def _gnorm(y, g):
    return y * lax.rsqrt(jnp.mean(y * y, axis=-1, keepdims=True) + EPS) * g


OUT_SUB = 256


def _out_kernel(ya_ref, yb_ref, yc_ref, yd_ref, gn_ref, w_ref, x_ref, mod_ref, nf_ref, rcat_ref,
                xn_ref, h_ref, aff_ref):
    ys = (ya_ref, yb_ref, yc_ref, yd_ref)
    gw = GROUP_WIDTH
    row_tiles = x_ref.shape[1] // LANES
    for sub in range(x_ref.shape[0] // OUT_SUB):
        rows = slice(sub * OUT_SUB, (sub + 1) * OUT_SUB)
        acc = None
        for g in range(4):
            yn = _gnorm(ys[g][rows, :], gn_ref[:, g * gw:(g + 1) * gw]).astype(BF16)
            part = jnp.dot(yn, w_ref[g * gw:(g + 1) * gw, :], preferred_element_type=F32)
            acc = part if acc is None else acc + part
        xn = x_ref[rows, :] + mod_ref[2:3, :] * acc
        xn_ref[rows, :] = xn
        h = _gnorm(xn, nf_ref[...]) * (1.0 + mod_ref[4:5, :]) + mod_ref[3:4, :]
        for c in range(row_tiles):
            h_ref[pl.ds(sub * OUT_SUB * row_tiles + c, OUT_SUB, stride=row_tiles), :] = h[:, c * LANES:(c + 1) * LANES]
        h_hi = h.astype(BF16)
        h_lo = (h - h_hi.astype(F32)).astype(BF16)
        a = _nt(rcat_ref[...], h_hi)
        b = _nt(rcat_ref[0:N_EXPERTS, :], h_lo)
        logits = a[0:N_EXPERTS] + a[N_EXPERTS:2 * N_EXPERTS] + b
        mxl = jnp.max(logits, axis=0, keepdims=True)
        ex = jnp.exp(logits - mxl)
        aff_ref[:, rows] = ex / jnp.sum(ex, axis=0, keepdims=True)


def _out_proj(ya, yb, yc, yd, gn, w_bf, x2, mod_l, nf, rcat, seq):
    t, d = x2.shape
    tm = 2 * OUT_SUB
    tpb = seq // tm
    ROW_TILES = d // LANES
    ysp = pl.BlockSpec((tm, GROUP_WIDTH), lambda i: (i, 0))
    return pl.pallas_call(
        _out_kernel,
        grid=(t // tm,),
        in_specs=[ysp] * 4 + [
            pl.BlockSpec((1, d), lambda i: (0, 0)),
            pl.BlockSpec((d, d), lambda i: (0, 0)),
            pl.BlockSpec((tm, d), lambda i: (i, 0)),
            pl.BlockSpec((None, 6, d), lambda i: (i // tpb, 0, 0)),
            pl.BlockSpec((1, d), lambda i: (0, 0)),
            pl.BlockSpec((2 * N_EXPERTS, d), lambda i: (0, 0))],
        out_specs=[pl.BlockSpec((tm, d), lambda i: (i, 0)),
                   pl.BlockSpec((tm * ROW_TILES, LANES), lambda i: (i, 0)),
                   pl.BlockSpec((N_EXPERTS, tm), lambda i: (0, i))],
        out_shape=[jax.ShapeDtypeStruct((t, d), F32),
                   jax.ShapeDtypeStruct((t * ROW_TILES, LANES), F32),
                   jax.ShapeDtypeStruct((N_EXPERTS, t), F32)],
        compiler_params=_cp(("arbitrary",)),
        name="out_proj",
    )(ya, yb, yc, yd, gn, w_bf, x2, mod_l, nf, rcat)


def _route_kernel(aff_ref, tri_ref, idx_ref, gate_ref, dst_ref, cnt_ref, start_ref,
                  sel_ref, pos_ref, q_ref, *, cap, seq):
    aff = aff_ref[...]

    def as_f32(bits):
        return lax.bitcast_convert_type(bits, F32)

    def step(k, thr):
        cand = thr | lax.shift_left(jnp.int32(1), 30 - k)
        cnt = jnp.sum((aff >= as_f32(cand)).astype(jnp.int32), axis=1, keepdims=True)
        return jnp.where(cnt >= cap, cand, thr)

    thr = lax.fori_loop(0, 31, step, jnp.zeros((aff.shape[0], 1), jnp.int32))
    gt = aff >= as_f32(thr + 1)
    eq = (aff >= as_f32(thr)) & jnp.logical_not(gt)
    need = (cap - jnp.sum(gt.astype(jnp.int32), axis=1, keepdims=True)).astype(F32)
    eq_rank = jnp.dot(eq.astype(BF16), tri_ref[...], preferred_element_type=F32)
    sel = gt | (eq & (eq_rank < need))
    n_e = aff.shape[0]
    sel_f = sel.astype(F32)
    sel_bf = sel.astype(BF16)
    sel_ref[...] = sel_f
    pos_ref[...] = jnp.dot(sel_bf, tri_ref[...], preferred_element_type=F32)

    cnt = jnp.sum(sel_f, axis=0, keepdims=True)
    cnt8 = jnp.broadcast_to(cnt, (8, seq)).astype(BF16)
    start = (jnp.dot(cnt8, tri_ref[...], preferred_element_type=F32)[0:1]
             + (pl.program_id(0) * (n_e * cap)).astype(F32))
    lower = (lax.broadcasted_iota(jnp.int32, (n_e, n_e), 1)
             < lax.broadcasted_iota(jnp.int32, (n_e, n_e), 0)).astype(BF16)
    q_ref[...] = start + jnp.dot(lower, sel_bf, preferred_element_type=F32)
    cnt_ref[...] = cnt
    start_ref[...] = start

    tok = (lax.broadcasted_iota(jnp.int32, (1, seq), 1) + pl.program_id(0) * seq).astype(F32)
    slot = lax.broadcasted_iota(jnp.int32, (cap, seq), 0).astype(F32)

    def per_expert(e, carry):
        oh = (pos_ref[pl.ds(e, 1), :] == slot) & (sel_ref[pl.ds(e, 1), :] > 0.5)

        def pick(row):
            return jnp.broadcast_to(jnp.sum(jnp.where(oh, row, 0.0), axis=1, keepdims=True), (cap, LANES))

        idx_ref[e] = pick(tok).astype(jnp.int32)
        gate_ref[e] = pick(aff_ref[pl.ds(e, 1), :])
        dst_ref[e] = pick(q_ref[pl.ds(e, 1), :]).astype(jnp.int32)
        return carry

    lax.fori_loop(0, n_e, per_expert, 0)


def _route(aff_t, tri, batch, seq, cap):
    e, t = aff_t.shape
    blk = pl.BlockSpec((e, seq), lambda b: (0, b))
    colblk = pl.BlockSpec((e, cap, LANES), lambda b: (0, b, 0))
    rowblk = pl.BlockSpec((1, seq), lambda b: (0, b))
    col_i32 = jax.ShapeDtypeStruct((e, batch * cap, LANES), jnp.int32)
    return pl.pallas_call(
        functools.partial(_route_kernel, cap=cap, seq=seq),
        grid=(batch,),
        in_specs=[blk, pl.BlockSpec((seq, seq), lambda b: (0, 0))],
        out_specs=[colblk, colblk, colblk, rowblk, rowblk],
        out_shape=[col_i32, jax.ShapeDtypeStruct((e, batch * cap, LANES), F32), col_i32,
                   jax.ShapeDtypeStruct((1, t), F32), jax.ShapeDtypeStruct((1, t), F32)],
        scratch_shapes=[pltpu.VMEM((e, seq), F32)] * 3,
        compiler_params=_cp(("arbitrary",)),
        name="route",
    )(aff_t, tri)


DMA_UNROLL = 32


def _moe_kernel(idx_ref, dst_ref, h_hbm, gate_ref, wg_ref, wu_ref, wd_ref, sorted_hbm,
                xg_ref, xe_ref, acc_ref, ob_ref, sem, *, rows):
    e = pl.program_id(0)
    f = pl.program_id(1)
    n_e = pl.num_programs(0)

    row_tiles = xe_ref.shape[1] // LANES

    def scatter():
        def body(i, carry):
            for u in range(DMA_UNROLL):
                r = i * DMA_UNROLL + u
                pltpu.make_async_copy(ob_ref.at[:, r, :], sorted_hbm.at[:, dst_ref[e * rows + r], :],
                                      sem.at[1]).start()
            return carry
        lax.fori_loop(0, rows // DMA_UNROLL, body, 0)

    def wait_scatter():
        for c in range(row_tiles):
            pltpu.make_async_copy(ob_ref.at[c], sorted_hbm.at[c, pl.ds(0, rows), :], sem.at[1]).wait()

    def gather(expert):
        def body(i, carry):
            for u in range(DMA_UNROLL):
                r = i * DMA_UNROLL + u
                tok = idx_ref[expert * rows + r]
                pltpu.make_async_copy(h_hbm.at[pl.ds(pl.multiple_of(tok * row_tiles, row_tiles), row_tiles), :],
                                      xg_ref.at[:, r, :], sem.at[0]).start()
            return carry
        lax.fori_loop(0, rows // DMA_UNROLL, body, 0)

    @pl.when((e == 0) & (f == 0))
    def _():
        gather(0)

    @pl.when(f == 0)
    def _():
        for c in range(row_tiles):
            pltpu.make_async_copy(h_hbm.at[pl.ds(0, rows), :], xg_ref.at[c], sem.at[0]).wait()
        for c in range(row_tiles):
            xe_ref[:, c * LANES:(c + 1) * LANES] = xg_ref[c].astype(BF16)
        acc_ref[...] = jnp.zeros_like(acc_ref)

        @pl.when(e + 1 < n_e)
        def _():
            gather(e + 1)

    xe = xe_ref[...]
    a = jnp.dot(xe, wg_ref[...].astype(BF16), preferred_element_type=F32)
    u = jnp.dot(xe, wu_ref[...].astype(BF16), preferred_element_type=F32)
    act = (a * jax.nn.sigmoid(a) * u).astype(BF16)
    acc_ref[...] += jnp.dot(act, wd_ref[...].astype(BF16), preferred_element_type=F32)

    @pl.when(f == pl.num_programs(1) - 1)
    def _():
        @pl.when(e > 0)
        def _():
            wait_scatter()

        gate = gate_ref[:, 0:1]
        for c in range(row_tiles):
            ob_ref[c] = acc_ref[:, c * LANES:(c + 1) * LANES] * gate
        scatter()

        @pl.when(e == n_e - 1)
        def _():
            wait_scatter()


def _moe(idx_flat, dst_flat, gate_col, h2, wg, wu, wd, layer):
    _, ne, d, ff = wg.shape
    rows = gate_col.shape[1]
    row_tiles = d // LANES
    tf = 256
    grid_spec = pltpu.PrefetchScalarGridSpec(
        num_scalar_prefetch=2,
        grid=(ne, ff // tf),
        in_specs=[pl.BlockSpec(memory_space=pl.ANY),
                  pl.BlockSpec((None, rows, LANES), lambda e, f, idx, dst: (e, 0, 0)),
                  pl.BlockSpec((None, None, d, tf), lambda e, f, idx, dst: (layer, e, 0, f)),
                  pl.BlockSpec((None, None, d, tf), lambda e, f, idx, dst: (layer, e, 0, f)),
                  pl.BlockSpec((None, None, tf, d), lambda e, f, idx, dst: (layer, e, f, 0))],
        out_specs=pl.BlockSpec(memory_space=pl.ANY),
        scratch_shapes=[pltpu.VMEM((row_tiles, rows, LANES), F32), pltpu.VMEM((rows, d), BF16),
                        pltpu.VMEM((rows, d), F32), pltpu.VMEM((row_tiles, rows, LANES), F32),
                        pltpu.SemaphoreType.DMA((2,))],
    )
    return pl.pallas_call(
        functools.partial(_moe_kernel, rows=rows),
        grid_spec=grid_spec,
        out_shape=jax.ShapeDtypeStruct((row_tiles, ne * rows, LANES), F32),
        compiler_params=_cp(("arbitrary", "arbitrary")),
        name="moe",
    )(idx_flat, dst_flat, h2, gate_col, wg, wu, wd)


COMBINE_CHUNK = 256
COMBINE_TM = 512


def _combine_kernel(tile_ref, start_ref, cnt_ref, sorted_hbm, xn_ref, mod_ref, fn_ref, out_ref,
                    buf_ref, acc_ref, done_ref, sem, *, tm, total, final_norm):
    i = pl.program_id(0)
    n_tiles = pl.num_programs(0)
    row_tiles = buf_ref.shape[1]
    kc = COMBINE_CHUNK

    def span(tile):
        lo = tile_ref[tile]
        base = lax.shift_left(lax.shift_right_logical(lo, 3), 3)
        n_chunks = jnp.maximum(lax.shift_right_logical(tile_ref[tile + 1] - base + (kc - 1), 8), 1)
        return base, n_chunks

    def chunk_rows(base, j):
        return pl.multiple_of(jnp.minimum(base + j * kc, total - kc), 8)

    def chunk_copy(first_row, slot):
        return pltpu.make_async_copy(sorted_hbm.at[:, pl.ds(first_row, kc), :], buf_ref.at[slot], sem.at[slot])

    base, n_chunks = span(i)

    @pl.when(i == 0)
    def _():
        done_ref[0] = 0
        chunk_copy(chunk_rows(base, 0), 0).start()

    eye = (lax.broadcasted_iota(jnp.int32, (tm, tm), 0) == lax.broadcasted_iota(jnp.int32, (tm, tm), 1))
    lo_col = jnp.sum(jnp.where(eye, start_ref[...], 0.0), axis=1, keepdims=True)
    hi_col = lo_col + jnp.sum(jnp.where(eye, cnt_ref[...], 0.0), axis=1, keepdims=True)
    acc_ref[...] = jnp.zeros_like(acc_ref)

    def chunk(j, carry):
        done = done_ref[0]
        slot = done % 2
        first_row = chunk_rows(base, j)
        chunk_copy(first_row, slot).wait()

        @pl.when(j + 1 < n_chunks)
        def _():
            chunk_copy(chunk_rows(base, j + 1), 1 - slot).start()

        @pl.when((j + 1 == n_chunks) & (i + 1 < n_tiles))
        def _():
            chunk_copy(chunk_rows(span(i + 1)[0], 0), 1 - slot).start()

        row = (first_row + lax.broadcasted_iota(jnp.int32, (1, kc), 1)).astype(F32)
        nominal = (base + j * kc).astype(F32)
        pick = ((row >= lo_col) & (row < hi_col) & (row >= nominal)).astype(BF16)
        for c in range(row_tiles):
            acc_ref[:, c * LANES:(c + 1) * LANES] += jnp.dot(pick, buf_ref[slot, c].astype(BF16),
                                                             preferred_element_type=F32)
        done_ref[0] = done + 1
        return carry

    lax.fori_loop(0, n_chunks, chunk, 0)

    x = xn_ref[...] + mod_ref[5:6, :] * acc_ref[...]
    if final_norm:
        x = _gnorm(x, fn_ref[...])
    out_ref[...] = x


def _combine(tile_start, start_row, cnt_row, sorted_rows, xn, mod_l, fn, seq, final_norm):
    t, d = xn.shape
    row_tiles, total, _ = sorted_rows.shape
    tm = COMBINE_TM
    tpb = seq // tm
    grid_spec = pltpu.PrefetchScalarGridSpec(
        num_scalar_prefetch=1,
        grid=(t // tm,),
        in_specs=[pl.BlockSpec((1, tm), lambda i, ts: (0, i)),
                  pl.BlockSpec((1, tm), lambda i, ts: (0, i)),
                  pl.BlockSpec(memory_space=pl.ANY),
                  pl.BlockSpec((tm, d), lambda i, ts: (i, 0)),
                  pl.BlockSpec((None, 6, d), lambda i, ts: (i // tpb, 0, 0)),
                  pl.BlockSpec((1, d), lambda i, ts: (0, 0))],
        out_specs=pl.BlockSpec((tm, d), lambda i, ts: (i, 0)),
        scratch_shapes=[pltpu.VMEM((2, row_tiles, COMBINE_CHUNK, LANES), F32), pltpu.VMEM((tm, d), F32),
                        pltpu.SMEM((1,), jnp.int32), pltpu.SemaphoreType.DMA((2,))],
    )
    return pl.pallas_call(
        functools.partial(_combine_kernel, tm=tm, total=total, final_norm=final_norm),
        grid_spec=grid_spec,
        out_shape=jax.ShapeDtypeStruct((t, d), F32),
        compiler_params=_cp(("arbitrary",)),
        name="combine",
    )(tile_start, start_row, cnt_row, sorted_rows, xn, mod_l, fn)


def _rope_tables(positions):
    b, s = positions.shape
    inv = jnp.power(jnp.float32(ROPE_THETA), -jnp.arange(0, ROPE_DIM, 2, dtype=F32) / ROPE_DIM)
    ang = positions.astype(F32)[..., None] * inv
    cos, sin = jnp.cos(ang), jnp.sin(ang)
    rest = HEAD_DIM - ROPE_DIM
    c64 = jnp.concatenate([cos, cos, jnp.ones((b, s, rest), F32)], axis=-1)
    s64 = jnp.concatenate([-sin, sin, jnp.zeros((b, s, rest), F32)], axis=-1)
    return jnp.tile(c64, (1, 1, 4)), jnp.tile(s64, (1, 1, 4))


def _dft_tables(n):
    k = jnp.arange(n, dtype=jnp.int32)
    ang = ((k[:, None] * k[None, :]) % n).astype(F32) * (2.0 * math.pi / n)
    return jnp.cos(ang), jnp.sin(ang)


def _dft_tables_split(n):
    k = jnp.arange(n, dtype=jnp.int32)[:, None]
    m0 = jnp.arange(LANES, dtype=jnp.int32)[None, :]
    m1 = jnp.arange(n // LANES, dtype=jnp.int32)[None, :]
    a = ((k * m0) % n).astype(F32) * (2.0 * math.pi / n)
    b = ((k * m1 * LANES) % n).astype(F32) * (2.0 * math.pi / n)
    ca, sa, cb, sb = jnp.cos(a)[:, None, :], jnp.sin(a)[:, None, :], jnp.cos(b)[:, :, None], jnp.sin(b)[:, :, None]
    return (ca * cb - sa * sb).reshape(n, n), (sa * cb + ca * sb).reshape(n, n)


def _block_diag(blocks):
    n, a, _ = blocks.shape
    eye = jnp.eye(n, dtype=blocks.dtype)
    return (eye[:, None, :, None] * blocks[:, :, None, :]).reshape(n * a, n * a)


def kernel(x, c, positions, ada_w, ada_b, norm_mix, w_in, fnet_w, sgu_norm, sgu_w, sgu_b, swa_sink,
           group_norm, w_out, norm_ffn, router_w, exp_w_gate, exp_w_up, exp_w_down, final_norm):
    batch, seq, d = x.shape
    depth = ada_w.shape[0]
    t = batch * seq
    cap = max(1, EC_FACTOR * seq // N_EXPERTS)

    rows = 16
    c_pad = jnp.zeros((rows, d), F32).at[:batch].set(c)
    mod = _ada(c_pad, ada_w, ada_b)[:, :batch].reshape(depth, batch, 6, d)

    cos_q, sin_q = _rope_tables(positions)
    dft_c, dft_s = _dft_tables_split(seq)
    dft_c, dft_s = dft_c.astype(BF16), dft_s.astype(BF16)
    cc, sc = _dft_tables(HEAD_DIM)
    bdc = _block_diag(jnp.broadcast_to(cc, (4, HEAD_DIM, HEAD_DIM)))
    bds = _block_diag(jnp.broadcast_to(sc, (4, HEAD_DIM, HEAD_DIM)))
    tri = (jnp.arange(seq)[:, None] < jnp.arange(seq)[None, :]).astype(BF16)

    x2 = x.reshape(t, d)
    for l in range(depth):
        mod_l = mod[l]
        proj = _in_proj(x2, mod_l, norm_mix[l].reshape(1, d), w_in[l].astype(BF16), seq)

        bdw = jnp.stack([_block_diag(fnet_w[l, 0:4]), _block_diag(fnet_w[l, 4:8])])
        ya = _fourier(proj, bdc, bds, bdw, dft_c, dft_s, batch, seq).reshape(t, GROUP_WIDTH)
        sgu_w_pairs = jnp.concatenate([sgu_w[l, 0::2], sgu_w[l, 1::2]], axis=-1).astype(BF16)
        yb = _sgu(proj, sgu_norm[l].reshape(1, GROUP_WIDTH), sgu_w_pairs, sgu_b[l].T)
        sink_row = jnp.repeat(swa_sink[l], HEAD_DIM).reshape(1, GROUP_WIDTH)
        yc = _swa(proj, cos_q, sin_q, sink_row, batch, seq)
        yd = _dilated(proj, cos_q, sin_q, batch, seq)

        r_t = router_w[l].T
        r_hi = r_t.astype(BF16)
        r_lo = (r_t - r_hi.astype(F32)).astype(BF16)
        rcat = jnp.concatenate([r_hi, r_lo], axis=0)
        xn, h2, aff_t = _out_proj(ya, yb, yc, yd, group_norm[l].reshape(1, d), w_out[l].astype(BF16),
                                  x2, mod_l, norm_ffn[l].reshape(1, d), rcat, seq)
        idx, gate_col, dst, cnt_row, start_row = _route(aff_t, tri, batch, seq, cap)
        sorted_rows = _moe(idx[:, :, 0].reshape(-1), dst[:, :, 0].reshape(-1), gate_col, h2,
                           exp_w_gate, exp_w_up, exp_w_down, l)
        n_pairs = batch * N_EXPERTS * cap
        tile_start = jnp.concatenate([start_row[0, ::COMBINE_TM].astype(jnp.int32), jnp.full((1,), n_pairs, jnp.int32)])
        x2 = _combine(tile_start, start_row, cnt_row, sorted_rows, xn, mod_l, final_norm.reshape(1, d), seq,
                      l == depth - 1)
    return x2.reshape(batch, seq, d)
```

```python
import functools
import math

import jax
import jax.numpy as jnp
from jax import lax
from jax.experimental import pallas as pl
from jax.experimental.pallas import tpu as pltpu

F32 = jnp.float32
BF16 = jnp.bfloat16

HEAD_DIM = 64
N_GROUP_HEADS = 8
GROUP_WIDTH = 512
CHUNK = 128
SWA_HALF_WINDOW = 128
DIL_CONFIGS = ((128, 1), (512, 4), (2048, 16))
ROPE_THETA = 500000.0
ROPE_DIM = 16
N_EXPERTS = 16
EC_FACTOR = 2
EPS = 1e-6
NEG_INF = -1e30

LANES = 128
SEC = 256
SEC_XA, SEC_U, SEC_V, SEC_QC, SEC_KVC, SEC_QD, SEC_KD, SEC_VD = 0, 2, 4, 6, 8, 9, 11, 13
N_SEC = 15

VMEM_LIMIT = 56 * 1024 * 1024


def _cp(sem, vmem=None):
    return pltpu.CompilerParams(dimension_semantics=sem, vmem_limit_bytes=vmem or VMEM_LIMIT)


def _nt(a, b):
    return lax.dot_general(a, b, (((1,), (1,)), ((), ())), preferred_element_type=F32)


ADA_STREAMS = 4


def _ada_kernel(c_ref, *refs):
    w_refs, b_ref, o_ref = refs[:ADA_STREAMS], refs[ADA_STREAMS], refs[ADA_STREAMS + 1]

    @pl.when(pl.program_id(1) == 0)
    def _():
        o_ref[0] = jnp.broadcast_to(b_ref[0], o_ref.shape[1:])

    c = c_ref[...]
    s = (c * jax.nn.sigmoid(c)).astype(BF16)
    tn = w_refs[0].shape[2]
    for j, w_ref in enumerate(w_refs):
        o_ref[0, :, j * tn:(j + 1) * tn] += jnp.dot(s, w_ref[0].astype(BF16), preferred_element_type=F32)


def _ada(c_pad, ada_w, ada_b):
    n_layers, d, n6 = ada_w.shape
    rows = c_pad.shape[0]
    tk = 256
    tn = n6 // ADA_STREAMS
    w_specs = [pl.BlockSpec((1, tk, tn), functools.partial(lambda l, k, j: (l, k, j), j=j))
               for j in range(ADA_STREAMS)]
    return pl.pallas_call(
        _ada_kernel,
        grid=(n_layers, d // tk),
        in_specs=[pl.BlockSpec((rows, tk), lambda l, k: (0, k))] + w_specs + [
            pl.BlockSpec((1, 1, n6), lambda l, k: (l, 0, 0))],
        out_specs=pl.BlockSpec((1, rows, n6), lambda l, k: (l, 0, 0)),
        out_shape=jax.ShapeDtypeStruct((n_layers, rows, n6), F32),
        compiler_params=_cp(("arbitrary", "arbitrary")),
        name="ada",
    )(c_pad, *([ada_w] * ADA_STREAMS), ada_b.reshape(n_layers, 1, n6))


NORM_ROWS = 16
IN_SECS = 5


def _in_kernel(x_ref, mod_ref, g_ref, w_ref, o_ref, h_ref):
    @pl.when(pl.program_id(1) == 0)
    def _():
        gain = g_ref[...] * (1.0 + mod_ref[1:2, :])
        shift = mod_ref[0:1, :]

        def chunk(c, carry):
            rows = pl.ds(pl.multiple_of(c * NORM_ROWS, NORM_ROWS), NORM_ROWS)
            x = x_ref[rows, :]
            r = lax.rsqrt(jnp.mean(x * x, axis=-1, keepdims=True) + EPS)
            h_ref[rows, :] = (x * r * gain + shift).astype(BF16)
            return carry

        lax.fori_loop(0, x_ref.shape[0] // NORM_ROWS, chunk, 0, unroll=8)

    res = jnp.dot(h_ref[...], w_ref[...], preferred_element_type=F32)
    for s in range(IN_SECS):
        o_ref[s] = res[:, s * SEC:(s + 1) * SEC]


def _in_proj(x2, mod_l, norm_g, w_bf, seq):
    t, d = x2.shape
    tm = 1024
    tpb = seq // tm
    return pl.pallas_call(
        _in_kernel,
        grid=(t // tm, N_SEC // IN_SECS),
        in_specs=[pl.BlockSpec((tm, d), lambda i, j: (i, 0)),
                  pl.BlockSpec((None, 6, d), lambda i, j: (i // tpb, 0, 0)),
                  pl.BlockSpec((1, d), lambda i, j: (0, 0)),
                  pl.BlockSpec((d, IN_SECS * SEC), lambda i, j: (0, j))],
        out_specs=pl.BlockSpec((IN_SECS, tm, SEC), lambda i, j: (j, i, 0)),
        out_shape=jax.ShapeDtypeStruct((N_SEC, t, SEC), F32),
        scratch_shapes=[pltpu.VMEM((tm, d), BF16)],
        compiler_params=_cp(("arbitrary", "arbitrary")),
        name="in_proj",
    )(x2, mod_l, norm_g, w_bf)


def _f1_kernel(x_ref, bdc_ref, bds_ref, bdw_ref, z_ref, m_ref, *, scale):
    @pl.when(pl.program_id(1) == 0)
    def _():
        w = bdw_ref[...]
        mc = jnp.dot(bdc_ref[...], w, preferred_element_type=F32, precision=lax.Precision.HIGHEST)
        msn = jnp.dot(bds_ref[...], w, preferred_element_type=F32, precision=lax.Precision.HIGHEST)
        m_ref[:, 0:SEC] = (mc * scale).astype(BF16)
        m_ref[:, SEC:2 * SEC] = (msn * scale).astype(BF16)

    z_ref[...] = jnp.dot(x_ref[...].astype(BF16), m_ref[...], preferred_element_type=F32).astype(BF16)


def _f2_kernel(c_ref, s_ref, z_ref, o_ref):
    z = z_ref[...]
    o_ref[...] = (jnp.dot(c_ref[...], z[:, 0:SEC], preferred_element_type=F32)
                  - jnp.dot(s_ref[...], z[:, SEC:2 * SEC], preferred_element_type=F32))


def _fourier(proj, bdc, bds, bdw, dft_c, dft_s, batch, seq):
    t = batch * seq
    tm = 1024
    scale = 1.0 / math.sqrt(seq * HEAD_DIM)
    z = pl.pallas_call(
        functools.partial(_f1_kernel, scale=scale),
        grid=(2, t // tm),
        in_specs=[pl.BlockSpec((None, tm, SEC), lambda s, i: (SEC_XA + s, i, 0)),
                  pl.BlockSpec((SEC, SEC), lambda s, i: (0, 0)),
                  pl.BlockSpec((SEC, SEC), lambda s, i: (0, 0)),
                  pl.BlockSpec((None, SEC, SEC), lambda s, i: (s, 0, 0))],
        out_specs=pl.BlockSpec((None, tm, 2 * SEC), lambda s, i: (s, i, 0)),
        out_shape=jax.ShapeDtypeStruct((2, t, 2 * SEC), BF16),
        scratch_shapes=[pltpu.VMEM((SEC, 2 * SEC), BF16)],
        compiler_params=_cp(("arbitrary", "arbitrary")),
        name="fnet_chan",
    )(proj, bdc, bds, bdw)
    rt = 512
    return pl.pallas_call(
        _f2_kernel,
        grid=(seq // rt, batch, 2),
        in_specs=[pl.BlockSpec((rt, seq), lambda r, b, s: (r, 0)),
                  pl.BlockSpec((rt, seq), lambda r, b, s: (r, 0)),
                  pl.BlockSpec((None, seq, 2 * SEC), lambda r, b, s: (s, b, 0))],
        out_specs=pl.BlockSpec((None, rt, SEC), lambda r, b, s: (b, r, s)),
        out_shape=jax.ShapeDtypeStruct((batch, seq, GROUP_WIDTH), F32),
        compiler_params=_cp(("arbitrary", "arbitrary", "arbitrary")),
        name="fnet_seq",
    )(dft_c, dft_s, z)


def _sgu_kernel(u_ref, v_ref, g_ref, w_ref, b_ref, o_ref, *, tm):
    gv = [jax.nn.gelu(v_ref[s]) for s in range(2)]
    ms = (jnp.sum(gv[0] * gv[0], axis=-1, keepdims=True)
          + jnp.sum(gv[1] * gv[1], axis=-1, keepdims=True)) * (1.0 / GROUP_WIDTH)
    r = lax.rsqrt(ms + EPS)
    vn = [(gv[s] * r * g_ref[:, s * SEC:(s + 1) * SEC]).astype(BF16) for s in range(2)]
    lo = lax.broadcasted_iota(jnp.int32, (CHUNK, 2 * HEAD_DIM), 1) < HEAD_DIM
    zero = jnp.zeros((CHUNK, 2 * HEAD_DIM), BF16)
    for s in range(2):
        gu = jax.nn.gelu(u_ref[s])
        for c in range(tm // CHUNK):
            rows = slice(c * CHUNK, (c + 1) * CHUNK)
            for pp in range(2):
                pair = 2 * s + pp
                lanes = slice(pp * 2 * HEAD_DIM, (pp + 1) * 2 * HEAD_DIM)
                vp = vn[s][rows, lanes]
                rhs = jnp.concatenate([jnp.where(lo, vp, zero), jnp.where(lo, zero, vp)], axis=0)
                z = jnp.dot(w_ref[pair], rhs, preferred_element_type=F32)
                bias = jnp.where(lo, b_ref[:, 2 * pair:2 * pair + 1], b_ref[:, 2 * pair + 1:2 * pair + 2])
                o_ref[rows, s * SEC + pp * 2 * HEAD_DIM:s * SEC + (pp + 1) * 2 * HEAD_DIM] = gu[rows, lanes] * (z + bias)


def _sgu(proj, sgu_norm, sgu_w_bf, sgu_bt):
    t = proj.shape[1]
    tm = 512
    return pl.pallas_call(
        functools.partial(_sgu_kernel, tm=tm),
        grid=(t // tm,),
        in_specs=[pl.BlockSpec((2, tm, SEC), lambda i: (SEC_U // 2, i, 0)),
                  pl.BlockSpec((2, tm, SEC), lambda i: (SEC_V // 2, i, 0)),
                  pl.BlockSpec((1, GROUP_WIDTH), lambda i: (0, 0)),
                  pl.BlockSpec((N_GROUP_HEADS // 2, CHUNK, 2 * CHUNK), lambda i: (0, 0, 0)),
                  pl.BlockSpec((CHUNK, N_GROUP_HEADS), lambda i: (0, 0))],
        out_specs=pl.BlockSpec((tm, GROUP_WIDTH), lambda i: (i, 0)),
        out_shape=jax.ShapeDtypeStruct((t, GROUP_WIDTH), F32),
        compiler_params=_cp(("arbitrary",)),
        name="sgu",
    )(proj, proj, sgu_norm, sgu_w_bf, sgu_bt)


ATT_TQ = 128
PAIR = 2 * HEAD_DIM


def _rope(x, cos_t, sin_t):
    n = x.shape[-1]
    half = ROPE_DIM // 2
    lane = lax.broadcasted_iota(jnp.int32, x.shape, 1) % HEAD_DIM
    swapped = jnp.where(lane < half, pltpu.roll(x, n - half, 1), pltpu.roll(x, half, 1))
    return x * cos_t + swapped * sin_t


def _lo_half(shape):
    return lax.broadcasted_iota(jnp.int32, shape, 1) < HEAD_DIM


def _stage_scores(q_pairs, k, valid, s_ref, slot, grp):
    tq = q_pairs[0].shape[0]
    lo = _lo_half(q_pairs[0].shape)
    zero = jnp.zeros_like(q_pairs[0])
    blocks = []
    for q in q_pairs:
        blocks += [jnp.where(lo, q, zero), jnp.where(lo, zero, q)]
    s = _nt(jnp.concatenate(blocks, axis=0), k)
    for i in range(len(blocks)):
        s_ref[slot, grp, i * tq:(i + 1) * tq, :] = jnp.where(valid, s[i * tq:(i + 1) * tq], NEG_INF)


def _stage_softmax(s_ref, p_ref, m_ref, l_ref, slot, grp, tq):
    s = s_ref[slot, grp]
    m = jnp.max(s, axis=-1, keepdims=True)
    p = jnp.exp(s - m)
    p_ref[slot, grp] = p.astype(BF16)
    l = jnp.sum(p, axis=-1, keepdims=True)
    n_pairs = s.shape[0] // (2 * tq)
    shp = (tq, PAIR)
    lo = _lo_half(shp)
    for i in range(n_pairs):
        a, b = slice(2 * i * tq, (2 * i + 1) * tq), slice((2 * i + 1) * tq, (2 * i + 2) * tq)
        m_ref[slot, grp * n_pairs + i] = jnp.where(lo, jnp.broadcast_to(m[a], shp), jnp.broadcast_to(m[b], shp))
        l_ref[slot, grp * n_pairs + i] = jnp.where(lo, jnp.broadcast_to(l[a], shp), jnp.broadcast_to(l[b], shp))


def _stage_pv(p_ref, v, slot, grp, tq):
    o = jnp.dot(p_ref[slot, grp], v, preferred_element_type=F32)
    lo = _lo_half((tq, PAIR))
    return [jnp.where(lo, o[2 * i * tq:(2 * i + 1) * tq], o[(2 * i + 1) * tq:(2 * i + 2) * tq])
            for i in range(o.shape[0] // (2 * tq))]


ATT_ITEMS = 4


def _pipeline(n_items, stage_a, stage_b, stage_c):
    def body(jj, carry):
        for stage in (stage_a, stage_b, stage_c):
            for t in range(ATT_ITEMS):
                stage(jj * ATT_ITEMS + t, t)
        return carry

    lax.fori_loop(0, n_items // ATT_ITEMS, body, 0)


def _band_valid(tq, tk, w, delta):
    d = (lax.broadcasted_iota(jnp.int32, (tq, tk), 1) - lax.broadcasted_iota(jnp.int32, (tq, tk), 0)) + delta
    return (d <= w) & (d >= -w)


def _swa_kernel(q_ref, kv_ref, cos_ref, sin_ref, sink_ref, o_ref, qs_ref, ks_ref, vs_ref,
                s_ref, p_ref, m_ref, l_ref, *, seq):
    w, tq = SWA_HALF_WINDOW, ATT_TQ
    tk = tq + 2 * w
    cos, sin = cos_ref[...], sin_ref[...]
    for s in range(2):
        qs_ref[s] = (_rope(q_ref[s], cos, sin) * (HEAD_DIM ** -0.5)).astype(BF16)
    kk = _rope(kv_ref[:, 0:PAIR], cos[:, 0:PAIR], sin[:, 0:PAIR])
    vv = kv_ref[:, PAIR:2 * PAIR]
    lo = lax.broadcasted_iota(jnp.int32, kk.shape, 1) < HEAD_DIM
    kk_sw, vv_sw = pltpu.roll(kk, HEAD_DIM, 1), pltpu.roll(vv, HEAD_DIM, 1)
    ks_ref[0] = jnp.where(lo, kk, kk_sw).astype(BF16)
    ks_ref[1] = jnp.where(lo, kk_sw, kk).astype(BF16)
    vs_ref[0] = jnp.where(lo, vv, vv_sw).astype(BF16)
    vs_ref[1] = jnp.where(lo, vv_sw, vv).astype(BF16)

    def coords(j):
        q0 = pl.multiple_of(j * tq, tq)
        return q0, pl.multiple_of(jnp.clip(q0 - w, 0, seq - tk), tq)

    def scores(j, slot):
        q0, k0 = coords(j)
        valid = _band_valid(tq, tk, w, k0 - q0)
        for g in range(2):
            q_pairs = [qs_ref[g, pl.ds(q0, tq), 0:PAIR], qs_ref[g, pl.ds(q0, tq), PAIR:2 * PAIR]]
            _stage_scores(q_pairs, ks_ref[g, pl.ds(k0, tk), :], valid, s_ref, slot, g)

    def softmax(j, slot):
        for g in range(2):
            _stage_softmax(s_ref, p_ref, m_ref, l_ref, slot, g, tq)

    def output(j, slot):
        q0, k0 = coords(j)
        for g in range(2):
            outs = _stage_pv(p_ref, vs_ref[g, pl.ds(k0, tk), :], slot, g, tq)
            for i, o in enumerate(outs):
                pair = 2 * g + i
                m, l = m_ref[slot, pair], l_ref[slot, pair]
                sink = sink_ref[:, pair * PAIR:(pair + 1) * PAIR]
                m2 = jnp.maximum(m, sink)
                a = jnp.exp(m - m2)
                o_ref[pl.ds(q0, tq), pair * PAIR:(pair + 1) * PAIR] = o * a / (l * a + jnp.exp(sink - m2))

    _pipeline(seq // tq, scores, softmax, output)


def _swa(proj, cos_q, sin_q, sink_row, batch, seq):
    t = batch * seq
    tab = pl.BlockSpec((None, seq, SEC), lambda b: (b, 0, 0))
    return pl.pallas_call(
        functools.partial(_swa_kernel, seq=seq),
        grid=(batch,),
        in_specs=[pl.BlockSpec((2, seq, SEC), lambda b: (SEC_QC // 2, b, 0)),
                  pl.BlockSpec((None, seq, SEC), lambda b: (SEC_KVC, b, 0)),
                  tab, tab,
                  pl.BlockSpec((1, GROUP_WIDTH), lambda b: (0, 0))],
        out_specs=pl.BlockSpec((seq, GROUP_WIDTH), lambda b: (b, 0)),
        out_shape=jax.ShapeDtypeStruct((t, GROUP_WIDTH), F32),
        scratch_shapes=[pltpu.VMEM((2, seq, SEC), BF16),
                        pltpu.VMEM((2, seq, PAIR), BF16),
                        pltpu.VMEM((2, seq, PAIR), BF16),
                        pltpu.VMEM((ATT_ITEMS, 2, 4 * ATT_TQ, ATT_TQ + 2 * SWA_HALF_WINDOW), F32),
                        pltpu.VMEM((ATT_ITEMS, 2, 4 * ATT_TQ, ATT_TQ + 2 * SWA_HALF_WINDOW), BF16),
                        pltpu.VMEM((ATT_ITEMS, 4, ATT_TQ, PAIR), F32),
                        pltpu.VMEM((ATT_ITEMS, 4, ATT_TQ, PAIR), F32)],
        compiler_params=_cp(("arbitrary",)),
        name="swa",
    )(proj, proj, cos_q, sin_q, sink_row)


def _dil_kernel(q_ref, k_ref, v_ref, cos_ref, sin_ref, o_ref,
                qf_ref, kf_ref, vf_ref, m_ref, l_ref, acc_ref, qd_ref, kd_ref, vd_ref,
                s_ref, p_ref, mt_ref, lt_ref, *, seq):
    cos, sin = cos_ref[...], sin_ref[...]
    q = _rope(q_ref[...], cos, sin) * (HEAD_DIM ** -0.5)
    k = _rope(k_ref[...], cos, sin)
    for pair in range(2):
        lanes = slice(pair * PAIR, (pair + 1) * PAIR)
        qf_ref[pair] = q[:, lanes]
        kf_ref[pair] = k[:, lanes]
        vf_ref[pair] = v_ref[:, lanes]
    m_ref[...] = jnp.full(m_ref.shape, NEG_INF, F32)
    l_ref[...] = jnp.zeros(l_ref.shape, F32)
    acc_ref[...] = jnp.zeros(acc_ref.shape, F32)

    for window, dil in DIL_CONFIGS:
        w = window // (2 * dil)
        ls = seq // dil
        tq = min(ATT_TQ, ls)
        tk = min(tq + 2 * w, ls)

        def rows_of(start, size, dil=dil):
            return pl.ds(start, size) if dil == 1 else pl.ds(start, size, stride=dil)

        def permute(r, carry, ls=ls, rows_of=rows_of):
            dst = pl.ds(pl.multiple_of(r * ls, ls), ls)
            for pair in range(2):
                qd_ref[pair, dst, :] = qf_ref[pair, rows_of(r, ls), :].astype(BF16)
                kd_ref[pair, dst, :] = kf_ref[pair, rows_of(r, ls), :].astype(BF16)
                vd_ref[pair, dst, :] = vf_ref[pair, rows_of(r, ls), :].astype(BF16)
            return carry

        lax.fori_loop(0, dil, permute, 0)
        tiles_per_res = ls // tq

        def coords(j, w=w, ls=ls, tq=tq, tk=tk, tpr=tiles_per_res):
            r = j // tpr
            q0 = (j % tpr) * tq
            k0 = jnp.clip(q0 - w, 0, ls - tk)
            return r, q0, pl.multiple_of(r * ls + k0, w), k0 - q0

        def scores(j, slot, w=w, tq=tq, tk=tk, coords=coords):
            _, _, krow, delta = coords(j)
            valid = _band_valid(tq, tk, w, delta)
            qrow = pl.multiple_of(j * tq, tq)
            for pair in range(2):
                _stage_scores([qd_ref[pair, pl.ds(qrow, tq), :]], kd_ref[pair, pl.ds(krow, tk), :], valid,
                              s_ref.at[:, :, :, 0:tk], slot, pair)

        def softmax(j, slot, tq=tq, tk=tk):
            for pair in range(2):
                _stage_softmax(s_ref.at[:, :, :, 0:tk], p_ref.at[:, :, :, 0:tk], mt_ref, lt_ref, slot, pair, tq)

        def merge(j, slot, tq=tq, tk=tk, dil=dil, coords=coords, rows_of=rows_of):
            r, q0, krow, _ = coords(j)
            rows = rows_of(r + dil * q0, tq)
            for pair in range(2):
                (o,) = _stage_pv(p_ref.at[:, :, :, 0:tk], vd_ref[pair, pl.ds(krow, tk), :], slot, pair, tq)
                m, l = mt_ref[slot, pair], lt_ref[slot, pair]
                m_old = m_ref[pair, rows, :]
                m_new = jnp.maximum(m_old, m)
                a, b = jnp.exp(m_old - m_new), jnp.exp(m - m_new)
                acc_ref[pair, rows, :] = acc_ref[pair, rows, :] * a + o * b
                l_ref[pair, rows, :] = l_ref[pair, rows, :] * a + l * b
                m_ref[pair, rows, :] = m_new

        _pipeline(seq // tq, scores, softmax, merge)

    for pair in range(2):
        o_ref[:, pair * PAIR:(pair + 1) * PAIR] = acc_ref[pair] / l_ref[pair]


def _dilated(proj, cos_q, sin_q, batch, seq):
    t = batch * seq

    def sec(base):
        return pl.BlockSpec((None, seq, SEC), lambda b, g: (base + g, b, 0))

    tab = pl.BlockSpec((None, seq, SEC), lambda b, g: (b, 0, 0))
    return pl.pallas_call(
        functools.partial(_dil_kernel, seq=seq),
        grid=(batch, 2),
        in_specs=[sec(SEC_QD), sec(SEC_KD), sec(SEC_VD), tab, tab],
        out_specs=pl.BlockSpec((seq, SEC), lambda b, g: (b, g)),
        out_shape=jax.ShapeDtypeStruct((t, GROUP_WIDTH), F32),
        scratch_shapes=[pltpu.VMEM((2, seq, PAIR), F32)] * 6 + [pltpu.VMEM((2, seq, PAIR), BF16)] * 3 + [
            pltpu.VMEM((ATT_ITEMS, 2, 2 * ATT_TQ, 2 * ATT_TQ), F32),
            pltpu.VMEM((ATT_ITEMS, 2, 2 * ATT_TQ, 2 * ATT_TQ), BF16),
            pltpu.VMEM((ATT_ITEMS, 2, ATT_TQ, PAIR), F32),
            pltpu.VMEM((ATT_ITEMS, 2, ATT_TQ, PAIR), F32)],
        compiler_params=_cp(("arbitrary", "arbitrary")),
        name="dilated",
    )(proj, proj, proj, cos_q, sin_q)


def _gnorm(y, g):
    return y * lax.rsqrt(jnp.mean(y * y, axis=-1, keepdims=True) + EPS) * g


OUT_SUB = 256


def _out_kernel(ya_ref, yb_ref, yc_ref, yd_ref, gn_ref, w_ref, x_ref, mod_ref, nf_ref, rcat_ref,
                xn_ref, h_ref, aff_ref):
    ys = (ya_ref, yb_ref, yc_ref, yd_ref)
    gw = GROUP_WIDTH
    row_tiles = x_ref.shape[1] // LANES
    for sub in range(x_ref.shape[0] // OUT_SUB):
        rows = slice(sub * OUT_SUB, (sub + 1) * OUT_SUB)
        acc = None
        for g in range(4):
            yn = _gnorm(ys[g][rows, :], gn_ref[:, g * gw:(g + 1) * gw]).astype(BF16)
            part = jnp.dot(yn, w_ref[g * gw:(g + 1) * gw, :], preferred_element_type=F32)
            acc = part if acc is None else acc + part
        xn = x_ref[rows, :] + mod_ref[2:3, :] * acc
        xn_ref[rows, :] = xn
        h = _gnorm(xn, nf_ref[...]) * (1.0 + mod_ref[4:5, :]) + mod_ref[3:4, :]
        for c in range(row_tiles):
            h_ref[pl.ds(sub * OUT_SUB * row_tiles + c, OUT_SUB, stride=row_tiles), :] = h[:, c * LANES:(c + 1) * LANES]
        h_hi = h.astype(BF16)
        h_lo = (h - h_hi.astype(F32)).astype(BF16)
        a = _nt(rcat_ref[...], h_hi)
        b = _nt(rcat_ref[0:N_EXPERTS, :], h_lo)
        logits = a[0:N_EXPERTS] + a[N_EXPERTS:2 * N_EXPERTS] + b
        mxl = jnp.max(logits, axis=0, keepdims=True)
        ex = jnp.exp(logits - mxl)
        aff_ref[:, rows] = ex / jnp.sum(ex, axis=0, keepdims=True)


def _out_proj(ya, yb, yc, yd, gn, w_bf, x2, mod_l, nf, rcat, seq):
    t, d = x2.shape
    tm = 2 * OUT_SUB
    tpb = seq // tm
    ROW_TILES = d // LANES
    ysp = pl.BlockSpec((tm, GROUP_WIDTH), lambda i: (i, 0))
    return pl.pallas_call(
        _out_kernel,
        grid=(t // tm,),
        in_specs=[ysp] * 4 + [
            pl.BlockSpec((1, d), lambda i: (0, 0)),
            pl.BlockSpec((d, d), lambda i: (0, 0)),
            pl.BlockSpec((tm, d), lambda i: (i, 0)),
            pl.BlockSpec((None, 6, d), lambda i: (i // tpb, 0, 0)),
            pl.BlockSpec((1, d), lambda i: (0, 0)),
            pl.BlockSpec((2 * N_EXPERTS, d), lambda i: (0, 0))],
        out_specs=[pl.BlockSpec((tm, d), lambda i: (i, 0)),
                   pl.BlockSpec((tm * ROW_TILES, LANES), lambda i: (i, 0)),
                   pl.BlockSpec((N_EXPERTS, tm), lambda i: (0, i))],
        out_shape=[jax.ShapeDtypeStruct((t, d), F32),
                   jax.ShapeDtypeStruct((t * ROW_TILES, LANES), F32),
                   jax.ShapeDtypeStruct((N_EXPERTS, t), F32)],
        compiler_params=_cp(("arbitrary",)),
        name="out_proj",
    )(ya, yb, yc, yd, gn, w_bf, x2, mod_l, nf, rcat)


def _route_kernel(aff_ref, tri_ref, idx_ref, gate_ref, dst_ref, cnt_ref, start_ref,
                  pos_ref, q_ref, *, cap, seq):
    aff = aff_ref[...]

    def as_f32(bits):
        return lax.bitcast_convert_type(bits, F32)

    def step(k, thr):
        cand = thr | lax.shift_left(jnp.int32(1), 30 - k)
        cnt = jnp.sum((aff >= as_f32(cand)).astype(jnp.int32), axis=1, keepdims=True)
        return jnp.where(cnt >= cap, cand, thr)

    thr = lax.fori_loop(0, 31, step, jnp.zeros((aff.shape[0], 1), jnp.int32))
    gt = aff >= as_f32(thr + 1)
    eq = (aff >= as_f32(thr)) & jnp.logical_not(gt)
    need = (cap - jnp.sum(gt.astype(jnp.int32), axis=1, keepdims=True)).astype(F32)
    eq_rank = jnp.dot(eq.astype(BF16), tri_ref[...], preferred_element_type=F32)
    sel = gt | (eq & (eq_rank < need))
    n_e = aff.shape[0]
    sel_f = sel.astype(F32)
    sel_bf = sel.astype(BF16)
    pos_ref[...] = jnp.where(sel, jnp.dot(sel_bf, tri_ref[...], preferred_element_type=F32), -1.0)

    cnt = jnp.sum(sel_f, axis=0, keepdims=True)
    cnt8 = jnp.broadcast_to(cnt, (8, seq)).astype(BF16)
    start = (jnp.dot(cnt8, tri_ref[...], preferred_element_type=F32)[0:1]
             + (pl.program_id(0) * (n_e * cap)).astype(F32))
    lower = (lax.broadcasted_iota(jnp.int32, (n_e, n_e), 1)
             < lax.broadcasted_iota(jnp.int32, (n_e, n_e), 0)).astype(BF16)
    q_ref[...] = start + jnp.dot(lower, sel_bf, preferred_element_type=F32)
    cnt_ref[...] = cnt
    start_ref[...] = start

    tok = lax.broadcasted_iota(jnp.int32, (1, seq), 1) + pl.program_id(0) * seq
    tok_hi = lax.shift_right_logical(tok, 7).astype(F32)
    tok_lo = (tok & (LANES - 1)).astype(F32)
    slot = lax.broadcasted_iota(jnp.int32, (cap, seq), 0).astype(F32)
    eye = (lax.broadcasted_iota(jnp.int32, (cap, cap), 0) == lax.broadcasted_iota(jnp.int32, (cap, cap), 1))
    zero_row = jnp.zeros((1, seq), F32)

    def per_expert(e, carry):
        onehot = (pos_ref[pl.ds(e, 1), :] == slot).astype(BF16)
        q = q_ref[pl.ds(e, 1), :]
        q_hi = jnp.floor(q * (1.0 / LANES))
        a = aff_ref[pl.ds(e, 1), :]
        a1 = a.astype(BF16).astype(F32)
        a2 = (a - a1).astype(BF16).astype(F32)
        pieces = jnp.concatenate([tok_hi, tok_lo, q_hi, q - q_hi * LANES, a1, a2, a - a1 - a2, zero_row], axis=0)
        got = _nt(pieces.astype(BF16), onehot)
        idx_ref[e] = (got[0:1] * LANES + got[1:2]).astype(jnp.int32)
        dst_ref[e] = (got[2:3] * LANES + got[3:4]).astype(jnp.int32)
        gate_row = got[4:5] + got[5:6] + got[6:7]
        gate_col = jnp.sum(jnp.where(eye, gate_row, 0.0), axis=1, keepdims=True)
        gate_ref[e] = jnp.broadcast_to(gate_col, (cap, LANES))
        return carry

    lax.fori_loop(0, n_e, per_expert, 0)


def _route(aff_t, tri, batch, seq, cap):
    e, t = aff_t.shape
    blk = pl.BlockSpec((e, seq), lambda b: (0, b))
    colblk = pl.BlockSpec((e, cap, LANES), lambda b: (0, b, 0))
    rowblk = pl.BlockSpec((1, seq), lambda b: (0, b))
    slotblk = pl.BlockSpec((e, None, 1, cap), lambda b: (0, b, 0, 0))
    slot_i32 = jax.ShapeDtypeStruct((e, batch, 1, cap), jnp.int32)
    return pl.pallas_call(
        functools.partial(_route_kernel, cap=cap, seq=seq),
        grid=(batch,),
        in_specs=[blk, pl.BlockSpec((seq, seq), lambda b: (0, 0))],
        out_specs=[slotblk, colblk, slotblk, rowblk, rowblk],
        out_shape=[slot_i32, jax.ShapeDtypeStruct((e, batch * cap, LANES), F32), slot_i32,
                   jax.ShapeDtypeStruct((1, t), F32), jax.ShapeDtypeStruct((1, t), F32)],
        scratch_shapes=[pltpu.VMEM((e, seq), F32)] * 2,
        compiler_params=_cp(("arbitrary",)),
        name="route",
    )(aff_t, tri)


DMA_UNROLL = 32


def _moe_kernel(idx_ref, dst_ref, h_hbm, gate_ref, wg_ref, wu_ref, wd_ref, sorted_hbm,
                xg_ref, xe_ref, acc_ref, ob_ref, sem, *, rows):
    e = pl.program_id(0)
    f = pl.program_id(1)
    n_e = pl.num_programs(0)

    row_tiles = xe_ref.shape[1] // LANES

    def scatter():
        def body(i, carry):
            for u in range(DMA_UNROLL):
                r = i * DMA_UNROLL + u
                pltpu.make_async_copy(ob_ref.at[:, r, :], sorted_hbm.at[:, dst_ref[e * rows + r], :],
                                      sem.at[1]).start()
            return carry
        lax.fori_loop(0, rows // DMA_UNROLL, body, 0)

    def wait_scatter():
        for c in range(row_tiles):
            pltpu.make_async_copy(ob_ref.at[c], sorted_hbm.at[c, pl.ds(0, rows), :], sem.at[1]).wait()

    def gather(expert):
        def body(i, carry):
            for u in range(DMA_UNROLL):
                r = i * DMA_UNROLL + u
                tok = idx_ref[expert * rows + r]
                pltpu.make_async_copy(h_hbm.at[pl.ds(pl.multiple_of(tok * row_tiles, row_tiles), row_tiles), :],
                                      xg_ref.at[:, r, :], sem.at[0]).start()
            return carry
        lax.fori_loop(0, rows // DMA_UNROLL, body, 0)

    @pl.when((e == 0) & (f == 0))
    def _():
        gather(0)

    @pl.when(f == 0)
    def _():
        for c in range(row_tiles):
            pltpu.make_async_copy(h_hbm.at[pl.ds(0, rows), :], xg_ref.at[c], sem.at[0]).wait()
        for c in range(row_tiles):
            xe_ref[:, c * LANES:(c + 1) * LANES] = xg_ref[c].astype(BF16)
        acc_ref[...] = jnp.zeros_like(acc_ref)

        @pl.when(e + 1 < n_e)
        def _():
            gather(e + 1)

    xe = xe_ref[...]
    a = jnp.dot(xe, wg_ref[...].astype(BF16), preferred_element_type=F32)
    u = jnp.dot(xe, wu_ref[...].astype(BF16), preferred_element_type=F32)
    act = (a * jax.nn.sigmoid(a) * u).astype(BF16)
    acc_ref[...] += jnp.dot(act, wd_ref[...].astype(BF16), preferred_element_type=F32)

    @pl.when(f == pl.num_programs(1) - 1)
    def _():
        @pl.when(e > 0)
        def _():
            wait_scatter()

        gate = gate_ref[:, 0:1]
        for c in range(row_tiles):
            ob_ref[c] = acc_ref[:, c * LANES:(c + 1) * LANES] * gate
        scatter()

        @pl.when(e == n_e - 1)
        def _():
            wait_scatter()


def _moe(idx_flat, dst_flat, gate_col, h2, wg, wu, wd, layer):
    _, ne, d, ff = wg.shape
    rows = gate_col.shape[1]
    row_tiles = d // LANES
    tf = 256
    grid_spec = pltpu.PrefetchScalarGridSpec(
        num_scalar_prefetch=2,
        grid=(ne, ff // tf),
        in_specs=[pl.BlockSpec(memory_space=pl.ANY),
                  pl.BlockSpec((None, rows, LANES), lambda e, f, idx, dst: (e, 0, 0)),
                  pl.BlockSpec((None, None, d, tf), lambda e, f, idx, dst: (layer, e, 0, f)),
                  pl.BlockSpec((None, None, d, tf), lambda e, f, idx, dst: (layer, e, 0, f)),
                  pl.BlockSpec((None, None, tf, d), lambda e, f, idx, dst: (layer, e, f, 0))],
        out_specs=pl.BlockSpec(memory_space=pl.ANY),
        scratch_shapes=[pltpu.VMEM((row_tiles, rows, LANES), F32), pltpu.VMEM((rows, d), BF16),
                        pltpu.VMEM((rows, d), F32), pltpu.VMEM((row_tiles, rows, LANES), F32),
                        pltpu.SemaphoreType.DMA((2,))],
    )
    return pl.pallas_call(
        functools.partial(_moe_kernel, rows=rows),
        grid_spec=grid_spec,
        out_shape=jax.ShapeDtypeStruct((row_tiles, ne * rows, LANES), F32),
        compiler_params=_cp(("arbitrary", "arbitrary")),
        name="moe",
    )(idx_flat, dst_flat, h2, gate_col, wg, wu, wd)


COMBINE_CHUNK = 256
COMBINE_TM = 512
COMBINE_STREAMS = 4


def _combine_kernel(tile_ref, start_ref, cnt_ref, sorted_hbm, xn_ref, mod_ref, fn_ref, out_ref,
                    buf_ref, acc_ref, done_ref, sem, *, tm, total, final_norm):
    i = pl.program_id(0)
    n_tiles = pl.num_programs(0)
    row_tiles = buf_ref.shape[1]
    kc = COMBINE_CHUNK

    def span(tile):
        lo = tile_ref[tile]
        base = lax.shift_left(lax.shift_right_logical(lo, 3), 3)
        n_chunks = jnp.maximum(lax.shift_right_logical(tile_ref[tile + 1] - base + (kc - 1), 8), 1)
        return base, n_chunks

    def chunk_rows(base, j):
        return pl.multiple_of(jnp.minimum(base + j * kc, total - kc), 8)

    def chunk_copies(first_row, slot):
        per = row_tiles // COMBINE_STREAMS
        return [pltpu.make_async_copy(sorted_hbm.at[pl.ds(s * per, per), pl.ds(first_row, kc), :],
                                      buf_ref.at[slot, pl.ds(s * per, per)], sem.at[slot])
                for s in range(COMBINE_STREAMS)]

    def start_chunk(first_row, slot):
        for s, cp in enumerate(chunk_copies(first_row, slot)):
            cp.start(priority=s % 2)

    def wait_chunk(first_row, slot):
        for cp in chunk_copies(first_row, slot):
            cp.wait()

    base, n_chunks = span(i)

    @pl.when(i == 0)
    def _():
        done_ref[0] = 0
        start_chunk(chunk_rows(base, 0), 0)

    eye = (lax.broadcasted_iota(jnp.int32, (tm, tm), 0) == lax.broadcasted_iota(jnp.int32, (tm, tm), 1))
    lo_col = jnp.sum(jnp.where(eye, start_ref[...], 0.0), axis=1, keepdims=True)
    hi_col = lo_col + jnp.sum(jnp.where(eye, cnt_ref[...], 0.0), axis=1, keepdims=True)
    acc_ref[...] = jnp.zeros_like(acc_ref)

    def chunk(j, carry):
        done = done_ref[0]
        slot = done % 2
        first_row = chunk_rows(base, j)
        wait_chunk(first_row, slot)

        @pl.when(j + 1 < n_chunks)
        def _():
            start_chunk(chunk_rows(base, j + 1), 1 - slot)

        @pl.when((j + 1 == n_chunks) & (i + 1 < n_tiles))
        def _():
            start_chunk(chunk_rows(span(i + 1)[0], 0), 1 - slot)

        row = (first_row + lax.broadcasted_iota(jnp.int32, (1, kc), 1)).astype(F32)
        nominal = (base + j * kc).astype(F32)
        pick = ((row >= lo_col) & (row < hi_col) & (row >= nominal)).astype(BF16)
        for c in range(row_tiles):
            acc_ref[:, c * LANES:(c + 1) * LANES] += jnp.dot(pick, buf_ref[slot, c].astype(BF16),
                                                             preferred_element_type=F32)
        done_ref[0] = done + 1
        return carry

    lax.fori_loop(0, n_chunks, chunk, 0)

    x = xn_ref[...] + mod_ref[5:6, :] * acc_ref[...]
    if final_norm:
        x = _gnorm(x, fn_ref[...])
    out_ref[...] = x


def _combine(tile_start, start_row, cnt_row, sorted_rows, xn, mod_l, fn, seq, final_norm):
    t, d = xn.shape
    row_tiles, total, _ = sorted_rows.shape
    tm = COMBINE_TM
    tpb = seq // tm
    grid_spec = pltpu.PrefetchScalarGridSpec(
        num_scalar_prefetch=1,
        grid=(t // tm,),
        in_specs=[pl.BlockSpec((1, tm), lambda i, ts: (0, i)),
                  pl.BlockSpec((1, tm), lambda i, ts: (0, i)),
                  pl.BlockSpec(memory_space=pl.ANY),
                  pl.BlockSpec((tm, d), lambda i, ts: (i, 0)),
                  pl.BlockSpec((None, 6, d), lambda i, ts: (i // tpb, 0, 0)),
                  pl.BlockSpec((1, d), lambda i, ts: (0, 0))],
        out_specs=pl.BlockSpec((tm, d), lambda i, ts: (i, 0)),
        scratch_shapes=[pltpu.VMEM((2, row_tiles, COMBINE_CHUNK, LANES), F32), pltpu.VMEM((tm, d), F32),
                        pltpu.SMEM((1,), jnp.int32), pltpu.SemaphoreType.DMA((2,))],
    )
    return pl.pallas_call(
        functools.partial(_combine_kernel, tm=tm, total=total, final_norm=final_norm),
        grid_spec=grid_spec,
        out_shape=jax.ShapeDtypeStruct((t, d), F32),
        compiler_params=_cp(("arbitrary",)),
        name="combine",
    )(tile_start, start_row, cnt_row, sorted_rows, xn, mod_l, fn)


def _rope_tables(positions):
    b, s = positions.shape
    inv = jnp.power(jnp.float32(ROPE_THETA), -jnp.arange(0, ROPE_DIM, 2, dtype=F32) / ROPE_DIM)
    ang = positions.astype(F32)[..., None] * inv
    cos, sin = jnp.cos(ang), jnp.sin(ang)
    rest = HEAD_DIM - ROPE_DIM
    c64 = jnp.concatenate([cos, cos, jnp.ones((b, s, rest), F32)], axis=-1)
    s64 = jnp.concatenate([-sin, sin, jnp.zeros((b, s, rest), F32)], axis=-1)
    return jnp.tile(c64, (1, 1, 4)), jnp.tile(s64, (1, 1, 4))


def _dft_tables(n):
    k = jnp.arange(n, dtype=jnp.int32)
    ang = ((k[:, None] * k[None, :]) % n).astype(F32) * (2.0 * math.pi / n)
    return jnp.cos(ang), jnp.sin(ang)


def _dft_tables_split(n):
    k = jnp.arange(n, dtype=jnp.int32)[:, None]
    m0 = jnp.arange(LANES, dtype=jnp.int32)[None, :]
    m1 = jnp.arange(n // LANES, dtype=jnp.int32)[None, :]
    a = ((k * m0) % n).astype(F32) * (2.0 * math.pi / n)
    b = ((k * m1 * LANES) % n).astype(F32) * (2.0 * math.pi / n)
    ca, sa, cb, sb = jnp.cos(a)[:, None, :], jnp.sin(a)[:, None, :], jnp.cos(b)[:, :, None], jnp.sin(b)[:, :, None]
    return (ca * cb - sa * sb).reshape(n, n), (sa * cb + ca * sb).reshape(n, n)


def _block_diag(blocks):
    n, a, _ = blocks.shape
    eye = jnp.eye(n, dtype=blocks.dtype)
    return (eye[:, None, :, None] * blocks[:, :, None, :]).reshape(n * a, n * a)


def kernel(x, c, positions, ada_w, ada_b, norm_mix, w_in, fnet_w, sgu_norm, sgu_w, sgu_b, swa_sink,
           group_norm, w_out, norm_ffn, router_w, exp_w_gate, exp_w_up, exp_w_down, final_norm):
    batch, seq, d = x.shape
    depth = ada_w.shape[0]
    t = batch * seq
    cap = max(1, EC_FACTOR * seq // N_EXPERTS)

    rows = 16
    c_pad = jnp.zeros((rows, d), F32).at[:batch].set(c)
    mod = _ada(c_pad, ada_w, ada_b)[:, :batch].reshape(depth, batch, 6, d)

    cos_q, sin_q = _rope_tables(positions)
    dft_c, dft_s = _dft_tables_split(seq)
    dft_c, dft_s = dft_c.astype(BF16), dft_s.astype(BF16)
    cc, sc = _dft_tables(HEAD_DIM)
    bdc = _block_diag(jnp.broadcast_to(cc, (4, HEAD_DIM, HEAD_DIM)))
    bds = _block_diag(jnp.broadcast_to(sc, (4, HEAD_DIM, HEAD_DIM)))
    tri = (jnp.arange(seq)[:, None] < jnp.arange(seq)[None, :]).astype(BF16)

    x2 = x.reshape(t, d)
    for l in range(depth):
        mod_l = mod[l]
        proj = _in_proj(x2, mod_l, norm_mix[l].reshape(1, d), w_in[l].astype(BF16), seq)

        bdw = jnp.stack([_block_diag(fnet_w[l, 0:4]), _block_diag(fnet_w[l, 4:8])])
        ya = _fourier(proj, bdc, bds, bdw, dft_c, dft_s, batch, seq).reshape(t, GROUP_WIDTH)
        sgu_w_pairs = jnp.concatenate([sgu_w[l, 0::2], sgu_w[l, 1::2]], axis=-1).astype(BF16)
        yb = _sgu(proj, sgu_norm[l].reshape(1, GROUP_WIDTH), sgu_w_pairs, sgu_b[l].T)
        sink_row = jnp.repeat(swa_sink[l], HEAD_DIM).reshape(1, GROUP_WIDTH)
        yc = _swa(proj, cos_q, sin_q, sink_row, batch, seq)
        yd = _dilated(proj, cos_q, sin_q, batch, seq)

        r_t = router_w[l].T
        r_hi = r_t.astype(BF16)
        r_lo = (r_t - r_hi.astype(F32)).astype(BF16)
        rcat = jnp.concatenate([r_hi, r_lo], axis=0)
        xn, h2, aff_t = _out_proj(ya, yb, yc, yd, group_norm[l].reshape(1, d), w_out[l].astype(BF16),
                                  x2, mod_l, norm_ffn[l].reshape(1, d), rcat, seq)
        idx, gate_col, dst, cnt_row, start_row = _route(aff_t, tri, batch, seq, cap)
        sorted_rows = _moe(idx.reshape(-1), dst.reshape(-1), gate_col, h2,
                           exp_w_gate, exp_w_up, exp_w_down, l)
        n_pairs = batch * N_EXPERTS * cap
        tile_start = jnp.concatenate([start_row[0, ::COMBINE_TM].astype(jnp.int32), jnp.full((1,), n_pairs, jnp.int32)])
        x2 = _combine(tile_start, start_row, cnt_row, sorted_rows, xn, mod_l, final_norm.reshape(1, d), seq,
                      l == depth - 1)
    return x2.reshape(batch, seq, d)
```

```python
import functools
import math

import jax
import jax.numpy as jnp
from jax import lax
from jax.experimental import pallas as pl
from jax.experimental.pallas import tpu as pltpu

F32 = jnp.float32
BF16 = jnp.bfloat16
ACT = jnp.bfloat16

HEAD_DIM = 64
N_GROUP_HEADS = 8
GROUP_WIDTH = 512
CHUNK = 128
SWA_HALF_WINDOW = 128
DIL_CONFIGS = ((128, 1), (512, 4), (2048, 16))
ROPE_THETA = 500000.0
ROPE_DIM = 16
N_EXPERTS = 16
EC_FACTOR = 2
EPS = 1e-6
NEG_INF = -1e30

LANES = 128
SEC = 256
SEC_XA, SEC_U, SEC_V, SEC_QC, SEC_KVC, SEC_QD, SEC_KD, SEC_VD = 0, 2, 4, 6, 8, 9, 11, 13
N_SEC = 15

VMEM_LIMIT = 56 * 1024 * 1024


def _cp(sem, vmem=None):
    return pltpu.CompilerParams(dimension_semantics=sem, vmem_limit_bytes=vmem or VMEM_LIMIT)


def _nt(a, b):
    return lax.dot_general(a, b, (((1,), (1,)), ((), ())), preferred_element_type=F32)


ADA_STREAMS = 4


def _ada_kernel(c_ref, *refs):
    w_refs, b_ref, o_ref = refs[:ADA_STREAMS], refs[ADA_STREAMS], refs[ADA_STREAMS + 1]

    @pl.when(pl.program_id(1) == 0)
    def _():
        o_ref[0] = jnp.broadcast_to(b_ref[0], o_ref.shape[1:])

    c = c_ref[...]
    s = (c * jax.nn.sigmoid(c)).astype(BF16)
    tn = w_refs[0].shape[2]
    for j, w_ref in enumerate(w_refs):
        o_ref[0, :, j * tn:(j + 1) * tn] += jnp.dot(s, w_ref[0].astype(BF16), preferred_element_type=F32)


def _ada(c_pad, ada_w, ada_b):
    n_layers, d, n6 = ada_w.shape
    rows = c_pad.shape[0]
    tk = 256
    tn = n6 // ADA_STREAMS
    w_specs = [pl.BlockSpec((1, tk, tn), functools.partial(lambda l, k, j: (l, k, j), j=j))
               for j in range(ADA_STREAMS)]
    return pl.pallas_call(
        _ada_kernel,
        grid=(n_layers, d // tk),
        in_specs=[pl.BlockSpec((rows, tk), lambda l, k: (0, k))] + w_specs + [
            pl.BlockSpec((1, 1, n6), lambda l, k: (l, 0, 0))],
        out_specs=pl.BlockSpec((1, rows, n6), lambda l, k: (l, 0, 0)),
        out_shape=jax.ShapeDtypeStruct((n_layers, rows, n6), F32),
        compiler_params=_cp(("arbitrary", "arbitrary")),
        name="ada",
    )(c_pad, *([ada_w] * ADA_STREAMS), ada_b.reshape(n_layers, 1, n6))


NORM_ROWS = 16
IN_SECS = 5


def _in_kernel(x_ref, mod_ref, g_ref, w_ref, o_ref, h_ref):
    @pl.when(pl.program_id(1) == 0)
    def _():
        gain = g_ref[...] * (1.0 + mod_ref[1:2, :])
        shift = mod_ref[0:1, :]

        def chunk(c, carry):
            rows = pl.ds(pl.multiple_of(c * NORM_ROWS, NORM_ROWS), NORM_ROWS)
            x = x_ref[rows, :]
            r = lax.rsqrt(jnp.mean(x * x, axis=-1, keepdims=True) + EPS)
            h_ref[rows, :] = (x * r * gain + shift).astype(BF16)
            return carry

        lax.fori_loop(0, x_ref.shape[0] // NORM_ROWS, chunk, 0, unroll=8)

    res = jnp.dot(h_ref[...], w_ref[...], preferred_element_type=F32)
    for s in range(IN_SECS):
        o_ref[s] = res[:, s * SEC:(s + 1) * SEC].astype(o_ref.dtype)


def _in_proj(x2, mod_l, norm_g, w_bf, seq):
    t, d = x2.shape
    tm = 1024
    tpb = seq // tm
    return pl.pallas_call(
        _in_kernel,
        grid=(t // tm, N_SEC // IN_SECS),
        in_specs=[pl.BlockSpec((tm, d), lambda i, j: (i, 0)),
                  pl.BlockSpec((None, 6, d), lambda i, j: (i // tpb, 0, 0)),
                  pl.BlockSpec((1, d), lambda i, j: (0, 0)),
                  pl.BlockSpec((d, IN_SECS * SEC), lambda i, j: (0, j))],
        out_specs=pl.BlockSpec((IN_SECS, tm, SEC), lambda i, j: (j, i, 0)),
        out_shape=jax.ShapeDtypeStruct((N_SEC, t, SEC), ACT),
        scratch_shapes=[pltpu.VMEM((tm, d), BF16)],
        compiler_params=_cp(("arbitrary", "arbitrary")),
        name="in_proj",
    )(x2, mod_l, norm_g, w_bf)


def _f1_kernel(x_ref, bdc_ref, bds_ref, bdw_ref, z_ref, m_ref, *, scale):
    @pl.when(pl.program_id(1) == 0)
    def _():
        w = bdw_ref[...]
        mc = jnp.dot(bdc_ref[...], w, preferred_element_type=F32, precision=lax.Precision.HIGHEST)
        msn = jnp.dot(bds_ref[...], w, preferred_element_type=F32, precision=lax.Precision.HIGHEST)
        m_ref[:, 0:SEC] = (mc * scale).astype(BF16)
        m_ref[:, SEC:2 * SEC] = (msn * scale).astype(BF16)

    z_ref[...] = jnp.dot(x_ref[...].astype(BF16), m_ref[...], preferred_element_type=F32).astype(BF16)


def _f2_kernel(c_ref, s_ref, z_ref, o_ref):
    z = z_ref[...]
    o_ref[...] = (jnp.dot(c_ref[...], z[:, 0:SEC], preferred_element_type=F32)
                  - jnp.dot(s_ref[...], z[:, SEC:2 * SEC], preferred_element_type=F32)).astype(o_ref.dtype)


def _fourier(proj, bdc, bds, bdw, dft_c, dft_s, batch, seq):
    t = batch * seq
    tm = 1024
    scale = 1.0 / math.sqrt(seq * HEAD_DIM)
    z = pl.pallas_call(
        functools.partial(_f1_kernel, scale=scale),
        grid=(2, t // tm),
        in_specs=[pl.BlockSpec((None, tm, SEC), lambda s, i: (SEC_XA + s, i, 0)),
                  pl.BlockSpec((SEC, SEC), lambda s, i: (0, 0)),
                  pl.BlockSpec((SEC, SEC), lambda s, i: (0, 0)),
                  pl.BlockSpec((None, SEC, SEC), lambda s, i: (s, 0, 0))],
        out_specs=pl.BlockSpec((None, tm, 2 * SEC), lambda s, i: (s, i, 0)),
        out_shape=jax.ShapeDtypeStruct((2, t, 2 * SEC), BF16),
        scratch_shapes=[pltpu.VMEM((SEC, 2 * SEC), BF16)],
        compiler_params=_cp(("arbitrary", "arbitrary")),
        name="fnet_chan",
    )(proj, bdc, bds, bdw)
    rt = 512
    return pl.pallas_call(
        _f2_kernel,
        grid=(seq // rt, batch, 2),
        in_specs=[pl.BlockSpec((rt, seq), lambda r, b, s: (r, 0)),
                  pl.BlockSpec((rt, seq), lambda r, b, s: (r, 0)),
                  pl.BlockSpec((None, seq, 2 * SEC), lambda r, b, s: (s, b, 0))],
        out_specs=pl.BlockSpec((None, rt, SEC), lambda r, b, s: (b, r, s)),
        out_shape=jax.ShapeDtypeStruct((batch, seq, GROUP_WIDTH), ACT),
        compiler_params=_cp(("arbitrary", "arbitrary", "arbitrary")),
        name="fnet_seq",
    )(dft_c, dft_s, z)


def _sgu_kernel(u_ref, v_ref, g_ref, w_ref, b_ref, o_ref, *, tm):
    gv = [jax.nn.gelu(v_ref[s].astype(F32)) for s in range(2)]
    ms = (jnp.sum(gv[0] * gv[0], axis=-1, keepdims=True)
          + jnp.sum(gv[1] * gv[1], axis=-1, keepdims=True)) * (1.0 / GROUP_WIDTH)
    r = lax.rsqrt(ms + EPS)
    vn = [(gv[s] * r * g_ref[:, s * SEC:(s + 1) * SEC]).astype(BF16) for s in range(2)]
    lo = lax.broadcasted_iota(jnp.int32, (CHUNK, 2 * HEAD_DIM), 1) < HEAD_DIM
    zero = jnp.zeros((CHUNK, 2 * HEAD_DIM), BF16)
    for s in range(2):
        gu = jax.nn.gelu(u_ref[s].astype(F32))
        for c in range(tm // CHUNK):
            rows = slice(c * CHUNK, (c + 1) * CHUNK)
            for pp in range(2):
                pair = 2 * s + pp
                lanes = slice(pp * 2 * HEAD_DIM, (pp + 1) * 2 * HEAD_DIM)
                vp = vn[s][rows, lanes]
                rhs = jnp.concatenate([jnp.where(lo, vp, zero), jnp.where(lo, zero, vp)], axis=0)
                z = jnp.dot(w_ref[pair], rhs, preferred_element_type=F32)
                bias = jnp.where(lo, b_ref[:, 2 * pair:2 * pair + 1], b_ref[:, 2 * pair + 1:2 * pair + 2])
                o_ref[rows, s * SEC + pp * 2 * HEAD_DIM:s * SEC + (pp + 1) * 2 * HEAD_DIM] = (gu[rows, lanes] * (z + bias)).astype(o_ref.dtype)


def _sgu(proj, sgu_norm, sgu_w_bf, sgu_bt):
    t = proj.shape[1]
    tm = 512
    return pl.pallas_call(
        functools.partial(_sgu_kernel, tm=tm),
        grid=(t // tm,),
        in_specs=[pl.BlockSpec((2, tm, SEC), lambda i: (SEC_U // 2, i, 0)),
                  pl.BlockSpec((2, tm, SEC), lambda i: (SEC_V // 2, i, 0)),
                  pl.BlockSpec((1, GROUP_WIDTH), lambda i: (0, 0)),
                  pl.BlockSpec((N_GROUP_HEADS // 2, CHUNK, 2 * CHUNK), lambda i: (0, 0, 0)),
                  pl.BlockSpec((CHUNK, N_GROUP_HEADS), lambda i: (0, 0))],
        out_specs=pl.BlockSpec((tm, GROUP_WIDTH), lambda i: (i, 0)),
        out_shape=jax.ShapeDtypeStruct((t, GROUP_WIDTH), ACT),
        compiler_params=_cp(("arbitrary",)),
        name="sgu",
    )(proj, proj, sgu_norm, sgu_w_bf, sgu_bt)


ATT_TQ = 128
PAIR = 2 * HEAD_DIM


def _rope(x, cos_t, sin_t):
    n = x.shape[-1]
    half = ROPE_DIM // 2
    lane = lax.broadcasted_iota(jnp.int32, x.shape, 1) % HEAD_DIM
    swapped = jnp.where(lane < half, pltpu.roll(x, n - half, 1), pltpu.roll(x, half, 1))
    return x * cos_t + swapped * sin_t


def _lo_half(shape):
    return lax.broadcasted_iota(jnp.int32, shape, 1) < HEAD_DIM


def _stage_scores(q_pairs, k, valid, s_ref, slot, grp):
    tq = q_pairs[0].shape[0]
    lo = _lo_half(q_pairs[0].shape)
    zero = jnp.zeros_like(q_pairs[0])
    blocks = []
    for q in q_pairs:
        blocks += [jnp.where(lo, q, zero), jnp.where(lo, zero, q)]
    s = _nt(jnp.concatenate(blocks, axis=0), k)
    for i in range(len(blocks)):
        s_ref[slot, grp, i * tq:(i + 1) * tq, :] = jnp.where(valid, s[i * tq:(i + 1) * tq], NEG_INF)


def _stage_softmax(s_ref, p_ref, m_ref, l_ref, slot, grp, tq):
    s = s_ref[slot, grp]
    m = jnp.max(s, axis=-1, keepdims=True)
    p = jnp.exp(s - m)
    p_ref[slot, grp] = p.astype(BF16)
    l = jnp.sum(p, axis=-1, keepdims=True)
    n_pairs = s.shape[0] // (2 * tq)
    shp = (tq, PAIR)
    lo = _lo_half(shp)
    for i in range(n_pairs):
        a, b = slice(2 * i * tq, (2 * i + 1) * tq), slice((2 * i + 1) * tq, (2 * i + 2) * tq)
        m_ref[slot, grp * n_pairs + i] = jnp.where(lo, jnp.broadcast_to(m[a], shp), jnp.broadcast_to(m[b], shp))
        l_ref[slot, grp * n_pairs + i] = jnp.where(lo, jnp.broadcast_to(l[a], shp), jnp.broadcast_to(l[b], shp))


def _stage_pv(p_ref, v, slot, grp, tq):
    o = jnp.dot(p_ref[slot, grp], v, preferred_element_type=F32)
    lo = _lo_half((tq, PAIR))
    return [jnp.where(lo, o[2 * i * tq:(2 * i + 1) * tq], o[(2 * i + 1) * tq:(2 * i + 2) * tq])
            for i in range(o.shape[0] // (2 * tq))]


ATT_ITEMS = 4


def _pipeline(n_items, stage_a, stage_b, stage_c):
    def body(jj, carry):
        for stage in (stage_a, stage_b, stage_c):
            for t in range(ATT_ITEMS):
                stage(jj * ATT_ITEMS + t, t)
        return carry

    lax.fori_loop(0, n_items // ATT_ITEMS, body, 0)


def _band_valid(tq, tk, w, delta):
    d = (lax.broadcasted_iota(jnp.int32, (tq, tk), 1) - lax.broadcasted_iota(jnp.int32, (tq, tk), 0)) + delta
    return (d <= w) & (d >= -w)


def _swa_kernel(q_ref, kv_ref, cos_ref, sin_ref, sink_ref, o_ref, qs_ref, ks_ref, vs_ref,
                s_ref, p_ref, m_ref, l_ref, *, seq):
    w, tq = SWA_HALF_WINDOW, ATT_TQ
    tk = tq + 2 * w
    cos, sin = cos_ref[...], sin_ref[...]
    for s in range(2):
        qs_ref[s] = (_rope(q_ref[s].astype(F32), cos, sin) * (HEAD_DIM ** -0.5)).astype(BF16)
    kk = _rope(kv_ref[:, 0:PAIR].astype(F32), cos[:, 0:PAIR], sin[:, 0:PAIR])
    vv = kv_ref[:, PAIR:2 * PAIR].astype(F32)
    lo = lax.broadcasted_iota(jnp.int32, kk.shape, 1) < HEAD_DIM
    kk_sw, vv_sw = pltpu.roll(kk, HEAD_DIM, 1), pltpu.roll(vv, HEAD_DIM, 1)
    ks_ref[0] = jnp.where(lo, kk, kk_sw).astype(BF16)
    ks_ref[1] = jnp.where(lo, kk_sw, kk).astype(BF16)
    vs_ref[0] = jnp.where(lo, vv, vv_sw).astype(BF16)
    vs_ref[1] = jnp.where(lo, vv_sw, vv).astype(BF16)

    def coords(j):
        q0 = pl.multiple_of(j * tq, tq)
        return q0, pl.multiple_of(jnp.clip(q0 - w, 0, seq - tk), tq)

    def scores(j, slot):
        q0, k0 = coords(j)
        valid = _band_valid(tq, tk, w, k0 - q0)
        for g in range(2):
            q_pairs = [qs_ref[g, pl.ds(q0, tq), 0:PAIR], qs_ref[g, pl.ds(q0, tq), PAIR:2 * PAIR]]
            _stage_scores(q_pairs, ks_ref[g, pl.ds(k0, tk), :], valid, s_ref, slot, g)

    def softmax(j, slot):
        for g in range(2):
            _stage_softmax(s_ref, p_ref, m_ref, l_ref, slot, g, tq)

    def output(j, slot):
        q0, k0 = coords(j)
        for g in range(2):
            outs = _stage_pv(p_ref, vs_ref[g, pl.ds(k0, tk), :], slot, g, tq)
            for i, o in enumerate(outs):
                pair = 2 * g + i
                m, l = m_ref[slot, pair], l_ref[slot, pair]
                sink = sink_ref[:, pair * PAIR:(pair + 1) * PAIR]
                m2 = jnp.maximum(m, sink)
                a = jnp.exp(m - m2)
                o_ref[pl.ds(q0, tq), pair * PAIR:(pair + 1) * PAIR] = (
                    o * a / (l * a + jnp.exp(sink - m2))).astype(o_ref.dtype)

    _pipeline(seq // tq, scores, softmax, output)


def _swa(proj, cos_q, sin_q, sink_row, batch, seq):
    t = batch * seq
    tab = pl.BlockSpec((None, seq, SEC), lambda b: (b, 0, 0))
    return pl.pallas_call(
        functools.partial(_swa_kernel, seq=seq),
        grid=(batch,),
        in_specs=[pl.BlockSpec((2, seq, SEC), lambda b: (SEC_QC // 2, b, 0)),
                  pl.BlockSpec((None, seq, SEC), lambda b: (SEC_KVC, b, 0)),
                  tab, tab,
                  pl.BlockSpec((1, GROUP_WIDTH), lambda b: (0, 0))],
        out_specs=pl.BlockSpec((seq, GROUP_WIDTH), lambda b: (b, 0)),
        out_shape=jax.ShapeDtypeStruct((t, GROUP_WIDTH), ACT),
        scratch_shapes=[pltpu.VMEM((2, seq, SEC), BF16),
                        pltpu.VMEM((2, seq, PAIR), BF16),
                        pltpu.VMEM((2, seq, PAIR), BF16),
                        pltpu.VMEM((ATT_ITEMS, 2, 4 * ATT_TQ, ATT_TQ + 2 * SWA_HALF_WINDOW), F32),
                        pltpu.VMEM((ATT_ITEMS, 2, 4 * ATT_TQ, ATT_TQ + 2 * SWA_HALF_WINDOW), BF16),
                        pltpu.VMEM((ATT_ITEMS, 4, ATT_TQ, PAIR), F32),
                        pltpu.VMEM((ATT_ITEMS, 4, ATT_TQ, PAIR), F32)],
        compiler_params=_cp(("arbitrary",)),
        name="swa",
    )(proj, proj, cos_q, sin_q, sink_row)


def _dil_kernel(q_ref, k_ref, v_ref, cos_ref, sin_ref, o_ref,
                qf_ref, kf_ref, vf_ref, m_ref, l_ref, acc_ref, qd_ref, kd_ref, vd_ref,
                s_ref, p_ref, mt_ref, lt_ref, *, seq):
    cos, sin = cos_ref[...], sin_ref[...]
    q = _rope(q_ref[...].astype(F32), cos, sin) * (HEAD_DIM ** -0.5)
    k = _rope(k_ref[...].astype(F32), cos, sin)
    for pair in range(2):
        lanes = slice(pair * PAIR, (pair + 1) * PAIR)
        qf_ref[pair] = q[:, lanes]
        kf_ref[pair] = k[:, lanes]
        vf_ref[pair] = v_ref[:, lanes].astype(F32)
    m_ref[...] = jnp.full(m_ref.shape, NEG_INF, F32)
    l_ref[...] = jnp.zeros(l_ref.shape, F32)
    acc_ref[...] = jnp.zeros(acc_ref.shape, F32)

    for window, dil in DIL_CONFIGS:
        w = window // (2 * dil)
        ls = seq // dil
        tq = min(ATT_TQ, ls)
        tk = min(tq + 2 * w, ls)

        def rows_of(start, size, dil=dil):
            return pl.ds(start, size) if dil == 1 else pl.ds(start, size, stride=dil)

        def permute(r, carry, ls=ls, rows_of=rows_of):
            dst = pl.ds(pl.multiple_of(r * ls, ls), ls)
            for pair in range(2):
                qd_ref[pair, dst, :] = qf_ref[pair, rows_of(r, ls), :].astype(BF16)
                kd_ref[pair, dst, :] = kf_ref[pair, rows_of(r, ls), :].astype(BF16)
                vd_ref[pair, dst, :] = vf_ref[pair, rows_of(r, ls), :].astype(BF16)
            return carry

        lax.fori_loop(0, dil, permute, 0)
        tiles_per_res = ls // tq

        def coords(j, w=w, ls=ls, tq=tq, tk=tk, tpr=tiles_per_res):
            r = j // tpr
            q0 = (j % tpr) * tq
            k0 = jnp.clip(q0 - w, 0, ls - tk)
            return r, q0, pl.multiple_of(r * ls + k0, w), k0 - q0

        def scores(j, slot, w=w, tq=tq, tk=tk, coords=coords):
            _, _, krow, delta = coords(j)
            valid = _band_valid(tq, tk, w, delta)
            qrow = pl.multiple_of(j * tq, tq)
            for pair in range(2):
                _stage_scores([qd_ref[pair, pl.ds(qrow, tq), :]], kd_ref[pair, pl.ds(krow, tk), :], valid,
                              s_ref.at[:, :, :, 0:tk], slot, pair)

        def softmax(j, slot, tq=tq, tk=tk):
            for pair in range(2):
                _stage_softmax(s_ref.at[:, :, :, 0:tk], p_ref.at[:, :, :, 0:tk], mt_ref, lt_ref, slot, pair, tq)

        def merge(j, slot, tq=tq, tk=tk, dil=dil, coords=coords, rows_of=rows_of):
            r, q0, krow, _ = coords(j)
            rows = rows_of(r + dil * q0, tq)
            for pair in range(2):
                (o,) = _stage_pv(p_ref.at[:, :, :, 0:tk], vd_ref[pair, pl.ds(krow, tk), :], slot, pair, tq)
                m, l = mt_ref[slot, pair], lt_ref[slot, pair]
                m_old = m_ref[pair, rows, :]
                m_new = jnp.maximum(m_old, m)
                a, b = jnp.exp(m_old - m_new), jnp.exp(m - m_new)
                acc_ref[pair, rows, :] = acc_ref[pair, rows, :] * a + o * b
                l_ref[pair, rows, :] = l_ref[pair, rows, :] * a + l * b
                m_ref[pair, rows, :] = m_new

        _pipeline(seq // tq, scores, softmax, merge)

    for pair in range(2):
        o_ref[:, pair * PAIR:(pair + 1) * PAIR] = (acc_ref[pair] / l_ref[pair]).astype(o_ref.dtype)


def _dilated(proj, cos_q, sin_q, batch, seq):
    t = batch * seq

    def sec(base):
        return pl.BlockSpec((None, seq, SEC), lambda b, g: (base + g, b, 0))

    tab = pl.BlockSpec((None, seq, SEC), lambda b, g: (b, 0, 0))
    return pl.pallas_call(
        functools.partial(_dil_kernel, seq=seq),
        grid=(batch, 2),
        in_specs=[sec(SEC_QD), sec(SEC_KD), sec(SEC_VD), tab, tab],
        out_specs=pl.BlockSpec((seq, SEC), lambda b, g: (b, g)),
        out_shape=jax.ShapeDtypeStruct((t, GROUP_WIDTH), ACT),
        scratch_shapes=[pltpu.VMEM((2, seq, PAIR), F32)] * 6 + [pltpu.VMEM((2, seq, PAIR), BF16)] * 3 + [
            pltpu.VMEM((ATT_ITEMS, 2, 2 * ATT_TQ, 2 * ATT_TQ), F32),
            pltpu.VMEM((ATT_ITEMS, 2, 2 * ATT_TQ, 2 * ATT_TQ), BF16),
            pltpu.VMEM((ATT_ITEMS, 2, ATT_TQ, PAIR), F32),
            pltpu.VMEM((ATT_ITEMS, 2, ATT_TQ, PAIR), F32)],
        compiler_params=_cp(("arbitrary", "arbitrary")),
        name="dilated",
    )(proj, proj, proj, cos_q, sin_q)


def _gnorm(y, g):
    return y * lax.rsqrt(jnp.mean(y * y, axis=-1, keepdims=True) + EPS) * g


OUT_SUB = 256


def _out_kernel(ya_ref, yb_ref, yc_ref, yd_ref, gn_ref, w_ref, x_ref, mod_ref, nf_ref, rcat_ref,
                xn_ref, h_ref, aff_ref):
    ys = (ya_ref, yb_ref, yc_ref, yd_ref)
    gw = GROUP_WIDTH
    row_tiles = x_ref.shape[1] // LANES
    for sub in range(x_ref.shape[0] // OUT_SUB):
        rows = slice(sub * OUT_SUB, (sub + 1) * OUT_SUB)
        acc = None
        for g in range(4):
            yn = _gnorm(ys[g][rows, :].astype(F32), gn_ref[:, g * gw:(g + 1) * gw]).astype(BF16)
            part = jnp.dot(yn, w_ref[g * gw:(g + 1) * gw, :], preferred_element_type=F32)
            acc = part if acc is None else acc + part
        xn = x_ref[rows, :] + mod_ref[2:3, :] * acc
        xn_ref[rows, :] = xn
        h = _gnorm(xn, nf_ref[...]) * (1.0 + mod_ref[4:5, :]) + mod_ref[3:4, :]
        for c in range(row_tiles):
            h_ref[pl.ds(sub * OUT_SUB * row_tiles + c, OUT_SUB, stride=row_tiles), :] = h[:, c * LANES:(c + 1) * LANES]
        h_hi = h.astype(BF16)
        h_lo = (h - h_hi.astype(F32)).astype(BF16)
        a = _nt(rcat_ref[...], h_hi)
        b = _nt(rcat_ref[0:N_EXPERTS, :], h_lo)
        logits = a[0:N_EXPERTS] + a[N_EXPERTS:2 * N_EXPERTS] + b
        mxl = jnp.max(logits, axis=0, keepdims=True)
        ex = jnp.exp(logits - mxl)
        aff_ref[:, rows] = ex / jnp.sum(ex, axis=0, keepdims=True)


def _out_proj(ya, yb, yc, yd, gn, w_bf, x2, mod_l, nf, rcat, seq):
    t, d = x2.shape
    tm = 2 * OUT_SUB
    tpb = seq // tm
    ROW_TILES = d // LANES
    ysp = pl.BlockSpec((tm, GROUP_WIDTH), lambda i: (i, 0))
    return pl.pallas_call(
        _out_kernel,
        grid=(t // tm,),
        in_specs=[ysp] * 4 + [
            pl.BlockSpec((1, d), lambda i: (0, 0)),
            pl.BlockSpec((d, d), lambda i: (0, 0)),
            pl.BlockSpec((tm, d), lambda i: (i, 0)),
            pl.BlockSpec((None, 6, d), lambda i: (i // tpb, 0, 0)),
            pl.BlockSpec((1, d), lambda i: (0, 0)),
            pl.BlockSpec((2 * N_EXPERTS, d), lambda i: (0, 0))],
        out_specs=[pl.BlockSpec((tm, d), lambda i: (i, 0)),
                   pl.BlockSpec((tm * ROW_TILES, LANES), lambda i: (i, 0)),
                   pl.BlockSpec((N_EXPERTS, tm), lambda i: (0, i))],
        out_shape=[jax.ShapeDtypeStruct((t, d), F32),
                   jax.ShapeDtypeStruct((t * ROW_TILES, LANES), F32),
                   jax.ShapeDtypeStruct((N_EXPERTS, t), F32)],
        compiler_params=_cp(("arbitrary",)),
        name="out_proj",
    )(ya, yb, yc, yd, gn, w_bf, x2, mod_l, nf, rcat)


def _route_kernel(aff_ref, tri_ref, idx_ref, gate_ref, dst_ref, cnt_ref, start_ref,
                  pos_ref, q_ref, *, cap, seq):
    aff = aff_ref[...]

    def as_f32(bits):
        return lax.bitcast_convert_type(bits, F32)

    def step(k, thr):
        cand = thr | lax.shift_left(jnp.int32(1), 30 - k)
        cnt = jnp.sum((aff >= as_f32(cand)).astype(jnp.int32), axis=1, keepdims=True)
        return jnp.where(cnt >= cap, cand, thr)

    thr = lax.fori_loop(0, 31, step, jnp.zeros((aff.shape[0], 1), jnp.int32))
    gt = aff >= as_f32(thr + 1)
    eq = (aff >= as_f32(thr)) & jnp.logical_not(gt)
    need = (cap - jnp.sum(gt.astype(jnp.int32), axis=1, keepdims=True)).astype(F32)
    eq_rank = jnp.dot(eq.astype(BF16), tri_ref[...], preferred_element_type=F32)
    sel = gt | (eq & (eq_rank < need))
    n_e = aff.shape[0]
    sel_f = sel.astype(F32)
    sel_bf = sel.astype(BF16)
    pos_ref[...] = jnp.where(sel, jnp.dot(sel_bf, tri_ref[...], preferred_element_type=F32), -1.0)

    cnt = jnp.sum(sel_f, axis=0, keepdims=True)
    cnt8 = jnp.broadcast_to(cnt, (8, seq)).astype(BF16)
    start = (jnp.dot(cnt8, tri_ref[...], preferred_element_type=F32)[0:1]
             + (pl.program_id(0) * (n_e * cap)).astype(F32))
    lower = (lax.broadcasted_iota(jnp.int32, (n_e, n_e), 1)
             < lax.broadcasted_iota(jnp.int32, (n_e, n_e), 0)).astype(BF16)
    q_ref[...] = start + jnp.dot(lower, sel_bf, preferred_element_type=F32)
    cnt_ref[...] = cnt
    start_ref[...] = start

    tok = lax.broadcasted_iota(jnp.int32, (1, seq), 1) + pl.program_id(0) * seq
    tok_hi = lax.shift_right_logical(tok, 7).astype(F32)
    tok_lo = (tok & (LANES - 1)).astype(F32)
    slot = lax.broadcasted_iota(jnp.int32, (cap, seq), 0).astype(F32)
    eye = (lax.broadcasted_iota(jnp.int32, (cap, cap), 0) == lax.broadcasted_iota(jnp.int32, (cap, cap), 1))
    zero_row = jnp.zeros((1, seq), F32)

    def per_expert(e, carry):
        onehot = (pos_ref[pl.ds(e, 1), :] == slot).astype(BF16)
        q = q_ref[pl.ds(e, 1), :]
        q_hi = jnp.floor(q * (1.0 / LANES))
        a = aff_ref[pl.ds(e, 1), :]
        a1 = a.astype(BF16).astype(F32)
        a2 = (a - a1).astype(BF16).astype(F32)
        pieces = jnp.concatenate([tok_hi, tok_lo, q_hi, q - q_hi * LANES, a1, a2, a - a1 - a2, zero_row], axis=0)
        got = _nt(pieces.astype(BF16), onehot)
        idx_ref[e] = (got[0:1] * LANES + got[1:2]).astype(jnp.int32)
        dst_ref[e] = (got[2:3] * LANES + got[3:4]).astype(jnp.int32)
        gate_row = got[4:5] + got[5:6] + got[6:7]
        gate_col = jnp.sum(jnp.where(eye, gate_row, 0.0), axis=1, keepdims=True)
        gate_ref[e] = jnp.broadcast_to(gate_col, (cap, LANES))
        return carry

    lax.fori_loop(0, n_e, per_expert, 0)


def _route(aff_t, tri, batch, seq, cap):
    e, t = aff_t.shape
    blk = pl.BlockSpec((e, seq), lambda b: (0, b))
    colblk = pl.BlockSpec((e, cap, LANES), lambda b: (0, b, 0))
    rowblk = pl.BlockSpec((1, seq), lambda b: (0, b))
    slotblk = pl.BlockSpec((e, None, 1, cap), lambda b: (0, b, 0, 0))
    slot_i32 = jax.ShapeDtypeStruct((e, batch, 1, cap), jnp.int32)
    return pl.pallas_call(
        functools.partial(_route_kernel, cap=cap, seq=seq),
        grid=(batch,),
        in_specs=[blk, pl.BlockSpec((seq, seq), lambda b: (0, 0))],
        out_specs=[slotblk, colblk, slotblk, rowblk, rowblk],
        out_shape=[slot_i32, jax.ShapeDtypeStruct((e, batch * cap, LANES), F32), slot_i32,
                   jax.ShapeDtypeStruct((1, t), F32), jax.ShapeDtypeStruct((1, t), F32)],
        scratch_shapes=[pltpu.VMEM((e, seq), F32)] * 2,
        compiler_params=_cp(("arbitrary",)),
        name="route",
    )(aff_t, tri)


DMA_UNROLL = 32


def _moe_kernel(idx_ref, dst_ref, h_hbm, gate_ref, wg_ref, wu_ref, wd_ref, sorted_hbm,
                xg_ref, xe_ref, acc_ref, ob_ref, sem, *, rows):
    e = pl.program_id(0)
    f = pl.program_id(1)
    n_e = pl.num_programs(0)

    row_tiles = xe_ref.shape[1] // LANES

    def scatter():
        def body(i, carry):
            for u in range(DMA_UNROLL):
                r = i * DMA_UNROLL + u
                pltpu.make_async_copy(ob_ref.at[:, r, :], sorted_hbm.at[:, dst_ref[e * rows + r], :],
                                      sem.at[1]).start()
            return carry
        lax.fori_loop(0, rows // DMA_UNROLL, body, 0)

    def wait_scatter():
        for c in range(row_tiles):
            pltpu.make_async_copy(ob_ref.at[c], sorted_hbm.at[c, pl.ds(0, rows), :], sem.at[1]).wait()

    def gather(expert):
        def body(i, carry):
            for u in range(DMA_UNROLL):
                r = i * DMA_UNROLL + u
                tok = idx_ref[expert * rows + r]
                pltpu.make_async_copy(h_hbm.at[pl.ds(pl.multiple_of(tok * row_tiles, row_tiles), row_tiles), :],
                                      xg_ref.at[:, r, :], sem.at[0]).start()
            return carry
        lax.fori_loop(0, rows // DMA_UNROLL, body, 0)

    @pl.when((e == 0) & (f == 0))
    def _():
        gather(0)

    @pl.when(f == 0)
    def _():
        for c in range(row_tiles):
            pltpu.make_async_copy(h_hbm.at[pl.ds(0, rows), :], xg_ref.at[c], sem.at[0]).wait()
        for c in range(row_tiles):
            xe_ref[:, c * LANES:(c + 1) * LANES] = xg_ref[c].astype(BF16)
        acc_ref[...] = jnp.zeros_like(acc_ref)

        @pl.when(e + 1 < n_e)
        def _():
            gather(e + 1)

    xe = xe_ref[...]
    a = jnp.dot(xe, wg_ref[...].astype(BF16), preferred_element_type=F32)
    u = jnp.dot(xe, wu_ref[...].astype(BF16), preferred_element_type=F32)
    act = (a * jax.nn.sigmoid(a) * u).astype(BF16)
    acc_ref[...] += jnp.dot(act, wd_ref[...].astype(BF16), preferred_element_type=F32)

    @pl.when(f == pl.num_programs(1) - 1)
    def _():
        @pl.when(e > 0)
        def _():
            wait_scatter()

        gate = gate_ref[:, 0:1]
        for c in range(row_tiles):
            ob_ref[c] = acc_ref[:, c * LANES:(c + 1) * LANES] * gate
        scatter()

        @pl.when(e == n_e - 1)
        def _():
            wait_scatter()


def _moe(idx_flat, dst_flat, gate_col, h2, wg, wu, wd, layer):
    _, ne, d, ff = wg.shape
    rows = gate_col.shape[1]
    row_tiles = d // LANES
    tf = 256
    grid_spec = pltpu.PrefetchScalarGridSpec(
        num_scalar_prefetch=2,
        grid=(ne, ff // tf),
        in_specs=[pl.BlockSpec(memory_space=pl.ANY),
                  pl.BlockSpec((None, rows, LANES), lambda e, f, idx, dst: (e, 0, 0)),
                  pl.BlockSpec((None, None, d, tf), lambda e, f, idx, dst: (layer, e, 0, f)),
                  pl.BlockSpec((None, None, d, tf), lambda e, f, idx, dst: (layer, e, 0, f)),
                  pl.BlockSpec((None, None, tf, d), lambda e, f, idx, dst: (layer, e, f, 0))],
        out_specs=pl.BlockSpec(memory_space=pl.ANY),
        scratch_shapes=[pltpu.VMEM((row_tiles, rows, LANES), F32), pltpu.VMEM((rows, d), BF16),
                        pltpu.VMEM((rows, d), F32), pltpu.VMEM((row_tiles, rows, LANES), F32),
                        pltpu.SemaphoreType.DMA((2,))],
    )
    return pl.pallas_call(
        functools.partial(_moe_kernel, rows=rows),
        grid_spec=grid_spec,
        out_shape=jax.ShapeDtypeStruct((row_tiles, ne * rows, LANES), F32),
        compiler_params=_cp(("arbitrary", "arbitrary")),
        name="moe",
    )(idx_flat, dst_flat, h2, gate_col, wg, wu, wd)


COMBINE_CHUNK = 256
COMBINE_TM = 512
COMBINE_STREAMS = 4


def _combine_kernel(tile_ref, start_ref, cnt_ref, sorted_hbm, xn_ref, mod_ref, fn_ref, out_ref,
                    buf_ref, acc_ref, done_ref, sem, *, tm, total, final_norm):
    i = pl.program_id(0)
    n_tiles = pl.num_programs(0)
    row_tiles = buf_ref.shape[1]
    kc = COMBINE_CHUNK

    def span(tile):
        lo = tile_ref[tile]
        base = lax.shift_left(lax.shift_right_logical(lo, 3), 3)
        n_chunks = jnp.maximum(lax.shift_right_logical(tile_ref[tile + 1] - base + (kc - 1), 8), 1)
        return base, n_chunks

    def chunk_rows(base, j):
        return pl.multiple_of(jnp.minimum(base + j * kc, total - kc), 8)

    def chunk_copies(first_row, slot):
        per = row_tiles // COMBINE_STREAMS
        return [pltpu.make_async_copy(sorted_hbm.at[pl.ds(s * per, per), pl.ds(first_row, kc), :],
                                      buf_ref.at[slot, pl.ds(s * per, per)], sem.at[slot])
                for s in range(COMBINE_STREAMS)]

    def start_chunk(first_row, slot):
        for s, cp in enumerate(chunk_copies(first_row, slot)):
            cp.start(priority=s % 2)

    def wait_chunk(first_row, slot):
        for cp in chunk_copies(first_row, slot):
            cp.wait()

    base, n_chunks = span(i)

    @pl.when(i == 0)
    def _():
        done_ref[0] = 0
        start_chunk(chunk_rows(base, 0), 0)

    eye = (lax.broadcasted_iota(jnp.int32, (tm, tm), 0) == lax.broadcasted_iota(jnp.int32, (tm, tm), 1))
    lo_col = jnp.sum(jnp.where(eye, start_ref[...], 0.0), axis=1, keepdims=True)
    hi_col = lo_col + jnp.sum(jnp.where(eye, cnt_ref[...], 0.0), axis=1, keepdims=True)
    acc_ref[...] = jnp.zeros_like(acc_ref)

    def chunk(j, carry):
        done = done_ref[0]
        slot = done % 2
        first_row = chunk_rows(base, j)
        wait_chunk(first_row, slot)

        @pl.when(j + 1 < n_chunks)
        def _():
            start_chunk(chunk_rows(base, j + 1), 1 - slot)

        @pl.when((j + 1 == n_chunks) & (i + 1 < n_tiles))
        def _():
            start_chunk(chunk_rows(span(i + 1)[0], 0), 1 - slot)

        row = (first_row + lax.broadcasted_iota(jnp.int32, (1, kc), 1)).astype(F32)
        nominal = (base + j * kc).astype(F32)
        pick = ((row >= lo_col) & (row < hi_col) & (row >= nominal)).astype(BF16)
        for c in range(0, row_tiles, 2):
            rhs = jnp.concatenate([buf_ref[slot, c], buf_ref[slot, c + 1]], axis=1).astype(BF16)
            acc_ref[:, c * LANES:(c + 2) * LANES] += jnp.dot(pick, rhs, preferred_element_type=F32)
        done_ref[0] = done + 1
        return carry

    lax.fori_loop(0, n_chunks, chunk, 0)

    x = xn_ref[...] + mod_ref[5:6, :] * acc_ref[...]
    if final_norm:
        x = _gnorm(x, fn_ref[...])
    out_ref[...] = x


def _combine(tile_start, start_row, cnt_row, sorted_rows, xn, mod_l, fn, seq, final_norm):
    t, d = xn.shape
    row_tiles, total, _ = sorted_rows.shape
    tm = COMBINE_TM
    tpb = seq // tm
    grid_spec = pltpu.PrefetchScalarGridSpec(
        num_scalar_prefetch=1,
        grid=(t // tm,),
        in_specs=[pl.BlockSpec((1, tm), lambda i, ts: (0, i)),
                  pl.BlockSpec((1, tm), lambda i, ts: (0, i)),
                  pl.BlockSpec(memory_space=pl.ANY),
                  pl.BlockSpec((tm, d), lambda i, ts: (i, 0)),
                  pl.BlockSpec((None, 6, d), lambda i, ts: (i // tpb, 0, 0)),
                  pl.BlockSpec((1, d), lambda i, ts: (0, 0))],
        out_specs=pl.BlockSpec((tm, d), lambda i, ts: (i, 0)),
        scratch_shapes=[pltpu.VMEM((2, row_tiles, COMBINE_CHUNK, LANES), F32), pltpu.VMEM((tm, d), F32),
                        pltpu.SMEM((1,), jnp.int32), pltpu.SemaphoreType.DMA((2,))],
    )
    return pl.pallas_call(
        functools.partial(_combine_kernel, tm=tm, total=total, final_norm=final_norm),
        grid_spec=grid_spec,
        out_shape=jax.ShapeDtypeStruct((t, d), F32),
        compiler_params=_cp(("arbitrary",)),
        name="combine",
    )(tile_start, start_row, cnt_row, sorted_rows, xn, mod_l, fn)


def _rope_tables(positions):
    b, s = positions.shape
    inv = jnp.power(jnp.float32(ROPE_THETA), -jnp.arange(0, ROPE_DIM, 2, dtype=F32) / ROPE_DIM)
    ang = positions.astype(F32)[..., None] * inv
    cos, sin = jnp.cos(ang), jnp.sin(ang)
    rest = HEAD_DIM - ROPE_DIM
    c64 = jnp.concatenate([cos, cos, jnp.ones((b, s, rest), F32)], axis=-1)
    s64 = jnp.concatenate([-sin, sin, jnp.zeros((b, s, rest), F32)], axis=-1)
    return jnp.tile(c64, (1, 1, 4)), jnp.tile(s64, (1, 1, 4))


def _dft_tables(n):
    k = jnp.arange(n, dtype=jnp.int32)
    ang = ((k[:, None] * k[None, :]) % n).astype(F32) * (2.0 * math.pi / n)
    return jnp.cos(ang), jnp.sin(ang)


def _dft_tables_split(n):
    k = jnp.arange(n, dtype=jnp.int32)[:, None]
    m0 = jnp.arange(LANES, dtype=jnp.int32)[None, :]
    m1 = jnp.arange(n // LANES, dtype=jnp.int32)[None, :]
    a = ((k * m0) % n).astype(F32) * (2.0 * math.pi / n)
    b = ((k * m1 * LANES) % n).astype(F32) * (2.0 * math.pi / n)
    ca, sa, cb, sb = jnp.cos(a)[:, None, :], jnp.sin(a)[:, None, :], jnp.cos(b)[:, :, None], jnp.sin(b)[:, :, None]
    return (ca * cb - sa * sb).reshape(n, n), (sa * cb + ca * sb).reshape(n, n)


def _block_diag(blocks):
    n, a, _ = blocks.shape
    eye = jnp.eye(n, dtype=blocks.dtype)
    return (eye[:, None, :, None] * blocks[:, :, None, :]).reshape(n * a, n * a)


def kernel(x, c, positions, ada_w, ada_b, norm_mix, w_in, fnet_w, sgu_norm, sgu_w, sgu_b, swa_sink,
           group_norm, w_out, norm_ffn, router_w, exp_w_gate, exp_w_up, exp_w_down, final_norm):
    batch, seq, d = x.shape
    depth = ada_w.shape[0]
    t = batch * seq
    cap = max(1, EC_FACTOR * seq // N_EXPERTS)

    rows = 16
    c_pad = jnp.zeros((rows, d), F32).at[:batch].set(c)
    mod = _ada(c_pad, ada_w, ada_b)[:, :batch].reshape(depth, batch, 6, d)

    cos_q, sin_q = _rope_tables(positions)
    dft_c, dft_s = _dft_tables_split(seq)
    dft_c, dft_s = dft_c.astype(BF16), dft_s.astype(BF16)
    cc, sc = _dft_tables(HEAD_DIM)
    bdc = _block_diag(jnp.broadcast_to(cc, (4, HEAD_DIM, HEAD_DIM)))
    bds = _block_diag(jnp.broadcast_to(sc, (4, HEAD_DIM, HEAD_DIM)))
    tri = (jnp.arange(seq)[:, None] < jnp.arange(seq)[None, :]).astype(BF16)

    x2 = x.reshape(t, d)
    for l in range(depth):
        mod_l = mod[l]
        proj = _in_proj(x2, mod_l, norm_mix[l].reshape(1, d), w_in[l].astype(BF16), seq)

        bdw = jnp.stack([_block_diag(fnet_w[l, 0:4]), _block_diag(fnet_w[l, 4:8])])
        ya = _fourier(proj, bdc, bds, bdw, dft_c, dft_s, batch, seq).reshape(t, GROUP_WIDTH)
        sgu_w_pairs = jnp.concatenate([sgu_w[l, 0::2], sgu_w[l, 1::2]], axis=-1).astype(BF16)
        yb = _sgu(proj, sgu_norm[l].reshape(1, GROUP_WIDTH), sgu_w_pairs, sgu_b[l].T)
        sink_row = jnp.repeat(swa_sink[l], HEAD_DIM).reshape(1, GROUP_WIDTH)
        yc = _swa(proj, cos_q, sin_q, sink_row, batch, seq)
        yd = _dilated(proj, cos_q, sin_q, batch, seq)

        r_t = router_w[l].T
        r_hi = r_t.astype(BF16)
        r_lo = (r_t - r_hi.astype(F32)).astype(BF16)
        rcat = jnp.concatenate([r_hi, r_lo], axis=0)
        xn, h2, aff_t = _out_proj(ya, yb, yc, yd, group_norm[l].reshape(1, d), w_out[l].astype(BF16),
                                  x2, mod_l, norm_ffn[l].reshape(1, d), rcat, seq)
        idx, gate_col, dst, cnt_row, start_row = _route(aff_t, tri, batch, seq, cap)
        sorted_rows = _moe(idx.reshape(-1), dst.reshape(-1), gate_col, h2,
                           exp_w_gate, exp_w_up, exp_w_down, l)
        n_pairs = batch * N_EXPERTS * cap
        tile_start = jnp.concatenate([start_row[0, ::COMBINE_TM].astype(jnp.int32), jnp.full((1,), n_pairs, jnp.int32)])
        x2 = _combine(tile_start, start_row, cnt_row, sorted_rows, xn, mod_l, final_norm.reshape(1, d), seq,
                      l == depth - 1)
    return x2.reshape(batch, seq, d)
```

```python
import functools
import math

import jax
import jax.numpy as jnp
from jax import lax
from jax.experimental import pallas as pl
from jax.experimental.pallas import tpu as pltpu

F32 = jnp.float32
BF16 = jnp.bfloat16
ACT = jnp.bfloat16

HEAD_DIM = 64
N_GROUP_HEADS = 8
GROUP_WIDTH = 512
CHUNK = 128
SWA_HALF_WINDOW = 128
DIL_CONFIGS = ((128, 1), (512, 4), (2048, 16))
ROPE_THETA = 500000.0
ROPE_DIM = 16
N_EXPERTS = 16
EC_FACTOR = 2
EPS = 1e-6
NEG_INF = -1e30

LANES = 128
SEC = 256
SEC_XA, SEC_U, SEC_V, SEC_QC, SEC_KVC, SEC_QD, SEC_KD, SEC_VD = 0, 2, 4, 6, 8, 9, 11, 13
N_SEC = 15

VMEM_LIMIT = 56 * 1024 * 1024


def _cp(sem, vmem=None):
    return pltpu.CompilerParams(dimension_semantics=sem, vmem_limit_bytes=vmem or VMEM_LIMIT)


def _nt(a, b):
    return lax.dot_general(a, b, (((1,), (1,)), ((), ())), preferred_element_type=F32)


ADA_STREAMS = 4


def _ada_kernel(c_ref, *refs):
    w_refs, b_ref, o_ref = refs[:ADA_STREAMS], refs[ADA_STREAMS], refs[ADA_STREAMS + 1]

    @pl.when(pl.program_id(1) == 0)
    def _():
        o_ref[0] = jnp.broadcast_to(b_ref[0], o_ref.shape[1:])

    c = c_ref[...]
    s = (c * jax.nn.sigmoid(c)).astype(BF16)
    tn = w_refs[0].shape[2]
    for j, w_ref in enumerate(w_refs):
        o_ref[0, :, j * tn:(j + 1) * tn] += jnp.dot(s, w_ref[0].astype(BF16), preferred_element_type=F32)


def _ada(c_pad, ada_w, ada_b):
    n_layers, d, n6 = ada_w.shape
    rows = c_pad.shape[0]
    tk = 256
    tn = n6 // ADA_STREAMS
    w_specs = [pl.BlockSpec((1, tk, tn), functools.partial(lambda l, k, j: (l, k, j), j=j))
               for j in range(ADA_STREAMS)]
    return pl.pallas_call(
        _ada_kernel,
        grid=(n_layers, d // tk),
        in_specs=[pl.BlockSpec((rows, tk), lambda l, k: (0, k))] + w_specs + [
            pl.BlockSpec((1, 1, n6), lambda l, k: (l, 0, 0))],
        out_specs=pl.BlockSpec((1, rows, n6), lambda l, k: (l, 0, 0)),
        out_shape=jax.ShapeDtypeStruct((n_layers, rows, n6), F32),
        compiler_params=_cp(("arbitrary", "arbitrary")),
        name="ada",
    )(c_pad, *([ada_w] * ADA_STREAMS), ada_b.reshape(n_layers, 1, n6))


NORM_ROWS = 16
IN_SECS = 5


def _in_kernel(x_ref, mod_ref, g_ref, w_ref, o_ref, h_ref):
    @pl.when(pl.program_id(1) == 0)
    def _():
        gain = g_ref[...] * (1.0 + mod_ref[1:2, :])
        shift = mod_ref[0:1, :]

        def chunk(c, carry):
            rows = pl.ds(pl.multiple_of(c * NORM_ROWS, NORM_ROWS), NORM_ROWS)
            x = x_ref[rows, :]
            r = lax.rsqrt(jnp.mean(x * x, axis=-1, keepdims=True) + EPS)
            h_ref[rows, :] = (x * r * gain + shift).astype(BF16)
            return carry

        lax.fori_loop(0, x_ref.shape[0] // NORM_ROWS, chunk, 0, unroll=8)

    res = jnp.dot(h_ref[...], w_ref[...], preferred_element_type=F32)
    for s in range(IN_SECS):
        o_ref[s] = res[:, s * SEC:(s + 1) * SEC].astype(o_ref.dtype)


def _in_proj(x2, mod_l, norm_g, w_bf, seq):
    t, d = x2.shape
    tm = 1024
    tpb = seq // tm
    return pl.pallas_call(
        _in_kernel,
        grid=(t // tm, N_SEC // IN_SECS),
        in_specs=[pl.BlockSpec((tm, d), lambda i, j: (i, 0)),
                  pl.BlockSpec((None, 6, d), lambda i, j: (i // tpb, 0, 0)),
                  pl.BlockSpec((1, d), lambda i, j: (0, 0)),
                  pl.BlockSpec((d, IN_SECS * SEC), lambda i, j: (0, j))],
        out_specs=pl.BlockSpec((IN_SECS, tm, SEC), lambda i, j: (j, i, 0)),
        out_shape=jax.ShapeDtypeStruct((N_SEC, t, SEC), ACT),
        scratch_shapes=[pltpu.VMEM((tm, d), BF16)],
        compiler_params=_cp(("arbitrary", "arbitrary")),
        name="in_proj",
    )(x2, mod_l, norm_g, w_bf)


def _f1_kernel(x_ref, bdc_ref, bds_ref, bdw_ref, z_ref, m_ref, *, scale):
    @pl.when(pl.program_id(1) == 0)
    def _():
        w = bdw_ref[...]
        mc = jnp.dot(bdc_ref[...], w, preferred_element_type=F32, precision=lax.Precision.HIGHEST)
        msn = jnp.dot(bds_ref[...], w, preferred_element_type=F32, precision=lax.Precision.HIGHEST)
        m_ref[:, 0:SEC] = (mc * scale).astype(BF16)
        m_ref[:, SEC:2 * SEC] = (msn * scale).astype(BF16)

    z_ref[...] = jnp.dot(x_ref[...].astype(BF16), m_ref[...], preferred_element_type=F32).astype(BF16)


def _f2_kernel(c_ref, s_ref, z_ref, o_ref):
    z = z_ref[...]
    o_ref[...] = (jnp.dot(c_ref[...], z[:, 0:SEC], preferred_element_type=F32)
                  - jnp.dot(s_ref[...], z[:, SEC:2 * SEC], preferred_element_type=F32)).astype(o_ref.dtype)


def _fourier(proj, bdc, bds, bdw, dft_c, dft_s, batch, seq):
    t = batch * seq
    tm = 1024
    scale = 1.0 / math.sqrt(seq * HEAD_DIM)
    z = pl.pallas_call(
        functools.partial(_f1_kernel, scale=scale),
        grid=(2, t // tm),
        in_specs=[pl.BlockSpec((None, tm, SEC), lambda s, i: (SEC_XA + s, i, 0)),
                  pl.BlockSpec((SEC, SEC), lambda s, i: (0, 0)),
                  pl.BlockSpec((SEC, SEC), lambda s, i: (0, 0)),
                  pl.BlockSpec((None, SEC, SEC), lambda s, i: (s, 0, 0))],
        out_specs=pl.BlockSpec((None, tm, 2 * SEC), lambda s, i: (s, i, 0)),
        out_shape=jax.ShapeDtypeStruct((2, t, 2 * SEC), BF16),
        scratch_shapes=[pltpu.VMEM((SEC, 2 * SEC), BF16)],
        compiler_params=_cp(("arbitrary", "arbitrary")),
        name="fnet_chan",
    )(proj, bdc, bds, bdw)
    rt = 512
    return pl.pallas_call(
        _f2_kernel,
        grid=(seq // rt, batch, 2),
        in_specs=[pl.BlockSpec((rt, seq), lambda r, b, s: (r, 0)),
                  pl.BlockSpec((rt, seq), lambda r, b, s: (r, 0)),
                  pl.BlockSpec((None, seq, 2 * SEC), lambda r, b, s: (s, b, 0))],
        out_specs=pl.BlockSpec((None, rt, SEC), lambda r, b, s: (b, r, s)),
        out_shape=jax.ShapeDtypeStruct((batch, seq, GROUP_WIDTH), ACT),
        compiler_params=_cp(("arbitrary", "arbitrary", "arbitrary")),
        name="fnet_seq",
    )(dft_c, dft_s, z)


def _sgu_kernel(u_ref, v_ref, g_ref, w_ref, b_ref, o_ref, *, tm):
    gv = [jax.nn.gelu(v_ref[s].astype(F32)) for s in range(2)]
    ms = (jnp.sum(gv[0] * gv[0], axis=-1, keepdims=True)
          + jnp.sum(gv[1] * gv[1], axis=-1, keepdims=True)) * (1.0 / GROUP_WIDTH)
    r = lax.rsqrt(ms + EPS)
    vn = [(gv[s] * r * g_ref[:, s * SEC:(s + 1) * SEC]).astype(BF16) for s in range(2)]
    lo = lax.broadcasted_iota(jnp.int32, (CHUNK, 2 * HEAD_DIM), 1) < HEAD_DIM
    zero = jnp.zeros((CHUNK, 2 * HEAD_DIM), BF16)
    for s in range(2):
        gu = jax.nn.gelu(u_ref[s].astype(F32))
        for c in range(tm // CHUNK):
            rows = slice(c * CHUNK, (c + 1) * CHUNK)
            for pp in range(2):
                pair = 2 * s + pp
                lanes = slice(pp * 2 * HEAD_DIM, (pp + 1) * 2 * HEAD_DIM)
                vp = vn[s][rows, lanes]
                rhs = jnp.concatenate([jnp.where(lo, vp, zero), jnp.where(lo, zero, vp)], axis=0)
                z = jnp.dot(w_ref[pair], rhs, preferred_element_type=F32)
                bias = jnp.where(lo, b_ref[:, 2 * pair:2 * pair + 1], b_ref[:, 2 * pair + 1:2 * pair + 2])
                o_ref[rows, s * SEC + pp * 2 * HEAD_DIM:s * SEC + (pp + 1) * 2 * HEAD_DIM] = (gu[rows, lanes] * (z + bias)).astype(o_ref.dtype)


def _sgu(proj, sgu_norm, sgu_w_bf, sgu_bt):
    t = proj.shape[1]
    tm = 512
    return pl.pallas_call(
        functools.partial(_sgu_kernel, tm=tm),
        grid=(t // tm,),
        in_specs=[pl.BlockSpec((2, tm, SEC), lambda i: (SEC_U // 2, i, 0)),
                  pl.BlockSpec((2, tm, SEC), lambda i: (SEC_V // 2, i, 0)),
                  pl.BlockSpec((1, GROUP_WIDTH), lambda i: (0, 0)),
                  pl.BlockSpec((N_GROUP_HEADS // 2, CHUNK, 2 * CHUNK), lambda i: (0, 0, 0)),
                  pl.BlockSpec((CHUNK, N_GROUP_HEADS), lambda i: (0, 0))],
        out_specs=pl.BlockSpec((tm, GROUP_WIDTH), lambda i: (i, 0)),
        out_shape=jax.ShapeDtypeStruct((t, GROUP_WIDTH), ACT),
        compiler_params=_cp(("arbitrary",)),
        name="sgu",
    )(proj, proj, sgu_norm, sgu_w_bf, sgu_bt)


ATT_TQ = 128
PAIR = 2 * HEAD_DIM


def _rope(x, cos_t, sin_t):
    n = x.shape[-1]
    half = ROPE_DIM // 2
    lane = lax.broadcasted_iota(jnp.int32, x.shape, 1) % HEAD_DIM
    swapped = jnp.where(lane < half, pltpu.roll(x, n - half, 1), pltpu.roll(x, half, 1))
    return x * cos_t + swapped * sin_t


def _lo_half(shape):
    return lax.broadcasted_iota(jnp.int32, shape, 1) < HEAD_DIM


def _stage_scores(q_pairs, k, valid, s_ref, slot, grp):
    tq = q_pairs[0].shape[0]
    lo = _lo_half(q_pairs[0].shape)
    zero = jnp.zeros_like(q_pairs[0])
    blocks = []
    for q in q_pairs:
        blocks += [jnp.where(lo, q, zero), jnp.where(lo, zero, q)]
    s = _nt(jnp.concatenate(blocks, axis=0), k)
    for i in range(len(blocks)):
        s_ref[slot, grp, i * tq:(i + 1) * tq, :] = jnp.where(valid, s[i * tq:(i + 1) * tq], NEG_INF)


def _stage_softmax(s_ref, p_ref, m_ref, l_ref, slot, grp, tq):
    s = s_ref[slot, grp]
    m = jnp.max(s, axis=-1, keepdims=True)
    p = jnp.exp(s - m)
    p_ref[slot, grp] = p.astype(BF16)
    l = jnp.sum(p, axis=-1, keepdims=True)
    n_pairs = s.shape[0] // (2 * tq)
    shp = (tq, PAIR)
    lo = _lo_half(shp)
    for i in range(n_pairs):
        a, b = slice(2 * i * tq, (2 * i + 1) * tq), slice((2 * i + 1) * tq, (2 * i + 2) * tq)
        m_ref[slot, grp * n_pairs + i] = jnp.where(lo, jnp.broadcast_to(m[a], shp), jnp.broadcast_to(m[b], shp))
        l_ref[slot, grp * n_pairs + i] = jnp.where(lo, jnp.broadcast_to(l[a], shp), jnp.broadcast_to(l[b], shp))


def _stage_pv(p_ref, v, slot, grp, tq):
    o = jnp.dot(p_ref[slot, grp], v, preferred_element_type=F32)
    lo = _lo_half((tq, PAIR))
    return [jnp.where(lo, o[2 * i * tq:(2 * i + 1) * tq], o[(2 * i + 1) * tq:(2 * i + 2) * tq])
            for i in range(o.shape[0] // (2 * tq))]


ATT_ITEMS = 4


def _pipeline(n_items, stage_a, stage_b, stage_c):
    def body(jj, carry):
        for stage in (stage_a, stage_b, stage_c):
            for t in range(ATT_ITEMS):
                stage(jj * ATT_ITEMS + t, t)
        return carry

    lax.fori_loop(0, n_items // ATT_ITEMS, body, 0)


def _band_valid(tq, tk, w, delta):
    d = (lax.broadcasted_iota(jnp.int32, (tq, tk), 1) - lax.broadcasted_iota(jnp.int32, (tq, tk), 0)) + delta
    return (d <= w) & (d >= -w)


def _swa_kernel(q_ref, kv_ref, cos_ref, sin_ref, sink_ref, o_ref, qs_ref, ks_ref, vs_ref,
                s_ref, p_ref, m_ref, l_ref, *, seq):
    w, tq = SWA_HALF_WINDOW, ATT_TQ
    tk = tq + 2 * w
    cos, sin = cos_ref[...], sin_ref[...]
    for s in range(2):
        qs_ref[s] = (_rope(q_ref[s].astype(F32), cos, sin) * (HEAD_DIM ** -0.5)).astype(BF16)
    kk = _rope(kv_ref[:, 0:PAIR].astype(F32), cos[:, 0:PAIR], sin[:, 0:PAIR])
    vv = kv_ref[:, PAIR:2 * PAIR].astype(F32)
    lo = lax.broadcasted_iota(jnp.int32, kk.shape, 1) < HEAD_DIM
    kk_sw, vv_sw = pltpu.roll(kk, HEAD_DIM, 1), pltpu.roll(vv, HEAD_DIM, 1)
    ks_ref[0] = jnp.where(lo, kk, kk_sw).astype(BF16)
    ks_ref[1] = jnp.where(lo, kk_sw, kk).astype(BF16)
    vs_ref[0] = jnp.where(lo, vv, vv_sw).astype(BF16)
    vs_ref[1] = jnp.where(lo, vv_sw, vv).astype(BF16)

    def coords(j):
        q0 = pl.multiple_of(j * tq, tq)
        return q0, pl.multiple_of(jnp.clip(q0 - w, 0, seq - tk), tq)

    def scores(j, slot):
        q0, k0 = coords(j)
        valid = _band_valid(tq, tk, w, k0 - q0)
        for g in range(2):
            q_pairs = [qs_ref[g, pl.ds(q0, tq), 0:PAIR], qs_ref[g, pl.ds(q0, tq), PAIR:2 * PAIR]]
            _stage_scores(q_pairs, ks_ref[g, pl.ds(k0, tk), :], valid, s_ref, slot, g)

    def softmax(j, slot):
        for g in range(2):
            _stage_softmax(s_ref, p_ref, m_ref, l_ref, slot, g, tq)

    def output(j, slot):
        q0, k0 = coords(j)
        for g in range(2):
            outs = _stage_pv(p_ref, vs_ref[g, pl.ds(k0, tk), :], slot, g, tq)
            for i, o in enumerate(outs):
                pair = 2 * g + i
                m, l = m_ref[slot, pair], l_ref[slot, pair]
                sink = sink_ref[:, pair * PAIR:(pair + 1) * PAIR]
                m2 = jnp.maximum(m, sink)
                a = jnp.exp(m - m2)
                o_ref[pl.ds(q0, tq), pair * PAIR:(pair + 1) * PAIR] = (
                    o * a / (l * a + jnp.exp(sink - m2))).astype(o_ref.dtype)

    _pipeline(seq // tq, scores, softmax, output)


def _swa(proj, cos_q, sin_q, sink_row, batch, seq):
    t = batch * seq
    tab = pl.BlockSpec((None, seq, SEC), lambda b: (b, 0, 0))
    return pl.pallas_call(
        functools.partial(_swa_kernel, seq=seq),
        grid=(batch,),
        in_specs=[pl.BlockSpec((2, seq, SEC), lambda b: (SEC_QC // 2, b, 0)),
                  pl.BlockSpec((None, seq, SEC), lambda b: (SEC_KVC, b, 0)),
                  tab, tab,
                  pl.BlockSpec((1, GROUP_WIDTH), lambda b: (0, 0))],
        out_specs=pl.BlockSpec((seq, GROUP_WIDTH), lambda b: (b, 0)),
        out_shape=jax.ShapeDtypeStruct((t, GROUP_WIDTH), ACT),
        scratch_shapes=[pltpu.VMEM((2, seq, SEC), BF16),
                        pltpu.VMEM((2, seq, PAIR), BF16),
                        pltpu.VMEM((2, seq, PAIR), BF16),
                        pltpu.VMEM((ATT_ITEMS, 2, 4 * ATT_TQ, ATT_TQ + 2 * SWA_HALF_WINDOW), F32),
                        pltpu.VMEM((ATT_ITEMS, 2, 4 * ATT_TQ, ATT_TQ + 2 * SWA_HALF_WINDOW), BF16),
                        pltpu.VMEM((ATT_ITEMS, 4, ATT_TQ, PAIR), F32),
                        pltpu.VMEM((ATT_ITEMS, 4, ATT_TQ, PAIR), F32)],
        compiler_params=_cp(("arbitrary",)),
        name="swa",
    )(proj, proj, cos_q, sin_q, sink_row)


def _dil_kernel(q_ref, k_ref, v_ref, cos_ref, sin_ref, o_ref,
                qf_ref, kf_ref, vf_ref, m_ref, l_ref, acc_ref, qd_ref, kd_ref, vd_ref,
                s_ref, p_ref, mt_ref, lt_ref, *, seq):
    cos, sin = cos_ref[...], sin_ref[...]
    q = _rope(q_ref[...].astype(F32), cos, sin) * (HEAD_DIM ** -0.5)
    k = _rope(k_ref[...].astype(F32), cos, sin)
    for pair in range(2):
        lanes = slice(pair * PAIR, (pair + 1) * PAIR)
        qf_ref[pair] = q[:, lanes]
        kf_ref[pair] = k[:, lanes]
        vf_ref[pair] = v_ref[:, lanes].astype(F32)
    m_ref[...] = jnp.full(m_ref.shape, NEG_INF, F32)
    l_ref[...] = jnp.zeros(l_ref.shape, F32)
    acc_ref[...] = jnp.zeros(acc_ref.shape, F32)

    for window, dil in DIL_CONFIGS:
        w = window // (2 * dil)
        ls = seq // dil
        tq = min(ATT_TQ, ls)
        tk = min(tq + 2 * w, ls)

        def rows_of(start, size, dil=dil):
            return pl.ds(start, size) if dil == 1 else pl.ds(start, size, stride=dil)

        def permute(r, carry, ls=ls, rows_of=rows_of):
            dst = pl.ds(pl.multiple_of(r * ls, ls), ls)
            for pair in range(2):
                qd_ref[pair, dst, :] = qf_ref[pair, rows_of(r, ls), :].astype(BF16)
                kd_ref[pair, dst, :] = kf_ref[pair, rows_of(r, ls), :].astype(BF16)
                vd_ref[pair, dst, :] = vf_ref[pair, rows_of(r, ls), :].astype(BF16)
            return carry

        lax.fori_loop(0, dil, permute, 0)
        tiles_per_res = ls // tq

        def coords(j, w=w, ls=ls, tq=tq, tk=tk, tpr=tiles_per_res):
            r = j // tpr
            q0 = (j % tpr) * tq
            k0 = jnp.clip(q0 - w, 0, ls - tk)
            return r, q0, pl.multiple_of(r * ls + k0, w), k0 - q0

        def scores(j, slot, w=w, tq=tq, tk=tk, coords=coords):
            _, _, krow, delta = coords(j)
            valid = _band_valid(tq, tk, w, delta)
            qrow = pl.multiple_of(j * tq, tq)
            for pair in range(2):
                _stage_scores([qd_ref[pair, pl.ds(qrow, tq), :]], kd_ref[pair, pl.ds(krow, tk), :], valid,
                              s_ref.at[:, :, :, 0:tk], slot, pair)

        def softmax(j, slot, tq=tq, tk=tk):
            for pair in range(2):
                _stage_softmax(s_ref.at[:, :, :, 0:tk], p_ref.at[:, :, :, 0:tk], mt_ref, lt_ref, slot, pair, tq)

        def merge(j, slot, tq=tq, tk=tk, dil=dil, coords=coords, rows_of=rows_of):
            r, q0, krow, _ = coords(j)
            rows = rows_of(r + dil * q0, tq)
            for pair in range(2):
                (o,) = _stage_pv(p_ref.at[:, :, :, 0:tk], vd_ref[pair, pl.ds(krow, tk), :], slot, pair, tq)
                m, l = mt_ref[slot, pair], lt_ref[slot, pair]
                m_old = m_ref[pair, rows, :]
                m_new = jnp.maximum(m_old, m)
                a, b = jnp.exp(m_old - m_new), jnp.exp(m - m_new)
                acc_ref[pair, rows, :] = acc_ref[pair, rows, :] * a + o * b
                l_ref[pair, rows, :] = l_ref[pair, rows, :] * a + l * b
                m_ref[pair, rows, :] = m_new

        _pipeline(seq // tq, scores, softmax, merge)

    for pair in range(2):
        o_ref[:, pair * PAIR:(pair + 1) * PAIR] = (acc_ref[pair] / l_ref[pair]).astype(o_ref.dtype)


def _dilated(proj, cos_q, sin_q, batch, seq):
    t = batch * seq

    def sec(base):
        return pl.BlockSpec((None, seq, SEC), lambda b, g: (base + g, b, 0))

    tab = pl.BlockSpec((None, seq, SEC), lambda b, g: (b, 0, 0))
    return pl.pallas_call(
        functools.partial(_dil_kernel, seq=seq),
        grid=(batch, 2),
        in_specs=[sec(SEC_QD), sec(SEC_KD), sec(SEC_VD), tab, tab],
        out_specs=pl.BlockSpec((seq, SEC), lambda b, g: (b, g)),
        out_shape=jax.ShapeDtypeStruct((t, GROUP_WIDTH), ACT),
        scratch_shapes=[pltpu.VMEM((2, seq, PAIR), F32)] * 6 + [pltpu.VMEM((2, seq, PAIR), BF16)] * 3 + [
            pltpu.VMEM((ATT_ITEMS, 2, 2 * ATT_TQ, 2 * ATT_TQ), F32),
            pltpu.VMEM((ATT_ITEMS, 2, 2 * ATT_TQ, 2 * ATT_TQ), BF16),
            pltpu.VMEM((ATT_ITEMS, 2, ATT_TQ, PAIR), F32),
            pltpu.VMEM((ATT_ITEMS, 2, ATT_TQ, PAIR), F32)],
        compiler_params=_cp(("arbitrary", "arbitrary")),
        name="dilated",
    )(proj, proj, proj, cos_q, sin_q)


def _gnorm(y, g):
    return y * lax.rsqrt(jnp.mean(y * y, axis=-1, keepdims=True) + EPS) * g


OUT_SUB = 256
OUT_TM = 512


def _out_kernel(ya_ref, yb_ref, yc_ref, yd_ref, gn_ref, w_ref, x_ref, mod_ref, nf_ref, rcat_ref,
                xn_ref, h_ref, aff_ref):
    ys = (ya_ref, yb_ref, yc_ref, yd_ref)
    gw = GROUP_WIDTH
    row_tiles = x_ref.shape[1] // LANES
    for sub in range(x_ref.shape[0] // OUT_SUB):
        rows = slice(sub * OUT_SUB, (sub + 1) * OUT_SUB)
        acc = None
        for g in range(4):
            yn = _gnorm(ys[g][rows, :].astype(F32), gn_ref[:, g * gw:(g + 1) * gw]).astype(BF16)
            part = jnp.dot(yn, w_ref[g * gw:(g + 1) * gw, :], preferred_element_type=F32)
            acc = part if acc is None else acc + part
        xn = x_ref[rows, :] + mod_ref[2:3, :] * acc
        xn_ref[rows, :] = xn
        h = _gnorm(xn, nf_ref[...]) * (1.0 + mod_ref[4:5, :]) + mod_ref[3:4, :]
        for c in range(row_tiles):
            h_ref[pl.ds(sub * OUT_SUB * row_tiles + c, OUT_SUB, stride=row_tiles), :] = h[:, c * LANES:(c + 1) * LANES]
        h_hi = h.astype(BF16)
        h_lo = (h - h_hi.astype(F32)).astype(BF16)
        a = _nt(rcat_ref[...], h_hi)
        b = _nt(rcat_ref[0:N_EXPERTS, :], h_lo)
        logits = a[0:N_EXPERTS] + a[N_EXPERTS:2 * N_EXPERTS] + b
        mxl = jnp.max(logits, axis=0, keepdims=True)
        ex = jnp.exp(logits - mxl)
        aff_ref[:, rows] = ex / jnp.sum(ex, axis=0, keepdims=True)


def _out_proj(ya, yb, yc, yd, gn, w_bf, x2, mod_l, nf, rcat, seq):
    t, d = x2.shape
    tm = OUT_TM
    tpb = seq // tm
    ROW_TILES = d // LANES
    ysp = pl.BlockSpec((tm, GROUP_WIDTH), lambda i: (i, 0))
    return pl.pallas_call(
        _out_kernel,
        grid=(t // tm,),
        in_specs=[ysp] * 4 + [
            pl.BlockSpec((1, d), lambda i: (0, 0)),
            pl.BlockSpec((d, d), lambda i: (0, 0)),
            pl.BlockSpec((tm, d), lambda i: (i, 0)),
            pl.BlockSpec((None, 6, d), lambda i: (i // tpb, 0, 0)),
            pl.BlockSpec((1, d), lambda i: (0, 0)),
            pl.BlockSpec((2 * N_EXPERTS, d), lambda i: (0, 0))],
        out_specs=[pl.BlockSpec((tm, d), lambda i: (i, 0)),
                   pl.BlockSpec((tm * ROW_TILES, LANES), lambda i: (i, 0)),
                   pl.BlockSpec((N_EXPERTS, tm), lambda i: (0, i))],
        out_shape=[jax.ShapeDtypeStruct((t, d), F32),
                   jax.ShapeDtypeStruct((t * ROW_TILES, LANES), F32),
                   jax.ShapeDtypeStruct((N_EXPERTS, t), F32)],
        compiler_params=_cp(("arbitrary",)),
        name="out_proj",
    )(ya, yb, yc, yd, gn, w_bf, x2, mod_l, nf, rcat)


def _route_kernel(aff_ref, tri_ref, idx_ref, gate_ref, dst_ref, cnt_ref, start_ref,
                  pos_ref, q_ref, *, cap, seq):
    aff = aff_ref[...]

    def as_f32(bits):
        return lax.bitcast_convert_type(bits, F32)

    def step(k, thr):
        cand = thr | lax.shift_left(jnp.int32(1), 30 - k)
        cnt = jnp.sum((aff >= as_f32(cand)).astype(jnp.int32), axis=1, keepdims=True)
        return jnp.where(cnt >= cap, cand, thr)

    thr = lax.fori_loop(0, 31, step, jnp.zeros((aff.shape[0], 1), jnp.int32))
    gt = aff >= as_f32(thr + 1)
    eq = (aff >= as_f32(thr)) & jnp.logical_not(gt)
    need = (cap - jnp.sum(gt.astype(jnp.int32), axis=1, keepdims=True)).astype(F32)
    eq_rank = jnp.dot(eq.astype(BF16), tri_ref[...], preferred_element_type=F32)
    sel = gt | (eq & (eq_rank < need))
    n_e = aff.shape[0]
    sel_f = sel.astype(F32)
    sel_bf = sel.astype(BF16)
    pos_ref[...] = jnp.where(sel, jnp.dot(sel_bf, tri_ref[...], preferred_element_type=F32), -1.0)

    cnt = jnp.sum(sel_f, axis=0, keepdims=True)
    cnt8 = jnp.broadcast_to(cnt, (8, seq)).astype(BF16)
    start = (jnp.dot(cnt8, tri_ref[...], preferred_element_type=F32)[0:1]
             + (pl.program_id(0) * (n_e * cap)).astype(F32))
    lower = (lax.broadcasted_iota(jnp.int32, (n_e, n_e), 1)
             < lax.broadcasted_iota(jnp.int32, (n_e, n_e), 0)).astype(BF16)
    q_ref[...] = start + jnp.dot(lower, sel_bf, preferred_element_type=F32)
    cnt_ref[...] = cnt
    start_ref[...] = start

    tok = lax.broadcasted_iota(jnp.int32, (1, seq), 1) + pl.program_id(0) * seq
    tok_hi = lax.shift_right_logical(tok, 7).astype(F32)
    tok_lo = (tok & (LANES - 1)).astype(F32)
    slot = lax.broadcasted_iota(jnp.int32, (cap, seq), 0).astype(F32)
    eye = (lax.broadcasted_iota(jnp.int32, (cap, cap), 0) == lax.broadcasted_iota(jnp.int32, (cap, cap), 1))
    zero_row = jnp.zeros((1, seq), F32)

    def per_expert(e, carry):
        onehot = (pos_ref[pl.ds(e, 1), :] == slot).astype(BF16)
        q = q_ref[pl.ds(e, 1), :]
        q_hi = jnp.floor(q * (1.0 / LANES))
        a = aff_ref[pl.ds(e, 1), :]
        a1 = a.astype(BF16).astype(F32)
        a2 = (a - a1).astype(BF16).astype(F32)
        pieces = jnp.concatenate([tok_hi, tok_lo, q_hi, q - q_hi * LANES, a1, a2, a - a1 - a2, zero_row], axis=0)
        got = _nt(pieces.astype(BF16), onehot)
        idx_ref[e] = (got[0:1] * LANES + got[1:2]).astype(jnp.int32)
        dst_ref[e] = (got[2:3] * LANES + got[3:4]).astype(jnp.int32)
        gate_row = got[4:5] + got[5:6] + got[6:7]
        gate_col = jnp.sum(jnp.where(eye, gate_row, 0.0), axis=1, keepdims=True)
        gate_ref[e] = jnp.broadcast_to(gate_col, (cap, LANES))
        return carry

    lax.fori_loop(0, n_e, per_expert, 0)


def _route(aff_t, tri, batch, seq, cap):
    e, t = aff_t.shape
    blk = pl.BlockSpec((e, seq), lambda b: (0, b))
    colblk = pl.BlockSpec((e, cap, LANES), lambda b: (0, b, 0))
    rowblk = pl.BlockSpec((1, seq), lambda b: (0, b))
    slotblk = pl.BlockSpec((e, None, 1, cap), lambda b: (0, b, 0, 0))
    slot_i32 = jax.ShapeDtypeStruct((e, batch, 1, cap), jnp.int32)
    return pl.pallas_call(
        functools.partial(_route_kernel, cap=cap, seq=seq),
        grid=(batch,),
        in_specs=[blk, pl.BlockSpec((seq, seq), lambda b: (0, 0))],
        out_specs=[slotblk, colblk, slotblk, rowblk, rowblk],
        out_shape=[slot_i32, jax.ShapeDtypeStruct((e, batch * cap, LANES), F32), slot_i32,
                   jax.ShapeDtypeStruct((1, t), F32), jax.ShapeDtypeStruct((1, t), F32)],
        scratch_shapes=[pltpu.VMEM((e, seq), F32)] * 2,
        compiler_params=_cp(("arbitrary",)),
        name="route",
    )(aff_t, tri)


DMA_UNROLL = 32


def _moe_kernel(idx_ref, dst_ref, h_hbm, gate_ref, wg_ref, wu_ref, wd_ref, sorted_hbm,
                xg_ref, xe_ref, acc_ref, ob_ref, sem, *, rows):
    e = pl.program_id(0)
    f = pl.program_id(1)
    n_e = pl.num_programs(0)

    row_tiles = xe_ref.shape[1] // LANES

    def scatter():
        def body(i, carry):
            for u in range(DMA_UNROLL):
                r = i * DMA_UNROLL + u
                pltpu.make_async_copy(ob_ref.at[:, r, :], sorted_hbm.at[:, dst_ref[e * rows + r], :],
                                      sem.at[1]).start()
            return carry
        lax.fori_loop(0, rows // DMA_UNROLL, body, 0)

    def wait_scatter():
        for c in range(row_tiles):
            pltpu.make_async_copy(ob_ref.at[c], sorted_hbm.at[c, pl.ds(0, rows), :], sem.at[1]).wait()

    def gather(expert):
        def body(i, carry):
            for u in range(DMA_UNROLL):
                r = i * DMA_UNROLL + u
                tok = idx_ref[expert * rows + r]
                pltpu.make_async_copy(h_hbm.at[pl.ds(pl.multiple_of(tok * row_tiles, row_tiles), row_tiles), :],
                                      xg_ref.at[:, r, :], sem.at[0]).start()
            return carry
        lax.fori_loop(0, rows // DMA_UNROLL, body, 0)

    @pl.when((e == 0) & (f == 0))
    def _():
        gather(0)

    @pl.when(f == 0)
    def _():
        for c in range(row_tiles):
            pltpu.make_async_copy(h_hbm.at[pl.ds(0, rows), :], xg_ref.at[c], sem.at[0]).wait()
        for c in range(row_tiles):
            xe_ref[:, c * LANES:(c + 1) * LANES] = xg_ref[c].astype(BF16)
        acc_ref[...] = jnp.zeros_like(acc_ref)

        @pl.when(e + 1 < n_e)
        def _():
            gather(e + 1)

    xe = xe_ref[...]
    a = jnp.dot(xe, wg_ref[...].astype(BF16), preferred_element_type=F32)
    u = jnp.dot(xe, wu_ref[...].astype(BF16), preferred_element_type=F32)
    act = (a * jax.nn.sigmoid(a) * u).astype(BF16)
    acc_ref[...] += jnp.dot(act, wd_ref[...].astype(BF16), preferred_element_type=F32)

    @pl.when(f == pl.num_programs(1) - 1)
    def _():
        @pl.when(e > 0)
        def _():
            wait_scatter()

        gate = gate_ref[:, 0:1]
        for c in range(row_tiles):
            ob_ref[c] = acc_ref[:, c * LANES:(c + 1) * LANES] * gate
        scatter()

        @pl.when(e == n_e - 1)
        def _():
            wait_scatter()


def _moe(idx_flat, dst_flat, gate_col, h2, wg, wu, wd, layer):
    _, ne, d, ff = wg.shape
    rows = gate_col.shape[1]
    row_tiles = d // LANES
    tf = 256
    grid_spec = pltpu.PrefetchScalarGridSpec(
        num_scalar_prefetch=2,
        grid=(ne, ff // tf),
        in_specs=[pl.BlockSpec(memory_space=pl.ANY),
                  pl.BlockSpec((None, rows, LANES), lambda e, f, idx, dst: (e, 0, 0)),
                  pl.BlockSpec((None, None, d, tf), lambda e, f, idx, dst: (layer, e, 0, f)),
                  pl.BlockSpec((None, None, d, tf), lambda e, f, idx, dst: (layer, e, 0, f)),
                  pl.BlockSpec((None, None, tf, d), lambda e, f, idx, dst: (layer, e, f, 0))],
        out_specs=pl.BlockSpec(memory_space=pl.ANY),
        scratch_shapes=[pltpu.VMEM((row_tiles, rows, LANES), F32), pltpu.VMEM((rows, d), BF16),
                        pltpu.VMEM((rows, d), F32), pltpu.VMEM((row_tiles, rows, LANES), F32),
                        pltpu.SemaphoreType.DMA((2,))],
    )
    return pl.pallas_call(
        functools.partial(_moe_kernel, rows=rows),
        grid_spec=grid_spec,
        out_shape=jax.ShapeDtypeStruct((row_tiles, ne * rows, LANES), F32),
        compiler_params=_cp(("arbitrary", "arbitrary")),
        name="moe",
    )(idx_flat, dst_flat, h2, gate_col, wg, wu, wd)


COMBINE_CHUNK = 256
COMBINE_TM = 512
COMBINE_STREAMS = 4
COMBINE_SLOTS = 3


def _combine_kernel(tile_ref, start_ref, cnt_ref, sorted_hbm, xn_ref, mod_ref, fn_ref, out_ref,
                    buf_ref, acc_ref, done_ref, sem, *, tm, total, final_norm):
    i = pl.program_id(0)
    n_tiles = pl.num_programs(0)
    row_tiles = buf_ref.shape[1]
    kc = COMBINE_CHUNK

    def span(tile):
        lo = tile_ref[tile]
        base = lax.shift_left(lax.shift_right_logical(lo, 3), 3)
        n_chunks = jnp.maximum(lax.shift_right_logical(tile_ref[tile + 1] - base + (kc - 1), 8), 1)
        return base, n_chunks

    def chunk_rows(base, j):
        return pl.multiple_of(jnp.minimum(base + j * kc, total - kc), 8)

    def chunk_copies(first_row, slot):
        per = row_tiles // COMBINE_STREAMS
        return [pltpu.make_async_copy(sorted_hbm.at[pl.ds(s * per, per), pl.ds(first_row, kc), :],
                                      buf_ref.at[slot, pl.ds(s * per, per)], sem.at[slot])
                for s in range(COMBINE_STREAMS)]

    def start_chunk(first_row, slot):
        for s, cp in enumerate(chunk_copies(first_row, slot)):
            cp.start(priority=s % 2)

    def wait_chunk(first_row, slot):
        for cp in chunk_copies(first_row, slot):
            cp.wait()

    def issue_next():
        t = done_ref[2]

        @pl.when(t < n_tiles)
        def _():
            j = done_ref[3]
            base_t, n_t = span(t)
            issued = done_ref[1]
            start_chunk(chunk_rows(base_t, j), issued % COMBINE_SLOTS)
            done_ref[1] = issued + 1
            wrap = j + 1 >= n_t
            done_ref[2] = jnp.where(wrap, t + 1, t)
            done_ref[3] = jnp.where(wrap, 0, j + 1)

    base, n_chunks = span(i)

    @pl.when(i == 0)
    def _():
        for k in range(4):
            done_ref[k] = 0
        for _ in range(COMBINE_SLOTS - 1):
            issue_next()

    eye = (lax.broadcasted_iota(jnp.int32, (tm, tm), 0) == lax.broadcasted_iota(jnp.int32, (tm, tm), 1))
    lo_col = jnp.sum(jnp.where(eye, start_ref[...], 0.0), axis=1, keepdims=True)
    hi_col = lo_col + jnp.sum(jnp.where(eye, cnt_ref[...], 0.0), axis=1, keepdims=True)
    acc_ref[...] = jnp.zeros_like(acc_ref)

    def chunk(j, carry):
        done = done_ref[0]
        slot = done % COMBINE_SLOTS
        first_row = chunk_rows(base, j)
        wait_chunk(first_row, slot)
        issue_next()

        row = (first_row + lax.broadcasted_iota(jnp.int32, (1, kc), 1)).astype(F32)
        nominal = (base + j * kc).astype(F32)
        pick = ((row >= lo_col) & (row < hi_col) & (row >= nominal)).astype(BF16)
        for c in range(0, row_tiles, 2):
            rhs = jnp.concatenate([buf_ref[slot, c], buf_ref[slot, c + 1]], axis=1).astype(BF16)
            acc_ref[:, c * LANES:(c + 2) * LANES] += jnp.dot(pick, rhs, preferred_element_type=F32)
        done_ref[0] = done + 1
        return carry

    lax.fori_loop(0, n_chunks, chunk, 0)

    x = xn_ref[...] + mod_ref[5:6, :] * acc_ref[...]
    if final_norm:
        x = _gnorm(x, fn_ref[...])
    out_ref[...] = x


def _combine(tile_start, start_row, cnt_row, sorted_rows, xn, mod_l, fn, seq, final_norm):
    t, d = xn.shape
    row_tiles, total, _ = sorted_rows.shape
    tm = COMBINE_TM
    tpb = seq // tm
    grid_spec = pltpu.PrefetchScalarGridSpec(
        num_scalar_prefetch=1,
        grid=(t // tm,),
        in_specs=[pl.BlockSpec((1, tm), lambda i, ts: (0, i)),
                  pl.BlockSpec((1, tm), lambda i, ts: (0, i)),
                  pl.BlockSpec(memory_space=pl.ANY),
                  pl.BlockSpec((tm, d), lambda i, ts: (i, 0)),
                  pl.BlockSpec((None, 6, d), lambda i, ts: (i // tpb, 0, 0)),
                  pl.BlockSpec((1, d), lambda i, ts: (0, 0))],
        out_specs=pl.BlockSpec((tm, d), lambda i, ts: (i, 0)),
        scratch_shapes=[pltpu.VMEM((COMBINE_SLOTS, row_tiles, COMBINE_CHUNK, LANES), F32),
                        pltpu.VMEM((tm, d), F32), pltpu.SMEM((4,), jnp.int32),
                        pltpu.SemaphoreType.DMA((COMBINE_SLOTS,))],
    )
    return pl.pallas_call(
        functools.partial(_combine_kernel, tm=tm, total=total, final_norm=final_norm),
        grid_spec=grid_spec,
        out_shape=jax.ShapeDtypeStruct((t, d), F32),
        compiler_params=_cp(("arbitrary",)),
        name="combine",
    )(tile_start, start_row, cnt_row, sorted_rows, xn, mod_l, fn)


def _rope_tables(positions):
    b, s = positions.shape
    inv = jnp.power(jnp.float32(ROPE_THETA), -jnp.arange(0, ROPE_DIM, 2, dtype=F32) / ROPE_DIM)
    ang = positions.astype(F32)[..., None] * inv
    cos, sin = jnp.cos(ang), jnp.sin(ang)
    rest = HEAD_DIM - ROPE_DIM
    c64 = jnp.concatenate([cos, cos, jnp.ones((b, s, rest), F32)], axis=-1)
    s64 = jnp.concatenate([-sin, sin, jnp.zeros((b, s, rest), F32)], axis=-1)
    return jnp.tile(c64, (1, 1, 4)), jnp.tile(s64, (1, 1, 4))


def _dft_tables(n):
    k = jnp.arange(n, dtype=jnp.int32)
    ang = ((k[:, None] * k[None, :]) % n).astype(F32) * (2.0 * math.pi / n)
    return jnp.cos(ang), jnp.sin(ang)


def _dft_tables_split(n):
    k = jnp.arange(n, dtype=jnp.int32)[:, None]
    m0 = jnp.arange(LANES, dtype=jnp.int32)[None, :]
    m1 = jnp.arange(n // LANES, dtype=jnp.int32)[None, :]
    a = ((k * m0) % n).astype(F32) * (2.0 * math.pi / n)
    b = ((k * m1 * LANES) % n).astype(F32) * (2.0 * math.pi / n)
    reps = n // LANES
    ca, sa = jnp.tile(jnp.cos(a), (1, reps)), jnp.tile(jnp.sin(a), (1, reps))
    cb, sb = jnp.repeat(jnp.cos(b), LANES, axis=1), jnp.repeat(jnp.sin(b), LANES, axis=1)
    return ca * cb - sa * sb, sa * cb + ca * sb


def _block_diag(blocks):
    n, a, _ = blocks.shape
    eye = jnp.eye(n, dtype=blocks.dtype)
    return (eye[:, None, :, None] * blocks[:, :, None, :]).reshape(n * a, n * a)


def kernel(x, c, positions, ada_w, ada_b, norm_mix, w_in, fnet_w, sgu_norm, sgu_w, sgu_b, swa_sink,
           group_norm, w_out, norm_ffn, router_w, exp_w_gate, exp_w_up, exp_w_down, final_norm):
    batch, seq, d = x.shape
    depth = ada_w.shape[0]
    t = batch * seq
    cap = max(1, EC_FACTOR * seq // N_EXPERTS)

    rows = 16
    c_pad = jnp.zeros((rows, d), F32).at[:batch].set(c)
    mod = _ada(c_pad, ada_w, ada_b)[:, :batch].reshape(depth, batch, 6, d)

    cos_q, sin_q = _rope_tables(positions)
    dft_c, dft_s = _dft_tables_split(seq)
    dft_c, dft_s = dft_c.astype(BF16), dft_s.astype(BF16)
    cc, sc = _dft_tables(HEAD_DIM)
    bdc = _block_diag(jnp.broadcast_to(cc, (4, HEAD_DIM, HEAD_DIM)))
    bds = _block_diag(jnp.broadcast_to(sc, (4, HEAD_DIM, HEAD_DIM)))
    tri = (jnp.arange(seq)[:, None] < jnp.arange(seq)[None, :]).astype(BF16)

    x2 = x.reshape(t, d)
    for l in range(depth):
        mod_l = mod[l]
        proj = _in_proj(x2, mod_l, norm_mix[l].reshape(1, d), w_in[l].astype(BF16), seq)

        bdw = jnp.stack([_block_diag(fnet_w[l, 0:4]), _block_diag(fnet_w[l, 4:8])])
        ya = _fourier(proj, bdc, bds, bdw, dft_c, dft_s, batch, seq).reshape(t, GROUP_WIDTH)
        sgu_w_pairs = jnp.concatenate([sgu_w[l, 0::2], sgu_w[l, 1::2]], axis=-1).astype(BF16)
        yb = _sgu(proj, sgu_norm[l].reshape(1, GROUP_WIDTH), sgu_w_pairs, sgu_b[l].T)
        sink_row = jnp.repeat(swa_sink[l], HEAD_DIM).reshape(1, GROUP_WIDTH)
        yc = _swa(proj, cos_q, sin_q, sink_row, batch, seq)
        yd = _dilated(proj, cos_q, sin_q, batch, seq)

        r_t = router_w[l].T
        r_hi = r_t.astype(BF16)
        r_lo = (r_t - r_hi.astype(F32)).astype(BF16)
        rcat = jnp.concatenate([r_hi, r_lo], axis=0)
        xn, h2, aff_t = _out_proj(ya, yb, yc, yd, group_norm[l].reshape(1, d), w_out[l].astype(BF16),
                                  x2, mod_l, norm_ffn[l].reshape(1, d), rcat, seq)
        idx, gate_col, dst, cnt_row, start_row = _route(aff_t, tri, batch, seq, cap)
        sorted_rows = _moe(idx.reshape(-1), dst.reshape(-1), gate_col, h2,
                           exp_w_gate, exp_w_up, exp_w_down, l)
        n_pairs = batch * N_EXPERTS * cap
        tile_start = jnp.concatenate([start_row[0, ::COMBINE_TM].astype(jnp.int32), jnp.full((1,), n_pairs, jnp.int32)])
        x2 = _combine(tile_start, start_row, cnt_row, sorted_rows, xn, mod_l, final_norm.reshape(1, d), seq,
                      l == depth - 1)
    return x2.reshape(batch, seq, d)
```

```python
import functools
import math

import jax
import jax.numpy as jnp
from jax import lax
from jax.experimental import pallas as pl
from jax.experimental.pallas import tpu as pltpu

F32 = jnp.float32
BF16 = jnp.bfloat16
ACT = jnp.bfloat16

HEAD_DIM = 64
N_GROUP_HEADS = 8
GROUP_WIDTH = 512
CHUNK = 128
SWA_HALF_WINDOW = 128
DIL_CONFIGS = ((128, 1), (512, 4), (2048, 16))
ROPE_THETA = 500000.0
ROPE_DIM = 16
N_EXPERTS = 16
EC_FACTOR = 2
EPS = 1e-6
NEG_INF = -1e30

LANES = 128
SEC = 256
SEC_XA, SEC_U, SEC_V, SEC_QC, SEC_KVC, SEC_QD, SEC_KD, SEC_VD = 0, 2, 4, 6, 8, 9, 11, 13
N_SEC = 15

VMEM_LIMIT = 56 * 1024 * 1024


def _cp(sem, vmem=None):
    return pltpu.CompilerParams(dimension_semantics=sem, vmem_limit_bytes=vmem or VMEM_LIMIT)


def _nt(a, b):
    return lax.dot_general(a, b, (((1,), (1,)), ((), ())), preferred_element_type=F32)


ADA_STREAMS = 4


def _ada_kernel(c_ref, *refs):
    w_refs, b_ref, o_ref = refs[:ADA_STREAMS], refs[ADA_STREAMS], refs[ADA_STREAMS + 1]

    @pl.when(pl.program_id(1) == 0)
    def _():
        o_ref[0] = jnp.broadcast_to(b_ref[0], o_ref.shape[1:])

    c = c_ref[...]
    s = (c * jax.nn.sigmoid(c)).astype(BF16)
    tn = w_refs[0].shape[2]
    for j, w_ref in enumerate(w_refs):
        o_ref[0, :, j * tn:(j + 1) * tn] += jnp.dot(s, w_ref[0].astype(BF16), preferred_element_type=F32)


def _ada(c_pad, ada_w, ada_b):
    n_layers, d, n6 = ada_w.shape
    rows = c_pad.shape[0]
    tk = 256
    tn = n6 // ADA_STREAMS
    w_specs = [pl.BlockSpec((1, tk, tn), functools.partial(lambda l, k, j: (l, k, j), j=j))
               for j in range(ADA_STREAMS)]
    return pl.pallas_call(
        _ada_kernel,
        grid=(n_layers, d // tk),
        in_specs=[pl.BlockSpec((rows, tk), lambda l, k: (0, k))] + w_specs + [
            pl.BlockSpec((1, 1, n6), lambda l, k: (l, 0, 0))],
        out_specs=pl.BlockSpec((1, rows, n6), lambda l, k: (l, 0, 0)),
        out_shape=jax.ShapeDtypeStruct((n_layers, rows, n6), F32),
        compiler_params=_cp(("arbitrary", "arbitrary")),
        name="ada",
    )(c_pad, *([ada_w] * ADA_STREAMS), ada_b.reshape(n_layers, 1, n6))


NORM_ROWS = 16
IN_SECS = 5


def _in_kernel(x_ref, mod_ref, g_ref, w_ref, o_ref, h_ref):
    @pl.when(pl.program_id(1) == 0)
    def _():
        gain = g_ref[...] * (1.0 + mod_ref[1:2, :])
        shift = mod_ref[0:1, :]

        def chunk(c, carry):
            rows = pl.ds(pl.multiple_of(c * NORM_ROWS, NORM_ROWS), NORM_ROWS)
            x = x_ref[rows, :]
            r = lax.rsqrt(jnp.mean(x * x, axis=-1, keepdims=True) + EPS)
            h_ref[rows, :] = (x * r * gain + shift).astype(BF16)
            return carry

        lax.fori_loop(0, x_ref.shape[0] // NORM_ROWS, chunk, 0, unroll=8)

    res = jnp.dot(h_ref[...], w_ref[...], preferred_element_type=F32)
    for s in range(IN_SECS):
        o_ref[s] = res[:, s * SEC:(s + 1) * SEC].astype(o_ref.dtype)


def _in_proj(x2, mod_l, norm_g, w_bf, layer, seq):
    t, d = x2.shape
    tm = 1024
    tpb = seq // tm
    return pl.pallas_call(
        _in_kernel,
        grid=(t // tm, N_SEC // IN_SECS),
        in_specs=[pl.BlockSpec((tm, d), lambda i, j: (i, 0)),
                  pl.BlockSpec((None, 6, d), lambda i, j: (i // tpb, 0, 0)),
                  pl.BlockSpec((1, d), lambda i, j: (0, 0)),
                  pl.BlockSpec((None, d, IN_SECS * SEC), lambda i, j: (layer, 0, j))],
        out_specs=pl.BlockSpec((IN_SECS, tm, SEC), lambda i, j: (j, i, 0)),
        out_shape=jax.ShapeDtypeStruct((N_SEC, t, SEC), ACT),
        scratch_shapes=[pltpu.VMEM((tm, d), BF16)],
        compiler_params=_cp(("arbitrary", "arbitrary")),
        name="in_proj",
    )(x2, mod_l, norm_g, w_bf)


def _f1_kernel(x_ref, bdc_ref, bds_ref, bdw_ref, z_ref, m_ref, *, scale):
    @pl.when(pl.program_id(1) == 0)
    def _():
        w = bdw_ref[...]
        mc = jnp.dot(bdc_ref[...], w, preferred_element_type=F32, precision=lax.Precision.HIGHEST)
        msn = jnp.dot(bds_ref[...], w, preferred_element_type=F32, precision=lax.Precision.HIGHEST)
        m_ref[:, 0:SEC] = (mc * scale).astype(BF16)
        m_ref[:, SEC:2 * SEC] = (msn * scale).astype(BF16)

    z_ref[...] = jnp.dot(x_ref[...].astype(BF16), m_ref[...], preferred_element_type=F32).astype(BF16)


def _f2_kernel(c_ref, s_ref, z_ref, o_ref):
    z = z_ref[...]
    o_ref[...] = (jnp.dot(c_ref[...], z[:, 0:SEC], preferred_element_type=F32)
                  - jnp.dot(s_ref[...], z[:, SEC:2 * SEC], preferred_element_type=F32)).astype(o_ref.dtype)


def _fourier(proj, bdc, bds, bdw, dft_c, dft_s, batch, seq):
    t = batch * seq
    tm = 1024
    scale = 1.0 / math.sqrt(seq * HEAD_DIM)
    z = pl.pallas_call(
        functools.partial(_f1_kernel, scale=scale),
        grid=(2, t // tm),
        in_specs=[pl.BlockSpec((None, tm, SEC), lambda s, i: (SEC_XA + s, i, 0)),
                  pl.BlockSpec((SEC, SEC), lambda s, i: (0, 0)),
                  pl.BlockSpec((SEC, SEC), lambda s, i: (0, 0)),
                  pl.BlockSpec((None, SEC, SEC), lambda s, i: (s, 0, 0))],
        out_specs=pl.BlockSpec((None, tm, 2 * SEC), lambda s, i: (s, i, 0)),
        out_shape=jax.ShapeDtypeStruct((2, t, 2 * SEC), BF16),
        scratch_shapes=[pltpu.VMEM((SEC, 2 * SEC), BF16)],
        compiler_params=_cp(("arbitrary", "arbitrary")),
        name="fnet_chan",
    )(proj, bdc, bds, bdw)
    rt = 512
    return pl.pallas_call(
        _f2_kernel,
        grid=(seq // rt, batch, 2),
        in_specs=[pl.BlockSpec((rt, seq), lambda r, b, s: (r, 0)),
                  pl.BlockSpec((rt, seq), lambda r, b, s: (r, 0)),
                  pl.BlockSpec((None, seq, 2 * SEC), lambda r, b, s: (s, b, 0))],
        out_specs=pl.BlockSpec((None, rt, SEC), lambda r, b, s: (b, r, s)),
        out_shape=jax.ShapeDtypeStruct((batch, seq, GROUP_WIDTH), ACT),
        compiler_params=_cp(("arbitrary", "arbitrary", "arbitrary")),
        name="fnet_seq",
    )(dft_c, dft_s, z)


def _sgu_kernel(u_ref, v_ref, g_ref, w_ref, b_ref, o_ref, *, tm):
    gv = [jax.nn.gelu(v_ref[s].astype(F32)) for s in range(2)]
    ms = (jnp.sum(gv[0] * gv[0], axis=-1, keepdims=True)
          + jnp.sum(gv[1] * gv[1], axis=-1, keepdims=True)) * (1.0 / GROUP_WIDTH)
    r = lax.rsqrt(ms + EPS)
    vn = [(gv[s] * r * g_ref[:, s * SEC:(s + 1) * SEC]).astype(BF16) for s in range(2)]
    lo = lax.broadcasted_iota(jnp.int32, (CHUNK, 2 * HEAD_DIM), 1) < HEAD_DIM
    zero = jnp.zeros((CHUNK, 2 * HEAD_DIM), BF16)
    for s in range(2):
        gu = jax.nn.gelu(u_ref[s].astype(F32))
        for c in range(tm // CHUNK):
            rows = slice(c * CHUNK, (c + 1) * CHUNK)
            for pp in range(2):
                pair = 2 * s + pp
                lanes = slice(pp * 2 * HEAD_DIM, (pp + 1) * 2 * HEAD_DIM)
                vp = vn[s][rows, lanes]
                rhs = jnp.concatenate([jnp.where(lo, vp, zero), jnp.where(lo, zero, vp)], axis=0)
                z = jnp.dot(w_ref[pair], rhs, preferred_element_type=F32)
                bias = jnp.where(lo, b_ref[:, 2 * pair:2 * pair + 1], b_ref[:, 2 * pair + 1:2 * pair + 2])
                o_ref[rows, s * SEC + pp * 2 * HEAD_DIM:s * SEC + (pp + 1) * 2 * HEAD_DIM] = (gu[rows, lanes] * (z + bias)).astype(o_ref.dtype)


def _sgu(proj, sgu_norm, sgu_w_bf, sgu_bt):
    t = proj.shape[1]
    tm = 512
    return pl.pallas_call(
        functools.partial(_sgu_kernel, tm=tm),
        grid=(t // tm,),
        in_specs=[pl.BlockSpec((2, tm, SEC), lambda i: (SEC_U // 2, i, 0)),
                  pl.BlockSpec((2, tm, SEC), lambda i: (SEC_V // 2, i, 0)),
                  pl.BlockSpec((1, GROUP_WIDTH), lambda i: (0, 0)),
                  pl.BlockSpec((N_GROUP_HEADS // 2, CHUNK, 2 * CHUNK), lambda i: (0, 0, 0)),
                  pl.BlockSpec((CHUNK, N_GROUP_HEADS), lambda i: (0, 0))],
        out_specs=pl.BlockSpec((tm, GROUP_WIDTH), lambda i: (i, 0)),
        out_shape=jax.ShapeDtypeStruct((t, GROUP_WIDTH), ACT),
        compiler_params=_cp(("arbitrary",)),
        name="sgu",
    )(proj, proj, sgu_norm, sgu_w_bf, sgu_bt)


ATT_TQ = 128
PAIR = 2 * HEAD_DIM


def _rope(x, cos_t, sin_t):
    n = x.shape[-1]
    half = ROPE_DIM // 2
    lane = lax.broadcasted_iota(jnp.int32, x.shape, 1) % HEAD_DIM
    swapped = jnp.where(lane < half, pltpu.roll(x, n - half, 1), pltpu.roll(x, half, 1))
    return x * cos_t + swapped * sin_t


def _lo_half(shape):
    return lax.broadcasted_iota(jnp.int32, shape, 1) < HEAD_DIM


def _stage_scores(q_pairs, k, valid, s_ref, slot, grp):
    tq = q_pairs[0].shape[0]
    lo = _lo_half(q_pairs[0].shape)
    zero = jnp.zeros_like(q_pairs[0])
    blocks = []
    for q in q_pairs:
        blocks += [jnp.where(lo, q, zero), jnp.where(lo, zero, q)]
    s = _nt(jnp.concatenate(blocks, axis=0), k)
    for i in range(len(blocks)):
        s_ref[slot, grp, i * tq:(i + 1) * tq, :] = jnp.where(valid, s[i * tq:(i + 1) * tq], NEG_INF)


def _stage_softmax(s_ref, p_ref, m_ref, l_ref, slot, grp, tq):
    s = s_ref[slot, grp]
    m = jnp.max(s, axis=-1, keepdims=True)
    p = jnp.exp(s - m)
    p_ref[slot, grp] = p.astype(BF16)
    l = jnp.sum(p, axis=-1, keepdims=True)
    n_pairs = s.shape[0] // (2 * tq)
    shp = (tq, PAIR)
    lo = _lo_half(shp)
    for i in range(n_pairs):
        a, b = slice(2 * i * tq, (2 * i + 1) * tq), slice((2 * i + 1) * tq, (2 * i + 2) * tq)
        m_ref[slot, grp * n_pairs + i] = jnp.where(lo, jnp.broadcast_to(m[a], shp), jnp.broadcast_to(m[b], shp))
        l_ref[slot, grp * n_pairs + i] = jnp.where(lo, jnp.broadcast_to(l[a], shp), jnp.broadcast_to(l[b], shp))


def _stage_pv(p_ref, v, slot, grp, tq):
    o = jnp.dot(p_ref[slot, grp], v, preferred_element_type=F32)
    lo = _lo_half((tq, PAIR))
    return [jnp.where(lo, o[2 * i * tq:(2 * i + 1) * tq], o[(2 * i + 1) * tq:(2 * i + 2) * tq])
            for i in range(o.shape[0] // (2 * tq))]


ATT_ITEMS = 8


def _pipeline(n_items, stage_a, stage_b, stage_c):
    def body(jj, carry):
        for stage in (stage_a, stage_b, stage_c):
            for t in range(ATT_ITEMS):
                stage(jj * ATT_ITEMS + t, t)
        return carry

    lax.fori_loop(0, n_items // ATT_ITEMS, body, 0)


def _band_valid(tq, tk, w, delta):
    d = (lax.broadcasted_iota(jnp.int32, (tq, tk), 1) - lax.broadcasted_iota(jnp.int32, (tq, tk), 0)) + delta
    return (d <= w) & (d >= -w)


def _swa_kernel(q_ref, kv_ref, cos_ref, sin_ref, sink_ref, o_ref, qs_ref, ks_ref, vs_ref,
                s_ref, p_ref, m_ref, l_ref, *, seq):
    w, tq = SWA_HALF_WINDOW, ATT_TQ
    tk = tq + 2 * w
    cos, sin = cos_ref[...], sin_ref[...]
    for s in range(2):
        qs_ref[s] = (_rope(q_ref[s].astype(F32), cos, sin) * (HEAD_DIM ** -0.5)).astype(BF16)
    kk = _rope(kv_ref[:, 0:PAIR].astype(F32), cos[:, 0:PAIR], sin[:, 0:PAIR])
    vv = kv_ref[:, PAIR:2 * PAIR].astype(F32)
    lo = lax.broadcasted_iota(jnp.int32, kk.shape, 1) < HEAD_DIM
    kk_sw, vv_sw = pltpu.roll(kk, HEAD_DIM, 1), pltpu.roll(vv, HEAD_DIM, 1)
    ks_ref[0] = jnp.where(lo, kk, kk_sw).astype(BF16)
    ks_ref[1] = jnp.where(lo, kk_sw, kk).astype(BF16)
    vs_ref[0] = jnp.where(lo, vv, vv_sw).astype(BF16)
    vs_ref[1] = jnp.where(lo, vv_sw, vv).astype(BF16)

    def coords(j):
        q0 = pl.multiple_of(j * tq, tq)
        return q0, pl.multiple_of(jnp.clip(q0 - w, 0, seq - tk), tq)

    def scores(j, slot):
        q0, k0 = coords(j)
        valid = _band_valid(tq, tk, w, k0 - q0)
        for g in range(2):
            q_pairs = [qs_ref[g, pl.ds(q0, tq), 0:PAIR], qs_ref[g, pl.ds(q0, tq), PAIR:2 * PAIR]]
            _stage_scores(q_pairs, ks_ref[g, pl.ds(k0, tk), :], valid, s_ref, slot, g)

    def softmax(j, slot):
        for g in range(2):
            _stage_softmax(s_ref, p_ref, m_ref, l_ref, slot, g, tq)

    def output(j, slot):
        q0, k0 = coords(j)
        for g in range(2):
            outs = _stage_pv(p_ref, vs_ref[g, pl.ds(k0, tk), :], slot, g, tq)
            for i, o in enumerate(outs):
                pair = 2 * g + i
                m, l = m_ref[slot, pair], l_ref[slot, pair]
                sink = sink_ref[:, pair * PAIR:(pair + 1) * PAIR]
                m2 = jnp.maximum(m, sink)
                a = jnp.exp(m - m2)
                o_ref[pl.ds(q0, tq), pair * PAIR:(pair + 1) * PAIR] = (
                    o * a / (l * a + jnp.exp(sink - m2))).astype(o_ref.dtype)

    _pipeline(seq // tq, scores, softmax, output)


def _swa(proj, cos_q, sin_q, sink_row, batch, seq):
    t = batch * seq
    tab = pl.BlockSpec((None, seq, SEC), lambda b: (b, 0, 0))
    return pl.pallas_call(
        functools.partial(_swa_kernel, seq=seq),
        grid=(batch,),
        in_specs=[pl.BlockSpec((2, seq, SEC), lambda b: (SEC_QC // 2, b, 0)),
                  pl.BlockSpec((None, seq, SEC), lambda b: (SEC_KVC, b, 0)),
                  tab, tab,
                  pl.BlockSpec((1, GROUP_WIDTH), lambda b: (0, 0))],
        out_specs=pl.BlockSpec((seq, GROUP_WIDTH), lambda b: (b, 0)),
        out_shape=jax.ShapeDtypeStruct((t, GROUP_WIDTH), ACT),
        scratch_shapes=[pltpu.VMEM((2, seq, SEC), BF16),
                        pltpu.VMEM((2, seq, PAIR), BF16),
                        pltpu.VMEM((2, seq, PAIR), BF16),
                        pltpu.VMEM((ATT_ITEMS, 2, 4 * ATT_TQ, ATT_TQ + 2 * SWA_HALF_WINDOW), F32),
                        pltpu.VMEM((ATT_ITEMS, 2, 4 * ATT_TQ, ATT_TQ + 2 * SWA_HALF_WINDOW), BF16),
                        pltpu.VMEM((ATT_ITEMS, 4, ATT_TQ, PAIR), F32),
                        pltpu.VMEM((ATT_ITEMS, 4, ATT_TQ, PAIR), F32)],
        compiler_params=_cp(("arbitrary",)),
        name="swa",
    )(proj, proj, cos_q, sin_q, sink_row)


def _dil_kernel(q_ref, k_ref, v_ref, cos_ref, sin_ref, o_ref,
                qf_ref, kf_ref, vf_ref, m_ref, l_ref, acc_ref, qd_ref, kd_ref, vd_ref,
                s_ref, p_ref, mt_ref, lt_ref, *, seq):
    cos, sin = cos_ref[...], sin_ref[...]
    q = _rope(q_ref[...].astype(F32), cos, sin) * (HEAD_DIM ** -0.5)
    k = _rope(k_ref[...].astype(F32), cos, sin)
    for pair in range(2):
        lanes = slice(pair * PAIR, (pair + 1) * PAIR)
        qf_ref[pair] = q[:, lanes]
        kf_ref[pair] = k[:, lanes]
        vf_ref[pair] = v_ref[:, lanes].astype(F32)

    for window, dil in DIL_CONFIGS:
        w = window // (2 * dil)
        ls = seq // dil
        tq = min(ATT_TQ, ls)
        tk = min(tq + 2 * w, ls)

        def rows_of(start, size, dil=dil):
            return pl.ds(start, size) if dil == 1 else pl.ds(start, size, stride=dil)

        def permute(r, carry, ls=ls, rows_of=rows_of):
            dst = pl.ds(pl.multiple_of(r * ls, ls), ls)
            for pair in range(2):
                qd_ref[pair, dst, :] = qf_ref[pair, rows_of(r, ls), :].astype(BF16)
                kd_ref[pair, dst, :] = kf_ref[pair, rows_of(r, ls), :].astype(BF16)
                vd_ref[pair, dst, :] = vf_ref[pair, rows_of(r, ls), :].astype(BF16)
            return carry

        lax.fori_loop(0, dil, permute, 0, unroll=min(dil, 4))
        tiles_per_res = ls // tq

        def coords(j, w=w, ls=ls, tq=tq, tk=tk, tpr=tiles_per_res):
            r = j // tpr
            q0 = (j % tpr) * tq
            k0 = jnp.clip(q0 - w, 0, ls - tk)
            return r, q0, pl.multiple_of(r * ls + k0, w), k0 - q0

        def scores(j, slot, w=w, tq=tq, tk=tk, coords=coords):
            _, _, krow, delta = coords(j)
            valid = _band_valid(tq, tk, w, delta)
            qrow = pl.multiple_of(j * tq, tq)
            for pair in range(2):
                _stage_scores([qd_ref[pair, pl.ds(qrow, tq), :]], kd_ref[pair, pl.ds(krow, tk), :], valid,
                              s_ref.at[:, :, :, 0:tk], slot, pair)

        def softmax(j, slot, tq=tq, tk=tk):
            for pair in range(2):
                _stage_softmax(s_ref.at[:, :, :, 0:tk], p_ref.at[:, :, :, 0:tk], mt_ref, lt_ref, slot, pair, tq)

        def merge(j, slot, tq=tq, tk=tk, dil=dil, coords=coords, rows_of=rows_of):
            r, q0, krow, _ = coords(j)
            rows = rows_of(r + dil * q0, tq)
            for pair in range(2):
                (o,) = _stage_pv(p_ref.at[:, :, :, 0:tk], vd_ref[pair, pl.ds(krow, tk), :], slot, pair, tq)
                m, l = mt_ref[slot, pair], lt_ref[slot, pair]
                if dil == DIL_CONFIGS[0][1]:
                    acc_ref[pair, rows, :], l_ref[pair, rows, :], m_ref[pair, rows, :] = o, l, m
                    continue
                m_old = m_ref[pair, rows, :]
                m_new = jnp.maximum(m_old, m)
                a, b = jnp.exp(m_old - m_new), jnp.exp(m - m_new)
                acc_ref[pair, rows, :] = acc_ref[pair, rows, :] * a + o * b
                l_ref[pair, rows, :] = l_ref[pair, rows, :] * a + l * b
                m_ref[pair, rows, :] = m_new

        _pipeline(seq // tq, scores, softmax, merge)

    for pair in range(2):
        o_ref[:, pair * PAIR:(pair + 1) * PAIR] = (acc_ref[pair] / l_ref[pair]).astype(o_ref.dtype)


def _dilated(proj, cos_q, sin_q, batch, seq):
    t = batch * seq

    def sec(base):
        return pl.BlockSpec((None, seq, SEC), lambda b, g: (base + g, b, 0))

    tab = pl.BlockSpec((None, seq, SEC), lambda b, g: (b, 0, 0))
    return pl.pallas_call(
        functools.partial(_dil_kernel, seq=seq),
        grid=(batch, 2),
        in_specs=[sec(SEC_QD), sec(SEC_KD), sec(SEC_VD), tab, tab],
        out_specs=pl.BlockSpec((seq, SEC), lambda b, g: (b, g)),
        out_shape=jax.ShapeDtypeStruct((t, GROUP_WIDTH), ACT),
        scratch_shapes=[pltpu.VMEM((2, seq, PAIR), F32)] * 6 + [pltpu.VMEM((2, seq, PAIR), BF16)] * 3 + [
            pltpu.VMEM((ATT_ITEMS, 2, 2 * ATT_TQ, 2 * ATT_TQ), F32),
            pltpu.VMEM((ATT_ITEMS, 2, 2 * ATT_TQ, 2 * ATT_TQ), BF16),
            pltpu.VMEM((ATT_ITEMS, 2, ATT_TQ, PAIR), F32),
            pltpu.VMEM((ATT_ITEMS, 2, ATT_TQ, PAIR), F32)],
        compiler_params=_cp(("arbitrary", "arbitrary")),
        name="dilated",
    )(proj, proj, proj, cos_q, sin_q)


def _gnorm(y, g):
    return y * lax.rsqrt(jnp.mean(y * y, axis=-1, keepdims=True) + EPS) * g


OUT_SUB = 256
OUT_TM = 512


def _out_kernel(ya_ref, yb_ref, yc_ref, yd_ref, gn_ref, w_ref, x_ref, mod_ref, nf_ref, rcat_ref,
                xn_ref, h_ref, aff_ref):
    ys = (ya_ref, yb_ref, yc_ref, yd_ref)
    gw = GROUP_WIDTH
    row_tiles = x_ref.shape[1] // LANES
    for sub in range(x_ref.shape[0] // OUT_SUB):
        rows = slice(sub * OUT_SUB, (sub + 1) * OUT_SUB)
        acc = None
        for g in range(4):
            yn = _gnorm(ys[g][rows, :].astype(F32), gn_ref[:, g * gw:(g + 1) * gw]).astype(BF16)
            part = jnp.dot(yn, w_ref[g * gw:(g + 1) * gw, :], preferred_element_type=F32)
            acc = part if acc is None else acc + part
        xn = x_ref[rows, :] + mod_ref[2:3, :] * acc
        xn_ref[rows, :] = xn
        h = _gnorm(xn, nf_ref[...]) * (1.0 + mod_ref[4:5, :]) + mod_ref[3:4, :]
        for c in range(row_tiles):
            h_ref[pl.ds(sub * OUT_SUB * row_tiles + c, OUT_SUB, stride=row_tiles), :] = h[:, c * LANES:(c + 1) * LANES]
        a = _nt(rcat_ref[...], h.astype(BF16))
        logits = a[0:N_EXPERTS] + a[N_EXPERTS:2 * N_EXPERTS]
        mxl = jnp.max(logits, axis=0, keepdims=True)
        ex = jnp.exp(logits - mxl)
        aff_ref[:, rows] = ex / jnp.sum(ex, axis=0, keepdims=True)


def _out_proj(ya, yb, yc, yd, gn, w_bf, layer, x2, mod_l, nf, rcat, seq):
    t, d = x2.shape
    tm = OUT_TM
    tpb = seq // tm
    ROW_TILES = d // LANES
    ysp = pl.BlockSpec((tm, GROUP_WIDTH), lambda i: (i, 0))
    return pl.pallas_call(
        _out_kernel,
        grid=(t // tm,),
        in_specs=[ysp] * 4 + [
            pl.BlockSpec((1, d), lambda i: (0, 0)),
            pl.BlockSpec((None, d, d), lambda i: (layer, 0, 0)),
            pl.BlockSpec((tm, d), lambda i: (i, 0)),
            pl.BlockSpec((None, 6, d), lambda i: (i // tpb, 0, 0)),
            pl.BlockSpec((1, d), lambda i: (0, 0)),
            pl.BlockSpec((2 * N_EXPERTS, d), lambda i: (0, 0))],
        out_specs=[pl.BlockSpec((tm, d), lambda i: (i, 0)),
                   pl.BlockSpec((tm * ROW_TILES, LANES), lambda i: (i, 0)),
                   pl.BlockSpec((N_EXPERTS, tm), lambda i: (0, i))],
        out_shape=[jax.ShapeDtypeStruct((t, d), F32),
                   jax.ShapeDtypeStruct((t * ROW_TILES, LANES), F32),
                   jax.ShapeDtypeStruct((N_EXPERTS, t), F32)],
        compiler_params=_cp(("arbitrary",)),
        name="out_proj",
    )(ya, yb, yc, yd, gn, w_bf, x2, mod_l, nf, rcat)


def _route_kernel(aff_ref, tri_ref, idx_ref, gate_ref, dst_ref, cnt_ref, start_ref,
                  pos_ref, q_ref, *, cap, seq):
    aff = aff_ref[...]

    def as_f32(bits):
        return lax.bitcast_convert_type(bits, F32)

    def step(k, thr):
        cand = thr | lax.shift_left(jnp.int32(1), 30 - k)
        cnt = jnp.sum((aff >= as_f32(cand)).astype(jnp.int32), axis=1, keepdims=True)
        return jnp.where(cnt >= cap, cand, thr)

    thr = lax.fori_loop(0, 31, step, jnp.zeros((aff.shape[0], 1), jnp.int32))
    gt = aff >= as_f32(thr + 1)
    eq = (aff >= as_f32(thr)) & jnp.logical_not(gt)
    need = (cap - jnp.sum(gt.astype(jnp.int32), axis=1, keepdims=True)).astype(F32)
    eq_rank = jnp.dot(eq.astype(BF16), tri_ref[...], preferred_element_type=F32)
    sel = gt | (eq & (eq_rank < need))
    n_e = aff.shape[0]
    sel_f = sel.astype(F32)
    sel_bf = sel.astype(BF16)
    pos_ref[...] = jnp.where(sel, jnp.dot(sel_bf, tri_ref[...], preferred_element_type=F32), -1.0)

    cnt = jnp.sum(sel_f, axis=0, keepdims=True)
    cnt8 = jnp.broadcast_to(cnt, (8, seq)).astype(BF16)
    start = (jnp.dot(cnt8, tri_ref[...], preferred_element_type=F32)[0:1]
             + (pl.program_id(0) * (n_e * cap)).astype(F32))
    lower = (lax.broadcasted_iota(jnp.int32, (n_e, n_e), 1)
             < lax.broadcasted_iota(jnp.int32, (n_e, n_e), 0)).astype(BF16)
    q_ref[...] = start + jnp.dot(lower, sel_bf, preferred_element_type=F32)
    cnt_ref[...] = cnt
    start_ref[...] = start

    tok = lax.broadcasted_iota(jnp.int32, (1, seq), 1) + pl.program_id(0) * seq
    tok_hi = lax.shift_right_logical(tok, 7).astype(F32)
    tok_lo = (tok & (LANES - 1)).astype(F32)
    slot = lax.broadcasted_iota(jnp.int32, (cap, seq), 0).astype(F32)
    eye = (lax.broadcasted_iota(jnp.int32, (cap, cap), 0) == lax.broadcasted_iota(jnp.int32, (cap, cap), 1))
    zero_row = jnp.zeros((1, seq), F32)

    def per_expert(e, carry):
        onehot = (pos_ref[pl.ds(e, 1), :] == slot).astype(BF16)
        q = q_ref[pl.ds(e, 1), :]
        q_hi = jnp.floor(q * (1.0 / LANES))
        a = aff_ref[pl.ds(e, 1), :]
        a1 = a.astype(BF16).astype(F32)
        a2 = (a - a1).astype(BF16).astype(F32)
        pieces = jnp.concatenate([tok_hi, tok_lo, q_hi, q - q_hi * LANES, a1, a2, a - a1 - a2, zero_row], axis=0)
        got = _nt(pieces.astype(BF16), onehot)
        idx_ref[e] = (got[0:1] * LANES + got[1:2]).astype(jnp.int32)
        dst_ref[e] = (got[2:3] * LANES + got[3:4]).astype(jnp.int32)
        gate_row = got[4:5] + got[5:6] + got[6:7]
        gate_col = jnp.sum(jnp.where(eye, gate_row, 0.0), axis=1, keepdims=True)
        gate_ref[e] = jnp.broadcast_to(gate_col, (cap, LANES))
        return carry

    lax.fori_loop(0, n_e, per_expert, 0)


def _route(aff_t, tri, batch, seq, cap):
    e, t = aff_t.shape
    blk = pl.BlockSpec((e, seq), lambda b: (0, b))
    colblk = pl.BlockSpec((e, cap, LANES), lambda b: (0, b, 0))
    rowblk = pl.BlockSpec((1, seq), lambda b: (0, b))
    slotblk = pl.BlockSpec((e, None, 1, cap), lambda b: (0, b, 0, 0))
    slot_i32 = jax.ShapeDtypeStruct((e, batch, 1, cap), jnp.int32)
    return pl.pallas_call(
        functools.partial(_route_kernel, cap=cap, seq=seq),
        grid=(batch,),
        in_specs=[blk, pl.BlockSpec((seq, seq), lambda b: (0, 0))],
        out_specs=[slotblk, colblk, slotblk, rowblk, rowblk],
        out_shape=[slot_i32, jax.ShapeDtypeStruct((e, batch * cap, LANES), F32), slot_i32,
                   jax.ShapeDtypeStruct((1, t), F32), jax.ShapeDtypeStruct((1, t), F32)],
        scratch_shapes=[pltpu.VMEM((e, seq), F32)] * 2,
        compiler_params=_cp(("arbitrary",)),
        name="route",
    )(aff_t, tri)


DMA_UNROLL = 32


def _moe_kernel(idx_ref, dst_ref, h_hbm, gate_ref, wg_ref, wu_ref, wd_ref, sorted_hbm,
                xg_ref, xe_ref, acc_ref, ob_ref, sem, *, rows):
    e = pl.program_id(0)
    f = pl.program_id(1)
    n_e = pl.num_programs(0)

    row_tiles = xe_ref.shape[1] // LANES

    def scatter():
        def body(i, carry):
            for u in range(DMA_UNROLL):
                r = i * DMA_UNROLL + u
                pltpu.make_async_copy(ob_ref.at[:, r, :], sorted_hbm.at[:, dst_ref[e * rows + r], :],
                                      sem.at[1]).start()
            return carry
        lax.fori_loop(0, rows // DMA_UNROLL, body, 0)

    def wait_scatter():
        for c in range(row_tiles):
            pltpu.make_async_copy(ob_ref.at[c], sorted_hbm.at[c, pl.ds(0, rows), :], sem.at[1]).wait()

    def gather(expert):
        def body(i, carry):
            for u in range(DMA_UNROLL):
                r = i * DMA_UNROLL + u
                tok = idx_ref[expert * rows + r]
                pltpu.make_async_copy(h_hbm.at[pl.ds(pl.multiple_of(tok * row_tiles, row_tiles), row_tiles), :],
                                      xg_ref.at[:, r, :], sem.at[0]).start()
            return carry
        lax.fori_loop(0, rows // DMA_UNROLL, body, 0)

    @pl.when((e == 0) & (f == 0))
    def _():
        gather(0)

    @pl.when(f == 0)
    def _():
        for c in range(row_tiles):
            pltpu.make_async_copy(h_hbm.at[pl.ds(0, rows), :], xg_ref.at[c], sem.at[0]).wait()
        for c in range(row_tiles):
            xe_ref[:, c * LANES:(c + 1) * LANES] = xg_ref[c].astype(BF16)
        acc_ref[...] = jnp.zeros_like(acc_ref)

        @pl.when(e + 1 < n_e)
        def _():
            gather(e + 1)

    xe = xe_ref[...]
    a = jnp.dot(xe, wg_ref[...].astype(BF16), preferred_element_type=F32)
    u = jnp.dot(xe, wu_ref[...].astype(BF16), preferred_element_type=F32)
    act = (a * jax.nn.sigmoid(a) * u).astype(BF16)
    acc_ref[...] += jnp.dot(act, wd_ref[...].astype(BF16), preferred_element_type=F32)

    @pl.when(f == pl.num_programs(1) - 1)
    def _():
        @pl.when(e > 0)
        def _():
            wait_scatter()

        gate = gate_ref[:, 0:1]
        for c in range(row_tiles):
            ob_ref[c] = acc_ref[:, c * LANES:(c + 1) * LANES] * gate
        scatter()

        @pl.when(e == n_e - 1)
        def _():
            wait_scatter()


def _moe(idx_flat, dst_flat, gate_col, h2, wg, wu, wd, layer):
    _, ne, d, ff = wg.shape
    rows = gate_col.shape[1]
    row_tiles = d // LANES
    tf = 256
    grid_spec = pltpu.PrefetchScalarGridSpec(
        num_scalar_prefetch=2,
        grid=(ne, ff // tf),
        in_specs=[pl.BlockSpec(memory_space=pl.ANY),
                  pl.BlockSpec((None, rows, LANES), lambda e, f, idx, dst: (e, 0, 0)),
                  pl.BlockSpec((None, None, d, tf), lambda e, f, idx, dst: (layer, e, 0, f)),
                  pl.BlockSpec((None, None, d, tf), lambda e, f, idx, dst: (layer, e, 0, f)),
                  pl.BlockSpec((None, None, tf, d), lambda e, f, idx, dst: (layer, e, f, 0))],
        out_specs=pl.BlockSpec(memory_space=pl.ANY),
        scratch_shapes=[pltpu.VMEM((row_tiles, rows, LANES), F32), pltpu.VMEM((rows, d), BF16),
                        pltpu.VMEM((rows, d), F32), pltpu.VMEM((row_tiles, rows, LANES), F32),
                        pltpu.SemaphoreType.DMA((2,))],
    )
    return pl.pallas_call(
        functools.partial(_moe_kernel, rows=rows),
        grid_spec=grid_spec,
        out_shape=jax.ShapeDtypeStruct((row_tiles, ne * rows, LANES), F32),
        compiler_params=_cp(("arbitrary", "arbitrary")),
        name="moe",
    )(idx_flat, dst_flat, h2, gate_col, wg, wu, wd)


COMBINE_CHUNK = 256
COMBINE_TM = 512
COMBINE_STREAMS = 4
COMBINE_SLOTS = 3


def _combine_kernel(tile_ref, start_ref, cnt_ref, sorted_hbm, xn_ref, mod_ref, fn_ref, out_ref,
                    buf_ref, acc_ref, done_ref, sem, *, tm, total, final_norm):
    i = pl.program_id(0)
    n_tiles = pl.num_programs(0)
    row_tiles = buf_ref.shape[1]
    kc = COMBINE_CHUNK

    def span(tile):
        lo = tile_ref[tile]
        base = lax.shift_left(lax.shift_right_logical(lo, 3), 3)
        n_chunks = jnp.maximum(lax.shift_right_logical(tile_ref[tile + 1] - base + (kc - 1), 8), 1)
        return base, n_chunks

    def chunk_rows(base, j):
        return pl.multiple_of(jnp.minimum(base + j * kc, total - kc), 8)

    def chunk_copies(first_row, slot):
        per = row_tiles // COMBINE_STREAMS
        return [pltpu.make_async_copy(sorted_hbm.at[pl.ds(s * per, per), pl.ds(first_row, kc), :],
                                      buf_ref.at[slot, pl.ds(s * per, per)], sem.at[slot])
                for s in range(COMBINE_STREAMS)]

    def start_chunk(first_row, slot):
        for s, cp in enumerate(chunk_copies(first_row, slot)):
            cp.start(priority=s % 2)

    def wait_chunk(first_row, slot):
        for cp in chunk_copies(first_row, slot):
            cp.wait()

    def issue_next():
        t = done_ref[2]

        @pl.when(t < n_tiles)
        def _():
            j = done_ref[3]
            base_t, n_t = span(t)
            issued = done_ref[1]
            start_chunk(chunk_rows(base_t, j), issued % COMBINE_SLOTS)
            done_ref[1] = issued + 1
            wrap = j + 1 >= n_t
            done_ref[2] = jnp.where(wrap, t + 1, t)
            done_ref[3] = jnp.where(wrap, 0, j + 1)

    base, n_chunks = span(i)

    @pl.when(i == 0)
    def _():
        for k in range(4):
            done_ref[k] = 0
        for _ in range(COMBINE_SLOTS - 1):
            issue_next()

    eye = (lax.broadcasted_iota(jnp.int32, (tm, tm), 0) == lax.broadcasted_iota(jnp.int32, (tm, tm), 1))
    lo_col = jnp.sum(jnp.where(eye, start_ref[...], 0.0), axis=1, keepdims=True)
    hi_col = lo_col + jnp.sum(jnp.where(eye, cnt_ref[...], 0.0), axis=1, keepdims=True)
    acc_ref[...] = jnp.zeros_like(acc_ref)

    def chunk(j, carry):
        done = done_ref[0]
        slot = done % COMBINE_SLOTS
        first_row = chunk_rows(base, j)
        wait_chunk(first_row, slot)
        issue_next()

        row = (first_row + lax.broadcasted_iota(jnp.int32, (1, kc), 1)).astype(F32)
        nominal = (base + j * kc).astype(F32)
        pick = ((row >= lo_col) & (row < hi_col) & (row >= nominal)).astype(BF16)
        for c in range(0, row_tiles, 2):
            rhs = jnp.concatenate([buf_ref[slot, c], buf_ref[slot, c + 1]], axis=1).astype(BF16)
            acc_ref[:, c * LANES:(c + 2) * LANES] += jnp.dot(pick, rhs, preferred_element_type=F32)
        done_ref[0] = done + 1
        return carry

    lax.fori_loop(0, n_chunks, chunk, 0)

    x = xn_ref[...] + mod_ref[5:6, :] * acc_ref[...]
    if final_norm:
        x = _gnorm(x, fn_ref[...])
    out_ref[...] = x


def _combine(tile_start, start_row, cnt_row, sorted_rows, xn, mod_l, fn, seq, final_norm):
    t, d = xn.shape
    row_tiles, total, _ = sorted_rows.shape
    tm = COMBINE_TM
    tpb = seq // tm
    grid_spec = pltpu.PrefetchScalarGridSpec(
        num_scalar_prefetch=1,
        grid=(t // tm,),
        in_specs=[pl.BlockSpec((1, tm), lambda i, ts: (0, i)),
                  pl.BlockSpec((1, tm), lambda i, ts: (0, i)),
                  pl.BlockSpec(memory_space=pl.ANY),
                  pl.BlockSpec((tm, d), lambda i, ts: (i, 0)),
                  pl.BlockSpec((None, 6, d), lambda i, ts: (i // tpb, 0, 0)),
                  pl.BlockSpec((1, d), lambda i, ts: (0, 0))],
        out_specs=pl.BlockSpec((tm, d), lambda i, ts: (i, 0)),
        scratch_shapes=[pltpu.VMEM((COMBINE_SLOTS, row_tiles, COMBINE_CHUNK, LANES), F32),
                        pltpu.VMEM((tm, d), F32), pltpu.SMEM((4,), jnp.int32),
                        pltpu.SemaphoreType.DMA((COMBINE_SLOTS,))],
    )
    return pl.pallas_call(
        functools.partial(_combine_kernel, tm=tm, total=total, final_norm=final_norm),
        grid_spec=grid_spec,
        out_shape=jax.ShapeDtypeStruct((t, d), F32),
        compiler_params=_cp(("arbitrary",)),
        name="combine",
    )(tile_start, start_row, cnt_row, sorted_rows, xn, mod_l, fn)


def _rope_tables(positions):
    b, s = positions.shape
    inv = jnp.power(jnp.float32(ROPE_THETA), -jnp.arange(0, ROPE_DIM, 2, dtype=F32) / ROPE_DIM)
    ang = positions.astype(F32)[..., None] * inv
    cos, sin = jnp.cos(ang), jnp.sin(ang)
    rest = HEAD_DIM - ROPE_DIM
    c64 = jnp.concatenate([cos, cos, jnp.ones((b, s, rest), F32)], axis=-1)
    s64 = jnp.concatenate([-sin, sin, jnp.zeros((b, s, rest), F32)], axis=-1)
    return jnp.tile(c64, (1, 1, 4)), jnp.tile(s64, (1, 1, 4))


def _dft_tables(n):
    k = jnp.arange(n, dtype=jnp.int32)
    ang = ((k[:, None] * k[None, :]) % n).astype(F32) * (2.0 * math.pi / n)
    return jnp.cos(ang), jnp.sin(ang)


def _dft_tables_split(n):
    k = jnp.arange(n, dtype=jnp.int32)[:, None]
    m0 = jnp.arange(LANES, dtype=jnp.int32)[None, :]
    m1 = jnp.arange(n // LANES, dtype=jnp.int32)[None, :]
    a = ((k * m0) % n).astype(F32) * (2.0 * math.pi / n)
    b = ((k * m1 * LANES) % n).astype(F32) * (2.0 * math.pi / n)
    reps = n // LANES
    ca, sa = jnp.tile(jnp.cos(a), (1, reps)), jnp.tile(jnp.sin(a), (1, reps))
    cb, sb = jnp.repeat(jnp.cos(b), LANES, axis=1), jnp.repeat(jnp.sin(b), LANES, axis=1)
    return ca * cb - sa * sb, sa * cb + ca * sb


def _block_diag(blocks):
    n, a, _ = blocks.shape
    eye = jnp.eye(n, dtype=blocks.dtype)
    return (eye[:, None, :, None] * blocks[:, :, None, :]).reshape(n * a, n * a)


def kernel(x, c, positions, ada_w, ada_b, norm_mix, w_in, fnet_w, sgu_norm, sgu_w, sgu_b, swa_sink,
           group_norm, w_out, norm_ffn, router_w, exp_w_gate, exp_w_up, exp_w_down, final_norm):
    batch, seq, d = x.shape
    depth = ada_w.shape[0]
    t = batch * seq
    cap = max(1, EC_FACTOR * seq // N_EXPERTS)

    rows = 16
    c_pad = jnp.zeros((rows, d), F32).at[:batch].set(c)
    mod = _ada(c_pad, ada_w, ada_b)[:, :batch].reshape(depth, batch, 6, d)

    cos_q, sin_q = _rope_tables(positions)
    dft_c, dft_s = _dft_tables_split(seq)
    dft_c, dft_s = dft_c.astype(BF16), dft_s.astype(BF16)
    cc, sc = _dft_tables(HEAD_DIM)
    bdc = _block_diag(jnp.broadcast_to(cc, (4, HEAD_DIM, HEAD_DIM)))
    bds = _block_diag(jnp.broadcast_to(sc, (4, HEAD_DIM, HEAD_DIM)))
    tri = (jnp.arange(seq)[:, None] < jnp.arange(seq)[None, :]).astype(BF16)

    w_in_bf, w_out_bf = w_in.astype(BF16), w_out.astype(BF16)
    x2 = x.reshape(t, d)
    for l in range(depth):
        mod_l = mod[l]
        proj = _in_proj(x2, mod_l, norm_mix[l].reshape(1, d), w_in_bf, l, seq)

        bdw = jnp.stack([_block_diag(fnet_w[l, 0:4]), _block_diag(fnet_w[l, 4:8])])
        ya = _fourier(proj, bdc, bds, bdw, dft_c, dft_s, batch, seq).reshape(t, GROUP_WIDTH)
        sgu_w_pairs = jnp.concatenate([sgu_w[l, 0::2], sgu_w[l, 1::2]], axis=-1).astype(BF16)
        yb = _sgu(proj, sgu_norm[l].reshape(1, GROUP_WIDTH), sgu_w_pairs, sgu_b[l].T)
        sink_row = jnp.repeat(swa_sink[l], HEAD_DIM).reshape(1, GROUP_WIDTH)
        yc = _swa(proj, cos_q, sin_q, sink_row, batch, seq)
        yd = _dilated(proj, cos_q, sin_q, batch, seq)

        r_t = router_w[l].T
        r_hi = r_t.astype(BF16)
        r_lo = (r_t - r_hi.astype(F32)).astype(BF16)
        rcat = jnp.concatenate([r_hi, r_lo], axis=0)
        xn, h2, aff_t = _out_proj(ya, yb, yc, yd, group_norm[l].reshape(1, d), w_out_bf, l,
                                  x2, mod_l, norm_ffn[l].reshape(1, d), rcat, seq)
        idx, gate_col, dst, cnt_row, start_row = _route(aff_t, tri, batch, seq, cap)
        sorted_rows = _moe(idx.reshape(-1), dst.reshape(-1), gate_col, h2,
                           exp_w_gate, exp_w_up, exp_w_down, l)
        n_pairs = batch * N_EXPERTS * cap
        tile_start = jnp.concatenate([start_row[0, ::COMBINE_TM].astype(jnp.int32), jnp.full((1,), n_pairs, jnp.int32)])
        x2 = _combine(tile_start, start_row, cnt_row, sorted_rows, xn, mod_l, final_norm.reshape(1, d), seq,
                      l == depth - 1)
    return x2.reshape(batch, seq, d)
```

```python
import functools
import math

import jax
import jax.numpy as jnp
from jax import lax
from jax.experimental import pallas as pl
from jax.experimental.pallas import tpu as pltpu

F32 = jnp.float32
BF16 = jnp.bfloat16
ACT = jnp.bfloat16

HEAD_DIM = 64
N_GROUP_HEADS = 8
GROUP_WIDTH = 512
CHUNK = 128
SWA_HALF_WINDOW = 128
DIL_CONFIGS = ((128, 1), (512, 4), (2048, 16))
ROPE_THETA = 500000.0
ROPE_DIM = 16
N_EXPERTS = 16
EC_FACTOR = 2
EPS = 1e-6
NEG_INF = -1e30

LANES = 128
SEC = 256
SEC_XA, SEC_U, SEC_V, SEC_QC, SEC_KVC, SEC_QD, SEC_KD, SEC_VD = 0, 2, 4, 6, 8, 9, 11, 13
N_SEC = 15

VMEM_LIMIT = 56 * 1024 * 1024


def _cp(sem, vmem=None):
    return pltpu.CompilerParams(dimension_semantics=sem, vmem_limit_bytes=vmem or VMEM_LIMIT)


def _nt(a, b):
    return lax.dot_general(a, b, (((1,), (1,)), ((), ())), preferred_element_type=F32)


ADA_STREAMS = 4


def _ada_kernel(c_ref, *refs):
    w_refs, b_ref, o_ref = refs[:ADA_STREAMS], refs[ADA_STREAMS], refs[ADA_STREAMS + 1]

    @pl.when(pl.program_id(1) == 0)
    def _():
        o_ref[0] = jnp.broadcast_to(b_ref[0], o_ref.shape[1:])

    c = c_ref[...]
    s = (c * jax.nn.sigmoid(c)).astype(BF16)
    tn = w_refs[0].shape[2]
    for j, w_ref in enumerate(w_refs):
        o_ref[0, :, j * tn:(j + 1) * tn] += jnp.dot(s, w_ref[0].astype(BF16), preferred_element_type=F32)


def _ada(c_pad, ada_w, ada_b):
    n_layers, d, n6 = ada_w.shape
    rows = c_pad.shape[0]
    tk = 256
    tn = n6 // ADA_STREAMS
    w_specs = [pl.BlockSpec((1, tk, tn), functools.partial(lambda l, k, j: (l, k, j), j=j))
               for j in range(ADA_STREAMS)]
    return pl.pallas_call(
        _ada_kernel,
        grid=(n_layers, d // tk),
        in_specs=[pl.BlockSpec((rows, tk), lambda l, k: (0, k))] + w_specs + [
            pl.BlockSpec((1, 1, n6), lambda l, k: (l, 0, 0))],
        out_specs=pl.BlockSpec((1, rows, n6), lambda l, k: (l, 0, 0)),
        out_shape=jax.ShapeDtypeStruct((n_layers, rows, n6), F32),
        compiler_params=_cp(("arbitrary", "arbitrary")),
        name="ada",
    )(c_pad, *([ada_w] * ADA_STREAMS), ada_b.reshape(n_layers, 1, n6))


NORM_ROWS = 16
IN_SECS = 5


def _in_kernel(x_ref, mod_ref, g_ref, w_ref, o_ref, h_ref):
    @pl.when(pl.program_id(1) == 0)
    def _():
        gain = g_ref[...] * (1.0 + mod_ref[1:2, :])
        shift = mod_ref[0:1, :]

        def chunk(c, carry):
            rows = pl.ds(pl.multiple_of(c * NORM_ROWS, NORM_ROWS), NORM_ROWS)
            x = x_ref[rows, :]
            r = lax.rsqrt(jnp.mean(x * x, axis=-1, keepdims=True) + EPS)
            h_ref[rows, :] = (x * r * gain + shift).astype(BF16)
            return carry

        lax.fori_loop(0, x_ref.shape[0] // NORM_ROWS, chunk, 0, unroll=8)

    res = jnp.dot(h_ref[...], w_ref[...], preferred_element_type=F32)
    for s in range(IN_SECS):
        o_ref[s] = res[:, s * SEC:(s + 1) * SEC].astype(o_ref.dtype)


def _in_proj(x2, mod_l, norm_g, w_bf, layer, seq):
    t, d = x2.shape
    tm = 1024
    tpb = seq // tm
    return pl.pallas_call(
        _in_kernel,
        grid=(t // tm, N_SEC // IN_SECS),
        in_specs=[pl.BlockSpec((tm, d), lambda i, j: (i, 0)),
                  pl.BlockSpec((None, 6, d), lambda i, j: (i // tpb, 0, 0)),
                  pl.BlockSpec((1, d), lambda i, j: (0, 0)),
                  pl.BlockSpec((None, d, IN_SECS * SEC), lambda i, j: (layer, 0, j))],
        out_specs=pl.BlockSpec((IN_SECS, tm, SEC), lambda i, j: (j, i, 0)),
        out_shape=jax.ShapeDtypeStruct((N_SEC, t, SEC), ACT),
        scratch_shapes=[pltpu.VMEM((tm, d), BF16)],
        compiler_params=_cp(("arbitrary", "arbitrary")),
        name="in_proj",
    )(x2, mod_l, norm_g, w_bf)


def _f1_kernel(x_ref, bdc_ref, bds_ref, bdw_ref, z_ref, m_ref, *, scale):
    @pl.when(pl.program_id(1) == 0)
    def _():
        w = bdw_ref[...]
        mc = jnp.dot(bdc_ref[...], w, preferred_element_type=F32, precision=lax.Precision.HIGHEST)
        msn = jnp.dot(bds_ref[...], w, preferred_element_type=F32, precision=lax.Precision.HIGHEST)
        m_ref[:, 0:SEC] = (mc * scale).astype(BF16)
        m_ref[:, SEC:2 * SEC] = (msn * scale).astype(BF16)

    z_ref[...] = jnp.dot(x_ref[...].astype(BF16), m_ref[...], preferred_element_type=F32).astype(BF16)


def _f2_kernel(c_ref, s_ref, z_ref, o_ref):
    z = z_ref[...]
    o_ref[...] = (jnp.dot(c_ref[...], z[:, 0:SEC], preferred_element_type=F32)
                  - jnp.dot(s_ref[...], z[:, SEC:2 * SEC], preferred_element_type=F32)).astype(o_ref.dtype)


def _fourier(proj, bdc, bds, bdw, dft_c, dft_s, batch, seq):
    t = batch * seq
    tm = 1024
    scale = 1.0 / math.sqrt(seq * HEAD_DIM)
    z = pl.pallas_call(
        functools.partial(_f1_kernel, scale=scale),
        grid=(2, t // tm),
        in_specs=[pl.BlockSpec((None, tm, SEC), lambda s, i: (SEC_XA + s, i, 0)),
                  pl.BlockSpec((SEC, SEC), lambda s, i: (0, 0)),
                  pl.BlockSpec((SEC, SEC), lambda s, i: (0, 0)),
                  pl.BlockSpec((None, SEC, SEC), lambda s, i: (s, 0, 0))],
        out_specs=pl.BlockSpec((None, tm, 2 * SEC), lambda s, i: (s, i, 0)),
        out_shape=jax.ShapeDtypeStruct((2, t, 2 * SEC), BF16),
        scratch_shapes=[pltpu.VMEM((SEC, 2 * SEC), BF16)],
        compiler_params=_cp(("arbitrary", "arbitrary")),
        name="fnet_chan",
    )(proj, bdc, bds, bdw)
    rt = 512
    return pl.pallas_call(
        _f2_kernel,
        grid=(seq // rt, batch, 2),
        in_specs=[pl.BlockSpec((rt, seq), lambda r, b, s: (r, 0)),
                  pl.BlockSpec((rt, seq), lambda r, b, s: (r, 0)),
                  pl.BlockSpec((None, seq, 2 * SEC), lambda r, b, s: (s, b, 0))],
        out_specs=pl.BlockSpec((None, rt, SEC), lambda r, b, s: (b, r, s)),
        out_shape=jax.ShapeDtypeStruct((batch, seq, GROUP_WIDTH), ACT),
        compiler_params=_cp(("arbitrary", "arbitrary", "arbitrary")),
        name="fnet_seq",
    )(dft_c, dft_s, z)


def _sgu_kernel(u_ref, v_ref, g_ref, w_ref, b_ref, o_ref, *, tm):
    gv = [jax.nn.gelu(v_ref[s].astype(F32)) for s in range(2)]
    ms = (jnp.sum(gv[0] * gv[0], axis=-1, keepdims=True)
          + jnp.sum(gv[1] * gv[1], axis=-1, keepdims=True)) * (1.0 / GROUP_WIDTH)
    r = lax.rsqrt(ms + EPS)
    vn = [(gv[s] * r * g_ref[:, s * SEC:(s + 1) * SEC]).astype(BF16) for s in range(2)]
    lo = lax.broadcasted_iota(jnp.int32, (CHUNK, 2 * HEAD_DIM), 1) < HEAD_DIM
    zero = jnp.zeros((CHUNK, 2 * HEAD_DIM), BF16)
    for s in range(2):
        gu = jax.nn.gelu(u_ref[s].astype(F32))
        for c in range(tm // CHUNK):
            rows = slice(c * CHUNK, (c + 1) * CHUNK)
            for pp in range(2):
                pair = 2 * s + pp
                lanes = slice(pp * 2 * HEAD_DIM, (pp + 1) * 2 * HEAD_DIM)
                vp = vn[s][rows, lanes]
                rhs = jnp.concatenate([jnp.where(lo, vp, zero), jnp.where(lo, zero, vp)], axis=0)
                z = jnp.dot(w_ref[pair], rhs, preferred_element_type=F32)
                bias = jnp.where(lo, b_ref[:, 2 * pair:2 * pair + 1], b_ref[:, 2 * pair + 1:2 * pair + 2])
                o_ref[rows, s * SEC + pp * 2 * HEAD_DIM:s * SEC + (pp + 1) * 2 * HEAD_DIM] = (gu[rows, lanes] * (z + bias)).astype(o_ref.dtype)


def _sgu(proj, sgu_norm, sgu_w_bf, sgu_bt):
    t = proj.shape[1]
    tm = 512
    return pl.pallas_call(
        functools.partial(_sgu_kernel, tm=tm),
        grid=(t // tm,),
        in_specs=[pl.BlockSpec((2, tm, SEC), lambda i: (SEC_U // 2, i, 0)),
                  pl.BlockSpec((2, tm, SEC), lambda i: (SEC_V // 2, i, 0)),
                  pl.BlockSpec((1, GROUP_WIDTH), lambda i: (0, 0)),
                  pl.BlockSpec((N_GROUP_HEADS // 2, CHUNK, 2 * CHUNK), lambda i: (0, 0, 0)),
                  pl.BlockSpec((CHUNK, N_GROUP_HEADS), lambda i: (0, 0))],
        out_specs=pl.BlockSpec((tm, GROUP_WIDTH), lambda i: (i, 0)),
        out_shape=jax.ShapeDtypeStruct((t, GROUP_WIDTH), ACT),
        compiler_params=_cp(("arbitrary",)),
        name="sgu",
    )(proj, proj, sgu_norm, sgu_w_bf, sgu_bt)


LOG2E = math.log2(math.e)
QK_SCALE = HEAD_DIM ** -0.5 * LOG2E
ATT_TQ = 128
PAIR = 2 * HEAD_DIM


def _rope(x, cos_t, sin_t):
    n = x.shape[-1]
    half = ROPE_DIM // 2
    lane = lax.broadcasted_iota(jnp.int32, x.shape, 1) % HEAD_DIM
    swapped = jnp.where(lane < half, pltpu.roll(x, n - half, 1), pltpu.roll(x, half, 1))
    return x * cos_t + swapped * sin_t


def _lo_half(shape):
    return lax.broadcasted_iota(jnp.int32, shape, 1) < HEAD_DIM


def _stage_scores(q_pairs, k, valid, s_ref, slot, grp):
    tq = q_pairs[0].shape[0]
    lo = _lo_half(q_pairs[0].shape)
    zero = jnp.zeros_like(q_pairs[0])
    blocks = []
    for q in q_pairs:
        blocks += [jnp.where(lo, q, zero), jnp.where(lo, zero, q)]
    s = _nt(jnp.concatenate(blocks, axis=0), k)
    for i in range(len(blocks)):
        s_ref[slot, grp, i * tq:(i + 1) * tq, :] = jnp.where(valid, s[i * tq:(i + 1) * tq], NEG_INF)


def _stage_softmax(s_ref, p_ref, m_ref, l_ref, slot, grp, tq):
    s = s_ref[slot, grp]
    m = jnp.max(s, axis=-1, keepdims=True)
    p = jnp.exp2(s - m)
    p_ref[slot, grp] = p.astype(BF16)
    l = jnp.sum(p, axis=-1, keepdims=True)
    n_pairs = s.shape[0] // (2 * tq)
    shp = (tq, PAIR)
    lo = _lo_half(shp)
    for i in range(n_pairs):
        a, b = slice(2 * i * tq, (2 * i + 1) * tq), slice((2 * i + 1) * tq, (2 * i + 2) * tq)
        m_ref[slot, grp * n_pairs + i] = jnp.where(lo, jnp.broadcast_to(m[a], shp), jnp.broadcast_to(m[b], shp))
        l_ref[slot, grp * n_pairs + i] = jnp.where(lo, jnp.broadcast_to(l[a], shp), jnp.broadcast_to(l[b], shp))


def _stage_pv(p_ref, v, slot, grp, tq):
    o = jnp.dot(p_ref[slot, grp], v, preferred_element_type=F32)
    lo = _lo_half((tq, PAIR))
    return [jnp.where(lo, o[2 * i * tq:(2 * i + 1) * tq], o[(2 * i + 1) * tq:(2 * i + 2) * tq])
            for i in range(o.shape[0] // (2 * tq))]


ATT_ITEMS = 8


def _pipeline(n_items, stage_a, stage_b, stage_c):
    def body(jj, carry):
        for stage in (stage_a, stage_b, stage_c):
            for t in range(ATT_ITEMS):
                stage(jj * ATT_ITEMS + t, t)
        return carry

    lax.fori_loop(0, n_items // ATT_ITEMS, body, 0)


def _band_valid(tq, tk, w, delta):
    d = (lax.broadcasted_iota(jnp.int32, (tq, tk), 1) - lax.broadcasted_iota(jnp.int32, (tq, tk), 0)) + delta
    return (d <= w) & (d >= -w)


def _swa_kernel(q_ref, kv_ref, cos_ref, sin_ref, sink_ref, o_ref, qs_ref, ks_ref, vs_ref,
                s_ref, p_ref, m_ref, l_ref, *, seq):
    w, tq = SWA_HALF_WINDOW, ATT_TQ
    tk = tq + 2 * w
    cos, sin = cos_ref[...], sin_ref[...]
    for s in range(2):
        qs_ref[s] = (_rope(q_ref[s].astype(F32), cos, sin) * QK_SCALE).astype(BF16)
    kk = _rope(kv_ref[:, 0:PAIR].astype(F32), cos[:, 0:PAIR], sin[:, 0:PAIR])
    vv = kv_ref[:, PAIR:2 * PAIR].astype(F32)
    lo = lax.broadcasted_iota(jnp.int32, kk.shape, 1) < HEAD_DIM
    kk_sw, vv_sw = pltpu.roll(kk, HEAD_DIM, 1), pltpu.roll(vv, HEAD_DIM, 1)
    ks_ref[0] = jnp.where(lo, kk, kk_sw).astype(BF16)
    ks_ref[1] = jnp.where(lo, kk_sw, kk).astype(BF16)
    vs_ref[0] = jnp.where(lo, vv, vv_sw).astype(BF16)
    vs_ref[1] = jnp.where(lo, vv_sw, vv).astype(BF16)

    def coords(j):
        q0 = pl.multiple_of(j * tq, tq)
        return q0, pl.multiple_of(jnp.clip(q0 - w, 0, seq - tk), tq)

    def scores(j, slot):
        q0, k0 = coords(j)
        valid = _band_valid(tq, tk, w, k0 - q0)
        for g in range(2):
            q_pairs = [qs_ref[g, pl.ds(q0, tq), 0:PAIR], qs_ref[g, pl.ds(q0, tq), PAIR:2 * PAIR]]
            _stage_scores(q_pairs, ks_ref[g, pl.ds(k0, tk), :], valid, s_ref, slot, g)

    def softmax(j, slot):
        for g in range(2):
            _stage_softmax(s_ref, p_ref, m_ref, l_ref, slot, g, tq)

    def output(j, slot):
        q0, k0 = coords(j)
        for g in range(2):
            outs = _stage_pv(p_ref, vs_ref[g, pl.ds(k0, tk), :], slot, g, tq)
            for i, o in enumerate(outs):
                pair = 2 * g + i
                m, l = m_ref[slot, pair], l_ref[slot, pair]
                sink = sink_ref[:, pair * PAIR:(pair + 1) * PAIR] * LOG2E
                m2 = jnp.maximum(m, sink)
                a = jnp.exp2(m - m2)
                o_ref[pl.ds(q0, tq), pair * PAIR:(pair + 1) * PAIR] = (
                    o * a / (l * a + jnp.exp2(sink - m2))).astype(o_ref.dtype)

    _pipeline(seq // tq, scores, softmax, output)


def _swa(proj, cos_q, sin_q, sink_row, batch, seq):
    t = batch * seq
    tab = pl.BlockSpec((None, seq, SEC), lambda b: (b, 0, 0))
    return pl.pallas_call(
        functools.partial(_swa_kernel, seq=seq),
        grid=(batch,),
        in_specs=[pl.BlockSpec((2, seq, SEC), lambda b: (SEC_QC // 2, b, 0)),
                  pl.BlockSpec((None, seq, SEC), lambda b: (SEC_KVC, b, 0)),
                  tab, tab,
                  pl.BlockSpec((1, GROUP_WIDTH), lambda b: (0, 0))],
        out_specs=pl.BlockSpec((seq, GROUP_WIDTH), lambda b: (b, 0)),
        out_shape=jax.ShapeDtypeStruct((t, GROUP_WIDTH), ACT),
        scratch_shapes=[pltpu.VMEM((2, seq, SEC), BF16),
                        pltpu.VMEM((2, seq, PAIR), BF16),
                        pltpu.VMEM((2, seq, PAIR), BF16),
                        pltpu.VMEM((ATT_ITEMS, 2, 4 * ATT_TQ, ATT_TQ + 2 * SWA_HALF_WINDOW), F32),
                        pltpu.VMEM((ATT_ITEMS, 2, 4 * ATT_TQ, ATT_TQ + 2 * SWA_HALF_WINDOW), BF16),
                        pltpu.VMEM((ATT_ITEMS, 4, ATT_TQ, PAIR), F32),
                        pltpu.VMEM((ATT_ITEMS, 4, ATT_TQ, PAIR), F32)],
        compiler_params=_cp(("arbitrary",)),
        name="swa",
    )(proj, proj, cos_q, sin_q, sink_row)


def _dil_kernel(q_ref, k_ref, v_ref, cos_ref, sin_ref, o_ref,
                qf_ref, kf_ref, vf_ref, m_ref, l_ref, acc_ref, qd_ref, kd_ref, vd_ref,
                s_ref, p_ref, mt_ref, lt_ref, *, seq):
    cos, sin = cos_ref[...], sin_ref[...]
    q = _rope(q_ref[...].astype(F32), cos, sin) * QK_SCALE
    k = _rope(k_ref[...].astype(F32), cos, sin)
    for pair in range(2):
        lanes = slice(pair * PAIR, (pair + 1) * PAIR)
        qf_ref[pair] = q[:, lanes]
        kf_ref[pair] = k[:, lanes]
        vf_ref[pair] = v_ref[:, lanes].astype(F32)

    for window, dil in DIL_CONFIGS:
        w = window // (2 * dil)
        ls = seq // dil
        tq = min(ATT_TQ, ls)
        tk = min(tq + 2 * w, ls)

        def rows_of(start, size, dil=dil):
            return pl.ds(start, size) if dil == 1 else pl.ds(start, size, stride=dil)

        def permute(r, carry, ls=ls, rows_of=rows_of):
            dst = pl.ds(pl.multiple_of(r * ls, ls), ls)
            for pair in range(2):
                qd_ref[pair, dst, :] = qf_ref[pair, rows_of(r, ls), :].astype(BF16)
                kd_ref[pair, dst, :] = kf_ref[pair, rows_of(r, ls), :].astype(BF16)
                vd_ref[pair, dst, :] = vf_ref[pair, rows_of(r, ls), :].astype(BF16)
            return carry

        lax.fori_loop(0, dil, permute, 0, unroll=min(dil, 4))
        tiles_per_res = ls // tq

        def coords(j, w=w, ls=ls, tq=tq, tk=tk, tpr=tiles_per_res):
            r = j // tpr
            q0 = (j % tpr) * tq
            k0 = jnp.clip(q0 - w, 0, ls - tk)
            return r, q0, pl.multiple_of(r * ls + k0, w), k0 - q0

        def scores(j, slot, w=w, tq=tq, tk=tk, coords=coords):
            _, _, krow, delta = coords(j)
            valid = _band_valid(tq, tk, w, delta)
            qrow = pl.multiple_of(j * tq, tq)
            for pair in range(2):
                _stage_scores([qd_ref[pair, pl.ds(qrow, tq), :]], kd_ref[pair, pl.ds(krow, tk), :], valid,
                              s_ref.at[:, :, :, 0:tk], slot, pair)

        def softmax(j, slot, tq=tq, tk=tk):
            for pair in range(2):
                _stage_softmax(s_ref.at[:, :, :, 0:tk], p_ref.at[:, :, :, 0:tk], mt_ref, lt_ref, slot, pair, tq)

        def merge(j, slot, tq=tq, tk=tk, dil=dil, coords=coords, rows_of=rows_of):
            r, q0, krow, _ = coords(j)
            rows = rows_of(r + dil * q0, tq)
            for pair in range(2):
                (o,) = _stage_pv(p_ref.at[:, :, :, 0:tk], vd_ref[pair, pl.ds(krow, tk), :], slot, pair, tq)
                m, l = mt_ref[slot, pair], lt_ref[slot, pair]
                if dil == DIL_CONFIGS[0][1]:
                    acc_ref[pair, rows, :], l_ref[pair, rows, :], m_ref[pair, rows, :] = o, l, m
                    continue
                m_old = m_ref[pair, rows, :]
                m_new = jnp.maximum(m_old, m)
                a, b = jnp.exp2(m_old - m_new), jnp.exp2(m - m_new)
                acc_ref[pair, rows, :] = acc_ref[pair, rows, :] * a + o * b
                l_ref[pair, rows, :] = l_ref[pair, rows, :] * a + l * b
                m_ref[pair, rows, :] = m_new

        _pipeline(seq // tq, scores, softmax, merge)

    for pair in range(2):
        o_ref[:, pair * PAIR:(pair + 1) * PAIR] = (acc_ref[pair] / l_ref[pair]).astype(o_ref.dtype)


def _dilated(proj, cos_q, sin_q, batch, seq):
    t = batch * seq

    def sec(base):
        return pl.BlockSpec((None, seq, SEC), lambda b, g: (base + g, b, 0))

    tab = pl.BlockSpec((None, seq, SEC), lambda b, g: (b, 0, 0))
    return pl.pallas_call(
        functools.partial(_dil_kernel, seq=seq),
        grid=(batch, 2),
        in_specs=[sec(SEC_QD), sec(SEC_KD), sec(SEC_VD), tab, tab],
        out_specs=pl.BlockSpec((seq, SEC), lambda b, g: (b, g)),
        out_shape=jax.ShapeDtypeStruct((t, GROUP_WIDTH), ACT),
        scratch_shapes=[pltpu.VMEM((2, seq, PAIR), F32)] * 6 + [pltpu.VMEM((2, seq, PAIR), BF16)] * 3 + [
            pltpu.VMEM((ATT_ITEMS, 2, 2 * ATT_TQ, 2 * ATT_TQ), F32),
            pltpu.VMEM((ATT_ITEMS, 2, 2 * ATT_TQ, 2 * ATT_TQ), BF16),
            pltpu.VMEM((ATT_ITEMS, 2, ATT_TQ, PAIR), F32),
            pltpu.VMEM((ATT_ITEMS, 2, ATT_TQ, PAIR), F32)],
        compiler_params=_cp(("arbitrary", "arbitrary")),
        name="dilated",
    )(proj, proj, proj, cos_q, sin_q)


def _gnorm(y, g):
    return y * lax.rsqrt(jnp.mean(y * y, axis=-1, keepdims=True) + EPS) * g


OUT_SUB = 256
OUT_TM = 512


def _out_kernel(ya_ref, yb_ref, yc_ref, yd_ref, gn_ref, w_ref, x_ref, mod_ref, nf_ref, rcat_ref,
                xn_ref, h_ref, aff_ref):
    ys = (ya_ref, yb_ref, yc_ref, yd_ref)
    gw = GROUP_WIDTH
    row_tiles = x_ref.shape[1] // LANES
    for sub in range(x_ref.shape[0] // OUT_SUB):
        rows = slice(sub * OUT_SUB, (sub + 1) * OUT_SUB)
        acc = None
        for g in range(4):
            yn = _gnorm(ys[g][rows, :].astype(F32), gn_ref[:, g * gw:(g + 1) * gw]).astype(BF16)
            part = jnp.dot(yn, w_ref[g * gw:(g + 1) * gw, :], preferred_element_type=F32)
            acc = part if acc is None else acc + part
        xn = x_ref[rows, :] + mod_ref[2:3, :] * acc
        xn_ref[rows, :] = xn
        h = _gnorm(xn, nf_ref[...]) * (1.0 + mod_ref[4:5, :]) + mod_ref[3:4, :]
        for c in range(row_tiles):
            h_ref[pl.ds(sub * OUT_SUB * row_tiles + c, OUT_SUB, stride=row_tiles), :] = h[:, c * LANES:(c + 1) * LANES]
        a = _nt(rcat_ref[...], h.astype(BF16))
        logits = a[0:N_EXPERTS] + a[N_EXPERTS:2 * N_EXPERTS]
        mxl = jnp.max(logits, axis=0, keepdims=True)
        ex = jnp.exp(logits - mxl)
        aff_ref[:, rows] = ex / jnp.sum(ex, axis=0, keepdims=True)


def _out_proj(ya, yb, yc, yd, gn, w_bf, layer, x2, mod_l, nf, rcat, seq):
    t, d = x2.shape
    tm = OUT_TM
    tpb = seq // tm
    ROW_TILES = d // LANES
    ysp = pl.BlockSpec((tm, GROUP_WIDTH), lambda i: (i, 0))
    return pl.pallas_call(
        _out_kernel,
        grid=(t // tm,),
        in_specs=[ysp] * 4 + [
            pl.BlockSpec((1, d), lambda i: (0, 0)),
            pl.BlockSpec((None, d, d), lambda i: (layer, 0, 0)),
            pl.BlockSpec((tm, d), lambda i: (i, 0)),
            pl.BlockSpec((None, 6, d), lambda i: (i // tpb, 0, 0)),
            pl.BlockSpec((1, d), lambda i: (0, 0)),
            pl.BlockSpec((2 * N_EXPERTS, d), lambda i: (0, 0))],
        out_specs=[pl.BlockSpec((tm, d), lambda i: (i, 0)),
                   pl.BlockSpec((tm * ROW_TILES, LANES), lambda i: (i, 0)),
                   pl.BlockSpec((N_EXPERTS, tm), lambda i: (0, i))],
        out_shape=[jax.ShapeDtypeStruct((t, d), F32),
                   jax.ShapeDtypeStruct((t * ROW_TILES, LANES), F32),
                   jax.ShapeDtypeStruct((N_EXPERTS, t), F32)],
        compiler_params=_cp(("arbitrary",)),
        name="out_proj",
    )(ya, yb, yc, yd, gn, w_bf, x2, mod_l, nf, rcat)


def _route_kernel(aff_ref, tri_ref, idx_ref, gate_ref, dst_ref, cnt_ref, start_ref,
                  pos_ref, q_ref, *, cap, seq):
    aff = aff_ref[...]

    def as_f32(bits):
        return lax.bitcast_convert_type(bits, F32)

    def step(k, thr):
        cand = thr | lax.shift_left(jnp.int32(1), 30 - k)
        cnt = jnp.sum((aff >= as_f32(cand)).astype(jnp.int32), axis=1, keepdims=True)
        return jnp.where(cnt >= cap, cand, thr)

    thr = lax.fori_loop(0, 31, step, jnp.zeros((aff.shape[0], 1), jnp.int32))
    gt = aff >= as_f32(thr + 1)
    eq = (aff >= as_f32(thr)) & jnp.logical_not(gt)
    need = (cap - jnp.sum(gt.astype(jnp.int32), axis=1, keepdims=True)).astype(F32)
    eq_rank = jnp.dot(eq.astype(BF16), tri_ref[...], preferred_element_type=F32)
    sel = gt | (eq & (eq_rank < need))
    n_e = aff.shape[0]
    sel_f = sel.astype(F32)
    sel_bf = sel.astype(BF16)
    pos_ref[...] = jnp.where(sel, jnp.dot(sel_bf, tri_ref[...], preferred_element_type=F32), -1.0)

    cnt = jnp.sum(sel_f, axis=0, keepdims=True)
    cnt8 = jnp.broadcast_to(cnt, (8, seq)).astype(BF16)
    start = (jnp.dot(cnt8, tri_ref[...], preferred_element_type=F32)[0:1]
             + (pl.program_id(0) * (n_e * cap)).astype(F32))
    lower = (lax.broadcasted_iota(jnp.int32, (n_e, n_e), 1)
             < lax.broadcasted_iota(jnp.int32, (n_e, n_e), 0)).astype(BF16)
    q_ref[...] = start + jnp.dot(lower, sel_bf, preferred_element_type=F32)
    cnt_ref[...] = cnt
    start_ref[...] = start

    tok = lax.broadcasted_iota(jnp.int32, (1, seq), 1) + pl.program_id(0) * seq
    tok_hi = lax.shift_right_logical(tok, 7).astype(F32)
    tok_lo = (tok & (LANES - 1)).astype(F32)
    slot = lax.broadcasted_iota(jnp.int32, (cap, seq), 0).astype(F32)
    eye = (lax.broadcasted_iota(jnp.int32, (cap, cap), 0) == lax.broadcasted_iota(jnp.int32, (cap, cap), 1))
    zero_row = jnp.zeros((1, seq), F32)

    def per_expert(e, carry):
        onehot = (pos_ref[pl.ds(e, 1), :] == slot).astype(BF16)
        q = q_ref[pl.ds(e, 1), :]
        q_hi = jnp.floor(q * (1.0 / LANES))
        a = aff_ref[pl.ds(e, 1), :]
        a1 = a.astype(BF16).astype(F32)
        a2 = (a - a1).astype(BF16).astype(F32)
        pieces = jnp.concatenate([tok_hi, tok_lo, q_hi, q - q_hi * LANES, a1, a2, a - a1 - a2, zero_row], axis=0)
        got = _nt(pieces.astype(BF16), onehot)
        idx_ref[e] = (got[0:1] * LANES + got[1:2]).astype(jnp.int32)
        dst_ref[e] = (got[2:3] * LANES + got[3:4]).astype(jnp.int32)
        gate_row = got[4:5] + got[5:6] + got[6:7]
        gate_col = jnp.sum(jnp.where(eye, gate_row, 0.0), axis=1, keepdims=True)
        gate_ref[e] = jnp.broadcast_to(gate_col, (cap, LANES))
        return carry

    lax.fori_loop(0, n_e, per_expert, 0)


def _route(aff_t, tri, batch, seq, cap):
    e, t = aff_t.shape
    blk = pl.BlockSpec((e, seq), lambda b: (0, b))
    colblk = pl.BlockSpec((e, cap, LANES), lambda b: (0, b, 0))
    rowblk = pl.BlockSpec((1, seq), lambda b: (0, b))
    slotblk = pl.BlockSpec((e, None, 1, cap), lambda b: (0, b, 0, 0))
    slot_i32 = jax.ShapeDtypeStruct((e, batch, 1, cap), jnp.int32)
    return pl.pallas_call(
        functools.partial(_route_kernel, cap=cap, seq=seq),
        grid=(batch,),
        in_specs=[blk, pl.BlockSpec((seq, seq), lambda b: (0, 0))],
        out_specs=[slotblk, colblk, slotblk, rowblk, rowblk],
        out_shape=[slot_i32, jax.ShapeDtypeStruct((e, batch * cap, LANES), F32), slot_i32,
                   jax.ShapeDtypeStruct((1, t), F32), jax.ShapeDtypeStruct((1, t), F32)],
        scratch_shapes=[pltpu.VMEM((e, seq), F32)] * 2,
        compiler_params=_cp(("arbitrary",)),
        name="route",
    )(aff_t, tri)


DMA_UNROLL = 32


def _moe_kernel(idx_ref, dst_ref, h_hbm, gate_ref, wg_ref, wu_ref, wd_ref, sorted_hbm,
                xg_ref, xe_ref, acc_ref, ob_ref, sem, *, rows):
    e = pl.program_id(0)
    f = pl.program_id(1)
    n_e = pl.num_programs(0)

    row_tiles = xe_ref.shape[1] // LANES

    def scatter():
        def body(i, carry):
            for u in range(DMA_UNROLL):
                r = i * DMA_UNROLL + u
                pltpu.make_async_copy(ob_ref.at[:, r, :], sorted_hbm.at[:, dst_ref[e * rows + r], :],
                                      sem.at[1]).start()
            return carry
        lax.fori_loop(0, rows // DMA_UNROLL, body, 0)

    def wait_scatter():
        for c in range(row_tiles):
            pltpu.make_async_copy(ob_ref.at[c], sorted_hbm.at[c, pl.ds(0, rows), :], sem.at[1]).wait()

    def gather(expert):
        def body(i, carry):
            for u in range(DMA_UNROLL):
                r = i * DMA_UNROLL + u
                tok = idx_ref[expert * rows + r]
                pltpu.make_async_copy(h_hbm.at[pl.ds(pl.multiple_of(tok * row_tiles, row_tiles), row_tiles), :],
                                      xg_ref.at[:, r, :], sem.at[0]).start()
            return carry
        lax.fori_loop(0, rows // DMA_UNROLL, body, 0)

    @pl.when((e == 0) & (f == 0))
    def _():
        gather(0)

    @pl.when(f == 0)
    def _():
        for c in range(row_tiles):
            pltpu.make_async_copy(h_hbm.at[pl.ds(0, rows), :], xg_ref.at[c], sem.at[0]).wait()
        for c in range(row_tiles):
            xe_ref[:, c * LANES:(c + 1) * LANES] = xg_ref[c].astype(BF16)
        acc_ref[...] = jnp.zeros_like(acc_ref)

        @pl.when(e + 1 < n_e)
        def _():
            gather(e + 1)

    xe = xe_ref[...]
    a = jnp.dot(xe, wg_ref[...].astype(BF16), preferred_element_type=F32)
    u = jnp.dot(xe, wu_ref[...].astype(BF16), preferred_element_type=F32)
    act = (a * jax.nn.sigmoid(a) * u).astype(BF16)
    acc_ref[...] += jnp.dot(act, wd_ref[...].astype(BF16), preferred_element_type=F32)

    @pl.when(f == pl.num_programs(1) - 1)
    def _():
        @pl.when(e > 0)
        def _():
            wait_scatter()

        gate = gate_ref[:, 0:1]
        for c in range(row_tiles):
            ob_ref[c] = acc_ref[:, c * LANES:(c + 1) * LANES] * gate
        scatter()

        @pl.when(e == n_e - 1)
        def _():
            wait_scatter()


def _moe(idx_flat, dst_flat, gate_col, h2, wg, wu, wd, layer):
    _, ne, d, ff = wg.shape
    rows = gate_col.shape[1]
    row_tiles = d // LANES
    tf = 256
    grid_spec = pltpu.PrefetchScalarGridSpec(
        num_scalar_prefetch=2,
        grid=(ne, ff // tf),
        in_specs=[pl.BlockSpec(memory_space=pl.ANY),
                  pl.BlockSpec((None, rows, LANES), lambda e, f, idx, dst: (e, 0, 0)),
                  pl.BlockSpec((None, None, d, tf), lambda e, f, idx, dst: (layer, e, 0, f)),
                  pl.BlockSpec((None, None, d, tf), lambda e, f, idx, dst: (layer, e, 0, f)),
                  pl.BlockSpec((None, None, tf, d), lambda e, f, idx, dst: (layer, e, f, 0))],
        out_specs=pl.BlockSpec(memory_space=pl.ANY),
        scratch_shapes=[pltpu.VMEM((row_tiles, rows, LANES), F32), pltpu.VMEM((rows, d), BF16),
                        pltpu.VMEM((rows, d), F32), pltpu.VMEM((row_tiles, rows, LANES), F32),
                        pltpu.SemaphoreType.DMA((2,))],
    )
    return pl.pallas_call(
        functools.partial(_moe_kernel, rows=rows),
        grid_spec=grid_spec,
        out_shape=jax.ShapeDtypeStruct((row_tiles, ne * rows, LANES), F32),
        compiler_params=_cp(("arbitrary", "arbitrary")),
        name="moe",
    )(idx_flat, dst_flat, h2, gate_col, wg, wu, wd)


COMBINE_CHUNK = 256
COMBINE_TM = 512
COMBINE_STREAMS = 4
COMBINE_SLOTS = 3


def _combine_kernel(tile_ref, start_ref, cnt_ref, sorted_hbm, xn_ref, mod_ref, fn_ref, out_ref,
                    buf_ref, acc_ref, done_ref, sem, *, tm, total, final_norm):
    i = pl.program_id(0)
    n_tiles = pl.num_programs(0)
    row_tiles = buf_ref.shape[1]
    kc = COMBINE_CHUNK

    def span(tile):
        lo = tile_ref[tile]
        base = lax.shift_left(lax.shift_right_logical(lo, 3), 3)
        n_chunks = jnp.maximum(lax.shift_right_logical(tile_ref[tile + 1] - base + (kc - 1), 8), 1)
        return base, n_chunks

    def chunk_rows(base, j):
        return pl.multiple_of(jnp.minimum(base + j * kc, total - kc), 8)

    def chunk_copies(first_row, slot):
        per = row_tiles // COMBINE_STREAMS
        return [pltpu.make_async_copy(sorted_hbm.at[pl.ds(s * per, per), pl.ds(first_row, kc), :],
                                      buf_ref.at[slot, pl.ds(s * per, per)], sem.at[slot])
                for s in range(COMBINE_STREAMS)]

    def start_chunk(first_row, slot):
        for s, cp in enumerate(chunk_copies(first_row, slot)):
            cp.start(priority=s % 2)

    def wait_chunk(first_row, slot):
        for cp in chunk_copies(first_row, slot):
            cp.wait()

    def issue_next():
        t = done_ref[2]

        @pl.when(t < n_tiles)
        def _():
            j = done_ref[3]
            base_t, n_t = span(t)
            issued = done_ref[1]
            start_chunk(chunk_rows(base_t, j), issued % COMBINE_SLOTS)
            done_ref[1] = issued + 1
            wrap = j + 1 >= n_t
            done_ref[2] = jnp.where(wrap, t + 1, t)
            done_ref[3] = jnp.where(wrap, 0, j + 1)

    base, n_chunks = span(i)

    @pl.when(i == 0)
    def _():
        for k in range(4):
            done_ref[k] = 0
        for _ in range(COMBINE_SLOTS - 1):
            issue_next()

    eye = (lax.broadcasted_iota(jnp.int32, (tm, tm), 0) == lax.broadcasted_iota(jnp.int32, (tm, tm), 1))
    lo_col = jnp.sum(jnp.where(eye, start_ref[...], 0.0), axis=1, keepdims=True)
    hi_col = lo_col + jnp.sum(jnp.where(eye, cnt_ref[...], 0.0), axis=1, keepdims=True)
    acc_ref[...] = jnp.zeros_like(acc_ref)

    def chunk(j, carry):
        done = done_ref[0]
        slot = done % COMBINE_SLOTS
        first_row = chunk_rows(base, j)
        wait_chunk(first_row, slot)
        issue_next()

        row = (first_row + lax.broadcasted_iota(jnp.int32, (1, kc), 1)).astype(F32)
        nominal = (base + j * kc).astype(F32)
        pick = ((row >= lo_col) & (row < hi_col) & (row >= nominal)).astype(BF16)
        for c in range(0, row_tiles, 2):
            rhs = jnp.concatenate([buf_ref[slot, c], buf_ref[slot, c + 1]], axis=1).astype(BF16)
            acc_ref[:, c * LANES:(c + 2) * LANES] += jnp.dot(pick, rhs, preferred_element_type=F32)
        done_ref[0] = done + 1
        return carry

    lax.fori_loop(0, n_chunks, chunk, 0)

    x = xn_ref[...] + mod_ref[5:6, :] * acc_ref[...]
    if final_norm:
        x = _gnorm(x, fn_ref[...])
    out_ref[...] = x


def _combine(tile_start, start_row, cnt_row, sorted_rows, xn, mod_l, fn, seq, final_norm):
    t, d = xn.shape
    row_tiles, total, _ = sorted_rows.shape
    tm = COMBINE_TM
    tpb = seq // tm
    grid_spec = pltpu.PrefetchScalarGridSpec(
        num_scalar_prefetch=1,
        grid=(t // tm,),
        in_specs=[pl.BlockSpec((1, tm), lambda i, ts: (0, i)),
                  pl.BlockSpec((1, tm), lambda i, ts: (0, i)),
                  pl.BlockSpec(memory_space=pl.ANY),
                  pl.BlockSpec((tm, d), lambda i, ts: (i, 0)),
                  pl.BlockSpec((None, 6, d), lambda i, ts: (i // tpb, 0, 0)),
                  pl.BlockSpec((1, d), lambda i, ts: (0, 0))],
        out_specs=pl.BlockSpec((tm, d), lambda i, ts: (i, 0)),
        scratch_shapes=[pltpu.VMEM((COMBINE_SLOTS, row_tiles, COMBINE_CHUNK, LANES), F32),
                        pltpu.VMEM((tm, d), F32), pltpu.SMEM((4,), jnp.int32),
                        pltpu.SemaphoreType.DMA((COMBINE_SLOTS,))],
    )
    return pl.pallas_call(
        functools.partial(_combine_kernel, tm=tm, total=total, final_norm=final_norm),
        grid_spec=grid_spec,
        out_shape=jax.ShapeDtypeStruct((t, d), F32),
        compiler_params=_cp(("arbitrary",)),
        name="combine",
    )(tile_start, start_row, cnt_row, sorted_rows, xn, mod_l, fn)


def _rope_tables(positions):
    b, s = positions.shape
    inv = jnp.power(jnp.float32(ROPE_THETA), -jnp.arange(0, ROPE_DIM, 2, dtype=F32) / ROPE_DIM)
    ang = positions.astype(F32)[..., None] * inv
    cos, sin = jnp.cos(ang), jnp.sin(ang)
    rest = HEAD_DIM - ROPE_DIM
    c64 = jnp.concatenate([cos, cos, jnp.ones((b, s, rest), F32)], axis=-1)
    s64 = jnp.concatenate([-sin, sin, jnp.zeros((b, s, rest), F32)], axis=-1)
    return jnp.tile(c64, (1, 1, 4)), jnp.tile(s64, (1, 1, 4))


def _dft_tables(n):
    k = jnp.arange(n, dtype=jnp.int32)
    ang = ((k[:, None] * k[None, :]) % n).astype(F32) * (2.0 * math.pi / n)
    return jnp.cos(ang), jnp.sin(ang)


def _dft_tables_split(n):
    k = jnp.arange(n, dtype=jnp.int32)[:, None]
    m0 = jnp.arange(LANES, dtype=jnp.int32)[None, :]
    m1 = jnp.arange(n // LANES, dtype=jnp.int32)[None, :]
    a = ((k * m0) % n).astype(F32) * (2.0 * math.pi / n)
    b = ((k * m1 * LANES) % n).astype(F32) * (2.0 * math.pi / n)
    reps = n // LANES
    ca, sa = jnp.tile(jnp.cos(a), (1, reps)), jnp.tile(jnp.sin(a), (1, reps))
    cb, sb = jnp.repeat(jnp.cos(b), LANES, axis=1), jnp.repeat(jnp.sin(b), LANES, axis=1)
    return ca * cb - sa * sb, sa * cb + ca * sb


def _block_diag(blocks):
    n, a, _ = blocks.shape
    eye = jnp.eye(n, dtype=blocks.dtype)
    return (eye[:, None, :, None] * blocks[:, :, None, :]).reshape(n * a, n * a)


def kernel(x, c, positions, ada_w, ada_b, norm_mix, w_in, fnet_w, sgu_norm, sgu_w, sgu_b, swa_sink,
           group_norm, w_out, norm_ffn, router_w, exp_w_gate, exp_w_up, exp_w_down, final_norm):
    batch, seq, d = x.shape
    depth = ada_w.shape[0]
    t = batch * seq
    cap = max(1, EC_FACTOR * seq // N_EXPERTS)

    rows = 16
    c_pad = jnp.zeros((rows, d), F32).at[:batch].set(c)
    mod = _ada(c_pad, ada_w, ada_b)[:, :batch].reshape(depth, batch, 6, d)

    cos_q, sin_q = _rope_tables(positions)
    dft_c, dft_s = _dft_tables_split(seq)
    dft_c, dft_s = dft_c.astype(BF16), dft_s.astype(BF16)
    cc, sc = _dft_tables(HEAD_DIM)
    bdc = _block_diag(jnp.broadcast_to(cc, (4, HEAD_DIM, HEAD_DIM)))
    bds = _block_diag(jnp.broadcast_to(sc, (4, HEAD_DIM, HEAD_DIM)))
    tri = (jnp.arange(seq)[:, None] < jnp.arange(seq)[None, :]).astype(BF16)

    w_in_bf, w_out_bf = w_in.astype(BF16), w_out.astype(BF16)
    x2 = x.reshape(t, d)
    for l in range(depth):
        mod_l = mod[l]
        proj = _in_proj(x2, mod_l, norm_mix[l].reshape(1, d), w_in_bf, l, seq)

        bdw = jnp.stack([_block_diag(fnet_w[l, 0:4]), _block_diag(fnet_w[l, 4:8])])
        ya = _fourier(proj, bdc, bds, bdw, dft_c, dft_s, batch, seq).reshape(t, GROUP_WIDTH)
        sgu_w_pairs = jnp.concatenate([sgu_w[l, 0::2], sgu_w[l, 1::2]], axis=-1).astype(BF16)
        yb = _sgu(proj, sgu_norm[l].reshape(1, GROUP_WIDTH), sgu_w_pairs, sgu_b[l].T)
        sink_row = jnp.repeat(swa_sink[l], HEAD_DIM).reshape(1, GROUP_WIDTH)
        yc = _swa(proj, cos_q, sin_q, sink_row, batch, seq)
        yd = _dilated(proj, cos_q, sin_q, batch, seq)

        r_t = router_w[l].T
        r_hi = r_t.astype(BF16)
        r_lo = (r_t - r_hi.astype(F32)).astype(BF16)
        rcat = jnp.concatenate([r_hi, r_lo], axis=0)
        xn, h2, aff_t = _out_proj(ya, yb, yc, yd, group_norm[l].reshape(1, d), w_out_bf, l,
                                  x2, mod_l, norm_ffn[l].reshape(1, d), rcat, seq)
        idx, gate_col, dst, cnt_row, start_row = _route(aff_t, tri, batch, seq, cap)
        sorted_rows = _moe(idx.reshape(-1), dst.reshape(-1), gate_col, h2,
                           exp_w_gate, exp_w_up, exp_w_down, l)
        n_pairs = batch * N_EXPERTS * cap
        tile_start = jnp.concatenate([start_row[0, ::COMBINE_TM].astype(jnp.int32), jnp.full((1,), n_pairs, jnp.int32)])
        x2 = _combine(tile_start, start_row, cnt_row, sorted_rows, xn, mod_l, final_norm.reshape(1, d), seq,
                      l == depth - 1)
    return x2.reshape(batch, seq, d)
```

```python
import functools
import math

import jax
import jax.numpy as jnp
from jax import lax
from jax.experimental import pallas as pl
from jax.experimental.pallas import tpu as pltpu

F32 = jnp.float32
BF16 = jnp.bfloat16
ACT = jnp.bfloat16

HEAD_DIM = 64
N_GROUP_HEADS = 8
GROUP_WIDTH = 512
CHUNK = 128
SWA_HALF_WINDOW = 128
DIL_CONFIGS = ((128, 1), (512, 4), (2048, 16))
ROPE_THETA = 500000.0
ROPE_DIM = 16
N_EXPERTS = 16
EC_FACTOR = 2
EPS = 1e-6
NEG_INF = -1e30

LANES = 128
SEC = 256
SEC_XA, SEC_U, SEC_V, SEC_QC, SEC_KVC, SEC_QD, SEC_KD, SEC_VD = 0, 2, 4, 6, 8, 9, 11, 13
N_SEC = 15

VMEM_LIMIT = 56 * 1024 * 1024


def _cp(sem, vmem=None):
    return pltpu.CompilerParams(dimension_semantics=sem, vmem_limit_bytes=vmem or VMEM_LIMIT)


def _nt(a, b):
    return lax.dot_general(a, b, (((1,), (1,)), ((), ())), preferred_element_type=F32)


ADA_STREAMS = 4


def _ada_kernel(c_ref, *refs):
    w_refs, b_ref, o_ref = refs[:ADA_STREAMS], refs[ADA_STREAMS], refs[ADA_STREAMS + 1]

    @pl.when(pl.program_id(1) == 0)
    def _():
        o_ref[0] = jnp.broadcast_to(b_ref[0], o_ref.shape[1:])

    c = c_ref[...]
    s = (c * jax.nn.sigmoid(c)).astype(BF16)
    tn = w_refs[0].shape[2]
    for j, w_ref in enumerate(w_refs):
        o_ref[0, :, j * tn:(j + 1) * tn] += jnp.dot(s, w_ref[0].astype(BF16), preferred_element_type=F32)


def _ada(c_pad, ada_w, ada_b):
    n_layers, d, n6 = ada_w.shape
    rows = c_pad.shape[0]
    tk = 256
    tn = n6 // ADA_STREAMS
    w_specs = [pl.BlockSpec((1, tk, tn), functools.partial(lambda l, k, j: (l, k, j), j=j))
               for j in range(ADA_STREAMS)]
    return pl.pallas_call(
        _ada_kernel,
        grid=(n_layers, d // tk),
        in_specs=[pl.BlockSpec((rows, tk), lambda l, k: (0, k))] + w_specs + [
            pl.BlockSpec((1, 1, n6), lambda l, k: (l, 0, 0))],
        out_specs=pl.BlockSpec((1, rows, n6), lambda l, k: (l, 0, 0)),
        out_shape=jax.ShapeDtypeStruct((n_layers, rows, n6), F32),
        compiler_params=_cp(("arbitrary", "arbitrary")),
        name="ada",
    )(c_pad, *([ada_w] * ADA_STREAMS), ada_b.reshape(n_layers, 1, n6))


NORM_ROWS = 16
IN_SECS = 5


def _in_kernel(x_ref, mod_ref, g_ref, w_ref, o_ref, h_ref):
    @pl.when(pl.program_id(1) == 0)
    def _():
        gain = g_ref[...] * (1.0 + mod_ref[1:2, :])
        shift = mod_ref[0:1, :]

        def chunk(c, carry):
            rows = pl.ds(pl.multiple_of(c * NORM_ROWS, NORM_ROWS), NORM_ROWS)
            x = x_ref[rows, :]
            r = lax.rsqrt(jnp.mean(x * x, axis=-1, keepdims=True) + EPS)
            h_ref[rows, :] = (x * r * gain + shift).astype(BF16)
            return carry

        lax.fori_loop(0, x_ref.shape[0] // NORM_ROWS, chunk, 0, unroll=8)

    res = jnp.dot(h_ref[...], w_ref[...], preferred_element_type=F32)
    for s in range(IN_SECS):
        o_ref[s] = res[:, s * SEC:(s + 1) * SEC].astype(o_ref.dtype)


def _in_proj(x2, mod_l, norm_g, w_bf, layer, seq):
    t, d = x2.shape
    tm = 1024
    tpb = seq // tm
    return pl.pallas_call(
        _in_kernel,
        grid=(t // tm, N_SEC // IN_SECS),
        in_specs=[pl.BlockSpec((tm, d), lambda i, j: (i, 0)),
                  pl.BlockSpec((None, 6, d), lambda i, j: (i // tpb, 0, 0)),
                  pl.BlockSpec((1, d), lambda i, j: (0, 0)),
                  pl.BlockSpec((None, d, IN_SECS * SEC), lambda i, j: (layer, 0, j))],
        out_specs=pl.BlockSpec((IN_SECS, tm, SEC), lambda i, j: (j, i, 0)),
        out_shape=jax.ShapeDtypeStruct((N_SEC, t, SEC), ACT),
        scratch_shapes=[pltpu.VMEM((tm, d), BF16)],
        compiler_params=_cp(("arbitrary", "arbitrary")),
        name="in_proj",
    )(x2, mod_l, norm_g, w_bf)


def _f1_kernel(x_ref, bdc_ref, bds_ref, bdw_ref, z_ref, m_ref, *, scale):
    @pl.when(pl.program_id(1) == 0)
    def _():
        w = bdw_ref[...]
        mc = jnp.dot(bdc_ref[...], w, preferred_element_type=F32, precision=lax.Precision.HIGHEST)
        msn = jnp.dot(bds_ref[...], w, preferred_element_type=F32, precision=lax.Precision.HIGHEST)
        m_ref[:, 0:SEC] = (mc * scale).astype(BF16)
        m_ref[:, SEC:2 * SEC] = (msn * scale).astype(BF16)

    z_ref[...] = jnp.dot(x_ref[...].astype(BF16), m_ref[...], preferred_element_type=F32).astype(BF16)


def _f2_kernel(c_ref, s_ref, z_ref, o_ref):
    z = z_ref[...]
    o_ref[...] = (jnp.dot(c_ref[...], z[:, 0:SEC], preferred_element_type=F32)
                  - jnp.dot(s_ref[...], z[:, SEC:2 * SEC], preferred_element_type=F32)).astype(o_ref.dtype)


def _fourier(proj, bdc, bds, bdw, dft_c, dft_s, batch, seq):
    t = batch * seq
    tm = 1024
    scale = 1.0 / math.sqrt(seq * HEAD_DIM)
    z = pl.pallas_call(
        functools.partial(_f1_kernel, scale=scale),
        grid=(2, t // tm),
        in_specs=[pl.BlockSpec((None, tm, SEC), lambda s, i: (SEC_XA + s, i, 0)),
                  pl.BlockSpec((SEC, SEC), lambda s, i: (0, 0)),
                  pl.BlockSpec((SEC, SEC), lambda s, i: (0, 0)),
                  pl.BlockSpec((None, SEC, SEC), lambda s, i: (s, 0, 0))],
        out_specs=pl.BlockSpec((None, tm, 2 * SEC), lambda s, i: (s, i, 0)),
        out_shape=jax.ShapeDtypeStruct((2, t, 2 * SEC), BF16),
        scratch_shapes=[pltpu.VMEM((SEC, 2 * SEC), BF16)],
        compiler_params=_cp(("arbitrary", "arbitrary")),
        name="fnet_chan",
    )(proj, bdc, bds, bdw)
    rt = 1024
    return pl.pallas_call(
        _f2_kernel,
        grid=(seq // rt, batch, 2),
        in_specs=[pl.BlockSpec((rt, seq), lambda r, b, s: (r, 0)),
                  pl.BlockSpec((rt, seq), lambda r, b, s: (r, 0)),
                  pl.BlockSpec((None, seq, 2 * SEC), lambda r, b, s: (s, b, 0))],
        out_specs=pl.BlockSpec((None, rt, SEC), lambda r, b, s: (b, r, s)),
        out_shape=jax.ShapeDtypeStruct((batch, seq, GROUP_WIDTH), ACT),
        compiler_params=_cp(("arbitrary", "arbitrary", "arbitrary")),
        name="fnet_seq",
    )(dft_c, dft_s, z)


def _sgu_kernel(u_ref, v_ref, g_ref, w_ref, b_ref, o_ref, *, tm):
    gv = [jax.nn.gelu(v_ref[s].astype(F32)) for s in range(2)]
    ms = (jnp.sum(gv[0] * gv[0], axis=-1, keepdims=True)
          + jnp.sum(gv[1] * gv[1], axis=-1, keepdims=True)) * (1.0 / GROUP_WIDTH)
    r = lax.rsqrt(ms + EPS)
    vn = [(gv[s] * r * g_ref[:, s * SEC:(s + 1) * SEC]).astype(BF16) for s in range(2)]
    lo = lax.broadcasted_iota(jnp.int32, (CHUNK, 2 * HEAD_DIM), 1) < HEAD_DIM
    zero = jnp.zeros((CHUNK, 2 * HEAD_DIM), BF16)
    for s in range(2):
        gu = jax.nn.gelu(u_ref[s].astype(F32))
        for c in range(tm // CHUNK):
            rows = slice(c * CHUNK, (c + 1) * CHUNK)
            for pp in range(2):
                pair = 2 * s + pp
                lanes = slice(pp * 2 * HEAD_DIM, (pp + 1) * 2 * HEAD_DIM)
                vp = vn[s][rows, lanes]
                rhs = jnp.concatenate([jnp.where(lo, vp, zero), jnp.where(lo, zero, vp)], axis=0)
                z = jnp.dot(w_ref[pair], rhs, preferred_element_type=F32)
                bias = jnp.where(lo, b_ref[:, 2 * pair:2 * pair + 1], b_ref[:, 2 * pair + 1:2 * pair + 2])
                o_ref[rows, s * SEC + pp * 2 * HEAD_DIM:s * SEC + (pp + 1) * 2 * HEAD_DIM] = (gu[rows, lanes] * (z + bias)).astype(o_ref.dtype)


def _sgu(proj, sgu_norm, sgu_w_bf, sgu_bt):
    t = proj.shape[1]
    tm = 512
    return pl.pallas_call(
        functools.partial(_sgu_kernel, tm=tm),
        grid=(t // tm,),
        in_specs=[pl.BlockSpec((2, tm, SEC), lambda i: (SEC_U // 2, i, 0)),
                  pl.BlockSpec((2, tm, SEC), lambda i: (SEC_V // 2, i, 0)),
                  pl.BlockSpec((1, GROUP_WIDTH), lambda i: (0, 0)),
                  pl.BlockSpec((N_GROUP_HEADS // 2, CHUNK, 2 * CHUNK), lambda i: (0, 0, 0)),
                  pl.BlockSpec((CHUNK, N_GROUP_HEADS), lambda i: (0, 0))],
        out_specs=pl.BlockSpec((tm, GROUP_WIDTH), lambda i: (i, 0)),
        out_shape=jax.ShapeDtypeStruct((t, GROUP_WIDTH), ACT),
        compiler_params=_cp(("arbitrary",)),
        name="sgu",
    )(proj, proj, sgu_norm, sgu_w_bf, sgu_bt)


LOG2E = math.log2(math.e)
QK_SCALE = HEAD_DIM ** -0.5 * LOG2E
ATT_TQ = 128
PAIR = 2 * HEAD_DIM


def _rope(x, cos_t, sin_t):
    n = x.shape[-1]
    half = ROPE_DIM // 2
    lane = lax.broadcasted_iota(jnp.int32, x.shape, 1) % HEAD_DIM
    swapped = jnp.where(lane < half, pltpu.roll(x, n - half, 1), pltpu.roll(x, half, 1))
    return x * cos_t + swapped * sin_t


def _lo_half(shape):
    return lax.broadcasted_iota(jnp.int32, shape, 1) < HEAD_DIM


def _stage_scores(q_pairs, k, valid, s_ref, slot, grp):
    tq = q_pairs[0].shape[0]
    lo = _lo_half(q_pairs[0].shape)
    zero = jnp.zeros_like(q_pairs[0])
    blocks = []
    for q in q_pairs:
        blocks += [jnp.where(lo, q, zero), jnp.where(lo, zero, q)]
    s = _nt(jnp.concatenate(blocks, axis=0), k)
    for i in range(len(blocks)):
        s_ref[slot, grp, i * tq:(i + 1) * tq, :] = jnp.where(valid, s[i * tq:(i + 1) * tq], NEG_INF)


def _stage_softmax(s_ref, p_ref, m_ref, l_ref, slot, grp, tq):
    s = s_ref[slot, grp]
    m = jnp.max(s, axis=-1, keepdims=True)
    p = jnp.exp2(s - m)
    p_ref[slot, grp] = p.astype(BF16)
    l = jnp.sum(p, axis=-1, keepdims=True)
    n_pairs = s.shape[0] // (2 * tq)
    shp = (tq, PAIR)
    lo = _lo_half(shp)
    for i in range(n_pairs):
        a, b = slice(2 * i * tq, (2 * i + 1) * tq), slice((2 * i + 1) * tq, (2 * i + 2) * tq)
        m_ref[slot, grp * n_pairs + i] = jnp.where(lo, jnp.broadcast_to(m[a], shp), jnp.broadcast_to(m[b], shp))
        l_ref[slot, grp * n_pairs + i] = jnp.where(lo, jnp.broadcast_to(l[a], shp), jnp.broadcast_to(l[b], shp))


def _stage_pv(p_ref, v, slot, grp, tq):
    o = jnp.dot(p_ref[slot, grp], v, preferred_element_type=F32)
    lo = _lo_half((tq, PAIR))
    return [jnp.where(lo, o[2 * i * tq:(2 * i + 1) * tq], o[(2 * i + 1) * tq:(2 * i + 2) * tq])
            for i in range(o.shape[0] // (2 * tq))]


ATT_ITEMS = 8


def _pipeline(n_items, stage_a, stage_b, stage_c):
    def body(jj, carry):
        for stage in (stage_a, stage_b, stage_c):
            for t in range(ATT_ITEMS):
                stage(jj * ATT_ITEMS + t, t)
        return carry

    lax.fori_loop(0, n_items // ATT_ITEMS, body, 0)


def _band_valid(tq, tk, w, delta):
    d = (lax.broadcasted_iota(jnp.int32, (tq, tk), 1) - lax.broadcasted_iota(jnp.int32, (tq, tk), 0)) + delta
    return (d <= w) & (d >= -w)


def _swa_kernel(q_ref, kv_ref, cos_ref, sin_ref, sink_ref, o_ref, qs_ref, ks_ref, vs_ref,
                s_ref, p_ref, m_ref, l_ref, *, seq):
    w, tq = SWA_HALF_WINDOW, ATT_TQ
    tk = tq + 2 * w
    cos, sin = cos_ref[...], sin_ref[...]
    for s in range(2):
        qs_ref[s] = (_rope(q_ref[s].astype(F32), cos, sin) * QK_SCALE).astype(BF16)
    kk = _rope(kv_ref[:, 0:PAIR].astype(F32), cos[:, 0:PAIR], sin[:, 0:PAIR])
    vv = kv_ref[:, PAIR:2 * PAIR].astype(F32)
    lo = lax.broadcasted_iota(jnp.int32, kk.shape, 1) < HEAD_DIM
    kk_sw, vv_sw = pltpu.roll(kk, HEAD_DIM, 1), pltpu.roll(vv, HEAD_DIM, 1)
    ks_ref[0] = jnp.where(lo, kk, kk_sw).astype(BF16)
    ks_ref[1] = jnp.where(lo, kk_sw, kk).astype(BF16)
    vs_ref[0] = jnp.where(lo, vv, vv_sw).astype(BF16)
    vs_ref[1] = jnp.where(lo, vv_sw, vv).astype(BF16)

    def coords(j):
        q0 = pl.multiple_of(j * tq, tq)
        return q0, pl.multiple_of(jnp.clip(q0 - w, 0, seq - tk), tq)

    def scores(j, slot):
        q0, k0 = coords(j)
        valid = _band_valid(tq, tk, w, k0 - q0)
        for g in range(2):
            q_pairs = [qs_ref[g, pl.ds(q0, tq), 0:PAIR], qs_ref[g, pl.ds(q0, tq), PAIR:2 * PAIR]]
            _stage_scores(q_pairs, ks_ref[g, pl.ds(k0, tk), :], valid, s_ref, slot, g)

    def softmax(j, slot):
        for g in range(2):
            _stage_softmax(s_ref, p_ref, m_ref, l_ref, slot, g, tq)

    def output(j, slot):
        q0, k0 = coords(j)
        for g in range(2):
            outs = _stage_pv(p_ref, vs_ref[g, pl.ds(k0, tk), :], slot, g, tq)
            for i, o in enumerate(outs):
                pair = 2 * g + i
                m, l = m_ref[slot, pair], l_ref[slot, pair]
                sink = sink_ref[:, pair * PAIR:(pair + 1) * PAIR] * LOG2E
                m2 = jnp.maximum(m, sink)
                a = jnp.exp2(m - m2)
                o_ref[pl.ds(q0, tq), pair * PAIR:(pair + 1) * PAIR] = (
                    o * a / (l * a + jnp.exp2(sink - m2))).astype(o_ref.dtype)

    _pipeline(seq // tq, scores, softmax, output)


def _swa(proj, cos_q, sin_q, sink_row, batch, seq):
    t = batch * seq
    tab = pl.BlockSpec((None, seq, SEC), lambda b: (b, 0, 0))
    return pl.pallas_call(
        functools.partial(_swa_kernel, seq=seq),
        grid=(batch,),
        in_specs=[pl.BlockSpec((2, seq, SEC), lambda b: (SEC_QC // 2, b, 0)),
                  pl.BlockSpec((None, seq, SEC), lambda b: (SEC_KVC, b, 0)),
                  tab, tab,
                  pl.BlockSpec((1, GROUP_WIDTH), lambda b: (0, 0))],
        out_specs=pl.BlockSpec((seq, GROUP_WIDTH), lambda b: (b, 0)),
        out_shape=jax.ShapeDtypeStruct((t, GROUP_WIDTH), ACT),
        scratch_shapes=[pltpu.VMEM((2, seq, SEC), BF16),
                        pltpu.VMEM((2, seq, PAIR), BF16),
                        pltpu.VMEM((2, seq, PAIR), BF16),
                        pltpu.VMEM((ATT_ITEMS, 2, 4 * ATT_TQ, ATT_TQ + 2 * SWA_HALF_WINDOW), F32),
                        pltpu.VMEM((ATT_ITEMS, 2, 4 * ATT_TQ, ATT_TQ + 2 * SWA_HALF_WINDOW), BF16),
                        pltpu.VMEM((ATT_ITEMS, 4, ATT_TQ, PAIR), F32),
                        pltpu.VMEM((ATT_ITEMS, 4, ATT_TQ, PAIR), F32)],
        compiler_params=_cp(("arbitrary",)),
        name="swa",
    )(proj, proj, cos_q, sin_q, sink_row)


def _dil_kernel(q_ref, k_ref, v_ref, cos_ref, sin_ref, o_ref,
                qf_ref, kf_ref, vf_ref, m_ref, l_ref, acc_ref, qd_ref, kd_ref, vd_ref,
                s_ref, p_ref, mt_ref, lt_ref, *, seq):
    cos, sin = cos_ref[...], sin_ref[...]
    q = _rope(q_ref[...].astype(F32), cos, sin) * QK_SCALE
    k = _rope(k_ref[...].astype(F32), cos, sin)
    for pair in range(2):
        lanes = slice(pair * PAIR, (pair + 1) * PAIR)
        qf_ref[pair] = q[:, lanes]
        kf_ref[pair] = k[:, lanes]
        vf_ref[pair] = v_ref[:, lanes].astype(F32)

    for window, dil in DIL_CONFIGS:
        w = window // (2 * dil)
        ls = seq // dil
        tq = min(ATT_TQ, ls)
        tk = min(tq + 2 * w, ls)

        def rows_of(start, size, dil=dil):
            return pl.ds(start, size) if dil == 1 else pl.ds(start, size, stride=dil)

        def permute(r, carry, ls=ls, rows_of=rows_of):
            dst = pl.ds(pl.multiple_of(r * ls, ls), ls)
            for pair in range(2):
                qd_ref[pair, dst, :] = qf_ref[pair, rows_of(r, ls), :].astype(BF16)
                kd_ref[pair, dst, :] = kf_ref[pair, rows_of(r, ls), :].astype(BF16)
                vd_ref[pair, dst, :] = vf_ref[pair, rows_of(r, ls), :].astype(BF16)
            return carry

        lax.fori_loop(0, dil, permute, 0, unroll=min(dil, 4))
        tiles_per_res = ls // tq

        def coords(j, w=w, ls=ls, tq=tq, tk=tk, tpr=tiles_per_res):
            r = j // tpr
            q0 = (j % tpr) * tq
            k0 = jnp.clip(q0 - w, 0, ls - tk)
            return r, q0, pl.multiple_of(r * ls + k0, w), k0 - q0

        def scores(j, slot, w=w, tq=tq, tk=tk, coords=coords):
            _, _, krow, delta = coords(j)
            valid = _band_valid(tq, tk, w, delta)
            qrow = pl.multiple_of(j * tq, tq)
            for pair in range(2):
                _stage_scores([qd_ref[pair, pl.ds(qrow, tq), :]], kd_ref[pair, pl.ds(krow, tk), :], valid,
                              s_ref.at[:, :, :, 0:tk], slot, pair)

        def softmax(j, slot, tq=tq, tk=tk):
            for pair in range(2):
                _stage_softmax(s_ref.at[:, :, :, 0:tk], p_ref.at[:, :, :, 0:tk], mt_ref, lt_ref, slot, pair, tq)

        def merge(j, slot, tq=tq, tk=tk, dil=dil, coords=coords, rows_of=rows_of):
            r, q0, krow, _ = coords(j)
            rows = rows_of(r + dil * q0, tq)
            for pair in range(2):
                (o,) = _stage_pv(p_ref.at[:, :, :, 0:tk], vd_ref[pair, pl.ds(krow, tk), :], slot, pair, tq)
                m, l = mt_ref[slot, pair], lt_ref[slot, pair]
                if dil == DIL_CONFIGS[0][1]:
                    acc_ref[pair, rows, :], l_ref[pair, rows, :], m_ref[pair, rows, :] = o, l, m
                    continue
                m_old = m_ref[pair, rows, :]
                m_new = jnp.maximum(m_old, m)
                a, b = jnp.exp2(m_old - m_new), jnp.exp2(m - m_new)
                acc_ref[pair, rows, :] = acc_ref[pair, rows, :] * a + o * b
                l_ref[pair, rows, :] = l_ref[pair, rows, :] * a + l * b
                m_ref[pair, rows, :] = m_new

        _pipeline(seq // tq, scores, softmax, merge)

    for pair in range(2):
        o_ref[:, pair * PAIR:(pair + 1) * PAIR] = (acc_ref[pair] / l_ref[pair]).astype(o_ref.dtype)


def _dilated(proj, cos_q, sin_q, batch, seq):
    t = batch * seq

    def sec(base):
        return pl.BlockSpec((None, seq, SEC), lambda b, g: (base + g, b, 0))

    tab = pl.BlockSpec((None, seq, SEC), lambda b, g: (b, 0, 0))
    return pl.pallas_call(
        functools.partial(_dil_kernel, seq=seq),
        grid=(batch, 2),
        in_specs=[sec(SEC_QD), sec(SEC_KD), sec(SEC_VD), tab, tab],
        out_specs=pl.BlockSpec((seq, SEC), lambda b, g: (b, g)),
        out_shape=jax.ShapeDtypeStruct((t, GROUP_WIDTH), ACT),
        scratch_shapes=[pltpu.VMEM((2, seq, PAIR), F32)] * 6 + [pltpu.VMEM((2, seq, PAIR), BF16)] * 3 + [
            pltpu.VMEM((ATT_ITEMS, 2, 2 * ATT_TQ, 2 * ATT_TQ), F32),
            pltpu.VMEM((ATT_ITEMS, 2, 2 * ATT_TQ, 2 * ATT_TQ), BF16),
            pltpu.VMEM((ATT_ITEMS, 2, ATT_TQ, PAIR), F32),
            pltpu.VMEM((ATT_ITEMS, 2, ATT_TQ, PAIR), F32)],
        compiler_params=_cp(("arbitrary", "arbitrary")),
        name="dilated",
    )(proj, proj, proj, cos_q, sin_q)


def _gnorm(y, g):
    return y * lax.rsqrt(jnp.mean(y * y, axis=-1, keepdims=True) + EPS) * g


OUT_SUB = 256
OUT_TM = 512


def _out_kernel(ya_ref, yb_ref, yc_ref, yd_ref, gn_ref, w_ref, x_ref, mod_ref, nf_ref, rcat_ref,
                xn_ref, h_ref, aff_ref):
    ys = (ya_ref, yb_ref, yc_ref, yd_ref)
    gw = GROUP_WIDTH
    row_tiles = x_ref.shape[1] // LANES
    for sub in range(x_ref.shape[0] // OUT_SUB):
        rows = slice(sub * OUT_SUB, (sub + 1) * OUT_SUB)
        acc = None
        for g in range(4):
            yn = _gnorm(ys[g][rows, :].astype(F32), gn_ref[:, g * gw:(g + 1) * gw]).astype(BF16)
            part = jnp.dot(yn, w_ref[g * gw:(g + 1) * gw, :], preferred_element_type=F32)
            acc = part if acc is None else acc + part
        xn = x_ref[rows, :] + mod_ref[2:3, :] * acc
        xn_ref[rows, :] = xn
        h = _gnorm(xn, nf_ref[...]) * (1.0 + mod_ref[4:5, :]) + mod_ref[3:4, :]
        for c in range(row_tiles):
            h_ref[pl.ds(sub * OUT_SUB * row_tiles + c, OUT_SUB, stride=row_tiles), :] = h[:, c * LANES:(c + 1) * LANES]
        a = _nt(rcat_ref[...], h.astype(BF16))
        logits = a[0:N_EXPERTS] + a[N_EXPERTS:2 * N_EXPERTS]
        mxl = jnp.max(logits, axis=0, keepdims=True)
        ex = jnp.exp(logits - mxl)
        aff_ref[:, rows] = ex / jnp.sum(ex, axis=0, keepdims=True)


def _out_proj(ya, yb, yc, yd, gn, w_bf, layer, x2, mod_l, nf, rcat, seq):
    t, d = x2.shape
    tm = OUT_TM
    tpb = seq // tm
    ROW_TILES = d // LANES
    ysp = pl.BlockSpec((tm, GROUP_WIDTH), lambda i: (i, 0))
    return pl.pallas_call(
        _out_kernel,
        grid=(t // tm,),
        in_specs=[ysp] * 4 + [
            pl.BlockSpec((1, d), lambda i: (0, 0)),
            pl.BlockSpec((None, d, d), lambda i: (layer, 0, 0)),
            pl.BlockSpec((tm, d), lambda i: (i, 0)),
            pl.BlockSpec((None, 6, d), lambda i: (i // tpb, 0, 0)),
            pl.BlockSpec((1, d), lambda i: (0, 0)),
            pl.BlockSpec((2 * N_EXPERTS, d), lambda i: (0, 0))],
        out_specs=[pl.BlockSpec((tm, d), lambda i: (i, 0)),
                   pl.BlockSpec((tm * ROW_TILES, LANES), lambda i: (i, 0)),
                   pl.BlockSpec((N_EXPERTS, tm), lambda i: (0, i))],
        out_shape=[jax.ShapeDtypeStruct((t, d), F32),
                   jax.ShapeDtypeStruct((t * ROW_TILES, LANES), F32),
                   jax.ShapeDtypeStruct((N_EXPERTS, t), F32)],
        compiler_params=_cp(("arbitrary",)),
        name="out_proj",
    )(ya, yb, yc, yd, gn, w_bf, x2, mod_l, nf, rcat)


def _route_kernel(aff_ref, tri_ref, idx_ref, gate_ref, dst_ref, cnt_ref, start_ref,
                  pos_ref, q_ref, *, cap, seq):
    aff = aff_ref[...]

    def as_f32(bits):
        return lax.bitcast_convert_type(bits, F32)

    def enough(cand):
        return jnp.sum((aff >= as_f32(cand)).astype(jnp.int32), axis=1, keepdims=True) >= cap

    top = jnp.full((aff.shape[0], 1), 1 << 30, jnp.int32)
    thr = jnp.where(enough(top), top, 0)

    def step(k, thr):
        hi = lax.shift_left(jnp.int32(1), 29 - 2 * k)
        lo = lax.shift_left(jnp.int32(1), 28 - 2 * k)
        c1, c2 = thr | lo, thr | hi
        c3 = c2 | lo
        return jnp.where(enough(c3), c3, jnp.where(enough(c2), c2, jnp.where(enough(c1), c1, thr)))

    thr = lax.fori_loop(0, 15, step, thr)
    gt = aff >= as_f32(thr + 1)
    eq = (aff >= as_f32(thr)) & jnp.logical_not(gt)
    need = (cap - jnp.sum(gt.astype(jnp.int32), axis=1, keepdims=True)).astype(F32)
    eq_rank = jnp.dot(eq.astype(BF16), tri_ref[...], preferred_element_type=F32)
    sel = gt | (eq & (eq_rank < need))
    n_e = aff.shape[0]
    sel_f = sel.astype(F32)
    sel_bf = sel.astype(BF16)
    pos_ref[...] = jnp.where(sel, jnp.dot(sel_bf, tri_ref[...], preferred_element_type=F32), -1.0)

    cnt = jnp.sum(sel_f, axis=0, keepdims=True)
    cnt8 = jnp.broadcast_to(cnt, (8, seq)).astype(BF16)
    start = (jnp.dot(cnt8, tri_ref[...], preferred_element_type=F32)[0:1]
             + (pl.program_id(0) * (n_e * cap)).astype(F32))
    lower = (lax.broadcasted_iota(jnp.int32, (n_e, n_e), 1)
             < lax.broadcasted_iota(jnp.int32, (n_e, n_e), 0)).astype(BF16)
    q_ref[...] = start + jnp.dot(lower, sel_bf, preferred_element_type=F32)
    cnt_ref[...] = cnt
    start_ref[...] = start

    tok = lax.broadcasted_iota(jnp.int32, (1, seq), 1) + pl.program_id(0) * seq
    tok_hi = lax.shift_right_logical(tok, 7).astype(F32)
    tok_lo = (tok & (LANES - 1)).astype(F32)
    slot = lax.broadcasted_iota(jnp.int32, (cap, seq), 0).astype(F32)
    eye = (lax.broadcasted_iota(jnp.int32, (cap, cap), 0) == lax.broadcasted_iota(jnp.int32, (cap, cap), 1))
    zero_row = jnp.zeros((1, seq), F32)

    def per_expert(e, carry):
        onehot = (pos_ref[pl.ds(e, 1), :] == slot).astype(BF16)
        q = q_ref[pl.ds(e, 1), :]
        q_hi = jnp.floor(q * (1.0 / LANES))
        a = aff_ref[pl.ds(e, 1), :]
        a1 = a.astype(BF16).astype(F32)
        a2 = (a - a1).astype(BF16).astype(F32)
        pieces = jnp.concatenate([tok_hi, tok_lo, q_hi, q - q_hi * LANES, a1, a2, a - a1 - a2, zero_row], axis=0)
        got = _nt(pieces.astype(BF16), onehot)
        idx_ref[e] = (got[0:1] * LANES + got[1:2]).astype(jnp.int32)
        dst_ref[e] = (got[2:3] * LANES + got[3:4]).astype(jnp.int32)
        gate_row = got[4:5] + got[5:6] + got[6:7]
        gate_col = jnp.sum(jnp.where(eye, gate_row, 0.0), axis=1, keepdims=True)
        gate_ref[e] = jnp.broadcast_to(gate_col, (cap, LANES))
        return carry

    lax.fori_loop(0, n_e, per_expert, 0)


def _route(aff_t, tri, batch, seq, cap):
    e, t = aff_t.shape
    blk = pl.BlockSpec((e, seq), lambda b: (0, b))
    colblk = pl.BlockSpec((e, cap, LANES), lambda b: (0, b, 0))
    rowblk = pl.BlockSpec((1, seq), lambda b: (0, b))
    slotblk = pl.BlockSpec((e, None, 1, cap), lambda b: (0, b, 0, 0))
    slot_i32 = jax.ShapeDtypeStruct((e, batch, 1, cap), jnp.int32)
    return pl.pallas_call(
        functools.partial(_route_kernel, cap=cap, seq=seq),
        grid=(batch,),
        in_specs=[blk, pl.BlockSpec((seq, seq), lambda b: (0, 0))],
        out_specs=[slotblk, colblk, slotblk, rowblk, rowblk],
        out_shape=[slot_i32, jax.ShapeDtypeStruct((e, batch * cap, LANES), F32), slot_i32,
                   jax.ShapeDtypeStruct((1, t), F32), jax.ShapeDtypeStruct((1, t), F32)],
        scratch_shapes=[pltpu.VMEM((e, seq), F32)] * 2,
        compiler_params=_cp(("arbitrary",)),
        name="route",
    )(aff_t, tri)


DMA_UNROLL = 32


def _moe_kernel(idx_ref, dst_ref, h_hbm, gate_ref, wg_ref, wu_ref, wd_ref, sorted_hbm,
                xg_ref, xe_ref, acc_ref, ob_ref, sem, *, rows):
    e = pl.program_id(0)
    f = pl.program_id(1)
    n_e = pl.num_programs(0)

    row_tiles = xe_ref.shape[1] // LANES

    def scatter():
        def body(i, carry):
            for u in range(DMA_UNROLL):
                r = i * DMA_UNROLL + u
                pltpu.make_async_copy(ob_ref.at[:, r, :], sorted_hbm.at[:, dst_ref[e * rows + r], :],
                                      sem.at[1]).start()
            return carry
        lax.fori_loop(0, rows // DMA_UNROLL, body, 0)

    def wait_scatter():
        for c in range(row_tiles):
            pltpu.make_async_copy(ob_ref.at[c], sorted_hbm.at[c, pl.ds(0, rows), :], sem.at[1]).wait()

    def gather(expert):
        def body(i, carry):
            for u in range(DMA_UNROLL):
                r = i * DMA_UNROLL + u
                tok = idx_ref[expert * rows + r]
                pltpu.make_async_copy(h_hbm.at[pl.ds(pl.multiple_of(tok * row_tiles, row_tiles), row_tiles), :],
                                      xg_ref.at[:, r, :], sem.at[0]).start()
            return carry
        lax.fori_loop(0, rows // DMA_UNROLL, body, 0)

    @pl.when((e == 0) & (f == 0))
    def _():
        gather(0)

    @pl.when(f == 0)
    def _():
        for c in range(row_tiles):
            pltpu.make_async_copy(h_hbm.at[pl.ds(0, rows), :], xg_ref.at[c], sem.at[0]).wait()
        for c in range(row_tiles):
            xe_ref[:, c * LANES:(c + 1) * LANES] = xg_ref[c].astype(BF16)
        acc_ref[...] = jnp.zeros_like(acc_ref)

        @pl.when(e + 1 < n_e)
        def _():
            gather(e + 1)

    xe = xe_ref[...]
    a = jnp.dot(xe, wg_ref[...].astype(BF16), preferred_element_type=F32)
    u = jnp.dot(xe, wu_ref[...].astype(BF16), preferred_element_type=F32)
    act = (a * jax.nn.sigmoid(a) * u).astype(BF16)
    acc_ref[...] += jnp.dot(act, wd_ref[...].astype(BF16), preferred_element_type=F32)

    @pl.when(f == pl.num_programs(1) - 1)
    def _():
        @pl.when(e > 0)
        def _():
            wait_scatter()

        gate = gate_ref[:, 0:1]
        for c in range(row_tiles):
            ob_ref[c] = acc_ref[:, c * LANES:(c + 1) * LANES] * gate
        scatter()

        @pl.when(e == n_e - 1)
        def _():
            wait_scatter()


def _moe(idx_flat, dst_flat, gate_col, h2, wg, wu, wd, layer):
    _, ne, d, ff = wg.shape
    rows = gate_col.shape[1]
    row_tiles = d // LANES
    tf = 256
    grid_spec = pltpu.PrefetchScalarGridSpec(
        num_scalar_prefetch=2,
        grid=(ne, ff // tf),
        in_specs=[pl.BlockSpec(memory_space=pl.ANY),
                  pl.BlockSpec((None, rows, LANES), lambda e, f, idx, dst: (e, 0, 0)),
                  pl.BlockSpec((None, None, d, tf), lambda e, f, idx, dst: (layer, e, 0, f)),
                  pl.BlockSpec((None, None, d, tf), lambda e, f, idx, dst: (layer, e, 0, f)),
                  pl.BlockSpec((None, None, tf, d), lambda e, f, idx, dst: (layer, e, f, 0))],
        out_specs=pl.BlockSpec(memory_space=pl.ANY),
        scratch_shapes=[pltpu.VMEM((row_tiles, rows, LANES), F32), pltpu.VMEM((rows, d), BF16),
                        pltpu.VMEM((rows, d), F32), pltpu.VMEM((row_tiles, rows, LANES), F32),
                        pltpu.SemaphoreType.DMA((2,))],
    )
    return pl.pallas_call(
        functools.partial(_moe_kernel, rows=rows),
        grid_spec=grid_spec,
        out_shape=jax.ShapeDtypeStruct((row_tiles, ne * rows, LANES), F32),
        compiler_params=_cp(("arbitrary", "arbitrary")),
        name="moe",
    )(idx_flat, dst_flat, h2, gate_col, wg, wu, wd)


COMBINE_CHUNK = 256
COMBINE_TM = 512
COMBINE_STREAMS = 4
COMBINE_SLOTS = 3


def _combine_kernel(tile_ref, start_ref, cnt_ref, sorted_hbm, xn_ref, mod_ref, fn_ref, out_ref,
                    buf_ref, acc_ref, done_ref, sem, *, tm, total, final_norm):
    i = pl.program_id(0)
    n_tiles = pl.num_programs(0)
    row_tiles = buf_ref.shape[1]
    kc = COMBINE_CHUNK

    def span(tile):
        lo = tile_ref[tile]
        base = lax.shift_left(lax.shift_right_logical(lo, 3), 3)
        n_chunks = jnp.maximum(lax.shift_right_logical(tile_ref[tile + 1] - base + (kc - 1), 8), 1)
        return base, n_chunks

    def chunk_rows(base, j):
        return pl.multiple_of(jnp.minimum(base + j * kc, total - kc), 8)

    def chunk_copies(first_row, slot):
        per = row_tiles // COMBINE_STREAMS
        return [pltpu.make_async_copy(sorted_hbm.at[pl.ds(s * per, per), pl.ds(first_row, kc), :],
                                      buf_ref.at[slot, pl.ds(s * per, per)], sem.at[slot])
                for s in range(COMBINE_STREAMS)]

    def start_chunk(first_row, slot):
        for s, cp in enumerate(chunk_copies(first_row, slot)):
            cp.start(priority=s % 2)

    def wait_chunk(first_row, slot):
        for cp in chunk_copies(first_row, slot):
            cp.wait()

    def issue_next():
        t = done_ref[2]

        @pl.when(t < n_tiles)
        def _():
            j = done_ref[3]
            base_t, n_t = span(t)
            issued = done_ref[1]
            start_chunk(chunk_rows(base_t, j), issued % COMBINE_SLOTS)
            done_ref[1] = issued + 1
            wrap = j + 1 >= n_t
            done_ref[2] = jnp.where(wrap, t + 1, t)
            done_ref[3] = jnp.where(wrap, 0, j + 1)

    base, n_chunks = span(i)

    @pl.when(i == 0)
    def _():
        for k in range(4):
            done_ref[k] = 0
        for _ in range(COMBINE_SLOTS - 1):
            issue_next()

    eye = (lax.broadcasted_iota(jnp.int32, (tm, tm), 0) == lax.broadcasted_iota(jnp.int32, (tm, tm), 1))
    lo_col = jnp.sum(jnp.where(eye, start_ref[...], 0.0), axis=1, keepdims=True)
    hi_col = lo_col + jnp.sum(jnp.where(eye, cnt_ref[...], 0.0), axis=1, keepdims=True)
    acc_ref[...] = jnp.zeros_like(acc_ref)

    def chunk(j, carry):
        done = done_ref[0]
        slot = done % COMBINE_SLOTS
        first_row = chunk_rows(base, j)
        wait_chunk(first_row, slot)
        issue_next()

        row = (first_row + lax.broadcasted_iota(jnp.int32, (1, kc), 1)).astype(F32)
        nominal = (base + j * kc).astype(F32)
        pick = ((row >= lo_col) & (row < hi_col) & (row >= nominal)).astype(BF16)
        for c in range(0, row_tiles, 2):
            rhs = jnp.concatenate([buf_ref[slot, c], buf_ref[slot, c + 1]], axis=1).astype(BF16)
            acc_ref[:, c * LANES:(c + 2) * LANES] += jnp.dot(pick, rhs, preferred_element_type=F32)
        done_ref[0] = done + 1
        return carry

    lax.fori_loop(0, n_chunks, chunk, 0)

    x = xn_ref[...] + mod_ref[5:6, :] * acc_ref[...]
    if final_norm:
        x = _gnorm(x, fn_ref[...])
    out_ref[...] = x


def _combine(tile_start, start_row, cnt_row, sorted_rows, xn, mod_l, fn, seq, final_norm):
    t, d = xn.shape
    row_tiles, total, _ = sorted_rows.shape
    tm = COMBINE_TM
    tpb = seq // tm
    grid_spec = pltpu.PrefetchScalarGridSpec(
        num_scalar_prefetch=1,
        grid=(t // tm,),
        in_specs=[pl.BlockSpec((1, tm), lambda i, ts: (0, i)),
                  pl.BlockSpec((1, tm), lambda i, ts: (0, i)),
                  pl.BlockSpec(memory_space=pl.ANY),
                  pl.BlockSpec((tm, d), lambda i, ts: (i, 0)),
                  pl.BlockSpec((None, 6, d), lambda i, ts: (i // tpb, 0, 0)),
                  pl.BlockSpec((1, d), lambda i, ts: (0, 0))],
        out_specs=pl.BlockSpec((tm, d), lambda i, ts: (i, 0)),
        scratch_shapes=[pltpu.VMEM((COMBINE_SLOTS, row_tiles, COMBINE_CHUNK, LANES), F32),
                        pltpu.VMEM((tm, d), F32), pltpu.SMEM((4,), jnp.int32),
                        pltpu.SemaphoreType.DMA((COMBINE_SLOTS,))],
    )
    return pl.pallas_call(
        functools.partial(_combine_kernel, tm=tm, total=total, final_norm=final_norm),
        grid_spec=grid_spec,
        out_shape=jax.ShapeDtypeStruct((t, d), F32),
        compiler_params=_cp(("arbitrary",)),
        name="combine",
    )(tile_start, start_row, cnt_row, sorted_rows, xn, mod_l, fn)


def _rope_tables(positions):
    b, s = positions.shape
    inv = jnp.power(jnp.float32(ROPE_THETA), -jnp.arange(0, ROPE_DIM, 2, dtype=F32) / ROPE_DIM)
    ang = positions.astype(F32)[..., None] * inv
    cos, sin = jnp.cos(ang), jnp.sin(ang)
    rest = HEAD_DIM - ROPE_DIM
    c64 = jnp.concatenate([cos, cos, jnp.ones((b, s, rest), F32)], axis=-1)
    s64 = jnp.concatenate([-sin, sin, jnp.zeros((b, s, rest), F32)], axis=-1)
    return jnp.tile(c64, (1, 1, 4)), jnp.tile(s64, (1, 1, 4))


def _dft_tables(n):
    k = jnp.arange(n, dtype=jnp.int32)
    ang = ((k[:, None] * k[None, :]) % n).astype(F32) * (2.0 * math.pi / n)
    return jnp.cos(ang), jnp.sin(ang)


def _dft_tables_split(n):
    k = jnp.arange(n, dtype=jnp.int32)[:, None]
    m0 = jnp.arange(LANES, dtype=jnp.int32)[None, :]
    m1 = jnp.arange(n // LANES, dtype=jnp.int32)[None, :]
    a = ((k * m0) % n).astype(F32) * (2.0 * math.pi / n)
    b = ((k * m1 * LANES) % n).astype(F32) * (2.0 * math.pi / n)
    reps = n // LANES
    ca, sa = jnp.tile(jnp.cos(a), (1, reps)), jnp.tile(jnp.sin(a), (1, reps))
    cb, sb = jnp.repeat(jnp.cos(b), LANES, axis=1), jnp.repeat(jnp.sin(b), LANES, axis=1)
    return ca * cb - sa * sb, sa * cb + ca * sb


def _block_diag(blocks):
    n, a, _ = blocks.shape
    eye = jnp.eye(n, dtype=blocks.dtype)
    return (eye[:, None, :, None] * blocks[:, :, None, :]).reshape(n * a, n * a)


def kernel(x, c, positions, ada_w, ada_b, norm_mix, w_in, fnet_w, sgu_norm, sgu_w, sgu_b, swa_sink,
           group_norm, w_out, norm_ffn, router_w, exp_w_gate, exp_w_up, exp_w_down, final_norm):
    batch, seq, d = x.shape
    depth = ada_w.shape[0]
    t = batch * seq
    cap = max(1, EC_FACTOR * seq // N_EXPERTS)

    rows = 16
    c_pad = jnp.zeros((rows, d), F32).at[:batch].set(c)
    mod = _ada(c_pad, ada_w, ada_b)[:, :batch].reshape(depth, batch, 6, d)

    cos_q, sin_q = _rope_tables(positions)
    dft_c, dft_s = _dft_tables_split(seq)
    dft_c, dft_s = dft_c.astype(BF16), dft_s.astype(BF16)
    cc, sc = _dft_tables(HEAD_DIM)
    bdc = _block_diag(jnp.broadcast_to(cc, (4, HEAD_DIM, HEAD_DIM)))
    bds = _block_diag(jnp.broadcast_to(sc, (4, HEAD_DIM, HEAD_DIM)))
    tri = (jnp.arange(seq)[:, None] < jnp.arange(seq)[None, :]).astype(BF16)

    w_in_bf, w_out_bf = w_in.astype(BF16), w_out.astype(BF16)
    x2 = x.reshape(t, d)
    for l in range(depth):
        mod_l = mod[l]
        proj = _in_proj(x2, mod_l, norm_mix[l].reshape(1, d), w_in_bf, l, seq)

        bdw = jnp.stack([_block_diag(fnet_w[l, 0:4]), _block_diag(fnet_w[l, 4:8])])
        ya = _fourier(proj, bdc, bds, bdw, dft_c, dft_s, batch, seq).reshape(t, GROUP_WIDTH)
        sgu_w_pairs = jnp.concatenate([sgu_w[l, 0::2], sgu_w[l, 1::2]], axis=-1).astype(BF16)
        yb = _sgu(proj, sgu_norm[l].reshape(1, GROUP_WIDTH), sgu_w_pairs, sgu_b[l].T)
        sink_row = jnp.repeat(swa_sink[l], HEAD_DIM).reshape(1, GROUP_WIDTH)
        yc = _swa(proj, cos_q, sin_q, sink_row, batch, seq)
        yd = _dilated(proj, cos_q, sin_q, batch, seq)

        r_t = router_w[l].T
        r_hi = r_t.astype(BF16)
        r_lo = (r_t - r_hi.astype(F32)).astype(BF16)
        rcat = jnp.concatenate([r_hi, r_lo], axis=0)
        xn, h2, aff_t = _out_proj(ya, yb, yc, yd, group_norm[l].reshape(1, d), w_out_bf, l,
                                  x2, mod_l, norm_ffn[l].reshape(1, d), rcat, seq)
        idx, gate_col, dst, cnt_row, start_row = _route(aff_t, tri, batch, seq, cap)
        sorted_rows = _moe(idx.reshape(-1), dst.reshape(-1), gate_col, h2,
                           exp_w_gate, exp_w_up, exp_w_down, l)
        n_pairs = batch * N_EXPERTS * cap
        tile_start = jnp.concatenate([start_row[0, ::COMBINE_TM].astype(jnp.int32), jnp.full((1,), n_pairs, jnp.int32)])
        x2 = _combine(tile_start, start_row, cnt_row, sorted_rows, xn, mod_l, final_norm.reshape(1, d), seq,
                      l == depth - 1)
    return x2.reshape(batch, seq, d)
```

```python
import functools
import math

import jax
import jax.numpy as jnp
from jax import lax
from jax.experimental import pallas as pl
from jax.experimental.pallas import tpu as pltpu

F32 = jnp.float32
BF16 = jnp.bfloat16
ACT = jnp.bfloat16

HEAD_DIM = 64
N_GROUP_HEADS = 8
GROUP_WIDTH = 512
CHUNK = 128
SWA_HALF_WINDOW = 128
DIL_CONFIGS = ((128, 1), (512, 4), (2048, 16))
ROPE_THETA = 500000.0
ROPE_DIM = 16
N_EXPERTS = 16
EC_FACTOR = 2
EPS = 1e-6
NEG_INF = -1e30

LANES = 128
SEC = 256
SEC_XA, SEC_U, SEC_V, SEC_QC, SEC_KVC, SEC_QD, SEC_KD, SEC_VD = 0, 2, 4, 6, 8, 9, 11, 13
N_SEC = 15

VMEM_LIMIT = 56 * 1024 * 1024


def _cp(sem, vmem=None):
    return pltpu.CompilerParams(dimension_semantics=sem, vmem_limit_bytes=vmem or VMEM_LIMIT)


def _nt(a, b):
    return lax.dot_general(a, b, (((1,), (1,)), ((), ())), preferred_element_type=F32)


ADA_STREAMS = 4


def _ada_kernel(c_ref, *refs):
    w_refs, b_ref, o_ref = refs[:ADA_STREAMS], refs[ADA_STREAMS], refs[ADA_STREAMS + 1]

    @pl.when(pl.program_id(1) == 0)
    def _():
        o_ref[0] = jnp.broadcast_to(b_ref[0], o_ref.shape[1:])

    c = c_ref[...]
    s = (c * jax.nn.sigmoid(c)).astype(BF16)
    tn = w_refs[0].shape[2]
    for j, w_ref in enumerate(w_refs):
        o_ref[0, :, j * tn:(j + 1) * tn] += jnp.dot(s, w_ref[0].astype(BF16), preferred_element_type=F32)


def _ada(c_pad, ada_w, ada_b):
    n_layers, d, n6 = ada_w.shape
    rows = c_pad.shape[0]
    tk = 256
    tn = n6 // ADA_STREAMS
    w_specs = [pl.BlockSpec((1, tk, tn), functools.partial(lambda l, k, j: (l, k, j), j=j))
               for j in range(ADA_STREAMS)]
    return pl.pallas_call(
        _ada_kernel,
        grid=(n_layers, d // tk),
        in_specs=[pl.BlockSpec((rows, tk), lambda l, k: (0, k))] + w_specs + [
            pl.BlockSpec((1, 1, n6), lambda l, k: (l, 0, 0))],
        out_specs=pl.BlockSpec((1, rows, n6), lambda l, k: (l, 0, 0)),
        out_shape=jax.ShapeDtypeStruct((n_layers, rows, n6), F32),
        compiler_params=_cp(("arbitrary", "arbitrary")),
        name="ada",
    )(c_pad, *([ada_w] * ADA_STREAMS), ada_b.reshape(n_layers, 1, n6))


NORM_ROWS = 16
IN_SECS = 5


def _in_kernel(x_ref, mod_ref, g_ref, w_ref, o_ref, h_ref):
    @pl.when(pl.program_id(1) == 0)
    def _():
        gain = g_ref[...] * (1.0 + mod_ref[1:2, :])
        shift = mod_ref[0:1, :]

        def chunk(c, carry):
            rows = pl.ds(pl.multiple_of(c * NORM_ROWS, NORM_ROWS), NORM_ROWS)
            x = x_ref[rows, :]
            r = lax.rsqrt(jnp.mean(x * x, axis=-1, keepdims=True) + EPS)
            h_ref[rows, :] = (x * r * gain + shift).astype(BF16)
            return carry

        lax.fori_loop(0, x_ref.shape[0] // NORM_ROWS, chunk, 0, unroll=8)

    res = jnp.dot(h_ref[...], w_ref[...], preferred_element_type=F32)
    for s in range(IN_SECS):
        o_ref[s] = res[:, s * SEC:(s + 1) * SEC].astype(o_ref.dtype)


def _in_proj(x2, mod_l, norm_g, w_bf, layer, seq):
    t, d = x2.shape
    tm = 1024
    tpb = seq // tm
    return pl.pallas_call(
        _in_kernel,
        grid=(t // tm, N_SEC // IN_SECS),
        in_specs=[pl.BlockSpec((tm, d), lambda i, j: (i, 0)),
                  pl.BlockSpec((None, 6, d), lambda i, j: (i // tpb, 0, 0)),
                  pl.BlockSpec((1, d), lambda i, j: (0, 0)),
                  pl.BlockSpec((None, d, IN_SECS * SEC), lambda i, j: (layer, 0, j))],
        out_specs=pl.BlockSpec((IN_SECS, tm, SEC), lambda i, j: (j, i, 0)),
        out_shape=jax.ShapeDtypeStruct((N_SEC, t, SEC), ACT),
        scratch_shapes=[pltpu.VMEM((tm, d), BF16)],
        compiler_params=_cp(("arbitrary", "arbitrary")),
        name="in_proj",
    )(x2, mod_l, norm_g, w_bf)


def _f1_kernel(x_ref, bdc_ref, bds_ref, bdw_ref, z_ref, m_ref, *, scale):
    @pl.when(pl.program_id(1) == 0)
    def _():
        w = bdw_ref[...]
        mc = jnp.dot(bdc_ref[...], w, preferred_element_type=F32, precision=lax.Precision.HIGHEST)
        msn = jnp.dot(bds_ref[...], w, preferred_element_type=F32, precision=lax.Precision.HIGHEST)
        m_ref[:, 0:SEC] = (mc * scale).astype(BF16)
        m_ref[:, SEC:2 * SEC] = (msn * scale).astype(BF16)

    z_ref[...] = jnp.dot(x_ref[...].astype(BF16), m_ref[...], preferred_element_type=F32).astype(BF16)


def _f2_kernel(c_ref, s_ref, z_ref, o_ref):
    z = z_ref[...]
    o_ref[...] = (jnp.dot(c_ref[...], z[:, 0:SEC], preferred_element_type=F32)
                  - jnp.dot(s_ref[...], z[:, SEC:2 * SEC], preferred_element_type=F32)).astype(o_ref.dtype)


def _fourier(proj, bdc, bds, bdw, dft_c, dft_s, batch, seq):
    t = batch * seq
    tm = 1024
    scale = 1.0 / math.sqrt(seq * HEAD_DIM)
    z = pl.pallas_call(
        functools.partial(_f1_kernel, scale=scale),
        grid=(2, t // tm),
        in_specs=[pl.BlockSpec((None, tm, SEC), lambda s, i: (SEC_XA + s, i, 0)),
                  pl.BlockSpec((SEC, SEC), lambda s, i: (0, 0)),
                  pl.BlockSpec((SEC, SEC), lambda s, i: (0, 0)),
                  pl.BlockSpec((None, SEC, SEC), lambda s, i: (s, 0, 0))],
        out_specs=pl.BlockSpec((None, tm, 2 * SEC), lambda s, i: (s, i, 0)),
        out_shape=jax.ShapeDtypeStruct((2, t, 2 * SEC), BF16),
        scratch_shapes=[pltpu.VMEM((SEC, 2 * SEC), BF16)],
        compiler_params=_cp(("arbitrary", "arbitrary")),
        name="fnet_chan",
    )(proj, bdc, bds, bdw)
    rt = 1024
    return pl.pallas_call(
        _f2_kernel,
        grid=(seq // rt, batch, 2),
        in_specs=[pl.BlockSpec((rt, seq), lambda r, b, s: (r, 0)),
                  pl.BlockSpec((rt, seq), lambda r, b, s: (r, 0)),
                  pl.BlockSpec((None, seq, 2 * SEC), lambda r, b, s: (s, b, 0))],
        out_specs=pl.BlockSpec((None, rt, SEC), lambda r, b, s: (b, r, s)),
        out_shape=jax.ShapeDtypeStruct((batch, seq, GROUP_WIDTH), ACT),
        compiler_params=_cp(("arbitrary", "arbitrary", "arbitrary")),
        name="fnet_seq",
    )(dft_c, dft_s, z)


def _sgu_kernel(u_ref, v_ref, g_ref, w_ref, b_ref, o_ref, *, tm):
    gv = [jax.nn.gelu(v_ref[s].astype(F32)) for s in range(2)]
    ms = (jnp.sum(gv[0] * gv[0], axis=-1, keepdims=True)
          + jnp.sum(gv[1] * gv[1], axis=-1, keepdims=True)) * (1.0 / GROUP_WIDTH)
    r = lax.rsqrt(ms + EPS)
    vn = [(gv[s] * r * g_ref[:, s * SEC:(s + 1) * SEC]).astype(BF16) for s in range(2)]
    lo = lax.broadcasted_iota(jnp.int32, (CHUNK, 2 * HEAD_DIM), 1) < HEAD_DIM
    zero = jnp.zeros((CHUNK, 2 * HEAD_DIM), BF16)
    for s in range(2):
        gu = jax.nn.gelu(u_ref[s].astype(F32))
        for c in range(tm // CHUNK):
            rows = slice(c * CHUNK, (c + 1) * CHUNK)
            for pp in range(2):
                pair = 2 * s + pp
                lanes = slice(pp * 2 * HEAD_DIM, (pp + 1) * 2 * HEAD_DIM)
                vp = vn[s][rows, lanes]
                rhs = jnp.concatenate([jnp.where(lo, vp, zero), jnp.where(lo, zero, vp)], axis=0)
                z = jnp.dot(w_ref[pair], rhs, preferred_element_type=F32)
                bias = jnp.where(lo, b_ref[:, 2 * pair:2 * pair + 1], b_ref[:, 2 * pair + 1:2 * pair + 2])
                o_ref[rows, s * SEC + pp * 2 * HEAD_DIM:s * SEC + (pp + 1) * 2 * HEAD_DIM] = (gu[rows, lanes] * (z + bias)).astype(o_ref.dtype)


def _sgu(proj, sgu_norm, sgu_w_bf, sgu_bt):
    t = proj.shape[1]
    tm = 512
    return pl.pallas_call(
        functools.partial(_sgu_kernel, tm=tm),
        grid=(t // tm,),
        in_specs=[pl.BlockSpec((2, tm, SEC), lambda i: (SEC_U // 2, i, 0)),
                  pl.BlockSpec((2, tm, SEC), lambda i: (SEC_V // 2, i, 0)),
                  pl.BlockSpec((1, GROUP_WIDTH), lambda i: (0, 0)),
                  pl.BlockSpec((N_GROUP_HEADS // 2, CHUNK, 2 * CHUNK), lambda i: (0, 0, 0)),
                  pl.BlockSpec((CHUNK, N_GROUP_HEADS), lambda i: (0, 0))],
        out_specs=pl.BlockSpec((tm, GROUP_WIDTH), lambda i: (i, 0)),
        out_shape=jax.ShapeDtypeStruct((t, GROUP_WIDTH), ACT),
        compiler_params=_cp(("arbitrary",)),
        name="sgu",
    )(proj, proj, sgu_norm, sgu_w_bf, sgu_bt)


LOG2E = math.log2(math.e)
QK_SCALE = HEAD_DIM ** -0.5 * LOG2E
ATT_TQ = 128
PAIR = 2 * HEAD_DIM


def _rope(x, cos_t, sin_t):
    n = x.shape[-1]
    half = ROPE_DIM // 2
    lane = lax.broadcasted_iota(jnp.int32, x.shape, 1) % HEAD_DIM
    swapped = jnp.where(lane < half, pltpu.roll(x, n - half, 1), pltpu.roll(x, half, 1))
    return x * cos_t + swapped * sin_t


def _lo_half(shape):
    return lax.broadcasted_iota(jnp.int32, shape, 1) < HEAD_DIM


def _stage_scores(q_pairs, k, valid, s_ref, slot, grp):
    tq = q_pairs[0].shape[0]
    lo = _lo_half(q_pairs[0].shape)
    zero = jnp.zeros_like(q_pairs[0])
    blocks = []
    for q in q_pairs:
        blocks += [jnp.where(lo, q, zero), jnp.where(lo, zero, q)]
    s = _nt(jnp.concatenate(blocks, axis=0), k)
    for i in range(len(blocks)):
        s_ref[slot, grp, i * tq:(i + 1) * tq, :] = jnp.where(valid, s[i * tq:(i + 1) * tq], NEG_INF)


def _stage_softmax(s_ref, p_ref, m_ref, l_ref, slot, grp, tq):
    s = s_ref[slot, grp]
    m = jnp.max(s, axis=-1, keepdims=True)
    p = jnp.exp2(s - m)
    p_ref[slot, grp] = p.astype(BF16)
    l = jnp.sum(p, axis=-1, keepdims=True)
    n_pairs = s.shape[0] // (2 * tq)
    shp = (tq, PAIR)
    lo = _lo_half(shp)
    for i in range(n_pairs):
        a, b = slice(2 * i * tq, (2 * i + 1) * tq), slice((2 * i + 1) * tq, (2 * i + 2) * tq)
        m_ref[slot, grp * n_pairs + i] = jnp.where(lo, jnp.broadcast_to(m[a], shp), jnp.broadcast_to(m[b], shp))
        l_ref[slot, grp * n_pairs + i] = jnp.where(lo, jnp.broadcast_to(l[a], shp), jnp.broadcast_to(l[b], shp))


def _stage_pv(p_ref, v, slot, grp, tq):
    o = jnp.dot(p_ref[slot, grp], v, preferred_element_type=F32)
    lo = _lo_half((tq, PAIR))
    return [jnp.where(lo, o[2 * i * tq:(2 * i + 1) * tq], o[(2 * i + 1) * tq:(2 * i + 2) * tq])
            for i in range(o.shape[0] // (2 * tq))]


ATT_ITEMS = 16


def _pipeline(n_items, stage_a, stage_b, stage_c):
    def body(jj, carry):
        for stage in (stage_a, stage_b, stage_c):
            for t in range(ATT_ITEMS):
                stage(jj * ATT_ITEMS + t, t)
        return carry

    lax.fori_loop(0, n_items // ATT_ITEMS, body, 0)


def _band_valid(tq, tk, w, delta):
    d = (lax.broadcasted_iota(jnp.int32, (tq, tk), 1) - lax.broadcasted_iota(jnp.int32, (tq, tk), 0)) + delta
    return (d <= w) & (d >= -w)


def _swa_kernel(q_ref, kv_ref, cos_ref, sin_ref, sink_ref, o_ref, qs_ref, ks_ref, vs_ref,
                s_ref, p_ref, m_ref, l_ref, *, seq):
    w, tq = SWA_HALF_WINDOW, ATT_TQ
    tk = tq + 2 * w
    cos, sin = cos_ref[...], sin_ref[...]
    for s in range(2):
        qs_ref[s] = (_rope(q_ref[s].astype(F32), cos, sin) * QK_SCALE).astype(BF16)
    kk = _rope(kv_ref[:, 0:PAIR].astype(F32), cos[:, 0:PAIR], sin[:, 0:PAIR])
    vv = kv_ref[:, PAIR:2 * PAIR].astype(F32)
    lo = lax.broadcasted_iota(jnp.int32, kk.shape, 1) < HEAD_DIM
    kk_sw, vv_sw = pltpu.roll(kk, HEAD_DIM, 1), pltpu.roll(vv, HEAD_DIM, 1)
    ks_ref[0] = jnp.where(lo, kk, kk_sw).astype(BF16)
    ks_ref[1] = jnp.where(lo, kk_sw, kk).astype(BF16)
    vs_ref[0] = jnp.where(lo, vv, vv_sw).astype(BF16)
    vs_ref[1] = jnp.where(lo, vv_sw, vv).astype(BF16)

    def coords(j):
        q0 = pl.multiple_of(j * tq, tq)
        return q0, pl.multiple_of(jnp.clip(q0 - w, 0, seq - tk), tq)

    def scores(j, slot):
        q0, k0 = coords(j)
        valid = _band_valid(tq, tk, w, k0 - q0)
        for g in range(2):
            q_pairs = [qs_ref[g, pl.ds(q0, tq), 0:PAIR], qs_ref[g, pl.ds(q0, tq), PAIR:2 * PAIR]]
            _stage_scores(q_pairs, ks_ref[g, pl.ds(k0, tk), :], valid, s_ref, slot, g)

    def softmax(j, slot):
        for g in range(2):
            _stage_softmax(s_ref, p_ref, m_ref, l_ref, slot, g, tq)

    def output(j, slot):
        q0, k0 = coords(j)
        for g in range(2):
            outs = _stage_pv(p_ref, vs_ref[g, pl.ds(k0, tk), :], slot, g, tq)
            for i, o in enumerate(outs):
                pair = 2 * g + i
                m, l = m_ref[slot, pair], l_ref[slot, pair]
                sink = sink_ref[:, pair * PAIR:(pair + 1) * PAIR] * LOG2E
                m2 = jnp.maximum(m, sink)
                a = jnp.exp2(m - m2)
                o_ref[pl.ds(q0, tq), pair * PAIR:(pair + 1) * PAIR] = (
                    o * a / (l * a + jnp.exp2(sink - m2))).astype(o_ref.dtype)

    _pipeline(seq // tq, scores, softmax, output)


def _swa(proj, cos_q, sin_q, sink_row, batch, seq):
    t = batch * seq
    tab = pl.BlockSpec((None, seq, SEC), lambda b: (b, 0, 0))
    return pl.pallas_call(
        functools.partial(_swa_kernel, seq=seq),
        grid=(batch,),
        in_specs=[pl.BlockSpec((2, seq, SEC), lambda b: (SEC_QC // 2, b, 0)),
                  pl.BlockSpec((None, seq, SEC), lambda b: (SEC_KVC, b, 0)),
                  tab, tab,
                  pl.BlockSpec((1, GROUP_WIDTH), lambda b: (0, 0))],
        out_specs=pl.BlockSpec((seq, GROUP_WIDTH), lambda b: (b, 0)),
        out_shape=jax.ShapeDtypeStruct((t, GROUP_WIDTH), ACT),
        scratch_shapes=[pltpu.VMEM((2, seq, SEC), BF16),
                        pltpu.VMEM((2, seq, PAIR), BF16),
                        pltpu.VMEM((2, seq, PAIR), BF16),
                        pltpu.VMEM((ATT_ITEMS, 2, 4 * ATT_TQ, ATT_TQ + 2 * SWA_HALF_WINDOW), F32),
                        pltpu.VMEM((ATT_ITEMS, 2, 4 * ATT_TQ, ATT_TQ + 2 * SWA_HALF_WINDOW), BF16),
                        pltpu.VMEM((ATT_ITEMS, 4, ATT_TQ, PAIR), F32),
                        pltpu.VMEM((ATT_ITEMS, 4, ATT_TQ, PAIR), F32)],
        compiler_params=_cp(("arbitrary",)),
        name="swa",
    )(proj, proj, cos_q, sin_q, sink_row)


def _dil_kernel(q_ref, k_ref, v_ref, cos_ref, sin_ref, o_ref,
                qf_ref, kf_ref, vf_ref, m_ref, l_ref, acc_ref, qd_ref, kd_ref, vd_ref,
                s_ref, p_ref, mt_ref, lt_ref, *, seq):
    cos, sin = cos_ref[...], sin_ref[...]
    q = _rope(q_ref[...].astype(F32), cos, sin) * QK_SCALE
    k = _rope(k_ref[...].astype(F32), cos, sin)
    for pair in range(2):
        lanes = slice(pair * PAIR, (pair + 1) * PAIR)
        qf_ref[pair] = q[:, lanes]
        kf_ref[pair] = k[:, lanes]
        vf_ref[pair] = v_ref[:, lanes].astype(F32)

    for window, dil in DIL_CONFIGS:
        w = window // (2 * dil)
        ls = seq // dil
        tq = min(ATT_TQ, ls)
        tk = min(tq + 2 * w, ls)

        def rows_of(start, size, dil=dil):
            return pl.ds(start, size) if dil == 1 else pl.ds(start, size, stride=dil)

        def permute(r, carry, ls=ls, rows_of=rows_of):
            dst = pl.ds(pl.multiple_of(r * ls, ls), ls)
            for pair in range(2):
                qd_ref[pair, dst, :] = qf_ref[pair, rows_of(r, ls), :].astype(BF16)
                kd_ref[pair, dst, :] = kf_ref[pair, rows_of(r, ls), :].astype(BF16)
                vd_ref[pair, dst, :] = vf_ref[pair, rows_of(r, ls), :].astype(BF16)
            return carry

        lax.fori_loop(0, dil, permute, 0, unroll=min(dil, 4))
        tiles_per_res = ls // tq

        def coords(j, w=w, ls=ls, tq=tq, tk=tk, tpr=tiles_per_res):
            r = j // tpr
            q0 = (j % tpr) * tq
            k0 = jnp.clip(q0 - w, 0, ls - tk)
            return r, q0, pl.multiple_of(r * ls + k0, w), k0 - q0

        def scores(j, slot, w=w, tq=tq, tk=tk, coords=coords):
            _, _, krow, delta = coords(j)
            valid = _band_valid(tq, tk, w, delta)
            qrow = pl.multiple_of(j * tq, tq)
            for pair in range(2):
                _stage_scores([qd_ref[pair, pl.ds(qrow, tq), :]], kd_ref[pair, pl.ds(krow, tk), :], valid,
                              s_ref.at[:, :, :, 0:tk], slot, pair)

        def softmax(j, slot, tq=tq, tk=tk):
            for pair in range(2):
                _stage_softmax(s_ref.at[:, :, :, 0:tk], p_ref.at[:, :, :, 0:tk], mt_ref, lt_ref, slot, pair, tq)

        def merge(j, slot, tq=tq, tk=tk, dil=dil, coords=coords, rows_of=rows_of):
            r, q0, krow, _ = coords(j)
            rows = rows_of(r + dil * q0, tq)
            for pair in range(2):
                (o,) = _stage_pv(p_ref.at[:, :, :, 0:tk], vd_ref[pair, pl.ds(krow, tk), :], slot, pair, tq)
                m, l = mt_ref[slot, pair], lt_ref[slot, pair]
                if dil == DIL_CONFIGS[0][1]:
                    acc_ref[pair, rows, :], l_ref[pair, rows, :], m_ref[pair, rows, :] = o, l, m
                    continue
                m_old = m_ref[pair, rows, :]
                m_new = jnp.maximum(m_old, m)
                a, b = jnp.exp2(m_old - m_new), jnp.exp2(m - m_new)
                acc_ref[pair, rows, :] = acc_ref[pair, rows, :] * a + o * b
                l_ref[pair, rows, :] = l_ref[pair, rows, :] * a + l * b
                m_ref[pair, rows, :] = m_new

        _pipeline(seq // tq, scores, softmax, merge)

    for pair in range(2):
        o_ref[:, pair * PAIR:(pair + 1) * PAIR] = (acc_ref[pair] / l_ref[pair]).astype(o_ref.dtype)


def _dilated(proj, cos_q, sin_q, batch, seq):
    t = batch * seq

    def sec(base):
        return pl.BlockSpec((None, seq, SEC), lambda b, g: (base + g, b, 0))

    tab = pl.BlockSpec((None, seq, SEC), lambda b, g: (b, 0, 0))
    return pl.pallas_call(
        functools.partial(_dil_kernel, seq=seq),
        grid=(batch, 2),
        in_specs=[sec(SEC_QD), sec(SEC_KD), sec(SEC_VD), tab, tab],
        out_specs=pl.BlockSpec((seq, SEC), lambda b, g: (b, g)),
        out_shape=jax.ShapeDtypeStruct((t, GROUP_WIDTH), ACT),
        scratch_shapes=[pltpu.VMEM((2, seq, PAIR), F32)] * 6 + [pltpu.VMEM((2, seq, PAIR), BF16)] * 3 + [
            pltpu.VMEM((ATT_ITEMS, 2, 2 * ATT_TQ, 2 * ATT_TQ), F32),
            pltpu.VMEM((ATT_ITEMS, 2, 2 * ATT_TQ, 2 * ATT_TQ), BF16),
            pltpu.VMEM((ATT_ITEMS, 2, ATT_TQ, PAIR), F32),
            pltpu.VMEM((ATT_ITEMS, 2, ATT_TQ, PAIR), F32)],
        compiler_params=_cp(("arbitrary", "arbitrary")),
        name="dilated",
    )(proj, proj, proj, cos_q, sin_q)


def _gnorm(y, g):
    return y * lax.rsqrt(jnp.mean(y * y, axis=-1, keepdims=True) + EPS) * g


OUT_SUB = 256
OUT_TM = 512


def _out_kernel(ya_ref, yb_ref, yc_ref, yd_ref, gn_ref, w_ref, x_ref, mod_ref, nf_ref, rcat_ref,
                xn_ref, h_ref, aff_ref):
    ys = (ya_ref, yb_ref, yc_ref, yd_ref)
    gw = GROUP_WIDTH
    row_tiles = x_ref.shape[1] // LANES
    for sub in range(x_ref.shape[0] // OUT_SUB):
        rows = slice(sub * OUT_SUB, (sub + 1) * OUT_SUB)
        acc = None
        for g in range(4):
            yn = _gnorm(ys[g][rows, :].astype(F32), gn_ref[:, g * gw:(g + 1) * gw]).astype(BF16)
            part = jnp.dot(yn, w_ref[g * gw:(g + 1) * gw, :], preferred_element_type=F32)
            acc = part if acc is None else acc + part
        xn = x_ref[rows, :] + mod_ref[2:3, :] * acc
        xn_ref[rows, :] = xn
        h = _gnorm(xn, nf_ref[...]) * (1.0 + mod_ref[4:5, :]) + mod_ref[3:4, :]
        for c in range(row_tiles):
            h_ref[pl.ds(sub * OUT_SUB * row_tiles + c, OUT_SUB, stride=row_tiles), :] = h[:, c * LANES:(c + 1) * LANES]
        a = _nt(rcat_ref[...], h.astype(BF16))
        logits = a[0:N_EXPERTS] + a[N_EXPERTS:2 * N_EXPERTS]
        mxl = jnp.max(logits, axis=0, keepdims=True)
        ex = jnp.exp(logits - mxl)
        aff_ref[:, rows] = ex / jnp.sum(ex, axis=0, keepdims=True)


def _out_proj(ya, yb, yc, yd, gn, w_bf, layer, x2, mod_l, nf, rcat, seq):
    t, d = x2.shape
    tm = OUT_TM
    tpb = seq // tm
    ROW_TILES = d // LANES
    ysp = pl.BlockSpec((tm, GROUP_WIDTH), lambda i: (i, 0))
    return pl.pallas_call(
        _out_kernel,
        grid=(t // tm,),
        in_specs=[ysp] * 4 + [
            pl.BlockSpec((1, d), lambda i: (0, 0)),
            pl.BlockSpec((None, d, d), lambda i: (layer, 0, 0)),
            pl.BlockSpec((tm, d), lambda i: (i, 0)),
            pl.BlockSpec((None, 6, d), lambda i: (i // tpb, 0, 0)),
            pl.BlockSpec((1, d), lambda i: (0, 0)),
            pl.BlockSpec((2 * N_EXPERTS, d), lambda i: (0, 0))],
        out_specs=[pl.BlockSpec((tm, d), lambda i: (i, 0)),
                   pl.BlockSpec((tm * ROW_TILES, LANES), lambda i: (i, 0)),
                   pl.BlockSpec((N_EXPERTS, tm), lambda i: (0, i))],
        out_shape=[jax.ShapeDtypeStruct((t, d), F32),
                   jax.ShapeDtypeStruct((t * ROW_TILES, LANES), F32),
                   jax.ShapeDtypeStruct((N_EXPERTS, t), F32)],
        compiler_params=_cp(("arbitrary",)),
        name="out_proj",
    )(ya, yb, yc, yd, gn, w_bf, x2, mod_l, nf, rcat)


def _route_kernel(aff_ref, tri_ref, idx_ref, gate_ref, dst_ref, cnt_ref, start_ref,
                  pos_ref, q_ref, *, cap, seq):
    aff = aff_ref[...]

    def as_f32(bits):
        return lax.bitcast_convert_type(bits, F32)

    def enough(cand):
        return jnp.sum((aff >= as_f32(cand)).astype(jnp.int32), axis=1, keepdims=True) >= cap

    top = jnp.full((aff.shape[0], 1), 1 << 30, jnp.int32)
    thr = jnp.where(enough(top), top, 0)

    def step(k, thr):
        hi = lax.shift_left(jnp.int32(1), 29 - 2 * k)
        lo = lax.shift_left(jnp.int32(1), 28 - 2 * k)
        c1, c2 = thr | lo, thr | hi
        c3 = c2 | lo
        return jnp.where(enough(c3), c3, jnp.where(enough(c2), c2, jnp.where(enough(c1), c1, thr)))

    thr = lax.fori_loop(0, 15, step, thr)
    gt = aff >= as_f32(thr + 1)
    eq = (aff >= as_f32(thr)) & jnp.logical_not(gt)
    need = (cap - jnp.sum(gt.astype(jnp.int32), axis=1, keepdims=True)).astype(F32)
    eq_rank = jnp.dot(eq.astype(BF16), tri_ref[...], preferred_element_type=F32)
    sel = gt | (eq & (eq_rank < need))
    n_e = aff.shape[0]
    sel_f = sel.astype(F32)
    sel_bf = sel.astype(BF16)
    pos_ref[...] = jnp.where(sel, jnp.dot(sel_bf, tri_ref[...], preferred_element_type=F32), -1.0)

    cnt = jnp.sum(sel_f, axis=0, keepdims=True)
    cnt8 = jnp.broadcast_to(cnt, (8, seq)).astype(BF16)
    start = (jnp.dot(cnt8, tri_ref[...], preferred_element_type=F32)[0:1]
             + (pl.program_id(0) * (n_e * cap)).astype(F32))
    lower = (lax.broadcasted_iota(jnp.int32, (n_e, n_e), 1)
             < lax.broadcasted_iota(jnp.int32, (n_e, n_e), 0)).astype(BF16)
    q_ref[...] = start + jnp.dot(lower, sel_bf, preferred_element_type=F32)
    cnt_ref[...] = cnt
    start_ref[...] = start

    tok = lax.broadcasted_iota(jnp.int32, (1, seq), 1) + pl.program_id(0) * seq
    tok_hi = lax.shift_right_logical(tok, 7).astype(F32)
    tok_lo = (tok & (LANES - 1)).astype(F32)
    slot = lax.broadcasted_iota(jnp.int32, (cap, seq), 0).astype(F32)
    eye = (lax.broadcasted_iota(jnp.int32, (cap, cap), 0) == lax.broadcasted_iota(jnp.int32, (cap, cap), 1))
    zero_row = jnp.zeros((1, seq), F32)

    def per_expert(e, carry):
        onehot = (pos_ref[pl.ds(e, 1), :] == slot).astype(BF16)
        q = q_ref[pl.ds(e, 1), :]
        q_hi = jnp.floor(q * (1.0 / LANES))
        a = aff_ref[pl.ds(e, 1), :]
        a1 = a.astype(BF16).astype(F32)
        a2 = (a - a1).astype(BF16).astype(F32)
        pieces = jnp.concatenate([tok_hi, tok_lo, q_hi, q - q_hi * LANES, a1, a2, a - a1 - a2, zero_row], axis=0)
        got = _nt(pieces.astype(BF16), onehot)
        idx_ref[e] = (got[0:1] * LANES + got[1:2]).astype(jnp.int32)
        dst_ref[e] = (got[2:3] * LANES + got[3:4]).astype(jnp.int32)
        gate_row = got[4:5] + got[5:6] + got[6:7]
        gate_col = jnp.sum(jnp.where(eye, gate_row, 0.0), axis=1, keepdims=True)
        gate_ref[e] = jnp.broadcast_to(gate_col, (cap, LANES))
        return carry

    lax.fori_loop(0, n_e, per_expert, 0)


def _route(aff_t, tri, batch, seq, cap):
    e, t = aff_t.shape
    blk = pl.BlockSpec((e, seq), lambda b: (0, b))
    colblk = pl.BlockSpec((e, cap, LANES), lambda b: (0, b, 0))
    rowblk = pl.BlockSpec((1, seq), lambda b: (0, b))
    slotblk = pl.BlockSpec((e, None, 1, cap), lambda b: (0, b, 0, 0))
    slot_i32 = jax.ShapeDtypeStruct((e, batch, 1, cap), jnp.int32)
    return pl.pallas_call(
        functools.partial(_route_kernel, cap=cap, seq=seq),
        grid=(batch,),
        in_specs=[blk, pl.BlockSpec((seq, seq), lambda b: (0, 0))],
        out_specs=[slotblk, colblk, slotblk, rowblk, rowblk],
        out_shape=[slot_i32, jax.ShapeDtypeStruct((e, batch * cap, LANES), F32), slot_i32,
                   jax.ShapeDtypeStruct((1, t), F32), jax.ShapeDtypeStruct((1, t), F32)],
        scratch_shapes=[pltpu.VMEM((e, seq), F32)] * 2,
        compiler_params=_cp(("arbitrary",)),
        name="route",
    )(aff_t, tri)


DMA_UNROLL = 32


def _moe_kernel(idx_ref, dst_ref, h_hbm, gate_ref, wg_ref, wu_ref, wd_ref, sorted_hbm,
                xg_ref, xe_ref, acc_ref, ob_ref, sem, *, rows):
    e = pl.program_id(0)
    f = pl.program_id(1)
    n_e = pl.num_programs(0)

    row_tiles = xe_ref.shape[1] // LANES

    def scatter():
        def body(i, carry):
            for u in range(DMA_UNROLL):
                r = i * DMA_UNROLL + u
                pltpu.make_async_copy(ob_ref.at[:, r, :], sorted_hbm.at[:, dst_ref[e * rows + r], :],
                                      sem.at[1]).start()
            return carry
        lax.fori_loop(0, rows // DMA_UNROLL, body, 0)

    def wait_scatter():
        for c in range(row_tiles):
            pltpu.make_async_copy(ob_ref.at[c], sorted_hbm.at[c, pl.ds(0, rows), :], sem.at[1]).wait()

    def gather(expert):
        def body(i, carry):
            for u in range(DMA_UNROLL):
                r = i * DMA_UNROLL + u
                tok = idx_ref[expert * rows + r]
                pltpu.make_async_copy(h_hbm.at[pl.ds(pl.multiple_of(tok * row_tiles, row_tiles), row_tiles), :],
                                      xg_ref.at[:, r, :], sem.at[0]).start()
            return carry
        lax.fori_loop(0, rows // DMA_UNROLL, body, 0)

    @pl.when((e == 0) & (f == 0))
    def _():
        gather(0)

    @pl.when(f == 0)
    def _():
        for c in range(row_tiles):
            pltpu.make_async_copy(h_hbm.at[pl.ds(0, rows), :], xg_ref.at[c], sem.at[0]).wait()
        for c in range(row_tiles):
            xe_ref[:, c * LANES:(c + 1) * LANES] = xg_ref[c].astype(BF16)
        acc_ref[...] = jnp.zeros_like(acc_ref)

        @pl.when(e + 1 < n_e)
        def _():
            gather(e + 1)

    xe = xe_ref[...]
    a = jnp.dot(xe, wg_ref[...].astype(BF16), preferred_element_type=F32)
    u = jnp.dot(xe, wu_ref[...].astype(BF16), preferred_element_type=F32)
    act = (a * jax.nn.sigmoid(a) * u).astype(BF16)
    acc_ref[...] += jnp.dot(act, wd_ref[...].astype(BF16), preferred_element_type=F32)

    @pl.when(f == pl.num_programs(1) - 1)
    def _():
        @pl.when(e > 0)
        def _():
            wait_scatter()

        gate = gate_ref[:, 0:1]
        for c in range(row_tiles):
            ob_ref[c] = acc_ref[:, c * LANES:(c + 1) * LANES] * gate
        scatter()

        @pl.when(e == n_e - 1)
        def _():
            wait_scatter()


def _moe(idx_flat, dst_flat, gate_col, h2, wg, wu, wd, layer):
    _, ne, d, ff = wg.shape
    rows = gate_col.shape[1]
    row_tiles = d // LANES
    tf = 256
    grid_spec = pltpu.PrefetchScalarGridSpec(
        num_scalar_prefetch=2,
        grid=(ne, ff // tf),
        in_specs=[pl.BlockSpec(memory_space=pl.ANY),
                  pl.BlockSpec((None, rows, LANES), lambda e, f, idx, dst: (e, 0, 0)),
                  pl.BlockSpec((None, None, d, tf), lambda e, f, idx, dst: (layer, e, 0, f)),
                  pl.BlockSpec((None, None, d, tf), lambda e, f, idx, dst: (layer, e, 0, f)),
                  pl.BlockSpec((None, None, tf, d), lambda e, f, idx, dst: (layer, e, f, 0))],
        out_specs=pl.BlockSpec(memory_space=pl.ANY),
        scratch_shapes=[pltpu.VMEM((row_tiles, rows, LANES), F32), pltpu.VMEM((rows, d), BF16),
                        pltpu.VMEM((rows, d), F32), pltpu.VMEM((row_tiles, rows, LANES), F32),
                        pltpu.SemaphoreType.DMA((2,))],
    )
    return pl.pallas_call(
        functools.partial(_moe_kernel, rows=rows),
        grid_spec=grid_spec,
        out_shape=jax.ShapeDtypeStruct((row_tiles, ne * rows, LANES), F32),
        compiler_params=_cp(("arbitrary", "arbitrary")),
        name="moe",
    )(idx_flat, dst_flat, h2, gate_col, wg, wu, wd)


COMBINE_CHUNK = 256
COMBINE_TM = 512
COMBINE_STREAMS = 4
COMBINE_SLOTS = 3


def _combine_kernel(tile_ref, start_ref, cnt_ref, sorted_hbm, xn_ref, mod_ref, fn_ref, out_ref,
                    buf_ref, acc_ref, done_ref, sem, *, tm, total, final_norm):
    i = pl.program_id(0)
    n_tiles = pl.num_programs(0)
    row_tiles = buf_ref.shape[1]
    kc = COMBINE_CHUNK

    def span(tile):
        lo = tile_ref[tile]
        base = lax.shift_left(lax.shift_right_logical(lo, 3), 3)
        n_chunks = jnp.maximum(lax.shift_right_logical(tile_ref[tile + 1] - base + (kc - 1), 8), 1)
        return base, n_chunks

    def chunk_rows(base, j):
        return pl.multiple_of(jnp.minimum(base + j * kc, total - kc), 8)

    def chunk_copies(first_row, slot):
        per = row_tiles // COMBINE_STREAMS
        return [pltpu.make_async_copy(sorted_hbm.at[pl.ds(s * per, per), pl.ds(first_row, kc), :],
                                      buf_ref.at[slot, pl.ds(s * per, per)], sem.at[slot])
                for s in range(COMBINE_STREAMS)]

    def start_chunk(first_row, slot):
        for s, cp in enumerate(chunk_copies(first_row, slot)):
            cp.start(priority=s % 2)

    def wait_chunk(first_row, slot):
        for cp in chunk_copies(first_row, slot):
            cp.wait()

    def issue_next():
        t = done_ref[2]

        @pl.when(t < n_tiles)
        def _():
            j = done_ref[3]
            base_t, n_t = span(t)
            issued = done_ref[1]
            start_chunk(chunk_rows(base_t, j), issued % COMBINE_SLOTS)
            done_ref[1] = issued + 1
            wrap = j + 1 >= n_t
            done_ref[2] = jnp.where(wrap, t + 1, t)
            done_ref[3] = jnp.where(wrap, 0, j + 1)

    base, n_chunks = span(i)

    @pl.when(i == 0)
    def _():
        for k in range(4):
            done_ref[k] = 0
        for _ in range(COMBINE_SLOTS - 1):
            issue_next()

    eye = (lax.broadcasted_iota(jnp.int32, (tm, tm), 0) == lax.broadcasted_iota(jnp.int32, (tm, tm), 1))
    lo_col = jnp.sum(jnp.where(eye, start_ref[...], 0.0), axis=1, keepdims=True)
    hi_col = lo_col + jnp.sum(jnp.where(eye, cnt_ref[...], 0.0), axis=1, keepdims=True)
    acc_ref[...] = jnp.zeros_like(acc_ref)

    def chunk(j, carry):
        done = done_ref[0]
        slot = done % COMBINE_SLOTS
        first_row = chunk_rows(base, j)
        wait_chunk(first_row, slot)
        issue_next()

        row = (first_row + lax.broadcasted_iota(jnp.int32, (1, kc), 1)).astype(F32)
        nominal = (base + j * kc).astype(F32)
        pick = ((row >= lo_col) & (row < hi_col) & (row >= nominal)).astype(BF16)
        for c in range(0, row_tiles, 2):
            rhs = jnp.concatenate([buf_ref[slot, c], buf_ref[slot, c + 1]], axis=1).astype(BF16)
            acc_ref[:, c * LANES:(c + 2) * LANES] += jnp.dot(pick, rhs, preferred_element_type=F32)
        done_ref[0] = done + 1
        return carry

    lax.fori_loop(0, n_chunks, chunk, 0)

    x = xn_ref[...] + mod_ref[5:6, :] * acc_ref[...]
    if final_norm:
        x = _gnorm(x, fn_ref[...])
    out_ref[...] = x


def _combine(tile_start, start_row, cnt_row, sorted_rows, xn, mod_l, fn, seq, final_norm):
    t, d = xn.shape
    row_tiles, total, _ = sorted_rows.shape
    tm = COMBINE_TM
    tpb = seq // tm
    grid_spec = pltpu.PrefetchScalarGridSpec(
        num_scalar_prefetch=1,
        grid=(t // tm,),
        in_specs=[pl.BlockSpec((1, tm), lambda i, ts: (0, i)),
                  pl.BlockSpec((1, tm), lambda i, ts: (0, i)),
                  pl.BlockSpec(memory_space=pl.ANY),
                  pl.BlockSpec((tm, d), lambda i, ts: (i, 0)),
                  pl.BlockSpec((None, 6, d), lambda i, ts: (i // tpb, 0, 0)),
                  pl.BlockSpec((1, d), lambda i, ts: (0, 0))],
        out_specs=pl.BlockSpec((tm, d), lambda i, ts: (i, 0)),
        scratch_shapes=[pltpu.VMEM((COMBINE_SLOTS, row_tiles, COMBINE_CHUNK, LANES), F32),
                        pltpu.VMEM((tm, d), F32), pltpu.SMEM((4,), jnp.int32),
                        pltpu.SemaphoreType.DMA((COMBINE_SLOTS,))],
    )
    return pl.pallas_call(
        functools.partial(_combine_kernel, tm=tm, total=total, final_norm=final_norm),
        grid_spec=grid_spec,
        out_shape=jax.ShapeDtypeStruct((t, d), F32),
        compiler_params=_cp(("arbitrary",)),
        name="combine",
    )(tile_start, start_row, cnt_row, sorted_rows, xn, mod_l, fn)


def _rope_tables(positions):
    b, s = positions.shape
    inv = jnp.power(jnp.float32(ROPE_THETA), -jnp.arange(0, ROPE_DIM, 2, dtype=F32) / ROPE_DIM)
    ang = positions.astype(F32)[..., None] * inv
    cos, sin = jnp.cos(ang), jnp.sin(ang)
    rest = HEAD_DIM - ROPE_DIM
    c64 = jnp.concatenate([cos, cos, jnp.ones((b, s, rest), F32)], axis=-1)
    s64 = jnp.concatenate([-sin, sin, jnp.zeros((b, s, rest), F32)], axis=-1)
    return jnp.tile(c64, (1, 1, 4)), jnp.tile(s64, (1, 1, 4))


def _dft_tables(n):
    k = jnp.arange(n, dtype=jnp.int32)
    ang = ((k[:, None] * k[None, :]) % n).astype(F32) * (2.0 * math.pi / n)
    return jnp.cos(ang), jnp.sin(ang)


def _dft_tables_split(n):
    k = jnp.arange(n, dtype=jnp.int32)[:, None]
    m0 = jnp.arange(LANES, dtype=jnp.int32)[None, :]
    m1 = jnp.arange(n // LANES, dtype=jnp.int32)[None, :]
    a = ((k * m0) % n).astype(F32) * (2.0 * math.pi / n)
    b = ((k * m1 * LANES) % n).astype(F32) * (2.0 * math.pi / n)
    reps = n // LANES
    ca, sa = jnp.tile(jnp.cos(a), (1, reps)), jnp.tile(jnp.sin(a), (1, reps))
    cb, sb = jnp.repeat(jnp.cos(b), LANES, axis=1), jnp.repeat(jnp.sin(b), LANES, axis=1)
    return ca * cb - sa * sb, sa * cb + ca * sb


def _block_diag(blocks):
    n, a, _ = blocks.shape
    eye = jnp.eye(n, dtype=blocks.dtype)
    return (eye[:, None, :, None] * blocks[:, :, None, :]).reshape(n * a, n * a)


def kernel(x, c, positions, ada_w, ada_b, norm_mix, w_in, fnet_w, sgu_norm, sgu_w, sgu_b, swa_sink,
           group_norm, w_out, norm_ffn, router_w, exp_w_gate, exp_w_up, exp_w_down, final_norm):
    batch, seq, d = x.shape
    depth = ada_w.shape[0]
    t = batch * seq
    cap = max(1, EC_FACTOR * seq // N_EXPERTS)

    rows = 16
    c_pad = jnp.zeros((rows, d), F32).at[:batch].set(c)
    mod = _ada(c_pad, ada_w, ada_b)[:, :batch].reshape(depth, batch, 6, d)

    cos_q, sin_q = _rope_tables(positions)
    dft_c, dft_s = _dft_tables_split(seq)
    dft_c, dft_s = dft_c.astype(BF16), dft_s.astype(BF16)
    cc, sc = _dft_tables(HEAD_DIM)
    bdc = _block_diag(jnp.broadcast_to(cc, (4, HEAD_DIM, HEAD_DIM)))
    bds = _block_diag(jnp.broadcast_to(sc, (4, HEAD_DIM, HEAD_DIM)))
    tri = (jnp.arange(seq)[:, None] < jnp.arange(seq)[None, :]).astype(BF16)

    w_in_bf, w_out_bf = w_in.astype(BF16), w_out.astype(BF16)
    x2 = x.reshape(t, d)
    for l in range(depth):
        mod_l = mod[l]
        proj = _in_proj(x2, mod_l, norm_mix[l].reshape(1, d), w_in_bf, l, seq)

        bdw = jnp.stack([_block_diag(fnet_w[l, 0:4]), _block_diag(fnet_w[l, 4:8])])
        ya = _fourier(proj, bdc, bds, bdw, dft_c, dft_s, batch, seq).reshape(t, GROUP_WIDTH)
        sgu_w_pairs = jnp.concatenate([sgu_w[l, 0::2], sgu_w[l, 1::2]], axis=-1).astype(BF16)
        yb = _sgu(proj, sgu_norm[l].reshape(1, GROUP_WIDTH), sgu_w_pairs, sgu_b[l].T)
        sink_row = jnp.repeat(swa_sink[l], HEAD_DIM).reshape(1, GROUP_WIDTH)
        yc = _swa(proj, cos_q, sin_q, sink_row, batch, seq)
        yd = _dilated(proj, cos_q, sin_q, batch, seq)

        r_t = router_w[l].T
        r_hi = r_t.astype(BF16)
        r_lo = (r_t - r_hi.astype(F32)).astype(BF16)
        rcat = jnp.concatenate([r_hi, r_lo], axis=0)
        xn, h2, aff_t = _out_proj(ya, yb, yc, yd, group_norm[l].reshape(1, d), w_out_bf, l,
                                  x2, mod_l, norm_ffn[l].reshape(1, d), rcat, seq)
        idx, gate_col, dst, cnt_row, start_row = _route(aff_t, tri, batch, seq, cap)
        sorted_rows = _moe(idx.reshape(-1), dst.reshape(-1), gate_col, h2,
                           exp_w_gate, exp_w_up, exp_w_down, l)
        n_pairs = batch * N_EXPERTS * cap
        tile_start = jnp.concatenate([start_row[0, ::COMBINE_TM].astype(jnp.int32), jnp.full((1,), n_pairs, jnp.int32)])
        x2 = _combine(tile_start, start_row, cnt_row, sorted_rows, xn, mod_l, final_norm.reshape(1, d), seq,
                      l == depth - 1)
    return x2.reshape(batch, seq, d)
```
